```python
import jax, jax.numpy as jnp
from jax import lax
import numpy as np

D_MODEL = 1024
BATCH = 2
SEQ = 8192
DEPTH = 1

CHUNK = 64
EPS = 1e-6
D_MIX = D_MODEL
D_A = D_MIX // 2
A_HEADS = 4
A_HEAD_DIM = D_A // A_HEADS
GMLP_BLOCK = 128
D_B = D_MIX - D_A
B_HEADS = 4
B_KEY_DIM = D_B // B_HEADS
B_VAL_DIM = D_B // B_HEADS
D_BK = B_HEADS * B_KEY_DIM
D_BV = B_HEADS * B_VAL_DIM
D_IN = 2 * D_A + 2 * D_BK + 2 * D_BV
N_GROUPS = 4
EXPERTS_PER_GROUP = 8
N_EXPERTS = N_GROUPS * EXPERTS_PER_GROUP
TOP_K_IN_GROUP = 2
D_EXPERT = D_MODEL // 2
MOE_BLOCK = 256

kernel_name = "hybrid_gmlp_hgrn2_hiermoe_block"


def rms_norm(x, gain):
    xf = x.astype(jnp.float32)
    y = xf * lax.rsqrt(jnp.mean(xf * xf, axis=-1, keepdims=True) + EPS)
    return (y * gain.astype(jnp.float32)).astype(x.dtype)


def gmlp_mixer(u, v, v_gain, w_s, b_s):
    bn, s, _ = u.shape
    nb = s // GMLP_BLOCK
    vh = rms_norm(v.reshape(bn, s, A_HEADS, A_HEAD_DIM), v_gain.reshape(A_HEADS, A_HEAD_DIM))
    vh = vh.reshape(bn, nb, GMLP_BLOCK, A_HEADS, A_HEAD_DIM)
    chunk_id = jnp.arange(GMLP_BLOCK) // CHUNK
    mask = chunk_id[None, :] <= chunk_id[:, None]
    w = jnp.where(mask[None], w_s, jnp.zeros_like(w_s))
    mixed = jnp.einsum('gij,bnjgc->bnigc', w, vh) + jnp.transpose(b_s)[:, :, None]
    return u * mixed.reshape(bn, s, D_A).astype(u.dtype)


def hgrn2_mixer(q_raw, f_raw, i_raw, g_raw, lb, out_gain):
    bn, s, _ = q_raw.shape
    n = s // CHUNK
    f32 = jnp.float32
    q = jax.nn.silu(q_raw.astype(f32))
    f = lb + (1.0 - lb) * jax.nn.sigmoid(f_raw.astype(f32))
    k = 1.0 - f
    lf = jnp.log(f)

    def split(t, d):
        return t.reshape(bn, n, CHUNK, B_HEADS, d).transpose(0, 3, 1, 2, 4)

    q, k, lf = split(q, B_KEY_DIM), split(k, B_KEY_DIM), split(lf, B_KEY_DIM)
    v = split(i_raw.astype(f32), B_VAL_DIM)
    b = lax.cumsum(lf, axis=3)
    ref = b[:, :, :, CHUNK // 2 - 1:CHUNK // 2, :]
    qe = q * jnp.exp(b - ref)
    ke = k * jnp.exp(ref - b)
    causal = jnp.tril(jnp.ones((CHUNK, CHUNK), dtype=bool))
    scores = jnp.einsum('bhncd,bhnsd->bhncs', qe, ke)
    scores = jnp.where(causal, scores, 0.0)
    o_intra = jnp.einsum('bhncs,bhnse->bhnce', scores, v)
    b_last = b[:, :, :, -1:, :]
    u_state = jnp.einsum('bhnsd,bhnse->bhnde', k * jnp.exp(b_last - b), v)
    decay = jnp.exp(b_last[:, :, :, 0, :])

    def step(state, xs):
        dec, upd = xs
        return dec[..., None] * state + upd, state

    s0 = jnp.zeros((bn, B_HEADS, B_KEY_DIM, B_VAL_DIM), f32)
    _, s_prev = lax.scan(step, s0, (jnp.moveaxis(decay, 2, 0), jnp.moveaxis(u_state, 2, 0)))
    s_prev = jnp.moveaxis(s_prev, 0, 2)
    o_inter = jnp.einsum('bhncd,bhnde->bhnce', q * jnp.exp(b), s_prev)
    o = (o_intra + o_inter).transpose(0, 2, 3, 1, 4).reshape(bn, s, B_HEADS, B_VAL_DIM)
    o = rms_norm(o, out_gain.reshape(B_HEADS, B_VAL_DIM))
    o = o * jax.nn.silu(g_raw.reshape(bn, s, B_HEADS, B_VAL_DIM).astype(f32))
    return o.reshape(bn, s, D_BV).astype(q_raw.dtype)


def hier_moe(xn, w_gr, b_gr, w_er, b_er, w_gate, w_up, w_down):
    bn, s, d = xn.shape
    t = bn * s
    xt = xn.reshape(t, d)
    f32 = jnp.float32
    g_logits = (xt @ w_gr).astype(f32) + b_gr.astype(f32)
    g_probs = jax.nn.softmax(g_logits, axis=-1)
    g_idx = jnp.argmax(g_logits, axis=-1)
    g_prob = jnp.take_along_axis(g_probs, g_idx[:, None], axis=1)[:, 0]
    e_logits = ((xt @ w_er).astype(f32) + b_er.astype(f32)).reshape(t, N_GROUPS, EXPERTS_PER_GROUP)
    e_logits = jnp.take_along_axis(e_logits, g_idx[:, None, None], axis=1)[:, 0]
    top_v, top_i = lax.top_k(e_logits, TOP_K_IN_GROUP)
    wts = g_prob[:, None] * jax.nn.softmax(top_v, axis=-1)
    eid = (g_idx[:, None] * EXPERTS_PER_GROUP + top_i).reshape(-1).astype(jnp.int32)
    tok = jnp.repeat(jnp.arange(t, dtype=jnp.int32), TOP_K_IN_GROUP)
    wt = wts.reshape(-1)
    n_assign = t * TOP_K_IN_GROUP

    order = jnp.argsort(eid)
    se, stok, swt = eid[order], tok[order], wt[order]
    counts = jnp.bincount(eid, length=N_EXPERTS).astype(jnp.int32)
    padded = ((counts + MOE_BLOCK - 1) // MOE_BLOCK) * MOE_BLOCK
    pend = jnp.cumsum(padded)
    pstart = pend - padded
    start = jnp.cumsum(counts) - counts
    dest = pstart[se] + (jnp.arange(n_assign, dtype=jnp.int32) - start[se])
    n_blk = -(-n_assign // MOE_BLOCK) + N_EXPERTS
    rows = n_blk * MOE_BLOCK
    buf_tok = jnp.full((rows,), t, jnp.int32).at[dest].set(stok)
    buf_wt = jnp.zeros((rows,), f32).at[dest].set(swt)
    blk_expert = jnp.clip(jnp.searchsorted(pend, jnp.arange(n_blk) * MOE_BLOCK, side='right'),
                          0, N_EXPERTS - 1)
    xpad = jnp.concatenate([xt, jnp.zeros((1, d), xt.dtype)], axis=0)

    def expert_block(args):
        tok_b, e = args
        xb = xpad[tok_b]
        h = jax.nn.silu(xb @ w_gate[e]) * (xb @ w_up[e])
        return h @ w_down[e]

    out = lax.map(expert_block, (buf_tok.reshape(n_blk, MOE_BLOCK), blk_expert))
    out = out.reshape(rows, d) * buf_wt[:, None].astype(out.dtype)
    y = jnp.zeros((t + 1, d), out.dtype).at[buf_tok].add(out)[:t]
    return y.reshape(bn, s, d).astype(xn.dtype)


def setup_inputs(seed: int = 0) -> dict:
    key = jax.random.key(seed)
    ks = jax.random.split(key, 20)
    nrm = jax.random.normal
    f32 = jnp.float32
    d = D_MODEL
    return {
        "x": nrm(ks[0], (BATCH, SEQ, d), f32),
        "norm1_gain": 1.0 + 0.05 * nrm(ks[1], (DEPTH, d), f32),
        "w_in": nrm(ks[2], (DEPTH, d, D_IN), f32) * d ** -0.5,
        "gmlp_v_gain": 1.0 + 0.05 * nrm(ks[3], (DEPTH, D_A), f32),
        "gmlp_w_s": nrm(ks[4], (DEPTH, A_HEADS, GMLP_BLOCK, GMLP_BLOCK), f32) * 0.5 * GMLP_BLOCK ** -0.5,
        "gmlp_b_s": 1.0 + 0.1 * nrm(ks[5], (DEPTH, A_HEADS, GMLP_BLOCK), f32),
        "hgrn_lower_bounds": 0.1 * nrm(ks[6], (DEPTH + 1, D_BK), f32),
        "hgrn_out_gain": 1.0 + 0.05 * nrm(ks[7], (DEPTH, D_BV), f32),
        "w_out": nrm(ks[8], (DEPTH, D_MIX, d), f32) * D_MIX ** -0.5,
        "norm2_gain": 1.0 + 0.05 * nrm(ks[9], (DEPTH, d), f32),
        "w_group_router": nrm(ks[10], (DEPTH, d, N_GROUPS), f32) * d ** -0.5,
        "b_group_router": 0.01 * nrm(ks[11], (DEPTH, N_GROUPS), f32),
        "w_expert_router": nrm(ks[12], (DEPTH, d, N_EXPERTS), f32) * d ** -0.5,
        "b_expert_router": 0.01 * nrm(ks[13], (DEPTH, N_EXPERTS), f32),
        "w_gate": nrm(ks[14], (DEPTH, N_EXPERTS, d, D_EXPERT), f32) * d ** -0.5,
        "w_up": nrm(ks[15], (DEPTH, N_EXPERTS, d, D_EXPERT), f32) * d ** -0.5,
        "w_down": nrm(ks[16], (DEPTH, N_EXPERTS, D_EXPERT, d), f32) * D_EXPERT ** -0.5,
        "final_gain": 1.0 + 0.05 * nrm(ks[17], (d,), f32),
    }


def reference(x, norm1_gain, w_in, gmlp_v_gain, gmlp_w_s, gmlp_b_s, hgrn_lower_bounds,
              hgrn_out_gain, w_out, norm2_gain, w_group_router, b_group_router,
              w_expert_router, b_expert_router, w_gate, w_up, w_down, final_gain):
    lb_all = jnp.cumsum(jax.nn.softmax(hgrn_lower_bounds.astype(jnp.float32), axis=0), axis=0)
    split_at = [D_A, 2 * D_A, 2 * D_A + D_BK, 2 * D_A + 2 * D_BK, 2 * D_A + 2 * D_BK + D_BV]
    h = x
    for l in range(DEPTH):
        n1 = rms_norm(h, norm1_gain[l])
        proj = n1 @ w_in[l]
        u, v, q, f, i, g = jnp.split(proj, split_at, axis=-1)
        a_out = gmlp_mixer(jax.nn.gelu(u, approximate=False), jax.nn.gelu(v, approximate=False),
                           gmlp_v_gain[l], gmlp_w_s[l], gmlp_b_s[l])
        b_out = hgrn2_mixer(q, f, i, g, lb_all[l], hgrn_out_gain[l])
        h = h + jnp.concatenate([a_out, b_out], axis=-1) @ w_out[l]
        n2 = rms_norm(h, norm2_gain[l])
        h = h + hier_moe(n2, w_group_router[l], b_group_router[l], w_expert_router[l],
                         b_expert_router[l], w_gate[l], w_up[l], w_down[l])
    return rms_norm(h, final_gain)
```

```python
import functools

import jax
import jax.numpy as jnp
from jax import lax
from jax.experimental import pallas as pl
from jax.experimental.pallas import tpu as pltpu

D_MODEL = 1024
BATCH = 2
SEQ = 8192
N_TOK = BATCH * SEQ
CHUNK = 64
EPS = 1e-6
D_A = 512
HEADS = 4
HEAD_DIM = 128
GMLP_BLOCK = 128
D_B = 512
D_IN = 3072
N_GROUPS = 4
EXPERTS_PER_GROUP = 8
N_EXPERTS = 32
TOP_K = 2
D_EXPERT = 512

LANES = 128
ROUTER_COLS = LANES
EXPERT_COL0 = N_GROUPS
META_COLS = 8

TS = 256
CHUNKS_PER_STEP = TS // CHUNK
MOE_BLOCK = 256
N_BLOCKS = N_TOK * TOP_K // MOE_BLOCK + N_EXPERTS
SORTED_ROWS = N_BLOCKS * MOE_BLOCK
MIN_USED_BLOCKS = N_TOK * TOP_K // MOE_BLOCK
N_ZERO_BLOCKS = N_EXPERTS + (N_BLOCKS - MIN_USED_BLOCKS)
TD = 512

F32 = jnp.float32
BF16 = jnp.bfloat16
NT_DIMS = (((1,), (1,)), ((), ()))
TN_DIMS = (((0,), (0,)), ((), ()))


def _sigmoid(x):
    return 1.0 / (1.0 + jnp.exp(-x))


def _gelu(x):
    return 0.5 * x * (1.0 + lax.erf(x * (2.0 ** -0.5)))


def _rms(x, gain):
    return x * lax.rsqrt(jnp.mean(x * x, axis=-1, keepdims=True) + EPS) * gain


def _split_bf16(x):
    hi = x.astype(BF16)
    lo = (x - hi.astype(F32)).astype(BF16)
    return hi, lo


def _mixer_kernel(x_ref, g1_ref, win_ref, vg_ref, ws_ref, bst_ref, lbraw_ref, og_ref,
                  wout_ref, g2_ref, wrh_ref, wrl_ref, br_ref,
                  h1_ref, n2_ref, mi_ref, mf_ref, cnt_ref,
                  st_ref, carry_ref, wm_ref, mix_ref):
    i = pl.program_id(0)

    @pl.when(i == 0)
    def _init():
        carry_ref[...] = jnp.zeros_like(carry_ref)
        r = lax.broadcasted_iota(jnp.int32, (GMLP_BLOCK, GMLP_BLOCK), 0)
        c = lax.broadcasted_iota(jnp.int32, (GMLP_BLOCK, GMLP_BLOCK), 1)
        keep = (c // CHUNK) <= (r // CHUNK)
        wm_ref[...] = jnp.zeros_like(wm_ref)
        for g in range(HEADS):
            w = jnp.where(keep, ws_ref[g], 0.0).astype(BF16)
            for p in range(TS // GMLP_BLOCK):
                wm_ref[g, p * GMLP_BLOCK:(p + 1) * GMLP_BLOCK,
                       p * GMLP_BLOCK:(p + 1) * GMLP_BLOCK] = w

    @pl.when(i % (SEQ // TS) == 0)
    def _reset_state():
        st_ref[...] = jnp.zeros_like(st_ref)

    x = x_ref[...]
    n1 = _rms(x, g1_ref[...]).astype(BF16)

    gu = _gelu(jnp.dot(n1, win_ref[:, 0:D_A], preferred_element_type=F32))
    gv = _gelu(jnp.dot(n1, win_ref[:, D_A:2 * D_A], preferred_element_type=F32))
    for g in range(HEADS):
        sl = slice(g * HEAD_DIM, (g + 1) * HEAD_DIM)
        vh = _rms(gv[:, sl], vg_ref[:, sl]).astype(BF16)
        mixed = jnp.dot(wm_ref[g], vh, preferred_element_type=F32) + bst_ref[:, g:g + 1]
        mix_ref[:, sl] = (gu[:, sl] * mixed).astype(BF16)

    o2 = 2 * D_A
    qr = jnp.dot(n1, win_ref[:, o2:o2 + D_B], preferred_element_type=F32)
    fr = jnp.dot(n1, win_ref[:, o2 + D_B:o2 + 2 * D_B], preferred_element_type=F32)
    vv = jnp.dot(n1, win_ref[:, o2 + 2 * D_B:o2 + 3 * D_B], preferred_element_type=F32).astype(BF16)
    gr = jnp.dot(n1, win_ref[:, o2 + 3 * D_B:o2 + 4 * D_B], preferred_element_type=F32)

    lbr = lbraw_ref[...]
    lbm = jnp.max(lbr, axis=0, keepdims=True)
    lbe = jnp.exp(lbr - lbm)
    lb = lbe[0:1, :] / jnp.sum(lbe, axis=0, keepdims=True)

    q = qr * _sigmoid(qr)
    f = lb + (1.0 - lb) * _sigmoid(fr)
    k = 1.0 - f
    lf = jnp.log(f)

    row = lax.broadcasted_iota(jnp.int32, (TS, TS), 0)
    col = lax.broadcasted_iota(jnp.int32, (TS, TS), 1)
    causal = (row >= col) & ((row // CHUNK) == (col // CHUNK))
    tri = jnp.where(causal, 1.0, 0.0).astype(BF16)
    lf_hi, lf_lo = _split_bf16(lf)
    b = (jnp.dot(tri, lf_hi, preferred_element_type=F32)
         + jnp.dot(tri, lf_lo, preferred_element_type=F32))

    def chunk_rows(r):
        return jnp.concatenate(
            [jnp.broadcast_to(b[c * CHUNK + r:c * CHUNK + r + 1, :], (CHUNK, D_B))
             for c in range(CHUNKS_PER_STEP)], axis=0)

    bref = chunk_rows(CHUNK // 2 - 1)
    blast = chunk_rows(CHUNK - 1)
    qe = (q * jnp.exp(b - bref)).astype(BF16)
    ke = (k * jnp.exp(bref - b)).astype(BF16)
    kd = (k * jnp.exp(blast - b)).astype(BF16)
    qb = (q * jnp.exp(b)).astype(BF16)

    for h in range(HEADS):
        sl = slice(h * HEAD_DIM, (h + 1) * HEAD_DIM)
        sc = lax.dot_general(qe[:, sl], ke[:, sl], NT_DIMS, preferred_element_type=F32)
        sc = jnp.where(causal, sc, 0.0).astype(BF16)
        o = jnp.dot(sc, vv[:, sl], preferred_element_type=F32)
        st = st_ref[h]
        inter = []
        for c in range(CHUNKS_PER_STEP):
            rows = slice(c * CHUNK, (c + 1) * CHUNK)
            inter.append(lax.dot_general(qb[rows, sl], st.astype(BF16), NT_DIMS,
                                         preferred_element_type=F32))
            upd = lax.dot_general(vv[rows, sl], kd[rows, sl], TN_DIMS,
                                  preferred_element_type=F32)
            decay = jnp.exp(b[(c + 1) * CHUNK - 1:(c + 1) * CHUNK, sl])
            st = st * decay + upd
        st_ref[h] = st
        o = o + jnp.concatenate(inter, axis=0)
        o = _rms(o, og_ref[:, sl])
        g_h = gr[:, sl]
        mix_ref[:, D_A + h * HEAD_DIM:D_A + (h + 1) * HEAD_DIM] = (
            o * (g_h * _sigmoid(g_h))).astype(BF16)

    h1 = x + jnp.dot(mix_ref[...], wout_ref[...], preferred_element_type=F32)
    h1_ref[...] = h1
    n2 = _rms(h1, g2_ref[...])
    n2_ref[...] = n2

    n2_hi, n2_lo = _split_bf16(n2)
    logits = (jnp.dot(n2_hi, wrh_ref[...], preferred_element_type=F32)
              + jnp.dot(n2_lo, wrh_ref[...], preferred_element_type=F32)
              + jnp.dot(n2_hi, wrl_ref[...], preferred_element_type=F32)
              + br_ref[...])
    lane = lax.broadcasted_iota(jnp.int32, (TS, ROUTER_COLS), 1)
    neg = -jnp.inf
    gl = jnp.where(lane < N_GROUPS, logits, neg)
    gmax = jnp.max(gl, axis=1, keepdims=True)
    g_idx = jnp.min(jnp.where(gl == gmax, lane, ROUTER_COLS), axis=1, keepdims=True)
    g_prob = 1.0 / jnp.sum(jnp.exp(gl - gmax), axis=1, keepdims=True)
    e_lo = EXPERT_COL0 + g_idx * EXPERTS_PER_GROUP
    el = jnp.where((lane >= e_lo) & (lane < e_lo + EXPERTS_PER_GROUP), logits, neg)
    m1 = jnp.max(el, axis=1, keepdims=True)
    i1 = jnp.min(jnp.where(el == m1, lane, ROUTER_COLS), axis=1, keepdims=True)
    el2 = jnp.where(lane == i1, neg, el)
    m2 = jnp.max(el2, axis=1, keepdims=True)
    i2 = jnp.min(jnp.where(el2 == m2, lane, ROUTER_COLS), axis=1, keepdims=True)
    e21 = jnp.exp(m2 - m1)
    w1 = g_prob / (1.0 + e21)
    w2 = g_prob * e21 / (1.0 + e21)

    hit1 = lane == i1
    hit2 = lane == i2
    onehot = jnp.where(hit1 | hit2, 1.0, 0.0)
    before = jnp.where(row > col, 1.0, 0.0).astype(BF16)
    prior = jnp.dot(before, onehot.astype(BF16), preferred_element_type=F32) + carry_ref[...]
    r1 = jnp.sum(jnp.where(hit1, prior, 0.0), axis=1, keepdims=True)
    r2 = jnp.sum(jnp.where(hit2, prior, 0.0), axis=1, keepdims=True)
    carry = carry_ref[...] + jnp.sum(onehot, axis=0, keepdims=True)
    carry_ref[...] = carry
    cnt_ref[...] = carry.astype(jnp.int32)

    mlane = lax.broadcasted_iota(jnp.int32, (TS, META_COLS), 1)
    e1 = i1 - EXPERT_COL0
    e2 = i2 - EXPERT_COL0
    mi_ref[...] = jnp.where(mlane == 0, e1,
                            jnp.where(mlane == 1, e2,
                                      jnp.where(mlane == 2, r1.astype(jnp.int32),
                                                jnp.where(mlane == 3, r2.astype(jnp.int32), 0))))
    mf_ref[...] = jnp.where(mlane == 0, w1, jnp.where(mlane == 1, w2, 0.0))


def _mixer(x2, g1, win, vg, ws, bst, lbraw, og, wout, g2, wrh, wrl, br):
    n_steps = N_TOK // TS
    const2 = lambda i: (0, 0)
    tok = lambda i: (i, 0)
    return pl.pallas_call(
        _mixer_kernel,
        grid=(n_steps,),
        in_specs=[
            pl.BlockSpec((TS, D_MODEL), tok),
            pl.BlockSpec((1, D_MODEL), const2),
            pl.BlockSpec((D_MODEL, D_IN), const2),
            pl.BlockSpec((1, D_A), const2),
            pl.BlockSpec((HEADS, GMLP_BLOCK, GMLP_BLOCK), lambda i: (0, 0, 0)),
            pl.BlockSpec((TS, HEADS), const2),
            pl.BlockSpec((2, D_B), const2),
            pl.BlockSpec((1, D_B), const2),
            pl.BlockSpec((D_MODEL, D_MODEL), const2),
            pl.BlockSpec((1, D_MODEL), const2),
            pl.BlockSpec((D_MODEL, ROUTER_COLS), const2),
            pl.BlockSpec((D_MODEL, ROUTER_COLS), const2),
            pl.BlockSpec((1, ROUTER_COLS), const2),
        ],
        out_specs=[
            pl.BlockSpec((TS, D_MODEL), tok),
            pl.BlockSpec((TS, D_MODEL), tok),
            pl.BlockSpec((TS, META_COLS), tok),
            pl.BlockSpec((TS, META_COLS), tok),
            pl.BlockSpec((1, ROUTER_COLS), const2),
        ],
        out_shape=[
            jax.ShapeDtypeStruct((N_TOK, D_MODEL), F32),
            jax.ShapeDtypeStruct((N_TOK, D_MODEL), F32),
            jax.ShapeDtypeStruct((N_TOK, META_COLS), jnp.int32),
            jax.ShapeDtypeStruct((N_TOK, META_COLS), F32),
            jax.ShapeDtypeStruct((1, ROUTER_COLS), jnp.int32),
        ],
        scratch_shapes=[
            pltpu.VMEM((HEADS, HEAD_DIM, HEAD_DIM), F32),
            pltpu.VMEM((1, ROUTER_COLS), F32),
            pltpu.VMEM((HEADS, TS, TS), BF16),
            pltpu.VMEM((TS, D_MODEL), BF16),
        ],
        compiler_params=pltpu.CompilerParams(
            dimension_semantics=("arbitrary",),
            vmem_limit_bytes=56 * 1024 * 1024,
        ),
        name="mixer",
    )(x2, g1, win, vg, ws, bst, lbraw, og, wout, g2, wrh, wrl, br)


def _dispatch_kernel(zblk_ref, n2_ref, dest_hbm, xs_hbm, dest_smem, zeros_ref, zsem, dsem, rsem):
    i = pl.program_id(0)

    def zero_copy(e):
        start = pl.multiple_of(zblk_ref[e], MOE_BLOCK)
        return pltpu.make_async_copy(zeros_ref, xs_hbm.at[pl.ds(start, MOE_BLOCK)], zsem)

    @pl.when(i == 0)
    def _zero_fill():
        zeros_ref[...] = jnp.zeros_like(zeros_ref)
        for e in range(N_ZERO_BLOCKS):
            @pl.when(zblk_ref[e] >= 0)
            def _():
                zero_copy(e).start()
        for e in range(N_ZERO_BLOCKS):
            @pl.when(zblk_ref[e] >= 0)
            def _():
                zero_copy(e).wait()

    dcopy = pltpu.make_async_copy(dest_hbm.at[i], dest_smem, dsem)
    dcopy.start()
    dcopy.wait()

    def row_copy(t, d):
        return pltpu.make_async_copy(n2_ref.at[pl.ds(t, 1)], xs_hbm.at[pl.ds(d, 1)], rsem)

    def body(t, carry):
        row_copy(t, dest_smem[2 * t]).start()
        row_copy(t, dest_smem[2 * t + 1]).start()
        return carry

    lax.fori_loop(0, TD, body, 0)

    def drain(t, carry):
        row_copy(t, 0).wait()
        row_copy(t, 0).wait()
        return carry

    lax.fori_loop(0, TD, drain, 0)


def _dispatch(zblk, n2, dest2d):
    n_steps = N_TOK // TD
    return pl.pallas_call(
        _dispatch_kernel,
        grid_spec=pltpu.PrefetchScalarGridSpec(
            num_scalar_prefetch=1,
            grid=(n_steps,),
            in_specs=[
                pl.BlockSpec((TD, D_MODEL), lambda i, z: (i, 0)),
                pl.BlockSpec(memory_space=pl.ANY),
            ],
            out_specs=pl.BlockSpec(memory_space=pl.ANY),
            scratch_shapes=[
                pltpu.SMEM((TOP_K * TD,), jnp.int32),
                pltpu.VMEM((MOE_BLOCK, D_MODEL), F32),
                pltpu.SemaphoreType.DMA,
                pltpu.SemaphoreType.DMA,
                pltpu.SemaphoreType.DMA,
            ],
        ),
        out_shape=jax.ShapeDtypeStruct((SORTED_ROWS, D_MODEL), F32),
        compiler_params=pltpu.CompilerParams(dimension_semantics=("arbitrary",)),
        name="dispatch",
    )(zblk, n2, dest2d)


def _expert_kernel(bexp_ref, nused_ref, xs_ref, wg_ref, wu_ref, wd_ref, ys_ref,
                   wg_bf, wu_bf, wd_bf):
    i = pl.program_id(0)
    prev = bexp_ref[jnp.maximum(i - 1, 0)]
    changed = (i == 0) | (bexp_ref[i] != prev)

    @pl.when(changed)
    def _cast_weights():
        wg_bf[...] = wg_ref[0].astype(BF16)
        wu_bf[...] = wu_ref[0].astype(BF16)
        wd_bf[...] = wd_ref[0].astype(BF16)

    @pl.when(i < nused_ref[0])
    def _compute():
        xb = xs_ref[...].astype(BF16)
        hg = jnp.dot(xb, wg_bf[...], preferred_element_type=F32)
        hu = jnp.dot(xb, wu_bf[...], preferred_element_type=F32)
        hh = (hg * _sigmoid(hg) * hu).astype(BF16)
        ys_ref[...] = jnp.dot(hh, wd_bf[...], preferred_element_type=F32)

    @pl.when(i >= nused_ref[0])
    def _unused_block():
        ys_ref[...] = jnp.zeros_like(ys_ref)


def _experts(bexp, nused, xs, w_gate, w_up, w_down):
    def row_blk(i, bexp, nused):
        return (jnp.minimum(i, nused[0] - 1), 0)

    def out_blk(i, bexp, nused):
        return (i, 0)

    def w_blk(i, bexp, nused):
        return (bexp[i], 0, 0)

    return pl.pallas_call(
        _expert_kernel,
        grid_spec=pltpu.PrefetchScalarGridSpec(
            num_scalar_prefetch=2,
            grid=(N_BLOCKS,),
            in_specs=[
                pl.BlockSpec((MOE_BLOCK, D_MODEL), row_blk),
                pl.BlockSpec((1, D_MODEL, D_EXPERT), w_blk),
                pl.BlockSpec((1, D_MODEL, D_EXPERT), w_blk),
                pl.BlockSpec((1, D_EXPERT, D_MODEL), w_blk),
            ],
            out_specs=pl.BlockSpec((MOE_BLOCK, D_MODEL), out_blk),
            scratch_shapes=[
                pltpu.VMEM((D_MODEL, D_EXPERT), BF16),
                pltpu.VMEM((D_MODEL, D_EXPERT), BF16),
                pltpu.VMEM((D_EXPERT, D_MODEL), BF16),
            ],
        ),
        out_shape=jax.ShapeDtypeStruct((SORTED_ROWS, D_MODEL), F32),
        compiler_params=pltpu.CompilerParams(
            dimension_semantics=("arbitrary",),
            vmem_limit_bytes=48 * 1024 * 1024,
        ),
        name="experts",
    )(bexp, nused, xs, w_gate, w_up, w_down)


def _combine_kernel(h1_ref, mf_ref, fg_ref, dest_hbm, ys_hbm, out_ref,
                    dest_smem, ybuf, dsem, rsem):
    i = pl.program_id(0)
    dcopy = pltpu.make_async_copy(dest_hbm.at[i], dest_smem, dsem)
    dcopy.start()
    dcopy.wait()

    def row_copy(t, kk, d):
        return pltpu.make_async_copy(ys_hbm.at[pl.ds(d, 1)], ybuf.at[kk, pl.ds(t, 1)], rsem)

    def body(t, carry):
        row_copy(t, 0, dest_smem[2 * t]).start()
        row_copy(t, 1, dest_smem[2 * t + 1]).start()
        return carry

    lax.fori_loop(0, TD, body, 0)

    def drain(t, carry):
        row_copy(t, 0, 0).wait()
        row_copy(t, 1, 0).wait()
        return carry

    lax.fori_loop(0, TD, drain, 0)

    w = mf_ref[...]
    h = h1_ref[...] + ybuf[0] * w[:, 0:1] + ybuf[1] * w[:, 1:2]
    out_ref[...] = _rms(h, fg_ref[...])


def _combine(h1, mf, fg, dest2d, ys):
    n_steps = N_TOK // TD
    tok = lambda i: (i, 0)
    return pl.pallas_call(
        _combine_kernel,
        grid=(n_steps,),
        in_specs=[
            pl.BlockSpec((TD, D_MODEL), tok),
            pl.BlockSpec((TD, META_COLS), tok),
            pl.BlockSpec((1, D_MODEL), lambda i: (0, 0)),
            pl.BlockSpec(memory_space=pl.ANY),
            pl.BlockSpec(memory_space=pl.ANY),
        ],
        out_specs=pl.BlockSpec((TD, D_MODEL), tok),
        out_shape=jax.ShapeDtypeStruct((N_TOK, D_MODEL), F32),
        scratch_shapes=[
            pltpu.SMEM((TOP_K * TD,), jnp.int32),
            pltpu.VMEM((TOP_K, TD, D_MODEL), F32),
            pltpu.SemaphoreType.DMA,
            pltpu.SemaphoreType.DMA,
        ],
        compiler_params=pltpu.CompilerParams(dimension_semantics=("arbitrary",)),
        name="combine",
    )(h1, mf, fg, dest2d, ys)


def kernel(x, norm1_gain, w_in, gmlp_v_gain, gmlp_w_s, gmlp_b_s, hgrn_lower_bounds,
           hgrn_out_gain, w_out, norm2_gain, w_group_router, b_group_router,
           w_expert_router, b_expert_router, w_gate, w_up, w_down, final_gain):
    l = 0
    x2 = x.reshape(N_TOK, D_MODEL)
    bst = jnp.tile(jnp.transpose(gmlp_b_s[l]), (TS // GMLP_BLOCK, 1))
    w_router = jnp.concatenate([w_group_router[l], w_expert_router[l]], axis=1)
    w_router = jnp.pad(w_router, ((0, 0), (0, ROUTER_COLS - w_router.shape[1])))
    wr_hi = w_router.astype(BF16)
    wr_lo = (w_router - wr_hi.astype(F32)).astype(BF16)
    b_router = jnp.concatenate([b_group_router[l], b_expert_router[l]])
    b_router = jnp.pad(b_router, (0, ROUTER_COLS - b_router.shape[0])).reshape(1, ROUTER_COLS)

    h1, n2, meta_i, meta_f, counts = _mixer(
        x2, norm1_gain[l].reshape(1, D_MODEL), w_in[l].astype(BF16),
        gmlp_v_gain[l].reshape(1, D_A), gmlp_w_s[l], bst,
        hgrn_lower_bounds, hgrn_out_gain[l].reshape(1, D_B), w_out[l].astype(BF16),
        norm2_gain[l].reshape(1, D_MODEL), wr_hi, wr_lo, b_router)

    cnt = counts[0, EXPERT_COL0:EXPERT_COL0 + N_EXPERTS]
    padded = ((cnt + MOE_BLOCK - 1) // MOE_BLOCK) * MOE_BLOCK
    pend = jnp.cumsum(padded)
    pstart = pend - padded
    eid = meta_i[:, 0:TOP_K]
    rank = meta_i[:, TOP_K:2 * TOP_K]
    base = jnp.sum(jnp.where(eid[:, :, None] == jnp.arange(N_EXPERTS)[None, None, :],
                             pstart[None, None, :], 0), axis=-1)
    dest2d = (base + rank).astype(jnp.int32).reshape(N_TOK // TD, TOP_K * TD)
    blk_start = jnp.arange(N_BLOCKS, dtype=jnp.int32) * MOE_BLOCK
    tail_start = blk_start[MIN_USED_BLOCKS:]
    zblk = jnp.concatenate([jnp.where(padded > 0, pend - MOE_BLOCK, -1),
                            jnp.where(tail_start >= pend[-1], tail_start, -1)]).astype(jnp.int32)
    bexp = jnp.clip(jnp.sum(blk_start[:, None] >= pend[None, :], axis=1), 0,
                    N_EXPERTS - 1).astype(jnp.int32)
    nused = (pend[-1:] // MOE_BLOCK).astype(jnp.int32)

    xs = _dispatch(zblk, n2, dest2d)
    ys = _experts(bexp, nused, xs, w_gate[l], w_up[l], w_down[l])
    out = _combine(h1, meta_f, final_gain.reshape(1, D_MODEL), dest2d, ys)
    return out.reshape(BATCH, SEQ, D_MODEL)
```

```python
import functools

import jax
import jax.numpy as jnp
from jax import lax
from jax.experimental import pallas as pl
from jax.experimental.pallas import tpu as pltpu

D_MODEL = 1024
BATCH = 2
SEQ = 8192
N_TOK = BATCH * SEQ
CHUNK = 64
EPS = 1e-6
D_A = 512
HEADS = 4
HEAD_DIM = 128
GMLP_BLOCK = 128
D_B = 512
D_IN = 3072
N_GROUPS = 4
EXPERTS_PER_GROUP = 8
N_EXPERTS = 32
TOP_K = 2
D_EXPERT = 512

LANES = 128
ROUTER_COLS = LANES
EXPERT_COL0 = N_GROUPS
META_COLS = 8

TS = 256
CHUNKS_PER_STEP = TS // CHUNK
MOE_BLOCK = 256
N_BLOCKS = N_TOK * TOP_K // MOE_BLOCK + N_EXPERTS
SORTED_ROWS = N_BLOCKS * MOE_BLOCK
N_ASSIGN = N_TOK * TOP_K
DUMMY_ROWS = 2 * MOE_BLOCK
TD = 512

F32 = jnp.float32
BF16 = jnp.bfloat16
NT_DIMS = (((1,), (1,)), ((), ()))
TN_DIMS = (((0,), (0,)), ((), ()))


def _sigmoid(x):
    return 1.0 / (1.0 + jnp.exp(-x))


def _gelu(x):
    return 0.5 * x * (1.0 + lax.erf(x * (2.0 ** -0.5)))


def _rms(x, gain):
    return x * lax.rsqrt(jnp.mean(x * x, axis=-1, keepdims=True) + EPS) * gain


def _split_bf16(x):
    hi = x.astype(BF16)
    lo = (x - hi.astype(F32)).astype(BF16)
    return hi, lo


def _mixer_kernel(x_ref, g1_ref, win_ref, vg_ref, ws_ref, bst_ref, lbraw_ref, og_ref,
                  wout_ref, g2_ref, wrh_ref, wrl_ref, br_ref,
                  h1_ref, n2_ref, mi_ref, mf_ref, cnt_ref,
                  st_ref, carry_ref, wm_ref, mix_ref):
    i = pl.program_id(0)

    @pl.when(i == 0)
    def _init():
        carry_ref[...] = jnp.zeros_like(carry_ref)
        r = lax.broadcasted_iota(jnp.int32, (GMLP_BLOCK, GMLP_BLOCK), 0)
        c = lax.broadcasted_iota(jnp.int32, (GMLP_BLOCK, GMLP_BLOCK), 1)
        keep = (c // CHUNK) <= (r // CHUNK)
        wm_ref[...] = jnp.zeros_like(wm_ref)
        for g in range(HEADS):
            w = jnp.where(keep, ws_ref[g], 0.0).astype(BF16)
            for p in range(TS // GMLP_BLOCK):
                wm_ref[g, p * GMLP_BLOCK:(p + 1) * GMLP_BLOCK,
                       p * GMLP_BLOCK:(p + 1) * GMLP_BLOCK] = w

    @pl.when(i % (SEQ // TS) == 0)
    def _reset_state():
        st_ref[...] = jnp.zeros_like(st_ref)

    x = x_ref[...]
    n1 = _rms(x, g1_ref[...]).astype(BF16)

    gu = _gelu(jnp.dot(n1, win_ref[:, 0:D_A], preferred_element_type=F32))
    gv = _gelu(jnp.dot(n1, win_ref[:, D_A:2 * D_A], preferred_element_type=F32))
    for g in range(HEADS):
        sl = slice(g * HEAD_DIM, (g + 1) * HEAD_DIM)
        vh = _rms(gv[:, sl], vg_ref[:, sl]).astype(BF16)
        mixed = jnp.dot(wm_ref[g], vh, preferred_element_type=F32) + bst_ref[:, g:g + 1]
        mix_ref[:, sl] = (gu[:, sl] * mixed).astype(BF16)

    o2 = 2 * D_A
    qr = jnp.dot(n1, win_ref[:, o2:o2 + D_B], preferred_element_type=F32)
    fr = jnp.dot(n1, win_ref[:, o2 + D_B:o2 + 2 * D_B], preferred_element_type=F32)
    vv = jnp.dot(n1, win_ref[:, o2 + 2 * D_B:o2 + 3 * D_B], preferred_element_type=F32).astype(BF16)
    gr = jnp.dot(n1, win_ref[:, o2 + 3 * D_B:o2 + 4 * D_B], preferred_element_type=F32)

    lbr = lbraw_ref[...]
    lbm = jnp.max(lbr, axis=0, keepdims=True)
    lbe = jnp.exp(lbr - lbm)
    lb = lbe[0:1, :] / jnp.sum(lbe, axis=0, keepdims=True)

    q = qr * _sigmoid(qr)
    f = lb + (1.0 - lb) * _sigmoid(fr)
    k = 1.0 - f
    lf = jnp.log(f)

    row = lax.broadcasted_iota(jnp.int32, (TS, TS), 0)
    col = lax.broadcasted_iota(jnp.int32, (TS, TS), 1)
    causal = (row >= col) & ((row // CHUNK) == (col // CHUNK))
    tri = jnp.where(causal, 1.0, 0.0).astype(BF16)
    lf_hi, lf_lo = _split_bf16(lf)
    b = (jnp.dot(tri, lf_hi, preferred_element_type=F32)
         + jnp.dot(tri, lf_lo, preferred_element_type=F32))

    def chunk_rows(r):
        return jnp.concatenate(
            [jnp.broadcast_to(b[c * CHUNK + r:c * CHUNK + r + 1, :], (CHUNK, D_B))
             for c in range(CHUNKS_PER_STEP)], axis=0)

    bref = chunk_rows(CHUNK // 2 - 1)
    blast = chunk_rows(CHUNK - 1)
    qe = (q * jnp.exp(b - bref)).astype(BF16)
    ke = (k * jnp.exp(bref - b)).astype(BF16)
    kd = (k * jnp.exp(blast - b)).astype(BF16)
    qb = (q * jnp.exp(b)).astype(BF16)

    for h in range(HEADS):
        sl = slice(h * HEAD_DIM, (h + 1) * HEAD_DIM)
        sc = lax.dot_general(qe[:, sl], ke[:, sl], NT_DIMS, preferred_element_type=F32)
        sc = jnp.where(causal, sc, 0.0).astype(BF16)
        o = jnp.dot(sc, vv[:, sl], preferred_element_type=F32)
        st = st_ref[h]
        inter = []
        for c in range(CHUNKS_PER_STEP):
            rows = slice(c * CHUNK, (c + 1) * CHUNK)
            inter.append(lax.dot_general(qb[rows, sl], st.astype(BF16), NT_DIMS,
                                         preferred_element_type=F32))
            upd = lax.dot_general(vv[rows, sl], kd[rows, sl], TN_DIMS,
                                  preferred_element_type=F32)
            decay = jnp.exp(b[(c + 1) * CHUNK - 1:(c + 1) * CHUNK, sl])
            st = st * decay + upd
        st_ref[h] = st
        o = o + jnp.concatenate(inter, axis=0)
        o = _rms(o, og_ref[:, sl])
        g_h = gr[:, sl]
        mix_ref[:, D_A + h * HEAD_DIM:D_A + (h + 1) * HEAD_DIM] = (
            o * (g_h * _sigmoid(g_h))).astype(BF16)

    h1 = x + jnp.dot(mix_ref[...], wout_ref[...], preferred_element_type=F32)
    h1_ref[...] = h1
    n2 = _rms(h1, g2_ref[...])
    n2_ref[...] = n2

    n2_hi, n2_lo = _split_bf16(n2)
    logits = (jnp.dot(n2_hi, wrh_ref[...], preferred_element_type=F32)
              + jnp.dot(n2_lo, wrh_ref[...], preferred_element_type=F32)
              + jnp.dot(n2_hi, wrl_ref[...], preferred_element_type=F32)
              + br_ref[...])
    lane = lax.broadcasted_iota(jnp.int32, (TS, ROUTER_COLS), 1)
    neg = -jnp.inf
    gl = jnp.where(lane < N_GROUPS, logits, neg)
    gmax = jnp.max(gl, axis=1, keepdims=True)
    g_idx = jnp.min(jnp.where(gl == gmax, lane, ROUTER_COLS), axis=1, keepdims=True)
    g_prob = 1.0 / jnp.sum(jnp.exp(gl - gmax), axis=1, keepdims=True)
    e_lo = EXPERT_COL0 + g_idx * EXPERTS_PER_GROUP
    el = jnp.where((lane >= e_lo) & (lane < e_lo + EXPERTS_PER_GROUP), logits, neg)
    m1 = jnp.max(el, axis=1, keepdims=True)
    i1 = jnp.min(jnp.where(el == m1, lane, ROUTER_COLS), axis=1, keepdims=True)
    el2 = jnp.where(lane == i1, neg, el)
    m2 = jnp.max(el2, axis=1, keepdims=True)
    i2 = jnp.min(jnp.where(el2 == m2, lane, ROUTER_COLS), axis=1, keepdims=True)
    e21 = jnp.exp(m2 - m1)
    w1 = g_prob / (1.0 + e21)
    w2 = g_prob * e21 / (1.0 + e21)

    hit1 = lane == i1
    hit2 = lane == i2
    onehot = jnp.where(hit1 | hit2, 1.0, 0.0)
    before = jnp.where(row > col, 1.0, 0.0).astype(BF16)
    prior = jnp.dot(before, onehot.astype(BF16), preferred_element_type=F32) + carry_ref[...]
    r1 = jnp.sum(jnp.where(hit1, prior, 0.0), axis=1, keepdims=True)
    r2 = jnp.sum(jnp.where(hit2, prior, 0.0), axis=1, keepdims=True)
    carry = carry_ref[...] + jnp.sum(onehot, axis=0, keepdims=True)
    carry_ref[...] = carry
    cnt_ref[...] = carry.astype(jnp.int32)

    mlane = lax.broadcasted_iota(jnp.int32, (TS, META_COLS), 1)
    e1 = i1 - EXPERT_COL0
    e2 = i2 - EXPERT_COL0
    mi_ref[...] = jnp.where(mlane == 0, e1,
                            jnp.where(mlane == 1, e2,
                                      jnp.where(mlane == 2, r1.astype(jnp.int32),
                                                jnp.where(mlane == 3, r2.astype(jnp.int32), 0))))
    mf_ref[...] = jnp.where(mlane == 0, w1, jnp.where(mlane == 1, w2, 0.0))


def _mixer(x2, g1, win, vg, ws, bst, lbraw, og, wout, g2, wrh, wrl, br):
    n_steps = N_TOK // TS
    const2 = lambda i: (0, 0)
    tok = lambda i: (i, 0)
    return pl.pallas_call(
        _mixer_kernel,
        grid=(n_steps,),
        in_specs=[
            pl.BlockSpec((TS, D_MODEL), tok),
            pl.BlockSpec((1, D_MODEL), const2),
            pl.BlockSpec((D_MODEL, D_IN), const2),
            pl.BlockSpec((1, D_A), const2),
            pl.BlockSpec((HEADS, GMLP_BLOCK, GMLP_BLOCK), lambda i: (0, 0, 0)),
            pl.BlockSpec((TS, HEADS), const2),
            pl.BlockSpec((2, D_B), const2),
            pl.BlockSpec((1, D_B), const2),
            pl.BlockSpec((D_MODEL, D_MODEL), const2),
            pl.BlockSpec((1, D_MODEL), const2),
            pl.BlockSpec((D_MODEL, ROUTER_COLS), const2),
            pl.BlockSpec((D_MODEL, ROUTER_COLS), const2),
            pl.BlockSpec((1, ROUTER_COLS), const2),
        ],
        out_specs=[
            pl.BlockSpec((TS, D_MODEL), tok),
            pl.BlockSpec((TS, D_MODEL), tok),
            pl.BlockSpec((TS, META_COLS), tok),
            pl.BlockSpec((TS, META_COLS), tok),
            pl.BlockSpec((1, ROUTER_COLS), const2),
        ],
        out_shape=[
            jax.ShapeDtypeStruct((N_TOK, D_MODEL), F32),
            jax.ShapeDtypeStruct((N_TOK, D_MODEL), F32),
            jax.ShapeDtypeStruct((N_TOK, META_COLS), jnp.int32),
            jax.ShapeDtypeStruct((N_TOK, META_COLS), F32),
            jax.ShapeDtypeStruct((1, ROUTER_COLS), jnp.int32),
        ],
        scratch_shapes=[
            pltpu.VMEM((HEADS, HEAD_DIM, HEAD_DIM), F32),
            pltpu.VMEM((1, ROUTER_COLS), F32),
            pltpu.VMEM((HEADS, TS, TS), BF16),
            pltpu.VMEM((TS, D_MODEL), BF16),
        ],
        compiler_params=pltpu.CompilerParams(
            dimension_semantics=("arbitrary",),
            vmem_limit_bytes=56 * 1024 * 1024,
        ),
        name="mixer",
    )(x2, g1, win, vg, ws, bst, lbraw, og, wout, g2, wrh, wrl, br)


def _expert_kernel(bexp_ref, nused_ref, src_hbm, dst_hbm, n2_hbm, wg_ref, wu_ref, wd_ref, y_hbm,
                   src_smem, dst_smem, xbuf, obuf, wg_bf, wu_bf, wd_bf,
                   src_sem, dst_sem, gsem, ssem):
    i = pl.program_id(0)
    nused = nused_ref[0]
    last = nused - 1
    slot = i % 2
    nxt = 1 - slot

    def src_copy(blk, s):
        return pltpu.make_async_copy(src_hbm.at[blk], src_smem.at[pl.ds(s, 1)], src_sem.at[s])

    def dst_copy(blk, s):
        return pltpu.make_async_copy(dst_hbm.at[blk], dst_smem.at[pl.ds(s, 1)], dst_sem.at[s])

    def start_gather(s):
        for j in range(MOE_BLOCK):
            pltpu.make_async_copy(n2_hbm.at[pl.ds(src_smem[s, j], 1)],
                                  xbuf.at[s, pl.ds(j, 1)], gsem.at[s]).start()

    def wait_gather(s):
        pltpu.make_async_copy(n2_hbm.at[pl.ds(0, MOE_BLOCK)], xbuf.at[s], gsem.at[s]).wait()

    def start_scatter(s):
        for j in range(MOE_BLOCK):
            pltpu.make_async_copy(obuf.at[s, pl.ds(j, 1)],
                                  y_hbm.at[pl.ds(dst_smem[s, j], 1)], ssem.at[s]).start()

    def wait_scatter(s):
        pltpu.make_async_copy(obuf.at[s], y_hbm.at[pl.ds(0, MOE_BLOCK)], ssem.at[s]).wait()

    @pl.when(i == 0)
    def _prologue():
        obuf[...] = jnp.zeros_like(obuf)
        pltpu.make_async_copy(obuf.at[0], y_hbm.at[pl.ds(N_ASSIGN, MOE_BLOCK)],
                              ssem.at[0]).start()
        src_copy(0, 0).start()
        src_copy(0, 0).wait()
        start_gather(0)
        src_copy(jnp.minimum(1, last), 1).start()
        dst_copy(N_BLOCKS, 1).start()

    prev = bexp_ref[jnp.maximum(i - 1, 0)]
    changed = (i == 0) | (bexp_ref[i] != prev)

    @pl.when(changed)
    def _cast_weights():
        wg_bf[...] = wg_ref[0].astype(BF16)
        wu_bf[...] = wu_ref[0].astype(BF16)
        wd_bf[...] = wd_ref[0].astype(BF16)

    def active(slot, nxt):
        src_copy(0, nxt).wait()
        dst_copy(0, nxt).wait()
        wait_gather(slot)
        wait_scatter(slot)
        src_copy(jnp.minimum(i + 2, last), slot).start()
        dst_copy(i, slot).start()
        start_gather(nxt)
        start_scatter(nxt)
        xb = xbuf[slot].astype(BF16)
        hg = jnp.dot(xb, wg_bf[...], preferred_element_type=F32)
        hu = jnp.dot(xb, wu_bf[...], preferred_element_type=F32)
        hh = (hg * _sigmoid(hg) * hu).astype(BF16)
        obuf[slot] = jnp.dot(hh, wd_bf[...], preferred_element_type=F32)

    def epilogue(slot, nxt):
        dst_copy(0, slot).wait()
        start_scatter(slot)
        wait_scatter(nxt)
        wait_scatter(slot)
        wait_gather(nxt)
        src_copy(0, slot).wait()

    for parity in range(2):
        @pl.when((i < nused) & (i % 2 == parity))
        def _():
            active(parity, 1 - parity)

        @pl.when((i == last) & (i % 2 == parity))
        def _():
            epilogue(parity, 1 - parity)


def _experts(bexp, nused, src2d, dst2d, n2, w_gate, w_up, w_down):
    def w_blk(i, bexp, nused):
        return (bexp[i], 0, 0)

    any_spec = pl.BlockSpec(memory_space=pl.ANY)
    return pl.pallas_call(
        _expert_kernel,
        grid_spec=pltpu.PrefetchScalarGridSpec(
            num_scalar_prefetch=2,
            grid=(N_BLOCKS,),
            in_specs=[
                any_spec, any_spec, any_spec,
                pl.BlockSpec((1, D_MODEL, D_EXPERT), w_blk),
                pl.BlockSpec((1, D_MODEL, D_EXPERT), w_blk),
                pl.BlockSpec((1, D_EXPERT, D_MODEL), w_blk),
            ],
            out_specs=any_spec,
            scratch_shapes=[
                pltpu.SMEM((2, MOE_BLOCK), jnp.int32),
                pltpu.SMEM((2, MOE_BLOCK), jnp.int32),
                pltpu.VMEM((2, MOE_BLOCK, D_MODEL), F32),
                pltpu.VMEM((2, MOE_BLOCK, D_MODEL), F32),
                pltpu.VMEM((D_MODEL, D_EXPERT), BF16),
                pltpu.VMEM((D_MODEL, D_EXPERT), BF16),
                pltpu.VMEM((D_EXPERT, D_MODEL), BF16),
                pltpu.SemaphoreType.DMA((2,)),
                pltpu.SemaphoreType.DMA((2,)),
                pltpu.SemaphoreType.DMA((2,)),
                pltpu.SemaphoreType.DMA((2,)),
            ],
        ),
        out_shape=jax.ShapeDtypeStruct((N_ASSIGN + DUMMY_ROWS, D_MODEL), F32),
        compiler_params=pltpu.CompilerParams(
            dimension_semantics=("arbitrary",),
            vmem_limit_bytes=48 * 1024 * 1024,
        ),
        name="experts",
    )(bexp, nused, src2d, dst2d, n2, w_gate, w_up, w_down)


def _combine_kernel(h1_ref, mf_ref, fg_ref, y_ref, out_ref):
    w = mf_ref[...]
    h = (h1_ref[...] + y_ref[:, 0:D_MODEL] * w[:, 0:1]
         + y_ref[:, D_MODEL:TOP_K * D_MODEL] * w[:, 1:2])
    out_ref[...] = _rms(h, fg_ref[...])


def _combine(h1, mf, fg, y2):
    n_steps = N_TOK // TD
    tok = lambda i: (i, 0)
    return pl.pallas_call(
        _combine_kernel,
        grid=(n_steps,),
        in_specs=[
            pl.BlockSpec((TD, D_MODEL), tok),
            pl.BlockSpec((TD, META_COLS), tok),
            pl.BlockSpec((1, D_MODEL), lambda i: (0, 0)),
            pl.BlockSpec((TD, TOP_K * D_MODEL), tok),
        ],
        out_specs=pl.BlockSpec((TD, D_MODEL), tok),
        out_shape=jax.ShapeDtypeStruct((N_TOK, D_MODEL), F32),
        compiler_params=pltpu.CompilerParams(dimension_semantics=("arbitrary",)),
        name="combine",
    )(h1, mf, fg, y2)


def kernel(x, norm1_gain, w_in, gmlp_v_gain, gmlp_w_s, gmlp_b_s, hgrn_lower_bounds,
           hgrn_out_gain, w_out, norm2_gain, w_group_router, b_group_router,
           w_expert_router, b_expert_router, w_gate, w_up, w_down, final_gain):
    l = 0
    x2 = x.reshape(N_TOK, D_MODEL)
    bst = jnp.tile(jnp.transpose(gmlp_b_s[l]), (TS // GMLP_BLOCK, 1))
    w_router = jnp.concatenate([w_group_router[l], w_expert_router[l]], axis=1)
    w_router = jnp.pad(w_router, ((0, 0), (0, ROUTER_COLS - w_router.shape[1])))
    wr_hi = w_router.astype(BF16)
    wr_lo = (w_router - wr_hi.astype(F32)).astype(BF16)
    b_router = jnp.concatenate([b_group_router[l], b_expert_router[l]])
    b_router = jnp.pad(b_router, (0, ROUTER_COLS - b_router.shape[0])).reshape(1, ROUTER_COLS)

    h1, n2, meta_i, meta_f, counts = _mixer(
        x2, norm1_gain[l].reshape(1, D_MODEL), w_in[l].astype(BF16),
        gmlp_v_gain[l].reshape(1, D_A), gmlp_w_s[l], bst,
        hgrn_lower_bounds, hgrn_out_gain[l].reshape(1, D_B), w_out[l].astype(BF16),
        norm2_gain[l].reshape(1, D_MODEL), wr_hi, wr_lo, b_router)

    cnt = counts[0, EXPERT_COL0:EXPERT_COL0 + N_EXPERTS]
    padded = ((cnt + MOE_BLOCK - 1) // MOE_BLOCK) * MOE_BLOCK
    pend = jnp.cumsum(padded)
    pstart = pend - padded
    eid = meta_i[:, 0:TOP_K]
    rank = meta_i[:, TOP_K:2 * TOP_K]
    base = jnp.sum(jnp.where(eid[:, :, None] == jnp.arange(N_EXPERTS)[None, None, :],
                             pstart[None, None, :], 0), axis=-1)
    dest = (base + rank).astype(jnp.int32).reshape(N_ASSIGN)
    inv = jnp.full((SORTED_ROWS,), -1, jnp.int32).at[dest].set(
        jnp.arange(N_ASSIGN, dtype=jnp.int32), unique_indices=True)
    inv = jnp.concatenate([inv, jnp.full((MOE_BLOCK,), -1, jnp.int32)])
    pos = jnp.arange(SORTED_ROWS + MOE_BLOCK, dtype=jnp.int32)
    blk = jnp.where(pos < SORTED_ROWS, pos // MOE_BLOCK, 1)
    dummy = N_ASSIGN + (blk % 2) * MOE_BLOCK + pos % MOE_BLOCK
    src2d = jnp.where(inv >= 0, inv // TOP_K, 0)[:SORTED_ROWS].reshape(N_BLOCKS, 1, MOE_BLOCK)
    dst2d = jnp.where(inv >= 0, inv, dummy).reshape(N_BLOCKS + 1, 1, MOE_BLOCK)
    blk_start = jnp.arange(N_BLOCKS, dtype=jnp.int32) * MOE_BLOCK
    bexp = jnp.clip(jnp.sum(blk_start[:, None] >= pend[None, :], axis=1), 0,
                    N_EXPERTS - 1).astype(jnp.int32)
    nused = (pend[-1:] // MOE_BLOCK).astype(jnp.int32)

    y = _experts(bexp, nused, src2d, dst2d, n2, w_gate[l], w_up[l], w_down[l])
    y2 = y.reshape((N_ASSIGN + DUMMY_ROWS) // TOP_K, TOP_K * D_MODEL)
    out = _combine(h1, meta_f, final_gain.reshape(1, D_MODEL), y2)
    return out.reshape(BATCH, SEQ, D_MODEL)
```

```python
import functools

import jax
import jax.numpy as jnp
from jax import lax
from jax.experimental import pallas as pl
from jax.experimental.pallas import tpu as pltpu

D_MODEL = 1024
BATCH = 2
SEQ = 8192
N_TOK = BATCH * SEQ
CHUNK = 64
EPS = 1e-6
D_A = 512
HEADS = 4
HEAD_DIM = 128
GMLP_BLOCK = 128
D_B = 512
D_IN = 3072
N_GROUPS = 4
EXPERTS_PER_GROUP = 8
N_EXPERTS = 32
TOP_K = 2
D_EXPERT = 512

LANES = 128
ROUTER_COLS = LANES
EXPERT_COL0 = N_GROUPS
META_COLS = 8
ROW_TILE = D_MODEL // LANES

TS = 256
CHUNKS_PER_STEP = TS // CHUNK
MOE_BLOCK = 256
N_BLOCKS = N_TOK * TOP_K // MOE_BLOCK + N_EXPERTS
SORTED_ROWS = N_BLOCKS * MOE_BLOCK
N_ASSIGN = N_TOK * TOP_K
DUMMY_ROWS = 2 * MOE_BLOCK
BLOCK_TILE_ROWS = MOE_BLOCK * (D_MODEL // LANES)
INV_LEN = SORTED_ROWS + MOE_BLOCK
TD = 512

F32 = jnp.float32
BF16 = jnp.bfloat16
NT_DIMS = (((1,), (1,)), ((), ()))
TN_DIMS = (((0,), (0,)), ((), ()))


def _sigmoid(x):
    return 1.0 / (1.0 + jnp.exp(-x))


def _gelu(x):
    return 0.5 * x * (1.0 + lax.erf(x * (2.0 ** -0.5)))


def _rms(x, gain):
    return x * lax.rsqrt(jnp.mean(x * x, axis=-1, keepdims=True) + EPS) * gain


def _store_row_tiles(ref, x, rows):
    for c in range(ROW_TILE):
        ref[pl.ds(c, rows, stride=ROW_TILE), :] = x[:, c * LANES:(c + 1) * LANES]


def _load_row_tiles(ref, rows):
    return jnp.concatenate(
        [ref[pl.ds(c, rows, stride=ROW_TILE), :] for c in range(ROW_TILE)], axis=1)


def _split_bf16(x):
    hi = x.astype(BF16)
    lo = (x - hi.astype(F32)).astype(BF16)
    return hi, lo


def _mixer_kernel(x_ref, g1_ref, win_ref, vg_ref, ws_ref, bst_ref, lbraw_ref, og_ref,
                  wout_ref, g2_ref, wrh_ref, wrl_ref, br_ref,
                  h1_ref, n2_ref, mi_ref, mf_ref, cnt_ref,
                  st_ref, carry_ref, wm_ref, mix_ref):
    i = pl.program_id(0)

    @pl.when(i == 0)
    def _init():
        carry_ref[...] = jnp.zeros_like(carry_ref)
        r = lax.broadcasted_iota(jnp.int32, (GMLP_BLOCK, GMLP_BLOCK), 0)
        c = lax.broadcasted_iota(jnp.int32, (GMLP_BLOCK, GMLP_BLOCK), 1)
        keep = (c // CHUNK) <= (r // CHUNK)
        wm_ref[...] = jnp.zeros_like(wm_ref)
        for g in range(HEADS):
            w = jnp.where(keep, ws_ref[g], 0.0).astype(BF16)
            for p in range(TS // GMLP_BLOCK):
                wm_ref[g, p * GMLP_BLOCK:(p + 1) * GMLP_BLOCK,
                       p * GMLP_BLOCK:(p + 1) * GMLP_BLOCK] = w

    @pl.when(i % (SEQ // TS) == 0)
    def _reset_state():
        st_ref[...] = jnp.zeros_like(st_ref)

    x = x_ref[...]
    n1 = _rms(x, g1_ref[...]).astype(BF16)

    gu = _gelu(jnp.dot(n1, win_ref[:, 0:D_A], preferred_element_type=F32))
    gv = _gelu(jnp.dot(n1, win_ref[:, D_A:2 * D_A], preferred_element_type=F32))
    for g in range(HEADS):
        sl = slice(g * HEAD_DIM, (g + 1) * HEAD_DIM)
        vh = _rms(gv[:, sl], vg_ref[:, sl]).astype(BF16)
        mixed = jnp.dot(wm_ref[g], vh, preferred_element_type=F32) + bst_ref[:, g:g + 1]
        mix_ref[:, sl] = (gu[:, sl] * mixed).astype(BF16)

    o2 = 2 * D_A
    qr = jnp.dot(n1, win_ref[:, o2:o2 + D_B], preferred_element_type=F32)
    fr = jnp.dot(n1, win_ref[:, o2 + D_B:o2 + 2 * D_B], preferred_element_type=F32)
    vv = jnp.dot(n1, win_ref[:, o2 + 2 * D_B:o2 + 3 * D_B], preferred_element_type=F32).astype(BF16)
    gr = jnp.dot(n1, win_ref[:, o2 + 3 * D_B:o2 + 4 * D_B], preferred_element_type=F32)

    lbr = lbraw_ref[...]
    lbm = jnp.max(lbr, axis=0, keepdims=True)
    lbe = jnp.exp(lbr - lbm)
    lb = lbe[0:1, :] / jnp.sum(lbe, axis=0, keepdims=True)

    q = qr * _sigmoid(qr)
    f = lb + (1.0 - lb) * _sigmoid(fr)
    k = 1.0 - f
    lf = jnp.log(f)

    row = lax.broadcasted_iota(jnp.int32, (TS, TS), 0)
    col = lax.broadcasted_iota(jnp.int32, (TS, TS), 1)
    causal = (row >= col) & ((row // CHUNK) == (col // CHUNK))
    tri = jnp.where(causal, 1.0, 0.0).astype(BF16)
    lf_hi, lf_lo = _split_bf16(lf)
    b = (jnp.dot(tri, lf_hi, preferred_element_type=F32)
         + jnp.dot(tri, lf_lo, preferred_element_type=F32))

    def chunk_rows(r):
        return jnp.concatenate(
            [jnp.broadcast_to(b[c * CHUNK + r:c * CHUNK + r + 1, :], (CHUNK, D_B))
             for c in range(CHUNKS_PER_STEP)], axis=0)

    bref = chunk_rows(CHUNK // 2 - 1)
    blast = chunk_rows(CHUNK - 1)
    qe = (q * jnp.exp(b - bref)).astype(BF16)
    ke = (k * jnp.exp(bref - b)).astype(BF16)
    kd = (k * jnp.exp(blast - b)).astype(BF16)
    qb = (q * jnp.exp(b)).astype(BF16)

    for h in range(HEADS):
        sl = slice(h * HEAD_DIM, (h + 1) * HEAD_DIM)
        sc = lax.dot_general(qe[:, sl], ke[:, sl], NT_DIMS, preferred_element_type=F32)
        sc = jnp.where(causal, sc, 0.0).astype(BF16)
        o = jnp.dot(sc, vv[:, sl], preferred_element_type=F32)
        st = st_ref[h]
        inter = []
        for c in range(CHUNKS_PER_STEP):
            rows = slice(c * CHUNK, (c + 1) * CHUNK)
            inter.append(lax.dot_general(qb[rows, sl], st.astype(BF16), NT_DIMS,
                                         preferred_element_type=F32))
            upd = lax.dot_general(vv[rows, sl], kd[rows, sl], TN_DIMS,
                                  preferred_element_type=F32)
            decay = jnp.exp(b[(c + 1) * CHUNK - 1:(c + 1) * CHUNK, sl])
            st = st * decay + upd
        st_ref[h] = st
        o = o + jnp.concatenate(inter, axis=0)
        o = _rms(o, og_ref[:, sl])
        g_h = gr[:, sl]
        mix_ref[:, D_A + h * HEAD_DIM:D_A + (h + 1) * HEAD_DIM] = (
            o * (g_h * _sigmoid(g_h))).astype(BF16)

    h1 = x + jnp.dot(mix_ref[...], wout_ref[...], preferred_element_type=F32)
    h1_ref[...] = h1
    n2 = _rms(h1, g2_ref[...])
    _store_row_tiles(n2_ref, n2, TS)

    n2_hi, n2_lo = _split_bf16(n2)
    logits = (jnp.dot(n2_hi, wrh_ref[...], preferred_element_type=F32)
              + jnp.dot(n2_lo, wrh_ref[...], preferred_element_type=F32)
              + jnp.dot(n2_hi, wrl_ref[...], preferred_element_type=F32)
              + br_ref[...])
    lane = lax.broadcasted_iota(jnp.int32, (TS, ROUTER_COLS), 1)
    neg = -jnp.inf
    gl = jnp.where(lane < N_GROUPS, logits, neg)
    gmax = jnp.max(gl, axis=1, keepdims=True)
    g_idx = jnp.min(jnp.where(gl == gmax, lane, ROUTER_COLS), axis=1, keepdims=True)
    g_prob = 1.0 / jnp.sum(jnp.exp(gl - gmax), axis=1, keepdims=True)
    e_lo = EXPERT_COL0 + g_idx * EXPERTS_PER_GROUP
    el = jnp.where((lane >= e_lo) & (lane < e_lo + EXPERTS_PER_GROUP), logits, neg)
    m1 = jnp.max(el, axis=1, keepdims=True)
    i1 = jnp.min(jnp.where(el == m1, lane, ROUTER_COLS), axis=1, keepdims=True)
    el2 = jnp.where(lane == i1, neg, el)
    m2 = jnp.max(el2, axis=1, keepdims=True)
    i2 = jnp.min(jnp.where(el2 == m2, lane, ROUTER_COLS), axis=1, keepdims=True)
    e21 = jnp.exp(m2 - m1)
    w1 = g_prob / (1.0 + e21)
    w2 = g_prob * e21 / (1.0 + e21)

    hit1 = lane == i1
    hit2 = lane == i2
    onehot = jnp.where(hit1 | hit2, 1.0, 0.0)
    before = jnp.where(row > col, 1.0, 0.0).astype(BF16)
    prior = jnp.dot(before, onehot.astype(BF16), preferred_element_type=F32) + carry_ref[...]
    r1 = jnp.sum(jnp.where(hit1, prior, 0.0), axis=1, keepdims=True)
    r2 = jnp.sum(jnp.where(hit2, prior, 0.0), axis=1, keepdims=True)
    carry = carry_ref[...] + jnp.sum(onehot, axis=0, keepdims=True)
    carry_ref[...] = carry
    cnt_ref[...] = carry.astype(jnp.int32)

    mlane = lax.broadcasted_iota(jnp.int32, (TS, META_COLS), 1)
    e1 = i1 - EXPERT_COL0
    e2 = i2 - EXPERT_COL0
    mi_ref[...] = jnp.where(mlane == 0, e1,
                            jnp.where(mlane == 1, e2,
                                      jnp.where(mlane == 2, r1.astype(jnp.int32),
                                                jnp.where(mlane == 3, r2.astype(jnp.int32), 0))))
    mf_ref[...] = jnp.where(mlane == 0, w1, jnp.where(mlane == 1, w2, 0.0))


def _mixer(x2, g1, win, vg, ws, bst, lbraw, og, wout, g2, wrh, wrl, br):
    n_steps = N_TOK // TS
    const2 = lambda i: (0, 0)
    tok = lambda i: (i, 0)
    return pl.pallas_call(
        _mixer_kernel,
        grid=(n_steps,),
        in_specs=[
            pl.BlockSpec((TS, D_MODEL), tok),
            pl.BlockSpec((1, D_MODEL), const2),
            pl.BlockSpec((D_MODEL, D_IN), const2),
            pl.BlockSpec((1, D_A), const2),
            pl.BlockSpec((HEADS, GMLP_BLOCK, GMLP_BLOCK), lambda i: (0, 0, 0)),
            pl.BlockSpec((TS, HEADS), const2),
            pl.BlockSpec((2, D_B), const2),
            pl.BlockSpec((1, D_B), const2),
            pl.BlockSpec((D_MODEL, D_MODEL), const2),
            pl.BlockSpec((1, D_MODEL), const2),
            pl.BlockSpec((D_MODEL, ROUTER_COLS), const2),
            pl.BlockSpec((D_MODEL, ROUTER_COLS), const2),
            pl.BlockSpec((1, ROUTER_COLS), const2),
        ],
        out_specs=[
            pl.BlockSpec((TS, D_MODEL), tok),
            pl.BlockSpec((TS * ROW_TILE, LANES), tok),
            pl.BlockSpec((TS, META_COLS), tok),
            pl.BlockSpec((TS, META_COLS), tok),
            pl.BlockSpec((1, ROUTER_COLS), const2),
        ],
        out_shape=[
            jax.ShapeDtypeStruct((N_TOK, D_MODEL), F32),
            jax.ShapeDtypeStruct((N_TOK * ROW_TILE, LANES), F32),
            jax.ShapeDtypeStruct((N_TOK, META_COLS), jnp.int32),
            jax.ShapeDtypeStruct((N_TOK, META_COLS), F32),
            jax.ShapeDtypeStruct((1, ROUTER_COLS), jnp.int32),
        ],
        scratch_shapes=[
            pltpu.VMEM((HEADS, HEAD_DIM, HEAD_DIM), F32),
            pltpu.VMEM((1, ROUTER_COLS), F32),
            pltpu.VMEM((HEADS, TS, TS), BF16),
            pltpu.VMEM((TS, D_MODEL), BF16),
        ],
        compiler_params=pltpu.CompilerParams(
            dimension_semantics=("arbitrary",),
            vmem_limit_bytes=56 * 1024 * 1024,
        ),
        name="mixer",
    )(x2, g1, win, vg, ws, bst, lbraw, og, wout, g2, wrh, wrl, br)


def _invert_kernel(dest_ref, inv_ref):
    def fill(p, carry):
        inv_ref[p] = -1
        return carry

    lax.fori_loop(0, INV_LEN, fill, 0, unroll=16)

    def place(a, carry):
        inv_ref[dest_ref[a]] = a
        return carry

    lax.fori_loop(0, N_ASSIGN, place, 0, unroll=16)


def _invert(dest):
    smem = pl.BlockSpec(memory_space=pltpu.SMEM)
    return pl.pallas_call(
        _invert_kernel,
        in_specs=[smem],
        out_specs=smem,
        out_shape=jax.ShapeDtypeStruct((INV_LEN,), jnp.int32),
        name="invert",
    )(dest)


def _expert_kernel(bexp_ref, nused_ref, src_hbm, dst_hbm, n2_hbm, wg_ref, wu_ref, wd_ref, y_hbm,
                   src_smem, dst_smem, xbuf, obuf, wg_bf, wu_bf, wd_bf,
                   src_sem, dst_sem, gsem, ssem):
    i = pl.program_id(0)
    nused = nused_ref[0]
    last = nused - 1
    slot = i % 2
    nxt = 1 - slot

    def src_copy(blk, s):
        return pltpu.make_async_copy(src_hbm.at[blk], src_smem.at[pl.ds(s, 1)], src_sem.at[s])

    def dst_copy(blk, s):
        return pltpu.make_async_copy(dst_hbm.at[blk], dst_smem.at[pl.ds(s, 1)], dst_sem.at[s])

    def start_gather(s):
        for j in range(MOE_BLOCK):
            r = pl.multiple_of(src_smem[s, j], ROW_TILE)
            pltpu.make_async_copy(n2_hbm.at[pl.ds(r, ROW_TILE)],
                                  xbuf.at[s, pl.ds(j * ROW_TILE, ROW_TILE)], gsem.at[s]).start()

    def wait_gather(s):
        pltpu.make_async_copy(n2_hbm.at[pl.ds(0, BLOCK_TILE_ROWS)], xbuf.at[s],
                              gsem.at[s]).wait()

    def start_scatter(s):
        for j in range(MOE_BLOCK):
            r = pl.multiple_of(dst_smem[s, j], ROW_TILE)
            pltpu.make_async_copy(obuf.at[s, pl.ds(j * ROW_TILE, ROW_TILE)],
                                  y_hbm.at[pl.ds(r, ROW_TILE)], ssem.at[s]).start()

    def wait_scatter(s):
        pltpu.make_async_copy(obuf.at[s], y_hbm.at[pl.ds(0, BLOCK_TILE_ROWS)],
                              ssem.at[s]).wait()

    @pl.when(i == 0)
    def _prologue():
        obuf[...] = jnp.zeros_like(obuf)
        pltpu.make_async_copy(obuf.at[0],
                              y_hbm.at[pl.ds(N_ASSIGN * ROW_TILE, BLOCK_TILE_ROWS)],
                              ssem.at[0]).start()
        src_copy(0, 0).start()
        src_copy(0, 0).wait()
        start_gather(0)
        src_copy(jnp.minimum(1, last), 1).start()
        dst_copy(N_BLOCKS, 1).start()

    prev = bexp_ref[jnp.maximum(i - 1, 0)]
    changed = (i == 0) | (bexp_ref[i] != prev)

    @pl.when(changed)
    def _cast_weights():
        wg_bf[...] = wg_ref[0].astype(BF16)
        wu_bf[...] = wu_ref[0].astype(BF16)
        wd_bf[...] = wd_ref[0].astype(BF16)

    def active(slot, nxt):
        src_copy(0, nxt).wait()
        dst_copy(0, nxt).wait()
        wait_gather(slot)
        wait_scatter(slot)
        src_copy(jnp.minimum(i + 2, last), slot).start()
        dst_copy(i, slot).start()
        start_gather(nxt)
        start_scatter(nxt)
        xb = _load_row_tiles(xbuf.at[slot], MOE_BLOCK).astype(BF16)
        hg = jnp.dot(xb, wg_bf[...], preferred_element_type=F32)
        hu = jnp.dot(xb, wu_bf[...], preferred_element_type=F32)
        hh = (hg * _sigmoid(hg) * hu).astype(BF16)
        _store_row_tiles(obuf.at[slot], jnp.dot(hh, wd_bf[...], preferred_element_type=F32),
                         MOE_BLOCK)

    def epilogue(slot, nxt):
        dst_copy(0, slot).wait()
        start_scatter(slot)
        wait_scatter(nxt)
        wait_scatter(slot)
        wait_gather(nxt)
        src_copy(0, slot).wait()

    for parity in range(2):
        @pl.when((i < nused) & (i % 2 == parity))
        def _():
            active(parity, 1 - parity)

        @pl.when((i == last) & (i % 2 == parity))
        def _():
            epilogue(parity, 1 - parity)


def _experts(bexp, nused, src2d, dst2d, n2, w_gate, w_up, w_down):
    def w_blk(i, bexp, nused):
        return (bexp[i], 0, 0)

    any_spec = pl.BlockSpec(memory_space=pl.ANY)
    return pl.pallas_call(
        _expert_kernel,
        grid_spec=pltpu.PrefetchScalarGridSpec(
            num_scalar_prefetch=2,
            grid=(N_BLOCKS,),
            in_specs=[
                any_spec, any_spec, any_spec,
                pl.BlockSpec((1, D_MODEL, D_EXPERT), w_blk),
                pl.BlockSpec((1, D_MODEL, D_EXPERT), w_blk),
                pl.BlockSpec((1, D_EXPERT, D_MODEL), w_blk),
            ],
            out_specs=any_spec,
            scratch_shapes=[
                pltpu.SMEM((2, MOE_BLOCK), jnp.int32),
                pltpu.SMEM((2, MOE_BLOCK), jnp.int32),
                pltpu.VMEM((2, BLOCK_TILE_ROWS, LANES), F32),
                pltpu.VMEM((2, BLOCK_TILE_ROWS, LANES), F32),
                pltpu.VMEM((D_MODEL, D_EXPERT), BF16),
                pltpu.VMEM((D_MODEL, D_EXPERT), BF16),
                pltpu.VMEM((D_EXPERT, D_MODEL), BF16),
                pltpu.SemaphoreType.DMA((2,)),
                pltpu.SemaphoreType.DMA((2,)),
                pltpu.SemaphoreType.DMA((2,)),
                pltpu.SemaphoreType.DMA((2,)),
            ],
        ),
        out_shape=jax.ShapeDtypeStruct(((N_ASSIGN + DUMMY_ROWS) * ROW_TILE, LANES), F32),
        compiler_params=pltpu.CompilerParams(
            dimension_semantics=("arbitrary",),
            vmem_limit_bytes=48 * 1024 * 1024,
        ),
        name="experts",
    )(bexp, nused, src2d, dst2d, n2, w_gate, w_up, w_down)


def _combine_kernel(h1_ref, mf_ref, fg_ref, y0_ref, y1_ref, out_ref):
    w = mf_ref[...]
    h = (h1_ref[...] + _load_row_tiles(y0_ref, TD) * w[:, 0:1]
         + _load_row_tiles(y1_ref, TD) * w[:, 1:2])
    out_ref[...] = _rms(h, fg_ref[...])


def _combine(h1, mf, fg, y):
    n_steps = N_TOK // TD
    tok = lambda i: (i, 0)
    return pl.pallas_call(
        _combine_kernel,
        grid=(n_steps,),
        in_specs=[
            pl.BlockSpec((TD, D_MODEL), tok),
            pl.BlockSpec((TD, META_COLS), tok),
            pl.BlockSpec((1, D_MODEL), lambda i: (0, 0)),
            pl.BlockSpec((TD * ROW_TILE, LANES), tok),
            pl.BlockSpec((TD * ROW_TILE, LANES), lambda i: (i + n_steps, 0)),
        ],
        out_specs=pl.BlockSpec((TD, D_MODEL), tok),
        out_shape=jax.ShapeDtypeStruct((N_TOK, D_MODEL), F32),
        compiler_params=pltpu.CompilerParams(dimension_semantics=("arbitrary",)),
        name="combine",
    )(h1, mf, fg, y, y)


def kernel(x, norm1_gain, w_in, gmlp_v_gain, gmlp_w_s, gmlp_b_s, hgrn_lower_bounds,
           hgrn_out_gain, w_out, norm2_gain, w_group_router, b_group_router,
           w_expert_router, b_expert_router, w_gate, w_up, w_down, final_gain):
    l = 0
    x2 = x.reshape(N_TOK, D_MODEL)
    bst = jnp.tile(jnp.transpose(gmlp_b_s[l]), (TS // GMLP_BLOCK, 1))
    w_router = jnp.concatenate([w_group_router[l], w_expert_router[l]], axis=1)
    w_router = jnp.pad(w_router, ((0, 0), (0, ROUTER_COLS - w_router.shape[1])))
    wr_hi = w_router.astype(BF16)
    wr_lo = (w_router - wr_hi.astype(F32)).astype(BF16)
    b_router = jnp.concatenate([b_group_router[l], b_expert_router[l]])
    b_router = jnp.pad(b_router, (0, ROUTER_COLS - b_router.shape[0])).reshape(1, ROUTER_COLS)

    h1, n2, meta_i, meta_f, counts = _mixer(
        x2, norm1_gain[l].reshape(1, D_MODEL), w_in[l].astype(BF16),
        gmlp_v_gain[l].reshape(1, D_A), gmlp_w_s[l], bst,
        hgrn_lower_bounds, hgrn_out_gain[l].reshape(1, D_B), w_out[l].astype(BF16),
        norm2_gain[l].reshape(1, D_MODEL), wr_hi, wr_lo, b_router)

    cnt = counts[0, EXPERT_COL0:EXPERT_COL0 + N_EXPERTS]
    padded = ((cnt + MOE_BLOCK - 1) // MOE_BLOCK) * MOE_BLOCK
    pend = jnp.cumsum(padded)
    pstart = pend - padded
    eid = meta_i[:, 0:TOP_K]
    rank = meta_i[:, TOP_K:2 * TOP_K]
    base = jnp.sum(jnp.where(eid[:, :, None] == jnp.arange(N_EXPERTS)[None, None, :],
                             pstart[None, None, :], 0), axis=-1)
    dest = (base + rank).astype(jnp.int32).reshape(N_ASSIGN)
    inv = _invert(dest)
    pos = jnp.arange(INV_LEN, dtype=jnp.int32)
    blk = jnp.where(pos < SORTED_ROWS, pos // MOE_BLOCK, 1)
    dummy = N_ASSIGN + (blk % 2) * MOE_BLOCK + pos % MOE_BLOCK
    tok_of = inv // TOP_K
    slot_of = (inv % TOP_K) * N_TOK + tok_of
    src2d = (jnp.where(inv >= 0, tok_of, 0) * ROW_TILE)[:SORTED_ROWS].reshape(
        N_BLOCKS, 1, MOE_BLOCK)
    dst2d = (jnp.where(inv >= 0, slot_of, dummy) * ROW_TILE).reshape(N_BLOCKS + 1, 1, MOE_BLOCK)
    blk_start = jnp.arange(N_BLOCKS, dtype=jnp.int32) * MOE_BLOCK
    bexp = jnp.clip(jnp.sum(blk_start[:, None] >= pend[None, :], axis=1), 0,
                    N_EXPERTS - 1).astype(jnp.int32)
    nused = (pend[-1:] // MOE_BLOCK).astype(jnp.int32)

    y = _experts(bexp, nused, src2d, dst2d, n2, w_gate[l], w_up[l], w_down[l])
    out = _combine(h1, meta_f, final_gain.reshape(1, D_MODEL), y)
    return out.reshape(BATCH, SEQ, D_MODEL)
```

```python
import functools

import jax
import jax.numpy as jnp
from jax import lax
from jax.experimental import pallas as pl
from jax.experimental.pallas import tpu as pltpu

D_MODEL = 1024
BATCH = 2
SEQ = 8192
N_TOK = BATCH * SEQ
CHUNK = 64
EPS = 1e-6
D_A = 512
HEADS = 4
HEAD_DIM = 128
GMLP_BLOCK = 128
D_B = 512
D_IN = 3072
N_GROUPS = 4
EXPERTS_PER_GROUP = 8
N_EXPERTS = 32
TOP_K = 2
D_EXPERT = 512

LANES = 128
ROUTER_COLS = LANES
EXPERT_COL0 = N_GROUPS
META_COLS = 8
ROW_TILE = D_MODEL // LANES

TS = 256
CHUNKS_PER_STEP = TS // CHUNK
MOE_BLOCK = 256
N_BLOCKS = N_TOK * TOP_K // MOE_BLOCK + N_EXPERTS
SORTED_ROWS = N_BLOCKS * MOE_BLOCK
N_ASSIGN = N_TOK * TOP_K
DUMMY_ROWS = 2 * MOE_BLOCK
BLOCK_TILE_ROWS = MOE_BLOCK * (D_MODEL // LANES)
INV_LEN = SORTED_ROWS + MOE_BLOCK
TD = 512

F32 = jnp.float32
BF16 = jnp.bfloat16
NT_DIMS = (((1,), (1,)), ((), ()))
TN_DIMS = (((0,), (0,)), ((), ()))


def _sigmoid(x):
    return 1.0 / (1.0 + jnp.exp(-x))


def _gelu(x):
    return 0.5 * x * (1.0 + lax.erf(x * (2.0 ** -0.5)))


def _rms(x, gain):
    return x * lax.rsqrt(jnp.mean(x * x, axis=-1, keepdims=True) + EPS) * gain


def _store_row_tiles(ref, x, rows):
    for c in range(ROW_TILE):
        ref[pl.ds(c, rows, stride=ROW_TILE), :] = x[:, c * LANES:(c + 1) * LANES]


def _load_row_tiles(ref, rows):
    return jnp.concatenate(
        [ref[pl.ds(c, rows, stride=ROW_TILE), :] for c in range(ROW_TILE)], axis=1)


def _split_bf16(x):
    hi = x.astype(BF16)
    lo = (x - hi.astype(F32)).astype(BF16)
    return hi, lo


def _mixer_kernel(x_ref, g1_ref, win_ref, vg_ref, ws_ref, bst_ref, lbraw_ref, og_ref,
                  wout_ref, g2_ref, wrh_ref, wrl_ref, br_ref,
                  h1_ref, n2_ref, mi_ref, mf_ref, cnt_ref,
                  st_ref, carry_ref, wm_ref, mix_ref):
    i = pl.program_id(0)

    @pl.when(i == 0)
    def _init():
        carry_ref[...] = jnp.zeros_like(carry_ref)
        r = lax.broadcasted_iota(jnp.int32, (GMLP_BLOCK, GMLP_BLOCK), 0)
        c = lax.broadcasted_iota(jnp.int32, (GMLP_BLOCK, GMLP_BLOCK), 1)
        keep = (c // CHUNK) <= (r // CHUNK)
        wm_ref[...] = jnp.zeros_like(wm_ref)
        for g in range(HEADS):
            w = jnp.where(keep, ws_ref[g], 0.0).astype(BF16)
            for p in range(TS // GMLP_BLOCK):
                wm_ref[g, p * GMLP_BLOCK:(p + 1) * GMLP_BLOCK,
                       p * GMLP_BLOCK:(p + 1) * GMLP_BLOCK] = w

    @pl.when(i % (SEQ // TS) == 0)
    def _reset_state():
        st_ref[...] = jnp.zeros_like(st_ref)

    x = x_ref[...]
    n1 = _rms(x, g1_ref[...]).astype(BF16)

    gu = _gelu(jnp.dot(n1, win_ref[:, 0:D_A], preferred_element_type=F32))
    gv = _gelu(jnp.dot(n1, win_ref[:, D_A:2 * D_A], preferred_element_type=F32))
    for g in range(HEADS):
        sl = slice(g * HEAD_DIM, (g + 1) * HEAD_DIM)
        vh = _rms(gv[:, sl], vg_ref[:, sl]).astype(BF16)
        mixed = jnp.dot(wm_ref[g], vh, preferred_element_type=F32) + bst_ref[:, g:g + 1]
        mix_ref[:, sl] = (gu[:, sl] * mixed).astype(BF16)

    o2 = 2 * D_A
    qr = jnp.dot(n1, win_ref[:, o2:o2 + D_B], preferred_element_type=F32)
    fr = jnp.dot(n1, win_ref[:, o2 + D_B:o2 + 2 * D_B], preferred_element_type=F32)
    vv = jnp.dot(n1, win_ref[:, o2 + 2 * D_B:o2 + 3 * D_B], preferred_element_type=F32).astype(BF16)
    gr = jnp.dot(n1, win_ref[:, o2 + 3 * D_B:o2 + 4 * D_B], preferred_element_type=F32)

    lbr = lbraw_ref[...]
    lbm = jnp.max(lbr, axis=0, keepdims=True)
    lbe = jnp.exp(lbr - lbm)
    lb = lbe[0:1, :] / jnp.sum(lbe, axis=0, keepdims=True)

    q = qr * _sigmoid(qr)
    f = lb + (1.0 - lb) * _sigmoid(fr)
    k = 1.0 - f
    lf = jnp.log(f)

    row = lax.broadcasted_iota(jnp.int32, (TS, TS), 0)
    col = lax.broadcasted_iota(jnp.int32, (TS, TS), 1)
    causal = (row >= col) & ((row // CHUNK) == (col // CHUNK))
    tri = jnp.where(causal, 1.0, 0.0).astype(BF16)
    lf_hi, lf_lo = _split_bf16(lf)
    b = (jnp.dot(tri, lf_hi, preferred_element_type=F32)
         + jnp.dot(tri, lf_lo, preferred_element_type=F32))

    def chunk_rows(r):
        return jnp.concatenate(
            [jnp.broadcast_to(b[c * CHUNK + r:c * CHUNK + r + 1, :], (CHUNK, D_B))
             for c in range(CHUNKS_PER_STEP)], axis=0)

    bref = chunk_rows(CHUNK // 2 - 1)
    blast = chunk_rows(CHUNK - 1)
    qe = (q * jnp.exp(b - bref)).astype(BF16)
    ke = (k * jnp.exp(bref - b)).astype(BF16)
    kd = (k * jnp.exp(blast - b)).astype(BF16)
    qb = (q * jnp.exp(b)).astype(BF16)

    for h in range(HEADS):
        sl = slice(h * HEAD_DIM, (h + 1) * HEAD_DIM)
        sc = lax.dot_general(qe[:, sl], ke[:, sl], NT_DIMS, preferred_element_type=F32)
        sc = jnp.where(causal, sc, 0.0).astype(BF16)
        o = jnp.dot(sc, vv[:, sl], preferred_element_type=F32)
        st = st_ref[h]
        inter = []
        for c in range(CHUNKS_PER_STEP):
            rows = slice(c * CHUNK, (c + 1) * CHUNK)
            inter.append(lax.dot_general(qb[rows, sl], st.astype(BF16), NT_DIMS,
                                         preferred_element_type=F32))
            upd = lax.dot_general(vv[rows, sl], kd[rows, sl], TN_DIMS,
                                  preferred_element_type=F32)
            decay = jnp.exp(b[(c + 1) * CHUNK - 1:(c + 1) * CHUNK, sl])
            st = st * decay + upd
        st_ref[h] = st
        o = o + jnp.concatenate(inter, axis=0)
        o = _rms(o, og_ref[:, sl])
        g_h = gr[:, sl]
        mix_ref[:, D_A + h * HEAD_DIM:D_A + (h + 1) * HEAD_DIM] = (
            o * (g_h * _sigmoid(g_h))).astype(BF16)

    h1 = x + jnp.dot(mix_ref[...], wout_ref[...], preferred_element_type=F32)
    h1_ref[...] = h1
    n2 = _rms(h1, g2_ref[...])
    _store_row_tiles(n2_ref, n2, TS)

    n2_hi, n2_lo = _split_bf16(n2)
    logits = (jnp.dot(n2_hi, wrh_ref[...], preferred_element_type=F32)
              + jnp.dot(n2_lo, wrh_ref[...], preferred_element_type=F32)
              + jnp.dot(n2_hi, wrl_ref[...], preferred_element_type=F32)
              + br_ref[...])
    lane = lax.broadcasted_iota(jnp.int32, (TS, ROUTER_COLS), 1)
    neg = -jnp.inf
    gl = jnp.where(lane < N_GROUPS, logits, neg)
    gmax = jnp.max(gl, axis=1, keepdims=True)
    g_idx = jnp.min(jnp.where(gl == gmax, lane, ROUTER_COLS), axis=1, keepdims=True)
    g_prob = 1.0 / jnp.sum(jnp.exp(gl - gmax), axis=1, keepdims=True)
    e_lo = EXPERT_COL0 + g_idx * EXPERTS_PER_GROUP
    el = jnp.where((lane >= e_lo) & (lane < e_lo + EXPERTS_PER_GROUP), logits, neg)
    m1 = jnp.max(el, axis=1, keepdims=True)
    i1 = jnp.min(jnp.where(el == m1, lane, ROUTER_COLS), axis=1, keepdims=True)
    el2 = jnp.where(lane == i1, neg, el)
    m2 = jnp.max(el2, axis=1, keepdims=True)
    i2 = jnp.min(jnp.where(el2 == m2, lane, ROUTER_COLS), axis=1, keepdims=True)
    e21 = jnp.exp(m2 - m1)
    w1 = g_prob / (1.0 + e21)
    w2 = g_prob * e21 / (1.0 + e21)

    hit1 = lane == i1
    hit2 = lane == i2
    onehot = jnp.where(hit1 | hit2, 1.0, 0.0)
    before = jnp.where(row > col, 1.0, 0.0).astype(BF16)
    prior = jnp.dot(before, onehot.astype(BF16), preferred_element_type=F32) + carry_ref[...]
    r1 = jnp.sum(jnp.where(hit1, prior, 0.0), axis=1, keepdims=True)
    r2 = jnp.sum(jnp.where(hit2, prior, 0.0), axis=1, keepdims=True)
    carry = carry_ref[...] + jnp.sum(onehot, axis=0, keepdims=True)
    carry_ref[...] = carry
    cnt_ref[...] = carry.astype(jnp.int32)

    mlane = lax.broadcasted_iota(jnp.int32, (TS, META_COLS), 1)
    e1 = i1 - EXPERT_COL0
    e2 = i2 - EXPERT_COL0
    mi_ref[...] = jnp.where(mlane == 0, e1,
                            jnp.where(mlane == 1, e2,
                                      jnp.where(mlane == 2, r1.astype(jnp.int32),
                                                jnp.where(mlane == 3, r2.astype(jnp.int32), 0))))
    mf_ref[...] = jnp.where(mlane == 0, w1, jnp.where(mlane == 1, w2, 0.0))


def _mixer(x2, g1, win, vg, ws, bst, lbraw, og, wout, g2, wrh, wrl, br):
    n_steps = N_TOK // TS
    const2 = lambda i: (0, 0)
    tok = lambda i: (i, 0)
    return pl.pallas_call(
        _mixer_kernel,
        grid=(n_steps,),
        in_specs=[
            pl.BlockSpec((TS, D_MODEL), tok),
            pl.BlockSpec((1, D_MODEL), const2),
            pl.BlockSpec((D_MODEL, D_IN), const2),
            pl.BlockSpec((1, D_A), const2),
            pl.BlockSpec((HEADS, GMLP_BLOCK, GMLP_BLOCK), lambda i: (0, 0, 0)),
            pl.BlockSpec((TS, HEADS), const2),
            pl.BlockSpec((2, D_B), const2),
            pl.BlockSpec((1, D_B), const2),
            pl.BlockSpec((D_MODEL, D_MODEL), const2),
            pl.BlockSpec((1, D_MODEL), const2),
            pl.BlockSpec((D_MODEL, ROUTER_COLS), const2),
            pl.BlockSpec((D_MODEL, ROUTER_COLS), const2),
            pl.BlockSpec((1, ROUTER_COLS), const2),
        ],
        out_specs=[
            pl.BlockSpec((TS, D_MODEL), tok),
            pl.BlockSpec((TS * ROW_TILE, LANES), tok),
            pl.BlockSpec((TS, META_COLS), tok),
            pl.BlockSpec((TS, META_COLS), tok),
            pl.BlockSpec((1, ROUTER_COLS), const2),
        ],
        out_shape=[
            jax.ShapeDtypeStruct((N_TOK, D_MODEL), F32),
            jax.ShapeDtypeStruct((N_TOK * ROW_TILE, LANES), F32),
            jax.ShapeDtypeStruct((N_TOK, META_COLS), jnp.int32),
            jax.ShapeDtypeStruct((N_TOK, META_COLS), F32),
            jax.ShapeDtypeStruct((1, ROUTER_COLS), jnp.int32),
        ],
        scratch_shapes=[
            pltpu.VMEM((HEADS, HEAD_DIM, HEAD_DIM), F32),
            pltpu.VMEM((1, ROUTER_COLS), F32),
            pltpu.VMEM((HEADS, TS, TS), BF16),
            pltpu.VMEM((TS, D_MODEL), BF16),
        ],
        compiler_params=pltpu.CompilerParams(
            dimension_semantics=("arbitrary",),
            vmem_limit_bytes=56 * 1024 * 1024,
        ),
        name="mixer",
    )(x2, g1, win, vg, ws, bst, lbraw, og, wout, g2, wrh, wrl, br)


def _invert_kernel(dest_ref, inv_ref):
    def fill(p, carry):
        inv_ref[p] = -1
        return carry

    lax.fori_loop(0, INV_LEN, fill, 0, unroll=16)

    def place(a, carry):
        inv_ref[dest_ref[a]] = a
        return carry

    lax.fori_loop(0, N_ASSIGN, place, 0, unroll=16)


def _invert(dest):
    smem = pl.BlockSpec(memory_space=pltpu.SMEM)
    return pl.pallas_call(
        _invert_kernel,
        in_specs=[smem],
        out_specs=smem,
        out_shape=jax.ShapeDtypeStruct((INV_LEN,), jnp.int32),
        name="invert",
    )(dest)


def _expert_kernel(bexp_ref, nused_ref, src_hbm, dst_hbm, n2_hbm, wg_ref, wu_ref, wd_ref, y_hbm,
                   src_smem, dst_smem, xbuf, obuf, wg_bf, wu_bf, wd_bf,
                   src_sem, dst_sem, gsem, ssem):
    i = pl.program_id(0)
    nused = nused_ref[0]
    last = nused - 1
    slot = i % 2
    nxt = 1 - slot

    def src_copy(blk, s):
        return pltpu.make_async_copy(src_hbm.at[blk], src_smem.at[pl.ds(s, 1)], src_sem.at[s])

    def dst_copy(blk, s):
        return pltpu.make_async_copy(dst_hbm.at[blk], dst_smem.at[pl.ds(s, 1)], dst_sem.at[s])

    def start_gather(s):
        for j in range(MOE_BLOCK):
            r = pl.multiple_of(src_smem[s, j], ROW_TILE)
            pltpu.make_async_copy(n2_hbm.at[pl.ds(r, ROW_TILE)],
                                  xbuf.at[s, pl.ds(j * ROW_TILE, ROW_TILE)], gsem.at[s]
                                  ).start(priority=j % 2)

    def wait_gather(s):
        pltpu.make_async_copy(n2_hbm.at[pl.ds(0, BLOCK_TILE_ROWS)], xbuf.at[s],
                              gsem.at[s]).wait()

    def start_scatter(s):
        for j in range(MOE_BLOCK):
            r = pl.multiple_of(dst_smem[s, j], ROW_TILE)
            pltpu.make_async_copy(obuf.at[s, pl.ds(j * ROW_TILE, ROW_TILE)],
                                  y_hbm.at[pl.ds(r, ROW_TILE)], ssem.at[s]
                                  ).start(priority=j % 2)

    def wait_scatter(s):
        pltpu.make_async_copy(obuf.at[s], y_hbm.at[pl.ds(0, BLOCK_TILE_ROWS)],
                              ssem.at[s]).wait()

    @pl.when(i == 0)
    def _prologue():
        obuf[...] = jnp.zeros_like(obuf)
        pltpu.make_async_copy(obuf.at[0],
                              y_hbm.at[pl.ds(N_ASSIGN * ROW_TILE, BLOCK_TILE_ROWS)],
                              ssem.at[0]).start()
        src_copy(0, 0).start()
        src_copy(0, 0).wait()
        start_gather(0)
        src_copy(jnp.minimum(1, last), 1).start()
        dst_copy(N_BLOCKS, 1).start()

    prev = bexp_ref[jnp.maximum(i - 1, 0)]
    changed = (i == 0) | (bexp_ref[i] != prev)

    @pl.when(changed)
    def _cast_weights():
        wg_bf[...] = wg_ref[0].astype(BF16)
        wu_bf[...] = wu_ref[0].astype(BF16)
        wd_bf[...] = wd_ref[0].astype(BF16)

    def active(slot, nxt):
        src_copy(0, nxt).wait()
        dst_copy(0, nxt).wait()
        wait_gather(slot)
        wait_scatter(slot)
        src_copy(jnp.minimum(i + 2, last), slot).start()
        dst_copy(i, slot).start()
        start_gather(nxt)
        start_scatter(nxt)
        xb = _load_row_tiles(xbuf.at[slot], MOE_BLOCK).astype(BF16)
        hg = jnp.dot(xb, wg_bf[...], preferred_element_type=F32)
        hu = jnp.dot(xb, wu_bf[...], preferred_element_type=F32)
        hh = (hg * _sigmoid(hg) * hu).astype(BF16)
        _store_row_tiles(obuf.at[slot], jnp.dot(hh, wd_bf[...], preferred_element_type=F32),
                         MOE_BLOCK)

    def epilogue(slot, nxt):
        dst_copy(0, slot).wait()
        start_scatter(slot)
        wait_scatter(nxt)
        wait_scatter(slot)
        wait_gather(nxt)
        src_copy(0, slot).wait()

    for parity in range(2):
        @pl.when((i < nused) & (i % 2 == parity))
        def _():
            active(parity, 1 - parity)

        @pl.when((i == last) & (i % 2 == parity))
        def _():
            epilogue(parity, 1 - parity)


def _experts(bexp, nused, src2d, dst2d, n2, w_gate, w_up, w_down):
    def w_blk(i, bexp, nused):
        return (bexp[i], 0, 0)

    any_spec = pl.BlockSpec(memory_space=pl.ANY)
    return pl.pallas_call(
        _expert_kernel,
        grid_spec=pltpu.PrefetchScalarGridSpec(
            num_scalar_prefetch=2,
            grid=(N_BLOCKS,),
            in_specs=[
                any_spec, any_spec, any_spec,
                pl.BlockSpec((1, D_MODEL, D_EXPERT), w_blk),
                pl.BlockSpec((1, D_MODEL, D_EXPERT), w_blk),
                pl.BlockSpec((1, D_EXPERT, D_MODEL), w_blk),
            ],
            out_specs=any_spec,
            scratch_shapes=[
                pltpu.SMEM((2, MOE_BLOCK), jnp.int32),
                pltpu.SMEM((2, MOE_BLOCK), jnp.int32),
                pltpu.VMEM((2, BLOCK_TILE_ROWS, LANES), F32),
                pltpu.VMEM((2, BLOCK_TILE_ROWS, LANES), F32),
                pltpu.VMEM((D_MODEL, D_EXPERT), BF16),
                pltpu.VMEM((D_MODEL, D_EXPERT), BF16),
                pltpu.VMEM((D_EXPERT, D_MODEL), BF16),
                pltpu.SemaphoreType.DMA((2,)),
                pltpu.SemaphoreType.DMA((2,)),
                pltpu.SemaphoreType.DMA((2,)),
                pltpu.SemaphoreType.DMA((2,)),
            ],
        ),
        out_shape=jax.ShapeDtypeStruct(((N_ASSIGN + DUMMY_ROWS) * ROW_TILE, LANES), F32),
        compiler_params=pltpu.CompilerParams(
            dimension_semantics=("arbitrary",),
            vmem_limit_bytes=48 * 1024 * 1024,
        ),
        name="experts",
    )(bexp, nused, src2d, dst2d, n2, w_gate, w_up, w_down)


def _combine_kernel(h1_ref, mf_ref, fg_ref, y0_ref, y1_ref, out_ref):
    w = mf_ref[...]
    h = (h1_ref[...] + _load_row_tiles(y0_ref, TD) * w[:, 0:1]
         + _load_row_tiles(y1_ref, TD) * w[:, 1:2])
    out_ref[...] = _rms(h, fg_ref[...])


def _combine(h1, mf, fg, y):
    n_steps = N_TOK // TD
    tok = lambda i: (i, 0)
    return pl.pallas_call(
        _combine_kernel,
        grid=(n_steps,),
        in_specs=[
            pl.BlockSpec((TD, D_MODEL), tok),
            pl.BlockSpec((TD, META_COLS), tok),
            pl.BlockSpec((1, D_MODEL), lambda i: (0, 0)),
            pl.BlockSpec((TD * ROW_TILE, LANES), tok),
            pl.BlockSpec((TD * ROW_TILE, LANES), lambda i: (i + n_steps, 0)),
        ],
        out_specs=pl.BlockSpec((TD, D_MODEL), tok),
        out_shape=jax.ShapeDtypeStruct((N_TOK, D_MODEL), F32),
        compiler_params=pltpu.CompilerParams(dimension_semantics=("arbitrary",)),
        name="combine",
    )(h1, mf, fg, y, y)


def kernel(x, norm1_gain, w_in, gmlp_v_gain, gmlp_w_s, gmlp_b_s, hgrn_lower_bounds,
           hgrn_out_gain, w_out, norm2_gain, w_group_router, b_group_router,
           w_expert_router, b_expert_router, w_gate, w_up, w_down, final_gain):
    l = 0
    x2 = x.reshape(N_TOK, D_MODEL)
    bst = jnp.tile(jnp.transpose(gmlp_b_s[l]), (TS // GMLP_BLOCK, 1))
    w_router = jnp.concatenate([w_group_router[l], w_expert_router[l]], axis=1)
    w_router = jnp.pad(w_router, ((0, 0), (0, ROUTER_COLS - w_router.shape[1])))
    wr_hi = w_router.astype(BF16)
    wr_lo = (w_router - wr_hi.astype(F32)).astype(BF16)
    b_router = jnp.concatenate([b_group_router[l], b_expert_router[l]])
    b_router = jnp.pad(b_router, (0, ROUTER_COLS - b_router.shape[0])).reshape(1, ROUTER_COLS)

    h1, n2, meta_i, meta_f, counts = _mixer(
        x2, norm1_gain[l].reshape(1, D_MODEL), w_in[l].astype(BF16),
        gmlp_v_gain[l].reshape(1, D_A), gmlp_w_s[l], bst,
        hgrn_lower_bounds, hgrn_out_gain[l].reshape(1, D_B), w_out[l].astype(BF16),
        norm2_gain[l].reshape(1, D_MODEL), wr_hi, wr_lo, b_router)

    cnt = counts[0, EXPERT_COL0:EXPERT_COL0 + N_EXPERTS]
    padded = ((cnt + MOE_BLOCK - 1) // MOE_BLOCK) * MOE_BLOCK
    pend = jnp.cumsum(padded)
    pstart = pend - padded
    eid = meta_i[:, 0:TOP_K]
    rank = meta_i[:, TOP_K:2 * TOP_K]
    base = jnp.sum(jnp.where(eid[:, :, None] == jnp.arange(N_EXPERTS)[None, None, :],
                             pstart[None, None, :], 0), axis=-1)
    dest = (base + rank).astype(jnp.int32).reshape(N_ASSIGN)
    inv = _invert(dest)
    pos = jnp.arange(INV_LEN, dtype=jnp.int32)
    blk = jnp.where(pos < SORTED_ROWS, pos // MOE_BLOCK, 1)
    dummy = N_ASSIGN + (blk % 2) * MOE_BLOCK + pos % MOE_BLOCK
    tok_of = inv // TOP_K
    slot_of = (inv % TOP_K) * N_TOK + tok_of
    src2d = (jnp.where(inv >= 0, tok_of, 0) * ROW_TILE)[:SORTED_ROWS].reshape(
        N_BLOCKS, 1, MOE_BLOCK)
    dst2d = (jnp.where(inv >= 0, slot_of, dummy) * ROW_TILE).reshape(N_BLOCKS + 1, 1, MOE_BLOCK)
    blk_start = jnp.arange(N_BLOCKS, dtype=jnp.int32) * MOE_BLOCK
    bexp = jnp.clip(jnp.sum(blk_start[:, None] >= pend[None, :], axis=1), 0,
                    N_EXPERTS - 1).astype(jnp.int32)
    nused = (pend[-1:] // MOE_BLOCK).astype(jnp.int32)

    y = _experts(bexp, nused, src2d, dst2d, n2, w_gate[l], w_up[l], w_down[l])
    out = _combine(h1, meta_f, final_gain.reshape(1, D_MODEL), y)
    return out.reshape(BATCH, SEQ, D_MODEL)
```

```python
import functools

import jax
import jax.numpy as jnp
from jax import lax
from jax.experimental import pallas as pl
from jax.experimental.pallas import tpu as pltpu

D_MODEL = 1024
BATCH = 2
SEQ = 8192
N_TOK = BATCH * SEQ
CHUNK = 64
EPS = 1e-6
D_A = 512
HEADS = 4
HEAD_DIM = 128
GMLP_BLOCK = 128
D_B = 512
D_IN = 3072
N_GROUPS = 4
EXPERTS_PER_GROUP = 8
N_EXPERTS = 32
TOP_K = 2
D_EXPERT = 512

LANES = 128
ROUTER_COLS = LANES
EXPERT_COL0 = N_GROUPS
META_COLS = 8
ROW_TILE = D_MODEL // LANES
PACK_COLS = D_MODEL // 2
PACK_ROWS = PACK_COLS // LANES
HI_MASK = 0xFFFF0000

TS = 256
CHUNKS_PER_STEP = TS // CHUNK
MOE_BLOCK = 256
N_BLOCKS = N_TOK * TOP_K // MOE_BLOCK + N_EXPERTS
SORTED_ROWS = N_BLOCKS * MOE_BLOCK
N_ASSIGN = N_TOK * TOP_K
DUMMY_ROWS = 2 * MOE_BLOCK
BLOCK_TILE_ROWS = MOE_BLOCK * (D_MODEL // LANES)
INV_LEN = SORTED_ROWS + MOE_BLOCK
TD = 512

F32 = jnp.float32
BF16 = jnp.bfloat16
NT_DIMS = (((1,), (1,)), ((), ()))
TN_DIMS = (((0,), (0,)), ((), ()))


def _sigmoid(x):
    return 1.0 / (1.0 + jnp.exp(-x))


def _gelu(x):
    return 0.5 * x * (1.0 + lax.erf(x * (2.0 ** -0.5)))


def _rms(x, gain):
    return x * lax.rsqrt(jnp.mean(x * x, axis=-1, keepdims=True) + EPS) * gain


def _store_row_tiles(ref, x, rows):
    for c in range(ROW_TILE):
        ref[pl.ds(c, rows, stride=ROW_TILE), :] = x[:, c * LANES:(c + 1) * LANES]


def _load_row_tiles(ref, rows):
    return jnp.concatenate(
        [ref[pl.ds(c, rows, stride=ROW_TILE), :] for c in range(ROW_TILE)], axis=1)


def _split_bf16(x):
    hi = x.astype(BF16)
    lo = (x - hi.astype(F32)).astype(BF16)
    return hi, lo


def _mixer_kernel(x_ref, g1_ref, win_ref, vg_ref, ws_ref, bst_ref, lbraw_ref, og_ref,
                  wout_ref, g2_ref, wrh_ref, wrl_ref, br_ref,
                  h1_ref, n2_ref, mi_ref, mf_ref, cnt_ref,
                  st_ref, carry_ref, wm_ref, mix_ref):
    i = pl.program_id(0)

    @pl.when(i == 0)
    def _init():
        carry_ref[...] = jnp.zeros_like(carry_ref)
        r = lax.broadcasted_iota(jnp.int32, (GMLP_BLOCK, GMLP_BLOCK), 0)
        c = lax.broadcasted_iota(jnp.int32, (GMLP_BLOCK, GMLP_BLOCK), 1)
        keep = (c // CHUNK) <= (r // CHUNK)
        wm_ref[...] = jnp.zeros_like(wm_ref)
        for g in range(HEADS):
            w = jnp.where(keep, ws_ref[g], 0.0).astype(BF16)
            for p in range(TS // GMLP_BLOCK):
                wm_ref[g, p * GMLP_BLOCK:(p + 1) * GMLP_BLOCK,
                       p * GMLP_BLOCK:(p + 1) * GMLP_BLOCK] = w

    @pl.when(i % (SEQ // TS) == 0)
    def _reset_state():
        st_ref[...] = jnp.zeros_like(st_ref)

    x = x_ref[...]
    n1 = _rms(x, g1_ref[...]).astype(BF16)

    gu = _gelu(jnp.dot(n1, win_ref[:, 0:D_A], preferred_element_type=F32))
    gv = _gelu(jnp.dot(n1, win_ref[:, D_A:2 * D_A], preferred_element_type=F32))
    for g in range(HEADS):
        sl = slice(g * HEAD_DIM, (g + 1) * HEAD_DIM)
        vh = _rms(gv[:, sl], vg_ref[:, sl]).astype(BF16)
        mixed = jnp.dot(wm_ref[g], vh, preferred_element_type=F32) + bst_ref[:, g:g + 1]
        mix_ref[:, sl] = (gu[:, sl] * mixed).astype(BF16)

    o2 = 2 * D_A
    qr = jnp.dot(n1, win_ref[:, o2:o2 + D_B], preferred_element_type=F32)
    fr = jnp.dot(n1, win_ref[:, o2 + D_B:o2 + 2 * D_B], preferred_element_type=F32)
    vv = jnp.dot(n1, win_ref[:, o2 + 2 * D_B:o2 + 3 * D_B], preferred_element_type=F32).astype(BF16)
    gr = jnp.dot(n1, win_ref[:, o2 + 3 * D_B:o2 + 4 * D_B], preferred_element_type=F32)

    lbr = lbraw_ref[...]
    lbm = jnp.max(lbr, axis=0, keepdims=True)
    lbe = jnp.exp(lbr - lbm)
    lb = lbe[0:1, :] / jnp.sum(lbe, axis=0, keepdims=True)

    q = qr * _sigmoid(qr)
    f = lb + (1.0 - lb) * _sigmoid(fr)
    k = 1.0 - f
    lf = jnp.log(f)

    row = lax.broadcasted_iota(jnp.int32, (TS, TS), 0)
    col = lax.broadcasted_iota(jnp.int32, (TS, TS), 1)
    causal = (row >= col) & ((row // CHUNK) == (col // CHUNK))
    tri = jnp.where(causal, 1.0, 0.0).astype(BF16)
    lf_hi, lf_lo = _split_bf16(lf)
    b = (jnp.dot(tri, lf_hi, preferred_element_type=F32)
         + jnp.dot(tri, lf_lo, preferred_element_type=F32))

    def chunk_rows(r):
        return jnp.concatenate(
            [jnp.broadcast_to(b[c * CHUNK + r:c * CHUNK + r + 1, :], (CHUNK, D_B))
             for c in range(CHUNKS_PER_STEP)], axis=0)

    bref = chunk_rows(CHUNK // 2 - 1)
    blast = chunk_rows(CHUNK - 1)
    qe = (q * jnp.exp(b - bref)).astype(BF16)
    ke = (k * jnp.exp(bref - b)).astype(BF16)
    kd = (k * jnp.exp(blast - b)).astype(BF16)
    qb = (q * jnp.exp(b)).astype(BF16)

    for h in range(HEADS):
        sl = slice(h * HEAD_DIM, (h + 1) * HEAD_DIM)
        sc = lax.dot_general(qe[:, sl], ke[:, sl], NT_DIMS, preferred_element_type=F32)
        sc = jnp.where(causal, sc, 0.0).astype(BF16)
        o = jnp.dot(sc, vv[:, sl], preferred_element_type=F32)
        st = st_ref[h]
        inter = []
        for c in range(CHUNKS_PER_STEP):
            rows = slice(c * CHUNK, (c + 1) * CHUNK)
            inter.append(lax.dot_general(qb[rows, sl], st.astype(BF16), NT_DIMS,
                                         preferred_element_type=F32))
            upd = lax.dot_general(vv[rows, sl], kd[rows, sl], TN_DIMS,
                                  preferred_element_type=F32)
            decay = jnp.exp(b[(c + 1) * CHUNK - 1:(c + 1) * CHUNK, sl])
            st = st * decay + upd
        st_ref[h] = st
        o = o + jnp.concatenate(inter, axis=0)
        o = _rms(o, og_ref[:, sl])
        g_h = gr[:, sl]
        mix_ref[:, D_A + h * HEAD_DIM:D_A + (h + 1) * HEAD_DIM] = (
            o * (g_h * _sigmoid(g_h))).astype(BF16)

    h1 = x + jnp.dot(mix_ref[...], wout_ref[...], preferred_element_type=F32)
    h1_ref[...] = h1
    n2 = _rms(h1, g2_ref[...])
    lo = pltpu.bitcast(n2[:, 0:PACK_COLS].astype(BF16).astype(F32), jnp.uint32) >> 16
    hi = pltpu.bitcast(n2[:, PACK_COLS:D_MODEL].astype(BF16).astype(F32), jnp.uint32) & jnp.uint32(HI_MASK)
    packed = lo | hi
    for c in range(PACK_ROWS):
        n2_ref[pl.ds(c, TS, stride=PACK_ROWS), :] = packed[:, c * LANES:(c + 1) * LANES]

    n2_hi, n2_lo = _split_bf16(n2)
    logits = (jnp.dot(n2_hi, wrh_ref[...], preferred_element_type=F32)
              + jnp.dot(n2_lo, wrh_ref[...], preferred_element_type=F32)
              + jnp.dot(n2_hi, wrl_ref[...], preferred_element_type=F32)
              + br_ref[...])
    lane = lax.broadcasted_iota(jnp.int32, (TS, ROUTER_COLS), 1)
    neg = -jnp.inf
    gl = jnp.where(lane < N_GROUPS, logits, neg)
    gmax = jnp.max(gl, axis=1, keepdims=True)
    g_idx = jnp.min(jnp.where(gl == gmax, lane, ROUTER_COLS), axis=1, keepdims=True)
    g_prob = 1.0 / jnp.sum(jnp.exp(gl - gmax), axis=1, keepdims=True)
    e_lo = EXPERT_COL0 + g_idx * EXPERTS_PER_GROUP
    el = jnp.where((lane >= e_lo) & (lane < e_lo + EXPERTS_PER_GROUP), logits, neg)
    m1 = jnp.max(el, axis=1, keepdims=True)
    i1 = jnp.min(jnp.where(el == m1, lane, ROUTER_COLS), axis=1, keepdims=True)
    el2 = jnp.where(lane == i1, neg, el)
    m2 = jnp.max(el2, axis=1, keepdims=True)
    i2 = jnp.min(jnp.where(el2 == m2, lane, ROUTER_COLS), axis=1, keepdims=True)
    e21 = jnp.exp(m2 - m1)
    w1 = g_prob / (1.0 + e21)
    w2 = g_prob * e21 / (1.0 + e21)

    hit1 = lane == i1
    hit2 = lane == i2
    onehot = jnp.where(hit1 | hit2, 1.0, 0.0)
    before = jnp.where(row > col, 1.0, 0.0).astype(BF16)
    prior = jnp.dot(before, onehot.astype(BF16), preferred_element_type=F32) + carry_ref[...]
    r1 = jnp.sum(jnp.where(hit1, prior, 0.0), axis=1, keepdims=True)
    r2 = jnp.sum(jnp.where(hit2, prior, 0.0), axis=1, keepdims=True)
    carry = carry_ref[...] + jnp.sum(onehot, axis=0, keepdims=True)
    carry_ref[...] = carry
    cnt_ref[...] = carry.astype(jnp.int32)

    mlane = lax.broadcasted_iota(jnp.int32, (TS, META_COLS), 1)
    e1 = i1 - EXPERT_COL0
    e2 = i2 - EXPERT_COL0
    mi_ref[...] = jnp.where(mlane == 0, e1,
                            jnp.where(mlane == 1, e2,
                                      jnp.where(mlane == 2, r1.astype(jnp.int32),
                                                jnp.where(mlane == 3, r2.astype(jnp.int32), 0))))
    mf_ref[...] = jnp.where(mlane == 0, w1, jnp.where(mlane == 1, w2, 0.0))


def _mixer(x2, g1, win, vg, ws, bst, lbraw, og, wout, g2, wrh, wrl, br):
    n_steps = N_TOK // TS
    const2 = lambda i: (0, 0)
    tok = lambda i: (i, 0)
    return pl.pallas_call(
        _mixer_kernel,
        grid=(n_steps,),
        in_specs=[
            pl.BlockSpec((TS, D_MODEL), tok),
            pl.BlockSpec((1, D_MODEL), const2),
            pl.BlockSpec((D_MODEL, D_IN), const2),
            pl.BlockSpec((1, D_A), const2),
            pl.BlockSpec((HEADS, GMLP_BLOCK, GMLP_BLOCK), lambda i: (0, 0, 0)),
            pl.BlockSpec((TS, HEADS), const2),
            pl.BlockSpec((2, D_B), const2),
            pl.BlockSpec((1, D_B), const2),
            pl.BlockSpec((D_MODEL, D_MODEL), const2),
            pl.BlockSpec((1, D_MODEL), const2),
            pl.BlockSpec((D_MODEL, ROUTER_COLS), const2),
            pl.BlockSpec((D_MODEL, ROUTER_COLS), const2),
            pl.BlockSpec((1, ROUTER_COLS), const2),
        ],
        out_specs=[
            pl.BlockSpec((TS, D_MODEL), tok),
            pl.BlockSpec((TS * PACK_ROWS, LANES), tok),
            pl.BlockSpec((TS, META_COLS), tok),
            pl.BlockSpec((TS, META_COLS), tok),
            pl.BlockSpec((1, ROUTER_COLS), const2),
        ],
        out_shape=[
            jax.ShapeDtypeStruct((N_TOK, D_MODEL), F32),
            jax.ShapeDtypeStruct((N_TOK * PACK_ROWS, LANES), jnp.uint32),
            jax.ShapeDtypeStruct((N_TOK, META_COLS), jnp.int32),
            jax.ShapeDtypeStruct((N_TOK, META_COLS), F32),
            jax.ShapeDtypeStruct((1, ROUTER_COLS), jnp.int32),
        ],
        scratch_shapes=[
            pltpu.VMEM((HEADS, HEAD_DIM, HEAD_DIM), F32),
            pltpu.VMEM((1, ROUTER_COLS), F32),
            pltpu.VMEM((HEADS, TS, TS), BF16),
            pltpu.VMEM((TS, D_MODEL), BF16),
        ],
        compiler_params=pltpu.CompilerParams(
            dimension_semantics=("arbitrary",),
            vmem_limit_bytes=56 * 1024 * 1024,
        ),
        name="mixer",
    )(x2, g1, win, vg, ws, bst, lbraw, og, wout, g2, wrh, wrl, br)


def _invert_kernel(dest_ref, inv_ref):
    def fill(p, carry):
        inv_ref[p] = -1
        return carry

    lax.fori_loop(0, INV_LEN, fill, 0, unroll=16)

    def place(a, carry):
        inv_ref[dest_ref[a]] = a
        return carry

    lax.fori_loop(0, N_ASSIGN, place, 0, unroll=16)


def _invert(dest):
    smem = pl.BlockSpec(memory_space=pltpu.SMEM)
    return pl.pallas_call(
        _invert_kernel,
        in_specs=[smem],
        out_specs=smem,
        out_shape=jax.ShapeDtypeStruct((INV_LEN,), jnp.int32),
        name="invert",
    )(dest)


def _expert_kernel(bexp_ref, nused_ref, src_hbm, dst_hbm, n2_hbm, wg_ref, wu_ref, wd_ref, y_hbm,
                   src_smem, dst_smem, n2_vmem, xbuf, obuf, wg_bf, wu_bf, wd_bf,
                   src_sem, dst_sem, nsem, ssem):
    i = pl.program_id(0)
    nused = nused_ref[0]
    last = nused - 1
    slot = i % 2
    nxt = 1 - slot

    def src_copy(blk, s):
        return pltpu.make_async_copy(src_hbm.at[blk], src_smem.at[pl.ds(s, 1)], src_sem.at[s])

    def dst_copy(blk, s):
        return pltpu.make_async_copy(dst_hbm.at[blk], dst_smem.at[pl.ds(s, 1)], dst_sem.at[s])

    def gather(s):
        for j in range(MOE_BLOCK):
            r = pl.multiple_of(src_smem[s, j], PACK_ROWS)
            xbuf[s, pl.ds(j * PACK_ROWS, PACK_ROWS), :] = n2_vmem[pl.ds(r, PACK_ROWS), :]

    def unpack(s):
        words = jnp.concatenate(
            [xbuf[s, pl.ds(c, MOE_BLOCK, stride=PACK_ROWS), :] for c in range(PACK_ROWS)],
            axis=1)
        lo = pltpu.bitcast(words << 16, F32)
        hi = pltpu.bitcast(words & jnp.uint32(HI_MASK), F32)
        return jnp.concatenate([lo, hi], axis=1).astype(BF16)

    def start_scatter(s):
        for j in range(MOE_BLOCK):
            r = pl.multiple_of(dst_smem[s, j], ROW_TILE)
            pltpu.make_async_copy(obuf.at[s, pl.ds(j * ROW_TILE, ROW_TILE)],
                                  y_hbm.at[pl.ds(r, ROW_TILE)], ssem.at[s]
                                  ).start(priority=j % 2)

    def wait_scatter(s):
        pltpu.make_async_copy(obuf.at[s], y_hbm.at[pl.ds(0, BLOCK_TILE_ROWS)],
                              ssem.at[s]).wait()

    @pl.when(i == 0)
    def _prologue():
        obuf[...] = jnp.zeros_like(obuf)
        pltpu.make_async_copy(obuf.at[0],
                              y_hbm.at[pl.ds(N_ASSIGN * ROW_TILE, BLOCK_TILE_ROWS)],
                              ssem.at[0]).start()
        resident = pltpu.make_async_copy(n2_hbm, n2_vmem, nsem)
        resident.start()
        src_copy(0, 0).start()
        src_copy(0, 0).wait()
        resident.wait()
        gather(0)
        src_copy(jnp.minimum(1, last), 1).start()
        dst_copy(N_BLOCKS, 1).start()

    prev = bexp_ref[jnp.maximum(i - 1, 0)]
    changed = (i == 0) | (bexp_ref[i] != prev)

    @pl.when(changed)
    def _cast_weights():
        wg_bf[...] = wg_ref[0].astype(BF16)
        wu_bf[...] = wu_ref[0].astype(BF16)
        wd_bf[...] = wd_ref[0].astype(BF16)

    def active(slot, nxt):
        src_copy(0, nxt).wait()
        dst_copy(0, nxt).wait()
        wait_scatter(slot)
        start_scatter(nxt)
        gather(nxt)
        src_copy(jnp.minimum(i + 2, last), slot).start()
        dst_copy(i, slot).start()
        xb = unpack(slot)
        hg = jnp.dot(xb, wg_bf[...], preferred_element_type=F32)
        hu = jnp.dot(xb, wu_bf[...], preferred_element_type=F32)
        hh = (hg * _sigmoid(hg) * hu).astype(BF16)
        _store_row_tiles(obuf.at[slot], jnp.dot(hh, wd_bf[...], preferred_element_type=F32),
                         MOE_BLOCK)

    def epilogue(slot, nxt):
        dst_copy(0, slot).wait()
        start_scatter(slot)
        wait_scatter(nxt)
        wait_scatter(slot)
        src_copy(0, slot).wait()

    for parity in range(2):
        @pl.when((i < nused) & (i % 2 == parity))
        def _():
            active(parity, 1 - parity)

        @pl.when((i == last) & (i % 2 == parity))
        def _():
            epilogue(parity, 1 - parity)


_EXPERT_W_BYTES = 3 * D_MODEL * D_EXPERT * 4
EXPERT_VMEM_LIMIT = (
    N_TOK * PACK_COLS * 4
    + 2 * _EXPERT_W_BYTES
    + _EXPERT_W_BYTES // 2
    + 2 * MOE_BLOCK * (PACK_COLS + D_MODEL) * 4
    + 4 * 1024 * 1024)


def _experts(bexp, nused, src2d, dst2d, n2, w_gate, w_up, w_down):
    def w_blk(i, bexp, nused):
        return (bexp[i], 0, 0)

    any_spec = pl.BlockSpec(memory_space=pl.ANY)
    return pl.pallas_call(
        _expert_kernel,
        grid_spec=pltpu.PrefetchScalarGridSpec(
            num_scalar_prefetch=2,
            grid=(N_BLOCKS,),
            in_specs=[
                any_spec, any_spec, any_spec,
                pl.BlockSpec((1, D_MODEL, D_EXPERT), w_blk),
                pl.BlockSpec((1, D_MODEL, D_EXPERT), w_blk),
                pl.BlockSpec((1, D_EXPERT, D_MODEL), w_blk),
            ],
            out_specs=any_spec,
            scratch_shapes=[
                pltpu.SMEM((2, MOE_BLOCK), jnp.int32),
                pltpu.SMEM((2, MOE_BLOCK), jnp.int32),
                pltpu.VMEM((N_TOK * PACK_ROWS, LANES), jnp.uint32),
                pltpu.VMEM((2, MOE_BLOCK * PACK_ROWS, LANES), jnp.uint32),
                pltpu.VMEM((2, BLOCK_TILE_ROWS, LANES), F32),
                pltpu.VMEM((D_MODEL, D_EXPERT), BF16),
                pltpu.VMEM((D_MODEL, D_EXPERT), BF16),
                pltpu.VMEM((D_EXPERT, D_MODEL), BF16),
                pltpu.SemaphoreType.DMA((2,)),
                pltpu.SemaphoreType.DMA((2,)),
                pltpu.SemaphoreType.DMA,
                pltpu.SemaphoreType.DMA((2,)),
            ],
        ),
        out_shape=jax.ShapeDtypeStruct(((N_ASSIGN + DUMMY_ROWS) * ROW_TILE, LANES), F32),
        compiler_params=pltpu.CompilerParams(
            dimension_semantics=("arbitrary",),
            vmem_limit_bytes=EXPERT_VMEM_LIMIT,
        ),
        name="experts",
    )(bexp, nused, src2d, dst2d, n2, w_gate, w_up, w_down)


def _combine_kernel(h1_ref, mf_ref, fg_ref, y0_ref, y1_ref, out_ref):
    w = mf_ref[...]
    h = (h1_ref[...] + _load_row_tiles(y0_ref, TD) * w[:, 0:1]
         + _load_row_tiles(y1_ref, TD) * w[:, 1:2])
    out_ref[...] = _rms(h, fg_ref[...])


def _combine(h1, mf, fg, y):
    n_steps = N_TOK // TD
    tok = lambda i: (i, 0)
    return pl.pallas_call(
        _combine_kernel,
        grid=(n_steps,),
        in_specs=[
            pl.BlockSpec((TD, D_MODEL), tok),
            pl.BlockSpec((TD, META_COLS), tok),
            pl.BlockSpec((1, D_MODEL), lambda i: (0, 0)),
            pl.BlockSpec((TD * ROW_TILE, LANES), tok),
            pl.BlockSpec((TD * ROW_TILE, LANES), lambda i: (i + n_steps, 0)),
        ],
        out_specs=pl.BlockSpec((TD, D_MODEL), tok),
        out_shape=jax.ShapeDtypeStruct((N_TOK, D_MODEL), F32),
        compiler_params=pltpu.CompilerParams(dimension_semantics=("arbitrary",)),
        name="combine",
    )(h1, mf, fg, y, y)


def kernel(x, norm1_gain, w_in, gmlp_v_gain, gmlp_w_s, gmlp_b_s, hgrn_lower_bounds,
           hgrn_out_gain, w_out, norm2_gain, w_group_router, b_group_router,
           w_expert_router, b_expert_router, w_gate, w_up, w_down, final_gain):
    l = 0
    x2 = x.reshape(N_TOK, D_MODEL)
    bst = jnp.tile(jnp.transpose(gmlp_b_s[l]), (TS // GMLP_BLOCK, 1))
    w_router = jnp.concatenate([w_group_router[l], w_expert_router[l]], axis=1)
    w_router = jnp.pad(w_router, ((0, 0), (0, ROUTER_COLS - w_router.shape[1])))
    wr_hi = w_router.astype(BF16)
    wr_lo = (w_router - wr_hi.astype(F32)).astype(BF16)
    b_router = jnp.concatenate([b_group_router[l], b_expert_router[l]])
    b_router = jnp.pad(b_router, (0, ROUTER_COLS - b_router.shape[0])).reshape(1, ROUTER_COLS)

    h1, n2, meta_i, meta_f, counts = _mixer(
        x2, norm1_gain[l].reshape(1, D_MODEL), w_in[l].astype(BF16),
        gmlp_v_gain[l].reshape(1, D_A), gmlp_w_s[l], bst,
        hgrn_lower_bounds, hgrn_out_gain[l].reshape(1, D_B), w_out[l].astype(BF16),
        norm2_gain[l].reshape(1, D_MODEL), wr_hi, wr_lo, b_router)

    cnt = counts[0, EXPERT_COL0:EXPERT_COL0 + N_EXPERTS]
    padded = ((cnt + MOE_BLOCK - 1) // MOE_BLOCK) * MOE_BLOCK
    pend = jnp.cumsum(padded)
    pstart = pend - padded
    eid = meta_i[:, 0:TOP_K]
    rank = meta_i[:, TOP_K:2 * TOP_K]
    base = jnp.sum(jnp.where(eid[:, :, None] == jnp.arange(N_EXPERTS)[None, None, :],
                             pstart[None, None, :], 0), axis=-1)
    dest = (base + rank).astype(jnp.int32).reshape(N_ASSIGN)
    inv = _invert(dest)
    pos = jnp.arange(INV_LEN, dtype=jnp.int32)
    blk = jnp.where(pos < SORTED_ROWS, pos // MOE_BLOCK, 1)
    dummy = N_ASSIGN + (blk % 2) * MOE_BLOCK + pos % MOE_BLOCK
    tok_of = inv // TOP_K
    slot_of = (inv % TOP_K) * N_TOK + tok_of
    src2d = (jnp.where(inv >= 0, tok_of, 0) * PACK_ROWS)[:SORTED_ROWS].reshape(
        N_BLOCKS, 1, MOE_BLOCK)
    dst2d = (jnp.where(inv >= 0, slot_of, dummy) * ROW_TILE).reshape(N_BLOCKS + 1, 1, MOE_BLOCK)
    blk_start = jnp.arange(N_BLOCKS, dtype=jnp.int32) * MOE_BLOCK
    bexp = jnp.clip(jnp.sum(blk_start[:, None] >= pend[None, :], axis=1), 0,
                    N_EXPERTS - 1).astype(jnp.int32)
    nused = (pend[-1:] // MOE_BLOCK).astype(jnp.int32)

    y = _experts(bexp, nused, src2d, dst2d, n2, w_gate[l], w_up[l], w_down[l])
    out = _combine(h1, meta_f, final_gain.reshape(1, D_MODEL), y)
    return out.reshape(BATCH, SEQ, D_MODEL)
```

```python
import jax
import jax.numpy as jnp
from jax import lax
from jax.experimental import pallas as pl
from jax.experimental.pallas import tpu as pltpu

D_MODEL = 1024
BATCH = 2
SEQ = 8192
N_TOK = BATCH * SEQ
CHUNK = 64
EPS = 1e-6
D_A = 512
HEADS = 4
HEAD_DIM = 128
GMLP_BLOCK = 128
D_B = 512
D_IN = 3072
N_GROUPS = 4
EXPERTS_PER_GROUP = 8
N_EXPERTS = 32
TOP_K = 2
D_EXPERT = 512

LANES = 128
ROUTER_ROWS = 48
EXPERT_ROW0 = N_GROUPS
META_ROWS = 8
ROW_TILE = D_MODEL // LANES
PACK_COLS = D_MODEL // 2
PACK_ROWS = PACK_COLS // LANES
HI_MASK = 0xFFFF0000

TS = 256
CHUNKS_PER_STEP = TS // CHUNK
MOE_BLOCK = 256
N_BLOCKS = N_TOK * TOP_K // MOE_BLOCK + N_EXPERTS
SORTED_ROWS = N_BLOCKS * MOE_BLOCK
N_ASSIGN = N_TOK * TOP_K
DUMMY_ROWS = 2 * MOE_BLOCK
BLOCK_TILE_ROWS = MOE_BLOCK * ROW_TILE
INV_LEN = SORTED_ROWS + MOE_BLOCK
TD = 512

F32 = jnp.float32
BF16 = jnp.bfloat16
NT_DIMS = (((1,), (1,)), ((), ()))
TN_DIMS = (((0,), (0,)), ((), ()))


def _sigmoid(x):
    return 1.0 / (1.0 + jnp.exp(-x))


def _gelu(x):
    return 0.5 * x * (1.0 + lax.erf(x * (2.0 ** -0.5)))


def _rms(x, gain):
    return x * lax.rsqrt(jnp.mean(x * x, axis=-1, keepdims=True) + EPS) * gain


def _store_row_tiles(ref, x, rows):
    for c in range(ROW_TILE):
        ref[pl.ds(c, rows, stride=ROW_TILE), :] = x[:, c * LANES:(c + 1) * LANES]


def _load_row_tiles(ref, rows):
    return jnp.concatenate(
        [ref[pl.ds(c, rows, stride=ROW_TILE), :] for c in range(ROW_TILE)], axis=1)


def _split_bf16(x):
    hi = x.astype(BF16)
    lo = (x - hi.astype(F32)).astype(BF16)
    return hi, lo


def _mixer_kernel(x_ref, g1_ref, win_ref, vg_ref, ws_ref, bst_ref, lbraw_ref, og_ref,
                  wout_ref, g2_ref, wr_ref, br_ref,
                  h1_ref, n2_ref, mi_ref, mf_ref, cnt_ref,
                  st_ref, carry_ref, wm_ref, mix_ref):
    i = pl.program_id(0)

    @pl.when(i == 0)
    def _init():
        carry_ref[...] = jnp.zeros_like(carry_ref)
        r = lax.broadcasted_iota(jnp.int32, (GMLP_BLOCK, GMLP_BLOCK), 0)
        c = lax.broadcasted_iota(jnp.int32, (GMLP_BLOCK, GMLP_BLOCK), 1)
        keep = (c // CHUNK) <= (r // CHUNK)
        wm_ref[...] = jnp.zeros_like(wm_ref)
        for g in range(HEADS):
            w = jnp.where(keep, ws_ref[g], 0.0).astype(BF16)
            for p in range(TS // GMLP_BLOCK):
                wm_ref[g, p * GMLP_BLOCK:(p + 1) * GMLP_BLOCK,
                       p * GMLP_BLOCK:(p + 1) * GMLP_BLOCK] = w

    @pl.when(i % (SEQ // TS) == 0)
    def _reset_state():
        st_ref[...] = jnp.zeros_like(st_ref)

    x = x_ref[...]
    n1 = _rms(x, g1_ref[...]).astype(BF16)

    gu = _gelu(jnp.dot(n1, win_ref[:, 0:D_A], preferred_element_type=F32))
    gv = _gelu(jnp.dot(n1, win_ref[:, D_A:2 * D_A], preferred_element_type=F32))
    for g in range(HEADS):
        sl = slice(g * HEAD_DIM, (g + 1) * HEAD_DIM)
        vh = _rms(gv[:, sl], vg_ref[:, sl]).astype(BF16)
        mixed = jnp.dot(wm_ref[g], vh, preferred_element_type=F32) + bst_ref[:, g:g + 1]
        mix_ref[:, sl] = (gu[:, sl] * mixed).astype(BF16)

    o2 = 2 * D_A
    qr = jnp.dot(n1, win_ref[:, o2:o2 + D_B], preferred_element_type=F32)
    fr = jnp.dot(n1, win_ref[:, o2 + D_B:o2 + 2 * D_B], preferred_element_type=F32)
    vv = jnp.dot(n1, win_ref[:, o2 + 2 * D_B:o2 + 3 * D_B], preferred_element_type=F32).astype(BF16)
    gr = jnp.dot(n1, win_ref[:, o2 + 3 * D_B:o2 + 4 * D_B], preferred_element_type=F32)

    lbr = lbraw_ref[...]
    lbm = jnp.max(lbr, axis=0, keepdims=True)
    lbe = jnp.exp(lbr - lbm)
    lb = lbe[0:1, :] / jnp.sum(lbe, axis=0, keepdims=True)

    q = qr * _sigmoid(qr)
    f = lb + (1.0 - lb) * _sigmoid(fr)
    k = 1.0 - f
    lf = jnp.log(f)

    row = lax.broadcasted_iota(jnp.int32, (TS, TS), 0)
    col = lax.broadcasted_iota(jnp.int32, (TS, TS), 1)
    causal = (row >= col) & ((row // CHUNK) == (col // CHUNK))
    tri = jnp.where(causal, 1.0, 0.0).astype(BF16)
    lf_hi, lf_lo = _split_bf16(lf)
    b = (jnp.dot(tri, lf_hi, preferred_element_type=F32)
         + jnp.dot(tri, lf_lo, preferred_element_type=F32))

    def chunk_rows(r):
        return jnp.concatenate(
            [jnp.broadcast_to(b[c * CHUNK + r:c * CHUNK + r + 1, :], (CHUNK, D_B))
             for c in range(CHUNKS_PER_STEP)], axis=0)

    bref = chunk_rows(CHUNK // 2 - 1)
    blast = chunk_rows(CHUNK - 1)
    qe = (q * jnp.exp(b - bref)).astype(BF16)
    ke = (k * jnp.exp(bref - b)).astype(BF16)
    kd = (k * jnp.exp(blast - b)).astype(BF16)
    qb = (q * jnp.exp(b)).astype(BF16)

    for h in range(HEADS):
        sl = slice(h * HEAD_DIM, (h + 1) * HEAD_DIM)
        sc = lax.dot_general(qe[:, sl], ke[:, sl], NT_DIMS, preferred_element_type=F32)
        sc = jnp.where(causal, sc, 0.0).astype(BF16)
        o = jnp.dot(sc, vv[:, sl], preferred_element_type=F32)
        st = st_ref[h]
        inter = []
        for c in range(CHUNKS_PER_STEP):
            rows = slice(c * CHUNK, (c + 1) * CHUNK)
            inter.append(lax.dot_general(qb[rows, sl], st.astype(BF16), NT_DIMS,
                                         preferred_element_type=F32))
            upd = lax.dot_general(vv[rows, sl], kd[rows, sl], TN_DIMS,
                                  preferred_element_type=F32)
            decay = jnp.exp(b[(c + 1) * CHUNK - 1:(c + 1) * CHUNK, sl])
            st = st * decay + upd
        st_ref[h] = st
        o = o + jnp.concatenate(inter, axis=0)
        o = _rms(o, og_ref[:, sl])
        g_h = gr[:, sl]
        mix_ref[:, D_A + h * HEAD_DIM:D_A + (h + 1) * HEAD_DIM] = (
            o * (g_h * _sigmoid(g_h))).astype(BF16)

    h1 = x + jnp.dot(mix_ref[...], wout_ref[...], preferred_element_type=F32)
    h1_ref[...] = h1
    n2 = _rms(h1, g2_ref[...])
    lo = pltpu.bitcast(n2[:, 0:PACK_COLS].astype(BF16).astype(F32), jnp.uint32) >> 16
    hi = (pltpu.bitcast(n2[:, PACK_COLS:D_MODEL].astype(BF16).astype(F32), jnp.uint32)
          & jnp.uint32(HI_MASK))
    packed = lo | hi
    for c in range(PACK_ROWS):
        n2_ref[pl.ds(c, TS, stride=PACK_ROWS), :] = packed[:, c * LANES:(c + 1) * LANES]

    logits = jnp.dot(n2.astype(BF16), wr_ref[...], preferred_element_type=F32)
    logits = jnp.transpose(logits)[0:ROUTER_ROWS, :] + br_ref[...]
    rid = lax.broadcasted_iota(jnp.int32, (ROUTER_ROWS, TS), 0)
    neg = -jnp.inf
    gl = jnp.where(rid < N_GROUPS, logits, neg)
    gmax = jnp.max(gl, axis=0, keepdims=True)
    g_idx = jnp.min(jnp.where(gl == gmax, rid, ROUTER_ROWS), axis=0, keepdims=True)
    g_prob = 1.0 / jnp.sum(jnp.exp(gl - gmax), axis=0, keepdims=True)
    e_lo = EXPERT_ROW0 + g_idx * EXPERTS_PER_GROUP
    el = jnp.where((rid >= e_lo) & (rid < e_lo + EXPERTS_PER_GROUP), logits, neg)
    m1 = jnp.max(el, axis=0, keepdims=True)
    i1 = jnp.min(jnp.where(el == m1, rid, ROUTER_ROWS), axis=0, keepdims=True)
    el2 = jnp.where(rid == i1, neg, el)
    m2 = jnp.max(el2, axis=0, keepdims=True)
    i2 = jnp.min(jnp.where(el2 == m2, rid, ROUTER_ROWS), axis=0, keepdims=True)
    e21 = jnp.exp(m2 - m1)
    w1 = g_prob / (1.0 + e21)
    w2 = g_prob * e21 / (1.0 + e21)

    hit1 = rid == i1
    hit2 = rid == i2
    onehot = jnp.where(hit1 | hit2, 1.0, 0.0)
    earlier = jnp.where(row < col, 1.0, 0.0).astype(BF16)
    seen = carry_ref[...]
    prior = (jnp.dot(onehot.astype(BF16), earlier, preferred_element_type=F32)
             + jnp.concatenate([seen] * (TS // LANES), axis=1))
    r1 = jnp.sum(jnp.where(hit1, prior, 0.0), axis=0, keepdims=True)
    r2 = jnp.sum(jnp.where(hit2, prior, 0.0), axis=0, keepdims=True)
    seen = seen + jnp.sum(onehot, axis=1, keepdims=True)
    carry_ref[...] = seen
    cnt_ref[...] = seen.astype(jnp.int32)

    pad_i = jnp.zeros((META_ROWS - 2 * TOP_K, TS), jnp.int32)
    mi_ref[...] = jnp.concatenate(
        [i1 - EXPERT_ROW0, i2 - EXPERT_ROW0, r1.astype(jnp.int32), r2.astype(jnp.int32), pad_i],
        axis=0)
    mf_ref[...] = jnp.concatenate([w1, w2, jnp.zeros((META_ROWS - TOP_K, TS), F32)], axis=0)


def _mixer(x2, g1, win, vg, ws, bst, lbraw, og, wout, g2, wr, br):
    n_steps = N_TOK // TS
    const2 = lambda i: (0, 0)
    tok = lambda i: (i, 0)
    lane_tok = lambda i: (0, i)
    return pl.pallas_call(
        _mixer_kernel,
        grid=(n_steps,),
        in_specs=[
            pl.BlockSpec((TS, D_MODEL), tok),
            pl.BlockSpec((1, D_MODEL), const2),
            pl.BlockSpec((D_MODEL, D_IN), const2),
            pl.BlockSpec((1, D_A), const2),
            pl.BlockSpec((HEADS, GMLP_BLOCK, GMLP_BLOCK), lambda i: (0, 0, 0)),
            pl.BlockSpec((TS, HEADS), const2),
            pl.BlockSpec((2, D_B), const2),
            pl.BlockSpec((1, D_B), const2),
            pl.BlockSpec((D_MODEL, D_MODEL), const2),
            pl.BlockSpec((1, D_MODEL), const2),
            pl.BlockSpec((D_MODEL, LANES), const2),
            pl.BlockSpec((ROUTER_ROWS, 1), const2),
        ],
        out_specs=[
            pl.BlockSpec((TS, D_MODEL), tok),
            pl.BlockSpec((TS * PACK_ROWS, LANES), tok),
            pl.BlockSpec((META_ROWS, TS), lane_tok),
            pl.BlockSpec((META_ROWS, TS), lane_tok),
            pl.BlockSpec((ROUTER_ROWS, LANES), const2),
        ],
        out_shape=[
            jax.ShapeDtypeStruct((N_TOK, D_MODEL), F32),
            jax.ShapeDtypeStruct((N_TOK * PACK_ROWS, LANES), jnp.uint32),
            jax.ShapeDtypeStruct((META_ROWS, N_TOK), jnp.int32),
            jax.ShapeDtypeStruct((META_ROWS, N_TOK), F32),
            jax.ShapeDtypeStruct((ROUTER_ROWS, LANES), jnp.int32),
        ],
        scratch_shapes=[
            pltpu.VMEM((HEADS, HEAD_DIM, HEAD_DIM), F32),
            pltpu.VMEM((ROUTER_ROWS, LANES), F32),
            pltpu.VMEM((HEADS, TS, TS), BF16),
            pltpu.VMEM((TS, D_MODEL), BF16),
        ],
        compiler_params=pltpu.CompilerParams(
            dimension_semantics=("arbitrary",),
            vmem_limit_bytes=56 * 1024 * 1024,
        ),
        name="mixer",
    )(x2, g1, win, vg, ws, bst, lbraw, og, wout, g2, wr, br)


FILL_UNROLL = 8


def _invert_kernel(dest_ref, gap_lo_ref, gap_hi_ref, inv_ref):
    for e in range(N_EXPERTS):
        first = gap_lo_ref[e] // FILL_UNROLL

        def fill(g, carry):
            for u in range(FILL_UNROLL):
                inv_ref[g * FILL_UNROLL + u] = -1
            return carry

        lax.fori_loop(first, gap_hi_ref[e] // FILL_UNROLL, fill, 0)

    def place(a, carry):
        inv_ref[dest_ref[a]] = a
        return carry

    lax.fori_loop(0, N_ASSIGN, place, 0, unroll=16)


def _invert(dest, gap_lo, gap_hi):
    smem = pl.BlockSpec(memory_space=pltpu.SMEM)
    return pl.pallas_call(
        _invert_kernel,
        in_specs=[smem, smem, smem],
        out_specs=smem,
        out_shape=jax.ShapeDtypeStruct((INV_LEN,), jnp.int32),
        name="invert",
    )(dest, gap_lo, gap_hi)


def _expert_kernel(bexp_ref, nused_ref, src_hbm, dst_hbm, n2_hbm, wg_ref, wu_ref, wd_ref, y_hbm,
                   src_smem, dst_smem, n2_vmem, xbuf, obuf, wg_bf, wu_bf, wd_bf,
                   src_sem, dst_sem, nsem, ssem):
    i = pl.program_id(0)
    nused = nused_ref[0]
    last = nused - 1

    def src_copy(blk, s):
        return pltpu.make_async_copy(src_hbm.at[blk], src_smem.at[pl.ds(s, 1)], src_sem.at[s])

    def dst_copy(blk, s):
        return pltpu.make_async_copy(dst_hbm.at[blk], dst_smem.at[pl.ds(s, 1)], dst_sem.at[s])

    def gather(s):
        for j in range(MOE_BLOCK):
            r = pl.multiple_of(src_smem[s, j], PACK_ROWS)
            xbuf[s, pl.ds(j * PACK_ROWS, PACK_ROWS), :] = n2_vmem[pl.ds(r, PACK_ROWS), :]

    def unpack(s):
        words = jnp.concatenate(
            [xbuf[s, pl.ds(c, MOE_BLOCK, stride=PACK_ROWS), :] for c in range(PACK_ROWS)],
            axis=1)
        lo = pltpu.bitcast(words << 16, F32)
        hi = pltpu.bitcast(words & jnp.uint32(HI_MASK), F32)
        return jnp.concatenate([lo, hi], axis=1).astype(BF16)

    def start_scatter(s):
        for j in range(MOE_BLOCK):
            r = pl.multiple_of(dst_smem[s, j], ROW_TILE)
            pltpu.make_async_copy(obuf.at[s, pl.ds(j * ROW_TILE, ROW_TILE)],
                                  y_hbm.at[pl.ds(r, ROW_TILE)], ssem.at[s]
                                  ).start(priority=j % 2)

    def wait_scatter(s):
        pltpu.make_async_copy(obuf.at[s], y_hbm.at[pl.ds(0, BLOCK_TILE_ROWS)],
                              ssem.at[s]).wait()

    @pl.when(i == 0)
    def _prologue():
        obuf[...] = jnp.zeros_like(obuf)
        pltpu.make_async_copy(obuf.at[0],
                              y_hbm.at[pl.ds(N_ASSIGN * ROW_TILE, BLOCK_TILE_ROWS)],
                              ssem.at[0]).start()
        resident = pltpu.make_async_copy(n2_hbm, n2_vmem, nsem)
        resident.start()
        src_copy(0, 0).start()
        src_copy(0, 0).wait()
        resident.wait()
        gather(0)
        src_copy(jnp.minimum(1, last), 1).start()
        dst_copy(N_BLOCKS, 1).start()

    prev = bexp_ref[jnp.maximum(i - 1, 0)]
    changed = (i == 0) | (bexp_ref[i] != prev)

    @pl.when(changed)
    def _cast_weights():
        wg_bf[...] = wg_ref[0].astype(BF16)
        wu_bf[...] = wu_ref[0].astype(BF16)
        wd_bf[...] = wd_ref[0].astype(BF16)

    def active(slot, nxt):
        src_copy(0, nxt).wait()
        dst_copy(0, nxt).wait()
        wait_scatter(slot)
        start_scatter(nxt)
        gather(nxt)
        src_copy(jnp.minimum(i + 2, last), slot).start()
        dst_copy(i, slot).start()
        xb = unpack(slot)
        hg = jnp.dot(xb, wg_bf[...], preferred_element_type=F32)
        hu = jnp.dot(xb, wu_bf[...], preferred_element_type=F32)
        hh = (hg * _sigmoid(hg) * hu).astype(BF16)
        _store_row_tiles(obuf.at[slot], jnp.dot(hh, wd_bf[...], preferred_element_type=F32),
                         MOE_BLOCK)

    def epilogue(slot, nxt):
        dst_copy(0, slot).wait()
        start_scatter(slot)
        wait_scatter(nxt)
        wait_scatter(slot)
        src_copy(0, slot).wait()

    for parity in range(2):
        @pl.when((i < nused) & (i % 2 == parity))
        def _():
            active(parity, 1 - parity)

        @pl.when((i == last) & (i % 2 == parity))
        def _():
            epilogue(parity, 1 - parity)


_EXPERT_W_BYTES = 3 * D_MODEL * D_EXPERT * 4
EXPERT_VMEM_LIMIT = (
    N_TOK * PACK_COLS * 4
    + 2 * _EXPERT_W_BYTES
    + _EXPERT_W_BYTES // 2
    + 2 * MOE_BLOCK * (PACK_COLS + D_MODEL) * 4
    + 4 * 1024 * 1024)


def _experts(bexp, nused, src2d, dst2d, n2, w_gate, w_up, w_down):
    def w_blk(i, bexp, nused):
        return (bexp[i], 0, 0)

    any_spec = pl.BlockSpec(memory_space=pl.ANY)
    return pl.pallas_call(
        _expert_kernel,
        grid_spec=pltpu.PrefetchScalarGridSpec(
            num_scalar_prefetch=2,
            grid=(N_BLOCKS,),
            in_specs=[
                any_spec, any_spec, any_spec,
                pl.BlockSpec((1, D_MODEL, D_EXPERT), w_blk),
                pl.BlockSpec((1, D_MODEL, D_EXPERT), w_blk),
                pl.BlockSpec((1, D_EXPERT, D_MODEL), w_blk),
            ],
            out_specs=any_spec,
            scratch_shapes=[
                pltpu.SMEM((2, MOE_BLOCK), jnp.int32),
                pltpu.SMEM((2, MOE_BLOCK), jnp.int32),
                pltpu.VMEM((N_TOK * PACK_ROWS, LANES), jnp.uint32),
                pltpu.VMEM((2, MOE_BLOCK * PACK_ROWS, LANES), jnp.uint32),
                pltpu.VMEM((2, BLOCK_TILE_ROWS, LANES), F32),
                pltpu.VMEM((D_MODEL, D_EXPERT), BF16),
                pltpu.VMEM((D_MODEL, D_EXPERT), BF16),
                pltpu.VMEM((D_EXPERT, D_MODEL), BF16),
                pltpu.SemaphoreType.DMA((2,)),
                pltpu.SemaphoreType.DMA((2,)),
                pltpu.SemaphoreType.DMA,
                pltpu.SemaphoreType.DMA((2,)),
            ],
        ),
        out_shape=jax.ShapeDtypeStruct(((N_ASSIGN + DUMMY_ROWS) * ROW_TILE, LANES), F32),
        compiler_params=pltpu.CompilerParams(
            dimension_semantics=("arbitrary",),
            vmem_limit_bytes=EXPERT_VMEM_LIMIT,
        ),
        name="experts",
    )(bexp, nused, src2d, dst2d, n2, w_gate, w_up, w_down)


def _combine_kernel(h1_ref, mf_ref, fg_ref, y0_ref, y1_ref, out_ref):
    w = mf_ref[...]
    h = (h1_ref[...] + _load_row_tiles(y0_ref, TD) * w[:, 0:1]
         + _load_row_tiles(y1_ref, TD) * w[:, 1:2])
    out_ref[...] = _rms(h, fg_ref[...])


def _combine(h1, mf, fg, y):
    n_steps = N_TOK // TD
    tok = lambda i: (i, 0)
    return pl.pallas_call(
        _combine_kernel,
        grid=(n_steps,),
        in_specs=[
            pl.BlockSpec((TD, D_MODEL), tok),
            pl.BlockSpec((TD, TOP_K), tok),
            pl.BlockSpec((1, D_MODEL), lambda i: (0, 0)),
            pl.BlockSpec((TD * ROW_TILE, LANES), tok),
            pl.BlockSpec((TD * ROW_TILE, LANES), lambda i: (i + n_steps, 0)),
        ],
        out_specs=pl.BlockSpec((TD, D_MODEL), tok),
        out_shape=jax.ShapeDtypeStruct((N_TOK, D_MODEL), F32),
        compiler_params=pltpu.CompilerParams(dimension_semantics=("arbitrary",)),
        name="combine",
    )(h1, mf, fg, y, y)


def kernel(x, norm1_gain, w_in, gmlp_v_gain, gmlp_w_s, gmlp_b_s, hgrn_lower_bounds,
           hgrn_out_gain, w_out, norm2_gain, w_group_router, b_group_router,
           w_expert_router, b_expert_router, w_gate, w_up, w_down, final_gain):
    l = 0
    x2 = x.reshape(N_TOK, D_MODEL)
    bst = jnp.tile(jnp.transpose(gmlp_b_s[l]), (TS // GMLP_BLOCK, 1))
    w_router = jnp.concatenate([w_group_router[l], w_expert_router[l]], axis=1)
    w_router = jnp.pad(w_router, ((0, 0), (0, LANES - w_router.shape[1]))).astype(BF16)
    b_router = jnp.concatenate([b_group_router[l], b_expert_router[l]])
    b_router = jnp.pad(b_router, (0, ROUTER_ROWS - b_router.shape[0])).reshape(ROUTER_ROWS, 1)

    h1, n2, meta_i, meta_f, counts = _mixer(
        x2, norm1_gain[l].reshape(1, D_MODEL), w_in[l].astype(BF16),
        gmlp_v_gain[l].reshape(1, D_A), gmlp_w_s[l], bst,
        hgrn_lower_bounds, hgrn_out_gain[l].reshape(1, D_B), w_out[l].astype(BF16),
        norm2_gain[l].reshape(1, D_MODEL), w_router, b_router)

    cnt = counts[EXPERT_ROW0:EXPERT_ROW0 + N_EXPERTS, 0]
    padded = ((cnt + MOE_BLOCK - 1) // MOE_BLOCK) * MOE_BLOCK
    pend = jnp.cumsum(padded)
    pstart = pend - padded
    eid = meta_i[0:TOP_K]
    rank = meta_i[TOP_K:2 * TOP_K]
    base = jnp.sum(jnp.where(eid[:, :, None] == jnp.arange(N_EXPERTS)[None, None, :],
                             pstart[None, None, :], 0), axis=-1)
    dest = (base + rank).astype(jnp.int32).reshape(N_ASSIGN)
    gap_lo = (pstart + cnt).astype(jnp.int32)
    gap_hi = jnp.concatenate([pstart[1:], jnp.full((1,), INV_LEN)]).astype(jnp.int32)
    inv = _invert(dest, gap_lo, gap_hi)
    pos = jnp.arange(INV_LEN, dtype=jnp.int32)
    blk = jnp.where(pos < SORTED_ROWS, pos // MOE_BLOCK, 1)
    dummy = N_ASSIGN + (blk % 2) * MOE_BLOCK + pos % MOE_BLOCK
    src2d = (jnp.where(inv >= 0, inv % N_TOK, 0) * PACK_ROWS)[:SORTED_ROWS].reshape(
        N_BLOCKS, 1, MOE_BLOCK)
    dst2d = (jnp.where(inv >= 0, inv, dummy) * ROW_TILE).reshape(N_BLOCKS + 1, 1, MOE_BLOCK)
    blk_start = jnp.arange(N_BLOCKS, dtype=jnp.int32) * MOE_BLOCK
    bexp = jnp.clip(jnp.sum(blk_start[:, None] >= pend[None, :], axis=1), 0,
                    N_EXPERTS - 1).astype(jnp.int32)
    nused = (pend[-1:] // MOE_BLOCK).astype(jnp.int32)

    y = _experts(bexp, nused, src2d, dst2d, n2, w_gate[l], w_up[l], w_down[l])
    out = _combine(h1, meta_f[0:TOP_K].T, final_gain.reshape(1, D_MODEL), y)
    return out.reshape(BATCH, SEQ, D_MODEL)
```

```python
import jax
import jax.numpy as jnp
from jax import lax
from jax.experimental import pallas as pl
from jax.experimental.pallas import tpu as pltpu

D_MODEL = 1024
BATCH = 2
SEQ = 8192
N_TOK = BATCH * SEQ
CHUNK = 64
EPS = 1e-6
D_A = 512
HEADS = 4
HEAD_DIM = 128
GMLP_BLOCK = 128
D_B = 512
D_IN = 3072
N_GROUPS = 4
EXPERTS_PER_GROUP = 8
N_EXPERTS = 32
TOP_K = 2
D_EXPERT = 512

LANES = 128
ROUTER_ROWS = 48
EXPERT_ROW0 = N_GROUPS
META_ROWS = 8
ROW_TILE = D_MODEL // LANES
PACK_COLS = D_MODEL // 2
PACK_ROWS = PACK_COLS // LANES
HI_MASK = 0xFFFF0000

TS = 256
CHUNKS_PER_STEP = TS // CHUNK
MOE_BLOCK = 256
N_BLOCKS = N_TOK * TOP_K // MOE_BLOCK + N_EXPERTS
SORTED_ROWS = N_BLOCKS * MOE_BLOCK
N_ASSIGN = N_TOK * TOP_K
N_SLOTS = 3
DUMMY_ROWS = N_SLOTS * MOE_BLOCK
BLOCK_TILE_ROWS = MOE_BLOCK * ROW_TILE
INV_LEN = SORTED_ROWS + MOE_BLOCK
TD = 512

F32 = jnp.float32
BF16 = jnp.bfloat16
NT_DIMS = (((1,), (1,)), ((), ()))
TN_DIMS = (((0,), (0,)), ((), ()))


def _sigmoid(x):
    return 1.0 / (1.0 + jnp.exp(-x))


def _gelu(x):
    return 0.5 * x * (1.0 + lax.erf(x * (2.0 ** -0.5)))


def _rms(x, gain):
    return x * lax.rsqrt(jnp.mean(x * x, axis=-1, keepdims=True) + EPS) * gain


def _store_row_tiles(ref, x, rows):
    for c in range(ROW_TILE):
        ref[pl.ds(c, rows, stride=ROW_TILE), :] = x[:, c * LANES:(c + 1) * LANES]


def _load_row_tiles(ref, rows):
    return jnp.concatenate(
        [ref[pl.ds(c, rows, stride=ROW_TILE), :] for c in range(ROW_TILE)], axis=1)


def _split_bf16(x):
    hi = x.astype(BF16)
    lo = (x - hi.astype(F32)).astype(BF16)
    return hi, lo


def _mixer_kernel(x_ref, g1_ref, win_ref, vg_ref, ws_ref, bst_ref, lbraw_ref, og_ref,
                  wout_ref, g2_ref, wr_ref, br_ref,
                  h1_ref, n2_ref, mi_ref, mf_ref, cnt_ref,
                  st_ref, carry_ref, wm_ref, mix_ref):
    i = pl.program_id(0)

    @pl.when(i == 0)
    def _init():
        carry_ref[...] = jnp.zeros_like(carry_ref)
        r = lax.broadcasted_iota(jnp.int32, (GMLP_BLOCK, GMLP_BLOCK), 0)
        c = lax.broadcasted_iota(jnp.int32, (GMLP_BLOCK, GMLP_BLOCK), 1)
        keep = (c // CHUNK) <= (r // CHUNK)
        wm_ref[...] = jnp.zeros_like(wm_ref)
        for g in range(HEADS):
            w = jnp.where(keep, ws_ref[g], 0.0).astype(BF16)
            for p in range(TS // GMLP_BLOCK):
                wm_ref[g, p * GMLP_BLOCK:(p + 1) * GMLP_BLOCK,
                       p * GMLP_BLOCK:(p + 1) * GMLP_BLOCK] = w

    @pl.when(i % (SEQ // TS) == 0)
    def _reset_state():
        st_ref[...] = jnp.zeros_like(st_ref)

    x = x_ref[...]
    n1 = _rms(x, g1_ref[...]).astype(BF16)

    gu = _gelu(jnp.dot(n1, win_ref[:, 0:D_A], preferred_element_type=F32))
    gv = _gelu(jnp.dot(n1, win_ref[:, D_A:2 * D_A], preferred_element_type=F32))
    for g in range(HEADS):
        sl = slice(g * HEAD_DIM, (g + 1) * HEAD_DIM)
        vh = _rms(gv[:, sl], vg_ref[:, sl]).astype(BF16)
        mixed = jnp.dot(wm_ref[g], vh, preferred_element_type=F32) + bst_ref[:, g:g + 1]
        mix_ref[:, sl] = (gu[:, sl] * mixed).astype(BF16)

    o2 = 2 * D_A
    qr = jnp.dot(n1, win_ref[:, o2:o2 + D_B], preferred_element_type=F32)
    fr = jnp.dot(n1, win_ref[:, o2 + D_B:o2 + 2 * D_B], preferred_element_type=F32)
    vv = jnp.dot(n1, win_ref[:, o2 + 2 * D_B:o2 + 3 * D_B], preferred_element_type=F32).astype(BF16)
    gr = jnp.dot(n1, win_ref[:, o2 + 3 * D_B:o2 + 4 * D_B], preferred_element_type=F32)

    lbr = lbraw_ref[...]
    lbm = jnp.max(lbr, axis=0, keepdims=True)
    lbe = jnp.exp(lbr - lbm)
    lb = lbe[0:1, :] / jnp.sum(lbe, axis=0, keepdims=True)

    q = qr * _sigmoid(qr)
    f = lb + (1.0 - lb) * _sigmoid(fr)
    k = 1.0 - f
    lf = jnp.log(f)

    row = lax.broadcasted_iota(jnp.int32, (TS, TS), 0)
    col = lax.broadcasted_iota(jnp.int32, (TS, TS), 1)
    causal = (row >= col) & ((row // CHUNK) == (col // CHUNK))
    tri = jnp.where(causal, 1.0, 0.0).astype(BF16)
    lf_hi, lf_lo = _split_bf16(lf)
    b = (jnp.dot(tri, lf_hi, preferred_element_type=F32)
         + jnp.dot(tri, lf_lo, preferred_element_type=F32))

    def chunk_rows(r):
        return jnp.concatenate(
            [jnp.broadcast_to(b[c * CHUNK + r:c * CHUNK + r + 1, :], (CHUNK, D_B))
             for c in range(CHUNKS_PER_STEP)], axis=0)

    bref = chunk_rows(CHUNK // 2 - 1)
    blast = chunk_rows(CHUNK - 1)
    qe = (q * jnp.exp(b - bref)).astype(BF16)
    ke = (k * jnp.exp(bref - b)).astype(BF16)
    kd = (k * jnp.exp(blast - b)).astype(BF16)
    qb = (q * jnp.exp(b)).astype(BF16)

    for h in range(HEADS):
        sl = slice(h * HEAD_DIM, (h + 1) * HEAD_DIM)
        sc = lax.dot_general(qe[:, sl], ke[:, sl], NT_DIMS, preferred_element_type=F32)
        sc = jnp.where(causal, sc, 0.0).astype(BF16)
        o = jnp.dot(sc, vv[:, sl], preferred_element_type=F32)
        st = st_ref[h]
        inter = []
        for c in range(CHUNKS_PER_STEP):
            rows = slice(c * CHUNK, (c + 1) * CHUNK)
            inter.append(lax.dot_general(qb[rows, sl], st.astype(BF16), NT_DIMS,
                                         preferred_element_type=F32))
            upd = lax.dot_general(vv[rows, sl], kd[rows, sl], TN_DIMS,
                                  preferred_element_type=F32)
            decay = jnp.exp(b[(c + 1) * CHUNK - 1:(c + 1) * CHUNK, sl])
            st = st * decay + upd
        st_ref[h] = st
        o = o + jnp.concatenate(inter, axis=0)
        o = _rms(o, og_ref[:, sl])
        g_h = gr[:, sl]
        mix_ref[:, D_A + h * HEAD_DIM:D_A + (h + 1) * HEAD_DIM] = (
            o * (g_h * _sigmoid(g_h))).astype(BF16)

    h1 = x + jnp.dot(mix_ref[...], wout_ref[...], preferred_element_type=F32)
    h1_ref[...] = h1
    n2 = _rms(h1, g2_ref[...])
    lo = pltpu.bitcast(n2[:, 0:PACK_COLS].astype(BF16).astype(F32), jnp.uint32) >> 16
    hi = (pltpu.bitcast(n2[:, PACK_COLS:D_MODEL].astype(BF16).astype(F32), jnp.uint32)
          & jnp.uint32(HI_MASK))
    packed = lo | hi
    for c in range(PACK_ROWS):
        n2_ref[pl.ds(c, TS, stride=PACK_ROWS), :] = packed[:, c * LANES:(c + 1) * LANES]

    logits = jnp.dot(n2.astype(BF16), wr_ref[...], preferred_element_type=F32)
    logits = jnp.transpose(logits)[0:ROUTER_ROWS, :] + br_ref[...]
    rid = lax.broadcasted_iota(jnp.int32, (ROUTER_ROWS, TS), 0)
    neg = -jnp.inf
    gl = jnp.where(rid < N_GROUPS, logits, neg)
    gmax = jnp.max(gl, axis=0, keepdims=True)
    g_idx = jnp.min(jnp.where(gl == gmax, rid, ROUTER_ROWS), axis=0, keepdims=True)
    g_prob = 1.0 / jnp.sum(jnp.exp(gl - gmax), axis=0, keepdims=True)
    e_lo = EXPERT_ROW0 + g_idx * EXPERTS_PER_GROUP
    el = jnp.where((rid >= e_lo) & (rid < e_lo + EXPERTS_PER_GROUP), logits, neg)
    m1 = jnp.max(el, axis=0, keepdims=True)
    i1 = jnp.min(jnp.where(el == m1, rid, ROUTER_ROWS), axis=0, keepdims=True)
    el2 = jnp.where(rid == i1, neg, el)
    m2 = jnp.max(el2, axis=0, keepdims=True)
    i2 = jnp.min(jnp.where(el2 == m2, rid, ROUTER_ROWS), axis=0, keepdims=True)
    e21 = jnp.exp(m2 - m1)
    w1 = g_prob / (1.0 + e21)
    w2 = g_prob * e21 / (1.0 + e21)

    hit1 = rid == i1
    hit2 = rid == i2
    onehot = jnp.where(hit1 | hit2, 1.0, 0.0)
    earlier = jnp.where(row < col, 1.0, 0.0).astype(BF16)
    seen = carry_ref[...]
    prior = (jnp.dot(onehot.astype(BF16), earlier, preferred_element_type=F32)
             + jnp.concatenate([seen] * (TS // LANES), axis=1))
    r1 = jnp.sum(jnp.where(hit1, prior, 0.0), axis=0, keepdims=True)
    r2 = jnp.sum(jnp.where(hit2, prior, 0.0), axis=0, keepdims=True)
    seen = seen + jnp.sum(onehot, axis=1, keepdims=True)
    carry_ref[...] = seen
    cnt_ref[...] = seen.astype(jnp.int32)

    pad_i = jnp.zeros((META_ROWS - 2 * TOP_K, TS), jnp.int32)
    mi_ref[...] = jnp.concatenate(
        [i1 - EXPERT_ROW0, i2 - EXPERT_ROW0, r1.astype(jnp.int32), r2.astype(jnp.int32), pad_i],
        axis=0)
    mf_ref[...] = jnp.concatenate([w1, w2, jnp.zeros((META_ROWS - TOP_K, TS), F32)], axis=0)


def _mixer(x2, g1, win, vg, ws, bst, lbraw, og, wout, g2, wr, br):
    n_steps = N_TOK // TS
    const2 = lambda i: (0, 0)
    tok = lambda i: (i, 0)
    lane_tok = lambda i: (0, i)
    return pl.pallas_call(
        _mixer_kernel,
        grid=(n_steps,),
        in_specs=[
            pl.BlockSpec((TS, D_MODEL), tok),
            pl.BlockSpec((1, D_MODEL), const2),
            pl.BlockSpec((D_MODEL, D_IN), const2),
            pl.BlockSpec((1, D_A), const2),
            pl.BlockSpec((HEADS, GMLP_BLOCK, GMLP_BLOCK), lambda i: (0, 0, 0)),
            pl.BlockSpec((TS, HEADS), const2),
            pl.BlockSpec((2, D_B), const2),
            pl.BlockSpec((1, D_B), const2),
            pl.BlockSpec((D_MODEL, D_MODEL), const2),
            pl.BlockSpec((1, D_MODEL), const2),
            pl.BlockSpec((D_MODEL, LANES), const2),
            pl.BlockSpec((ROUTER_ROWS, 1), const2),
        ],
        out_specs=[
            pl.BlockSpec((TS, D_MODEL), tok),
            pl.BlockSpec((TS * PACK_ROWS, LANES), tok),
            pl.BlockSpec((META_ROWS, TS), lane_tok),
            pl.BlockSpec((META_ROWS, TS), lane_tok),
            pl.BlockSpec((ROUTER_ROWS, LANES), const2),
        ],
        out_shape=[
            jax.ShapeDtypeStruct((N_TOK, D_MODEL), F32),
            jax.ShapeDtypeStruct((N_TOK * PACK_ROWS, LANES), jnp.uint32),
            jax.ShapeDtypeStruct((META_ROWS, N_TOK), jnp.int32),
            jax.ShapeDtypeStruct((META_ROWS, N_TOK), F32),
            jax.ShapeDtypeStruct((ROUTER_ROWS, LANES), jnp.int32),
        ],
        scratch_shapes=[
            pltpu.VMEM((HEADS, HEAD_DIM, HEAD_DIM), F32),
            pltpu.VMEM((ROUTER_ROWS, LANES), F32),
            pltpu.VMEM((HEADS, TS, TS), BF16),
            pltpu.VMEM((TS, D_MODEL), BF16),
        ],
        compiler_params=pltpu.CompilerParams(
            dimension_semantics=("arbitrary",),
            vmem_limit_bytes=56 * 1024 * 1024,
        ),
        name="mixer",
    )(x2, g1, win, vg, ws, bst, lbraw, og, wout, g2, wr, br)


FILL_UNROLL = 8


def _invert_kernel(dest_ref, gap_lo_ref, gap_hi_ref, inv_ref):
    for e in range(N_EXPERTS):
        first = gap_lo_ref[e] // FILL_UNROLL

        def fill(g, carry):
            for u in range(FILL_UNROLL):
                inv_ref[g * FILL_UNROLL + u] = -1
            return carry

        lax.fori_loop(first, gap_hi_ref[e] // FILL_UNROLL, fill, 0)

    def place(a, carry):
        inv_ref[dest_ref[a]] = a
        return carry

    lax.fori_loop(0, N_ASSIGN, place, 0, unroll=16)


def _invert(dest, gap_lo, gap_hi):
    smem = pl.BlockSpec(memory_space=pltpu.SMEM)
    return pl.pallas_call(
        _invert_kernel,
        in_specs=[smem, smem, smem],
        out_specs=smem,
        out_shape=jax.ShapeDtypeStruct((INV_LEN,), jnp.int32),
        name="invert",
    )(dest, gap_lo, gap_hi)


def _expert_kernel(bexp_ref, nused_ref, src_hbm, dst_hbm, n2_hbm, wg_ref, wu_ref, wd_ref, y_hbm,
                   src_smem, dst_smem, n2_vmem, xbuf, obuf, wg_bf, wu_bf, wd_bf,
                   src_sem, dst_sem, nsem, ssem):
    i = pl.program_id(0)
    nused = nused_ref[0]
    last = nused - 1

    def src_copy(blk, s):
        return pltpu.make_async_copy(src_hbm.at[blk], src_smem.at[pl.ds(s, 1)], src_sem.at[s])

    def dst_copy(blk, s):
        return pltpu.make_async_copy(dst_hbm.at[blk], dst_smem.at[pl.ds(s, 1)], dst_sem.at[s])

    def gather(s):
        for j in range(MOE_BLOCK):
            r = pl.multiple_of(src_smem[s, j], PACK_ROWS)
            xbuf[s, pl.ds(j * PACK_ROWS, PACK_ROWS), :] = n2_vmem[pl.ds(r, PACK_ROWS), :]

    def unpack(s):
        words = jnp.concatenate(
            [xbuf[s, pl.ds(c, MOE_BLOCK, stride=PACK_ROWS), :] for c in range(PACK_ROWS)],
            axis=1)
        lo = pltpu.bitcast(words << 16, F32)
        hi = pltpu.bitcast(words & jnp.uint32(HI_MASK), F32)
        return jnp.concatenate([lo, hi], axis=1).astype(BF16)

    def start_scatter(s):
        for j in range(MOE_BLOCK):
            r = pl.multiple_of(dst_smem[s, j], ROW_TILE)
            pltpu.make_async_copy(obuf.at[s, pl.ds(j * ROW_TILE, ROW_TILE)],
                                  y_hbm.at[pl.ds(r, ROW_TILE)], ssem.at[s]
                                  ).start(priority=j % 2)

    def wait_scatter(s):
        pltpu.make_async_copy(obuf.at[s], y_hbm.at[pl.ds(0, BLOCK_TILE_ROWS)],
                              ssem.at[s]).wait()

    @pl.when(i == 0)
    def _prologue():
        obuf[...] = jnp.zeros_like(obuf)
        for s in range(N_SLOTS - 1):
            pltpu.make_async_copy(
                obuf.at[s],
                y_hbm.at[pl.ds((N_ASSIGN + s * MOE_BLOCK) * ROW_TILE, BLOCK_TILE_ROWS)],
                ssem.at[s]).start()
        resident = pltpu.make_async_copy(n2_hbm, n2_vmem, nsem)
        resident.start()
        src_copy(0, 0).start()
        src_copy(0, 0).wait()
        resident.wait()
        gather(0)
        src_copy(jnp.minimum(1, last), 1).start()
        dst_copy(N_BLOCKS, N_SLOTS - 1).start()

    prev = bexp_ref[jnp.maximum(i - 1, 0)]
    changed = (i == 0) | (bexp_ref[i] != prev)

    @pl.when(changed)
    def _cast_weights():
        wg_bf[...] = wg_ref[0].astype(BF16)
        wu_bf[...] = wu_ref[0].astype(BF16)
        wd_bf[...] = wd_ref[0].astype(BF16)

    def active(cur, nxt, prv):
        src_copy(0, nxt).wait()
        dst_copy(0, prv).wait()
        wait_scatter(cur)
        start_scatter(prv)
        gather(nxt)
        src_copy(jnp.minimum(i + 2, last), prv).start()
        dst_copy(i, cur).start()
        xb = unpack(cur)
        hg = jnp.dot(xb, wg_bf[...], preferred_element_type=F32)
        hu = jnp.dot(xb, wu_bf[...], preferred_element_type=F32)
        hh = (hg * _sigmoid(hg) * hu).astype(BF16)
        _store_row_tiles(obuf.at[cur], jnp.dot(hh, wd_bf[...], preferred_element_type=F32),
                         MOE_BLOCK)

    def epilogue(cur, nxt, prv):
        dst_copy(0, cur).wait()
        start_scatter(cur)
        wait_scatter(nxt)
        wait_scatter(prv)
        wait_scatter(cur)
        src_copy(0, prv).wait()

    for cur in range(N_SLOTS):
        slots = (cur, (cur + 1) % N_SLOTS, (cur + 2) % N_SLOTS)

        @pl.when((i < nused) & (i % N_SLOTS == cur))
        def _():
            active(*slots)

        @pl.when((i == last) & (i % N_SLOTS == cur))
        def _():
            epilogue(*slots)


_EXPERT_W_BYTES = 3 * D_MODEL * D_EXPERT * 4
EXPERT_VMEM_LIMIT = (
    N_TOK * PACK_COLS * 4
    + 2 * _EXPERT_W_BYTES
    + _EXPERT_W_BYTES // 2
    + N_SLOTS * MOE_BLOCK * (PACK_COLS + D_MODEL) * 4
    + 4 * 1024 * 1024)


def _experts(bexp, nused, src2d, dst2d, n2, w_gate, w_up, w_down):
    def w_blk(i, bexp, nused):
        return (bexp[i], 0, 0)

    any_spec = pl.BlockSpec(memory_space=pl.ANY)
    return pl.pallas_call(
        _expert_kernel,
        grid_spec=pltpu.PrefetchScalarGridSpec(
            num_scalar_prefetch=2,
            grid=(N_BLOCKS,),
            in_specs=[
                any_spec, any_spec, any_spec,
                pl.BlockSpec((1, D_MODEL, D_EXPERT), w_blk),
                pl.BlockSpec((1, D_MODEL, D_EXPERT), w_blk),
                pl.BlockSpec((1, D_EXPERT, D_MODEL), w_blk),
            ],
            out_specs=any_spec,
            scratch_shapes=[
                pltpu.SMEM((N_SLOTS, MOE_BLOCK), jnp.int32),
                pltpu.SMEM((N_SLOTS, MOE_BLOCK), jnp.int32),
                pltpu.VMEM((N_TOK * PACK_ROWS, LANES), jnp.uint32),
                pltpu.VMEM((N_SLOTS, MOE_BLOCK * PACK_ROWS, LANES), jnp.uint32),
                pltpu.VMEM((N_SLOTS, BLOCK_TILE_ROWS, LANES), F32),
                pltpu.VMEM((D_MODEL, D_EXPERT), BF16),
                pltpu.VMEM((D_MODEL, D_EXPERT), BF16),
                pltpu.VMEM((D_EXPERT, D_MODEL), BF16),
                pltpu.SemaphoreType.DMA((N_SLOTS,)),
                pltpu.SemaphoreType.DMA((N_SLOTS,)),
                pltpu.SemaphoreType.DMA,
                pltpu.SemaphoreType.DMA((N_SLOTS,)),
            ],
        ),
        out_shape=jax.ShapeDtypeStruct(((N_ASSIGN + DUMMY_ROWS) * ROW_TILE, LANES), F32),
        compiler_params=pltpu.CompilerParams(
            dimension_semantics=("arbitrary",),
            vmem_limit_bytes=EXPERT_VMEM_LIMIT,
        ),
        name="experts",
    )(bexp, nused, src2d, dst2d, n2, w_gate, w_up, w_down)


def _combine_kernel(h1_ref, mf_ref, fg_ref, y0_ref, y1_ref, out_ref):
    w = mf_ref[...]
    h = (h1_ref[...] + _load_row_tiles(y0_ref, TD) * w[:, 0:1]
         + _load_row_tiles(y1_ref, TD) * w[:, 1:2])
    out_ref[...] = _rms(h, fg_ref[...])


def _combine(h1, mf, fg, y):
    n_steps = N_TOK // TD
    tok = lambda i: (i, 0)
    return pl.pallas_call(
        _combine_kernel,
        grid=(n_steps,),
        in_specs=[
            pl.BlockSpec((TD, D_MODEL), tok),
            pl.BlockSpec((TD, TOP_K), tok),
            pl.BlockSpec((1, D_MODEL), lambda i: (0, 0)),
            pl.BlockSpec((TD * ROW_TILE, LANES), tok),
            pl.BlockSpec((TD * ROW_TILE, LANES), lambda i: (i + n_steps, 0)),
        ],
        out_specs=pl.BlockSpec((TD, D_MODEL), tok),
        out_shape=jax.ShapeDtypeStruct((N_TOK, D_MODEL), F32),
        compiler_params=pltpu.CompilerParams(dimension_semantics=("arbitrary",)),
        name="combine",
    )(h1, mf, fg, y, y)


def kernel(x, norm1_gain, w_in, gmlp_v_gain, gmlp_w_s, gmlp_b_s, hgrn_lower_bounds,
           hgrn_out_gain, w_out, norm2_gain, w_group_router, b_group_router,
           w_expert_router, b_expert_router, w_gate, w_up, w_down, final_gain):
    l = 0
    x2 = x.reshape(N_TOK, D_MODEL)
    bst = jnp.tile(jnp.transpose(gmlp_b_s[l]), (TS // GMLP_BLOCK, 1))
    w_router = jnp.concatenate([w_group_router[l], w_expert_router[l]], axis=1)
    w_router = jnp.pad(w_router, ((0, 0), (0, LANES - w_router.shape[1]))).astype(BF16)
    b_router = jnp.concatenate([b_group_router[l], b_expert_router[l]])
    b_router = jnp.pad(b_router, (0, ROUTER_ROWS - b_router.shape[0])).reshape(ROUTER_ROWS, 1)

    h1, n2, meta_i, meta_f, counts = _mixer(
        x2, norm1_gain[l].reshape(1, D_MODEL), w_in[l].astype(BF16),
        gmlp_v_gain[l].reshape(1, D_A), gmlp_w_s[l], bst,
        hgrn_lower_bounds, hgrn_out_gain[l].reshape(1, D_B), w_out[l].astype(BF16),
        norm2_gain[l].reshape(1, D_MODEL), w_router, b_router)

    cnt = counts[EXPERT_ROW0:EXPERT_ROW0 + N_EXPERTS, 0]
    padded = ((cnt + MOE_BLOCK - 1) // MOE_BLOCK) * MOE_BLOCK
    pend = jnp.cumsum(padded)
    pstart = pend - padded
    eid = meta_i[0:TOP_K]
    rank = meta_i[TOP_K:2 * TOP_K]
    base = jnp.sum(jnp.where(eid[:, :, None] == jnp.arange(N_EXPERTS)[None, None, :],
                             pstart[None, None, :], 0), axis=-1)
    dest = (base + rank).astype(jnp.int32).reshape(N_ASSIGN)
    gap_lo = (pstart + cnt).astype(jnp.int32)
    gap_hi = jnp.concatenate([pstart[1:], jnp.full((1,), INV_LEN)]).astype(jnp.int32)
    inv = _invert(dest, gap_lo, gap_hi)
    pos = jnp.arange(INV_LEN, dtype=jnp.int32)
    blk = jnp.where(pos < SORTED_ROWS, pos // MOE_BLOCK, N_SLOTS - 1)
    dummy = N_ASSIGN + (blk % N_SLOTS) * MOE_BLOCK + pos % MOE_BLOCK
    src2d = (jnp.where(inv >= 0, inv % N_TOK, 0) * PACK_ROWS)[:SORTED_ROWS].reshape(
        N_BLOCKS, 1, MOE_BLOCK)
    dst2d = (jnp.where(inv >= 0, inv, dummy) * ROW_TILE).reshape(N_BLOCKS + 1, 1, MOE_BLOCK)
    blk_start = jnp.arange(N_BLOCKS, dtype=jnp.int32) * MOE_BLOCK
    bexp = jnp.clip(jnp.sum(blk_start[:, None] >= pend[None, :], axis=1), 0,
                    N_EXPERTS - 1).astype(jnp.int32)
    nused = (pend[-1:] // MOE_BLOCK).astype(jnp.int32)

    y = _experts(bexp, nused, src2d, dst2d, n2, w_gate[l], w_up[l], w_down[l])
    out = _combine(h1, meta_f[0:TOP_K].T, final_gain.reshape(1, D_MODEL), y)
    return out.reshape(BATCH, SEQ, D_MODEL)
```

```python
import functools

import jax
import jax.numpy as jnp
from jax import lax
from jax.experimental import pallas as pl
from jax.experimental.pallas import tpu as pltpu

D_MODEL = 1024
BATCH = 2
SEQ = 8192
N_TOK = BATCH * SEQ
CHUNK = 64
EPS = 1e-6
D_A = 512
HEADS = 4
HEAD_DIM = 128
GMLP_BLOCK = 128
D_B = 512
D_IN = 3072
N_GROUPS = 4
EXPERTS_PER_GROUP = 8
N_EXPERTS = 32
TOP_K = 2
D_EXPERT = 512

LANES = 128
ROUTER_ROWS = 48
EXPERT_ROW0 = N_GROUPS
META_ROWS = 8
ROW_TILE = D_MODEL // LANES
PACK_COLS = D_MODEL // 2
PACK_ROWS = PACK_COLS // LANES
HI_MASK = 0xFFFF0000

TS = 256
CHUNKS_PER_STEP = TS // CHUNK
MOE_BLOCK = 256
N_BLOCKS = N_TOK * TOP_K // MOE_BLOCK + N_EXPERTS
SORTED_ROWS = N_BLOCKS * MOE_BLOCK
N_ASSIGN = N_TOK * TOP_K
N_SLOTS = 3
DUMMY_ROWS = N_SLOTS * MOE_BLOCK
BLOCK_TILE_ROWS = MOE_BLOCK * ROW_TILE
INV_LEN = SORTED_ROWS + MOE_BLOCK
TD = 512

F32 = jnp.float32
BF16 = jnp.bfloat16
NT_DIMS = (((1,), (1,)), ((), ()))
TN_DIMS = (((0,), (0,)), ((), ()))


def _sigmoid(x):
    return 1.0 / (1.0 + jnp.exp(-x))


def _gelu(x):
    return 0.5 * x * (1.0 + lax.erf(x * (2.0 ** -0.5)))


def _rms(x, gain):
    return x * lax.rsqrt(jnp.mean(x * x, axis=-1, keepdims=True) + EPS) * gain


def _store_row_tiles(ref, x, rows):
    for c in range(ROW_TILE):
        ref[pl.ds(c, rows, stride=ROW_TILE), :] = x[:, c * LANES:(c + 1) * LANES]


def _load_row_tiles(ref, rows):
    return jnp.concatenate(
        [ref[pl.ds(c, rows, stride=ROW_TILE), :] for c in range(ROW_TILE)], axis=1)


def _split_bf16(x):
    hi = x.astype(BF16)
    lo = (x - hi.astype(F32)).astype(BF16)
    return hi, lo


def _mixer_kernel(*refs):
    for parity in range(2):
        pl.when(pl.program_id(0) % 2 == parity)(functools.partial(_mixer_step, parity, *refs))


def _mixer_step(cur, x_ref, xn_ref, g1_ref, win_ref, vg_ref, ws_ref, bst_ref, lbraw_ref, og_ref,
                wout_ref, g2_ref, wr_ref, br_ref,
                h1_ref, n2_ref, mi_ref, mf_ref, cnt_ref,
                st_ref, carry_ref, wm_ref, mix_ref, proj_ref):
    i = pl.program_id(0)

    @pl.when(i == 0)
    def _init():
        carry_ref[...] = jnp.zeros_like(carry_ref)
        r = lax.broadcasted_iota(jnp.int32, (GMLP_BLOCK, GMLP_BLOCK), 0)
        c = lax.broadcasted_iota(jnp.int32, (GMLP_BLOCK, GMLP_BLOCK), 1)
        keep = (c // CHUNK) <= (r // CHUNK)
        wm_ref[...] = jnp.zeros_like(wm_ref)
        for g in range(HEADS):
            w = jnp.where(keep, ws_ref[g], 0.0).astype(BF16)
            for p in range(TS // GMLP_BLOCK):
                wm_ref[g, p * GMLP_BLOCK:(p + 1) * GMLP_BLOCK,
                       p * GMLP_BLOCK:(p + 1) * GMLP_BLOCK] = w

    @pl.when(i % (SEQ // TS) == 0)
    def _reset_state():
        st_ref[...] = jnp.zeros_like(st_ref)

    def project(n1, slot, c):
        proj_ref[slot, :, c * D_A:(c + 1) * D_A] = jnp.dot(
            n1, win_ref[:, c * D_A:(c + 1) * D_A], preferred_element_type=F32)

    @pl.when(i == 0)
    def _first_projection():
        n1 = _rms(x_ref[...], g1_ref[...]).astype(BF16)
        for c in range(D_IN // D_A):
            project(n1, 0, c)

    n1_next = _rms(xn_ref[...], g1_ref[...]).astype(BF16)
    x = x_ref[...]

    project(n1_next, 1 - cur, 0)
    gu = _gelu(proj_ref[cur, :, 0:D_A])
    project(n1_next, 1 - cur, 1)
    gv = _gelu(proj_ref[cur, :, D_A:2 * D_A])
    for g in range(HEADS):
        sl = slice(g * HEAD_DIM, (g + 1) * HEAD_DIM)
        vh = _rms(gv[:, sl], vg_ref[:, sl]).astype(BF16)
        mixed = jnp.dot(wm_ref[g], vh, preferred_element_type=F32) + bst_ref[:, g:g + 1]
        mix_ref[:, sl] = (gu[:, sl] * mixed).astype(BF16)

    o2 = 2 * D_A
    qr = proj_ref[cur, :, o2:o2 + D_B]
    fr = proj_ref[cur, :, o2 + D_B:o2 + 2 * D_B]
    vv = proj_ref[cur, :, o2 + 2 * D_B:o2 + 3 * D_B].astype(BF16)
    gr = proj_ref[cur, :, o2 + 3 * D_B:o2 + 4 * D_B]

    lbr = lbraw_ref[...]
    lbm = jnp.max(lbr, axis=0, keepdims=True)
    lbe = jnp.exp(lbr - lbm)
    lb = lbe[0:1, :] / jnp.sum(lbe, axis=0, keepdims=True)

    project(n1_next, 1 - cur, 2)
    q = qr * _sigmoid(qr)
    f = lb + (1.0 - lb) * _sigmoid(fr)
    k = 1.0 - f
    lf = jnp.log(f)

    row = lax.broadcasted_iota(jnp.int32, (TS, TS), 0)
    col = lax.broadcasted_iota(jnp.int32, (TS, TS), 1)
    causal = (row >= col) & ((row // CHUNK) == (col // CHUNK))
    tri = jnp.where(causal, 1.0, 0.0).astype(BF16)
    lf_hi, lf_lo = _split_bf16(lf)
    b = (jnp.dot(tri, lf_hi, preferred_element_type=F32)
         + jnp.dot(tri, lf_lo, preferred_element_type=F32))

    def chunk_rows(r):
        return jnp.concatenate(
            [jnp.broadcast_to(b[c * CHUNK + r:c * CHUNK + r + 1, :], (CHUNK, D_B))
             for c in range(CHUNKS_PER_STEP)], axis=0)

    project(n1_next, 1 - cur, 3)
    bref = chunk_rows(CHUNK // 2 - 1)
    blast = chunk_rows(CHUNK - 1)
    qe = (q * jnp.exp(b - bref)).astype(BF16)
    ke = (k * jnp.exp(bref - b)).astype(BF16)
    kd = (k * jnp.exp(blast - b)).astype(BF16)
    qb = (q * jnp.exp(b)).astype(BF16)

    head_sl = [slice(h * HEAD_DIM, (h + 1) * HEAD_DIM) for h in range(HEADS)]
    chunk_sl = [slice(c * CHUNK, (c + 1) * CHUNK) for c in range(CHUNKS_PER_STEP)]
    scores = [lax.dot_general(qe[:, sl], ke[:, sl], NT_DIMS, preferred_element_type=F32)
              for sl in head_sl]
    upd = [[lax.dot_general(vv[rows, sl], kd[rows, sl], TN_DIMS, preferred_element_type=F32)
            for rows in chunk_sl] for sl in head_sl]
    project(n1_next, 1 - cur, 4)
    scores = [jnp.where(causal, sc, 0.0).astype(BF16) for sc in scores]
    intra = [jnp.dot(scores[h], vv[:, head_sl[h]], preferred_element_type=F32)
             for h in range(HEADS)]
    entering = []
    for h in range(HEADS):
        st = st_ref[h]
        per_chunk = []
        for c in range(CHUNKS_PER_STEP):
            per_chunk.append(st.astype(BF16))
            decay = jnp.exp(b[(c + 1) * CHUNK - 1:(c + 1) * CHUNK, head_sl[h]])
            st = st * decay + upd[h][c]
        st_ref[h] = st
        entering.append(per_chunk)
    inter = [jnp.concatenate(
        [lax.dot_general(qb[chunk_sl[c], head_sl[h]], entering[h][c], NT_DIMS,
                         preferred_element_type=F32) for c in range(CHUNKS_PER_STEP)], axis=0)
        for h in range(HEADS)]
    for h in range(HEADS):
        o = _rms(intra[h] + inter[h], og_ref[:, head_sl[h]])
        g_h = gr[:, head_sl[h]]
        mix_ref[:, D_A + h * HEAD_DIM:D_A + (h + 1) * HEAD_DIM] = (
            o * (g_h * _sigmoid(g_h))).astype(BF16)

    h1 = x + jnp.dot(mix_ref[...], wout_ref[...], preferred_element_type=F32)
    project(n1_next, 1 - cur, 5)
    h1_ref[...] = h1
    n2 = _rms(h1, g2_ref[...])
    lo = pltpu.bitcast(n2[:, 0:PACK_COLS].astype(BF16).astype(F32), jnp.uint32) >> 16
    hi = (pltpu.bitcast(n2[:, PACK_COLS:D_MODEL].astype(BF16).astype(F32), jnp.uint32)
          & jnp.uint32(HI_MASK))
    packed = lo | hi
    for c in range(PACK_ROWS):
        n2_ref[pl.ds(c, TS, stride=PACK_ROWS), :] = packed[:, c * LANES:(c + 1) * LANES]

    logits = jnp.dot(n2.astype(BF16), wr_ref[...], preferred_element_type=F32)
    logits = jnp.transpose(logits)[0:ROUTER_ROWS, :] + br_ref[...]
    rid = lax.broadcasted_iota(jnp.int32, (ROUTER_ROWS, TS), 0)
    neg = -jnp.inf
    gl = jnp.where(rid < N_GROUPS, logits, neg)
    gmax = jnp.max(gl, axis=0, keepdims=True)
    g_idx = jnp.min(jnp.where(gl == gmax, rid, ROUTER_ROWS), axis=0, keepdims=True)
    g_prob = 1.0 / jnp.sum(jnp.exp(gl - gmax), axis=0, keepdims=True)
    e_lo = EXPERT_ROW0 + g_idx * EXPERTS_PER_GROUP
    el = jnp.where((rid >= e_lo) & (rid < e_lo + EXPERTS_PER_GROUP), logits, neg)
    m1 = jnp.max(el, axis=0, keepdims=True)
    i1 = jnp.min(jnp.where(el == m1, rid, ROUTER_ROWS), axis=0, keepdims=True)
    el2 = jnp.where(rid == i1, neg, el)
    m2 = jnp.max(el2, axis=0, keepdims=True)
    i2 = jnp.min(jnp.where(el2 == m2, rid, ROUTER_ROWS), axis=0, keepdims=True)
    e21 = jnp.exp(m2 - m1)
    w1 = g_prob / (1.0 + e21)
    w2 = g_prob * e21 / (1.0 + e21)

    hit1 = rid == i1
    hit2 = rid == i2
    onehot = jnp.where(hit1 | hit2, 1.0, 0.0)
    earlier = jnp.where(row < col, 1.0, 0.0).astype(BF16)
    seen = carry_ref[...]
    prior = (jnp.dot(onehot.astype(BF16), earlier, preferred_element_type=F32)
             + jnp.concatenate([seen] * (TS // LANES), axis=1))
    r1 = jnp.sum(jnp.where(hit1, prior, 0.0), axis=0, keepdims=True)
    r2 = jnp.sum(jnp.where(hit2, prior, 0.0), axis=0, keepdims=True)
    seen = seen + jnp.sum(onehot, axis=1, keepdims=True)
    carry_ref[...] = seen
    cnt_ref[...] = seen.astype(jnp.int32)

    pad_i = jnp.zeros((META_ROWS - 2 * TOP_K, TS), jnp.int32)
    mi_ref[...] = jnp.concatenate(
        [i1 - EXPERT_ROW0, i2 - EXPERT_ROW0, r1.astype(jnp.int32), r2.astype(jnp.int32), pad_i],
        axis=0)
    mf_ref[...] = jnp.concatenate([w1, w2, jnp.zeros((META_ROWS - TOP_K, TS), F32)], axis=0)


def _mixer(x2, g1, win, vg, ws, bst, lbraw, og, wout, g2, wr, br):
    n_steps = N_TOK // TS
    const2 = lambda i: (0, 0)
    tok = lambda i: (i, 0)
    lane_tok = lambda i: (0, i)
    return pl.pallas_call(
        _mixer_kernel,
        grid=(n_steps,),
        in_specs=[
            pl.BlockSpec((TS, D_MODEL), tok),
            pl.BlockSpec((TS, D_MODEL), lambda i: (jnp.minimum(i + 1, n_steps - 1), 0)),
            pl.BlockSpec((1, D_MODEL), const2),
            pl.BlockSpec((D_MODEL, D_IN), const2),
            pl.BlockSpec((1, D_A), const2),
            pl.BlockSpec((HEADS, GMLP_BLOCK, GMLP_BLOCK), lambda i: (0, 0, 0)),
            pl.BlockSpec((TS, HEADS), const2),
            pl.BlockSpec((2, D_B), const2),
            pl.BlockSpec((1, D_B), const2),
            pl.BlockSpec((D_MODEL, D_MODEL), const2),
            pl.BlockSpec((1, D_MODEL), const2),
            pl.BlockSpec((D_MODEL, LANES), const2),
            pl.BlockSpec((ROUTER_ROWS, 1), const2),
        ],
        out_specs=[
            pl.BlockSpec((TS, D_MODEL), tok),
            pl.BlockSpec((TS * PACK_ROWS, LANES), tok),
            pl.BlockSpec((META_ROWS, TS), lane_tok),
            pl.BlockSpec((META_ROWS, TS), lane_tok),
            pl.BlockSpec((ROUTER_ROWS, LANES), const2),
        ],
        out_shape=[
            jax.ShapeDtypeStruct((N_TOK, D_MODEL), F32),
            jax.ShapeDtypeStruct((N_TOK * PACK_ROWS, LANES), jnp.uint32),
            jax.ShapeDtypeStruct((META_ROWS, N_TOK), jnp.int32),
            jax.ShapeDtypeStruct((META_ROWS, N_TOK), F32),
            jax.ShapeDtypeStruct((ROUTER_ROWS, LANES), jnp.int32),
        ],
        scratch_shapes=[
            pltpu.VMEM((HEADS, HEAD_DIM, HEAD_DIM), F32),
            pltpu.VMEM((ROUTER_ROWS, LANES), F32),
            pltpu.VMEM((HEADS, TS, TS), BF16),
            pltpu.VMEM((TS, D_MODEL), BF16),
            pltpu.VMEM((2, TS, D_IN), F32),
        ],
        compiler_params=pltpu.CompilerParams(
            dimension_semantics=("arbitrary",),
            vmem_limit_bytes=56 * 1024 * 1024,
        ),
        name="mixer",
    )(x2, x2, g1, win, vg, ws, bst, lbraw, og, wout, g2, wr, br)


FILL_UNROLL = 8


def _invert_kernel(dest_ref, gap_lo_ref, gap_hi_ref, inv_ref):
    for e in range(N_EXPERTS):
        first = gap_lo_ref[e] // FILL_UNROLL

        def fill(g, carry):
            for u in range(FILL_UNROLL):
                inv_ref[g * FILL_UNROLL + u] = -1
            return carry

        lax.fori_loop(first, gap_hi_ref[e] // FILL_UNROLL, fill, 0)

    def place(a, carry):
        inv_ref[dest_ref[a]] = a
        return carry

    lax.fori_loop(0, N_ASSIGN, place, 0, unroll=16)


def _invert(dest, gap_lo, gap_hi):
    smem = pl.BlockSpec(memory_space=pltpu.SMEM)
    return pl.pallas_call(
        _invert_kernel,
        in_specs=[smem, smem, smem],
        out_specs=smem,
        out_shape=jax.ShapeDtypeStruct((INV_LEN,), jnp.int32),
        name="invert",
    )(dest, gap_lo, gap_hi)


def _expert_kernel(bexp_ref, nused_ref, src_hbm, dst_hbm, n2_hbm, wg_ref, wu_ref, wd_ref, y_hbm,
                   src_smem, dst_smem, n2_vmem, xbuf, obuf, wg_bf, wu_bf, wd_bf,
                   src_sem, dst_sem, nsem, ssem):
    i = pl.program_id(0)
    nused = nused_ref[0]
    last = nused - 1

    def src_copy(blk, s):
        return pltpu.make_async_copy(src_hbm.at[blk], src_smem.at[pl.ds(s, 1)], src_sem.at[s])

    def dst_copy(blk, s):
        return pltpu.make_async_copy(dst_hbm.at[blk], dst_smem.at[pl.ds(s, 1)], dst_sem.at[s])

    def gather(s):
        for j in range(MOE_BLOCK):
            r = pl.multiple_of(src_smem[s, j], PACK_ROWS)
            xbuf[s, pl.ds(j * PACK_ROWS, PACK_ROWS), :] = n2_vmem[pl.ds(r, PACK_ROWS), :]

    def unpack(s):
        words = jnp.concatenate(
            [xbuf[s, pl.ds(c, MOE_BLOCK, stride=PACK_ROWS), :] for c in range(PACK_ROWS)],
            axis=1)
        lo = pltpu.bitcast(words << 16, F32)
        hi = pltpu.bitcast(words & jnp.uint32(HI_MASK), F32)
        return jnp.concatenate([lo, hi], axis=1).astype(BF16)

    def start_scatter(s):
        for j in range(MOE_BLOCK):
            r = pl.multiple_of(dst_smem[s, j], ROW_TILE)
            pltpu.make_async_copy(obuf.at[s, pl.ds(j * ROW_TILE, ROW_TILE)],
                                  y_hbm.at[pl.ds(r, ROW_TILE)], ssem.at[s]
                                  ).start(priority=j % 2)

    def wait_scatter(s):
        pltpu.make_async_copy(obuf.at[s], y_hbm.at[pl.ds(0, BLOCK_TILE_ROWS)],
                              ssem.at[s]).wait()

    @pl.when(i == 0)
    def _prologue():
        obuf[...] = jnp.zeros_like(obuf)
        for s in range(N_SLOTS - 1):
            pltpu.make_async_copy(
                obuf.at[s],
                y_hbm.at[pl.ds((N_ASSIGN + s * MOE_BLOCK) * ROW_TILE, BLOCK_TILE_ROWS)],
                ssem.at[s]).start()
        resident = pltpu.make_async_copy(n2_hbm, n2_vmem, nsem)
        resident.start()
        src_copy(0, 0).start()
        src_copy(0, 0).wait()
        resident.wait()
        gather(0)
        src_copy(jnp.minimum(1, last), 1).start()
        dst_copy(N_BLOCKS, N_SLOTS - 1).start()

    prev = bexp_ref[jnp.maximum(i - 1, 0)]
    changed = (i == 0) | (bexp_ref[i] != prev)

    @pl.when(changed)
    def _cast_weights():
        wg_bf[...] = wg_ref[0].astype(BF16)
        wu_bf[...] = wu_ref[0].astype(BF16)
        wd_bf[...] = wd_ref[0].astype(BF16)

    def active(cur, nxt, prv):
        src_copy(0, nxt).wait()
        dst_copy(0, prv).wait()
        wait_scatter(cur)
        start_scatter(prv)
        gather(nxt)
        src_copy(jnp.minimum(i + 2, last), prv).start()
        dst_copy(i, cur).start()
        xb = unpack(cur)
        hg = jnp.dot(xb, wg_bf[...], preferred_element_type=F32)
        hu = jnp.dot(xb, wu_bf[...], preferred_element_type=F32)
        hh = (hg * _sigmoid(hg) * hu).astype(BF16)
        _store_row_tiles(obuf.at[cur], jnp.dot(hh, wd_bf[...], preferred_element_type=F32),
                         MOE_BLOCK)

    def epilogue(cur, nxt, prv):
        dst_copy(0, cur).wait()
        start_scatter(cur)
        wait_scatter(nxt)
        wait_scatter(prv)
        wait_scatter(cur)
        src_copy(0, prv).wait()

    for cur in range(N_SLOTS):
        slots = (cur, (cur + 1) % N_SLOTS, (cur + 2) % N_SLOTS)

        @pl.when((i < nused) & (i % N_SLOTS == cur))
        def _():
            active(*slots)

        @pl.when((i == last) & (i % N_SLOTS == cur))
        def _():
            epilogue(*slots)


_EXPERT_W_BYTES = 3 * D_MODEL * D_EXPERT * 4
EXPERT_VMEM_LIMIT = (
    N_TOK * PACK_COLS * 4
    + 2 * _EXPERT_W_BYTES
    + _EXPERT_W_BYTES // 2
    + N_SLOTS * MOE_BLOCK * (PACK_COLS + D_MODEL) * 4
    + 4 * 1024 * 1024)


def _experts(bexp, nused, src2d, dst2d, n2, w_gate, w_up, w_down):
    def w_blk(i, bexp, nused):
        return (bexp[i], 0, 0)

    any_spec = pl.BlockSpec(memory_space=pl.ANY)
    return pl.pallas_call(
        _expert_kernel,
        grid_spec=pltpu.PrefetchScalarGridSpec(
            num_scalar_prefetch=2,
            grid=(N_BLOCKS,),
            in_specs=[
                any_spec, any_spec, any_spec,
                pl.BlockSpec((1, D_MODEL, D_EXPERT), w_blk),
                pl.BlockSpec((1, D_MODEL, D_EXPERT), w_blk),
                pl.BlockSpec((1, D_EXPERT, D_MODEL), w_blk),
            ],
            out_specs=any_spec,
            scratch_shapes=[
                pltpu.SMEM((N_SLOTS, MOE_BLOCK), jnp.int32),
                pltpu.SMEM((N_SLOTS, MOE_BLOCK), jnp.int32),
                pltpu.VMEM((N_TOK * PACK_ROWS, LANES), jnp.uint32),
                pltpu.VMEM((N_SLOTS, MOE_BLOCK * PACK_ROWS, LANES), jnp.uint32),
                pltpu.VMEM((N_SLOTS, BLOCK_TILE_ROWS, LANES), F32),
                pltpu.VMEM((D_MODEL, D_EXPERT), BF16),
                pltpu.VMEM((D_MODEL, D_EXPERT), BF16),
                pltpu.VMEM((D_EXPERT, D_MODEL), BF16),
                pltpu.SemaphoreType.DMA((N_SLOTS,)),
                pltpu.SemaphoreType.DMA((N_SLOTS,)),
                pltpu.SemaphoreType.DMA,
                pltpu.SemaphoreType.DMA((N_SLOTS,)),
            ],
        ),
        out_shape=jax.ShapeDtypeStruct(((N_ASSIGN + DUMMY_ROWS) * ROW_TILE, LANES), F32),
        compiler_params=pltpu.CompilerParams(
            dimension_semantics=("arbitrary",),
            vmem_limit_bytes=EXPERT_VMEM_LIMIT,
        ),
        name="experts",
    )(bexp, nused, src2d, dst2d, n2, w_gate, w_up, w_down)


def _combine_kernel(h1_ref, mf_ref, fg_ref, y0_ref, y1_ref, out_ref):
    w = mf_ref[...]
    h = (h1_ref[...] + _load_row_tiles(y0_ref, TD) * w[:, 0:1]
         + _load_row_tiles(y1_ref, TD) * w[:, 1:2])
    out_ref[...] = _rms(h, fg_ref[...])


def _combine(h1, mf, fg, y):
    n_steps = N_TOK // TD
    tok = lambda i: (i, 0)
    return pl.pallas_call(
        _combine_kernel,
        grid=(n_steps,),
        in_specs=[
            pl.BlockSpec((TD, D_MODEL), tok),
            pl.BlockSpec((TD, TOP_K), tok),
            pl.BlockSpec((1, D_MODEL), lambda i: (0, 0)),
            pl.BlockSpec((TD * ROW_TILE, LANES), tok),
            pl.BlockSpec((TD * ROW_TILE, LANES), lambda i: (i + n_steps, 0)),
        ],
        out_specs=pl.BlockSpec((TD, D_MODEL), tok),
        out_shape=jax.ShapeDtypeStruct((N_TOK, D_MODEL), F32),
        compiler_params=pltpu.CompilerParams(dimension_semantics=("arbitrary",)),
        name="combine",
    )(h1, mf, fg, y, y)


def kernel(x, norm1_gain, w_in, gmlp_v_gain, gmlp_w_s, gmlp_b_s, hgrn_lower_bounds,
           hgrn_out_gain, w_out, norm2_gain, w_group_router, b_group_router,
           w_expert_router, b_expert_router, w_gate, w_up, w_down, final_gain):
    l = 0
    x2 = x.reshape(N_TOK, D_MODEL)
    bst = jnp.tile(jnp.transpose(gmlp_b_s[l]), (TS // GMLP_BLOCK, 1))
    w_router = jnp.concatenate([w_group_router[l], w_expert_router[l]], axis=1)
    w_router = jnp.pad(w_router, ((0, 0), (0, LANES - w_router.shape[1]))).astype(BF16)
    b_router = jnp.concatenate([b_group_router[l], b_expert_router[l]])
    b_router = jnp.pad(b_router, (0, ROUTER_ROWS - b_router.shape[0])).reshape(ROUTER_ROWS, 1)

    h1, n2, meta_i, meta_f, counts = _mixer(
        x2, norm1_gain[l].reshape(1, D_MODEL), w_in[l].astype(BF16),
        gmlp_v_gain[l].reshape(1, D_A), gmlp_w_s[l], bst,
        hgrn_lower_bounds, hgrn_out_gain[l].reshape(1, D_B), w_out[l].astype(BF16),
        norm2_gain[l].reshape(1, D_MODEL), w_router, b_router)

    cnt = counts[EXPERT_ROW0:EXPERT_ROW0 + N_EXPERTS, 0]
    padded = ((cnt + MOE_BLOCK - 1) // MOE_BLOCK) * MOE_BLOCK
    pend = jnp.cumsum(padded)
    pstart = pend - padded
    eid = meta_i[0:TOP_K]
    rank = meta_i[TOP_K:2 * TOP_K]
    base = jnp.sum(jnp.where(eid[:, :, None] == jnp.arange(N_EXPERTS)[None, None, :],
                             pstart[None, None, :], 0), axis=-1)
    dest = (base + rank).astype(jnp.int32).reshape(N_ASSIGN)
    gap_lo = (pstart + cnt).astype(jnp.int32)
    gap_hi = jnp.concatenate([pstart[1:], jnp.full((1,), INV_LEN)]).astype(jnp.int32)
    inv = _invert(dest, gap_lo, gap_hi)
    pos = jnp.arange(INV_LEN, dtype=jnp.int32)
    blk = jnp.where(pos < SORTED_ROWS, pos // MOE_BLOCK, N_SLOTS - 1)
    dummy = N_ASSIGN + (blk % N_SLOTS) * MOE_BLOCK + pos % MOE_BLOCK
    src2d = (jnp.where(inv >= 0, inv % N_TOK, 0) * PACK_ROWS)[:SORTED_ROWS].reshape(
        N_BLOCKS, 1, MOE_BLOCK)
    dst2d = (jnp.where(inv >= 0, inv, dummy) * ROW_TILE).reshape(N_BLOCKS + 1, 1, MOE_BLOCK)
    blk_start = jnp.arange(N_BLOCKS, dtype=jnp.int32) * MOE_BLOCK
    bexp = jnp.clip(jnp.sum(blk_start[:, None] >= pend[None, :], axis=1), 0,
                    N_EXPERTS - 1).astype(jnp.int32)
    nused = (pend[-1:] // MOE_BLOCK).astype(jnp.int32)

    y = _experts(bexp, nused, src2d, dst2d, n2, w_gate[l], w_up[l], w_down[l])
    out = _combine(h1, meta_f[0:TOP_K].T, final_gain.reshape(1, D_MODEL), y)
    return out.reshape(BATCH, SEQ, D_MODEL)
```

```python
import functools

import jax
import jax.numpy as jnp
from jax import lax
from jax.experimental import pallas as pl
from jax.experimental.pallas import tpu as pltpu

D_MODEL = 1024
BATCH = 2
SEQ = 8192
N_TOK = BATCH * SEQ
CHUNK = 64
EPS = 1e-6
D_A = 512
HEADS = 4
HEAD_DIM = 128
GMLP_BLOCK = 128
D_B = 512
D_IN = 3072
N_GROUPS = 4
EXPERTS_PER_GROUP = 8
N_EXPERTS = 32
TOP_K = 2
D_EXPERT = 512

LANES = 128
ROUTER_ROWS = 48
EXPERT_ROW0 = N_GROUPS
META_ROWS = 8
PACK_COLS = D_MODEL // 2
PACK_ROWS = PACK_COLS // LANES
HI_MASK = 0xFFFF0000

TS = 256
CHUNKS_PER_STEP = TS // CHUNK
MOE_BLOCK = 256
N_BLOCKS = N_TOK * TOP_K // MOE_BLOCK + N_EXPERTS
SORTED_ROWS = N_BLOCKS * MOE_BLOCK
N_ASSIGN = N_TOK * TOP_K
N_SLOTS = 3
DUMMY_ROWS = N_SLOTS * MOE_BLOCK
BLOCK_PACK_ROWS = MOE_BLOCK * PACK_ROWS
INV_LEN = SORTED_ROWS + MOE_BLOCK
TD = 512

F32 = jnp.float32
BF16 = jnp.bfloat16
NT_DIMS = (((1,), (1,)), ((), ()))
TN_DIMS = (((0,), (0,)), ((), ()))


def _sigmoid(x):
    return 1.0 / (1.0 + jnp.exp(-x))


def _gelu(x):
    return 0.5 * x * (1.0 + lax.erf(x * (2.0 ** -0.5)))


def _rms(x, gain):
    return x * lax.rsqrt(jnp.mean(x * x, axis=-1, keepdims=True) + EPS) * gain


def _pack_bf16_pairs(x):
    lo = pltpu.bitcast(x[:, 0:PACK_COLS].astype(BF16).astype(F32), jnp.uint32) >> 16
    hi = (pltpu.bitcast(x[:, PACK_COLS:D_MODEL].astype(BF16).astype(F32), jnp.uint32)
          & jnp.uint32(HI_MASK))
    return lo | hi


def _unpack_bf16_pairs(words):
    lo = pltpu.bitcast(words << 16, F32)
    hi = pltpu.bitcast(words & jnp.uint32(HI_MASK), F32)
    return jnp.concatenate([lo, hi], axis=1)


def _store_packed_rows(ref, words, rows):
    for c in range(PACK_ROWS):
        ref[pl.ds(c, rows, stride=PACK_ROWS), :] = words[:, c * LANES:(c + 1) * LANES]


def _load_packed_rows(ref, rows):
    return jnp.concatenate(
        [ref[pl.ds(c, rows, stride=PACK_ROWS), :] for c in range(PACK_ROWS)], axis=1)


def _split_bf16(x):
    hi = x.astype(BF16)
    lo = (x - hi.astype(F32)).astype(BF16)
    return hi, lo


def _mixer_kernel(*refs):
    for parity in range(2):
        pl.when(pl.program_id(0) % 2 == parity)(functools.partial(_mixer_step, parity, *refs))


def _mixer_step(cur, x_ref, xn_ref, g1_ref, win_ref, vg_ref, ws_ref, bst_ref, lbraw_ref, og_ref,
                wout_ref, g2_ref, wr_ref, br_ref,
                h1_ref, n2_ref, mi_ref, mf_ref, cnt_ref,
                st_ref, carry_ref, wm_ref, mix_ref, proj_ref):
    i = pl.program_id(0)

    @pl.when(i == 0)
    def _init():
        carry_ref[...] = jnp.zeros_like(carry_ref)
        r = lax.broadcasted_iota(jnp.int32, (GMLP_BLOCK, GMLP_BLOCK), 0)
        c = lax.broadcasted_iota(jnp.int32, (GMLP_BLOCK, GMLP_BLOCK), 1)
        keep = (c // CHUNK) <= (r // CHUNK)
        wm_ref[...] = jnp.zeros_like(wm_ref)
        for g in range(HEADS):
            w = jnp.where(keep, ws_ref[g], 0.0).astype(BF16)
            for p in range(TS // GMLP_BLOCK):
                wm_ref[g, p * GMLP_BLOCK:(p + 1) * GMLP_BLOCK,
                       p * GMLP_BLOCK:(p + 1) * GMLP_BLOCK] = w

    @pl.when(i % (SEQ // TS) == 0)
    def _reset_state():
        st_ref[...] = jnp.zeros_like(st_ref)

    def project(n1, slot, c):
        proj_ref[slot, :, c * D_A:(c + 1) * D_A] = jnp.dot(
            n1, win_ref[:, c * D_A:(c + 1) * D_A], preferred_element_type=F32)

    @pl.when(i == 0)
    def _first_projection():
        n1 = _rms(x_ref[...], g1_ref[...]).astype(BF16)
        for c in range(D_IN // D_A):
            project(n1, 0, c)

    n1_next = _rms(xn_ref[...], g1_ref[...]).astype(BF16)
    x = x_ref[...]

    project(n1_next, 1 - cur, 0)
    gu = _gelu(proj_ref[cur, :, 0:D_A])
    project(n1_next, 1 - cur, 1)
    gv = _gelu(proj_ref[cur, :, D_A:2 * D_A])
    for g in range(HEADS):
        sl = slice(g * HEAD_DIM, (g + 1) * HEAD_DIM)
        vh = _rms(gv[:, sl], vg_ref[:, sl]).astype(BF16)
        mixed = jnp.dot(wm_ref[g], vh, preferred_element_type=F32) + bst_ref[:, g:g + 1]
        mix_ref[:, sl] = (gu[:, sl] * mixed).astype(BF16)

    o2 = 2 * D_A
    qr = proj_ref[cur, :, o2:o2 + D_B]
    fr = proj_ref[cur, :, o2 + D_B:o2 + 2 * D_B]
    vv = proj_ref[cur, :, o2 + 2 * D_B:o2 + 3 * D_B].astype(BF16)
    gr = proj_ref[cur, :, o2 + 3 * D_B:o2 + 4 * D_B]

    lbr = lbraw_ref[...]
    lbm = jnp.max(lbr, axis=0, keepdims=True)
    lbe = jnp.exp(lbr - lbm)
    lb = lbe[0:1, :] / jnp.sum(lbe, axis=0, keepdims=True)

    project(n1_next, 1 - cur, 2)
    q = qr * _sigmoid(qr)
    f = lb + (1.0 - lb) * _sigmoid(fr)
    k = 1.0 - f
    lf = jnp.log(f)

    row = lax.broadcasted_iota(jnp.int32, (TS, TS), 0)
    col = lax.broadcasted_iota(jnp.int32, (TS, TS), 1)
    causal = (row >= col) & ((row // CHUNK) == (col // CHUNK))
    tri = jnp.where(causal, 1.0, 0.0).astype(BF16)
    lf_hi, lf_lo = _split_bf16(lf)
    b = (jnp.dot(tri, lf_hi, preferred_element_type=F32)
         + jnp.dot(tri, lf_lo, preferred_element_type=F32))

    def chunk_rows(r):
        return jnp.concatenate(
            [jnp.broadcast_to(b[c * CHUNK + r:c * CHUNK + r + 1, :], (CHUNK, D_B))
             for c in range(CHUNKS_PER_STEP)], axis=0)

    project(n1_next, 1 - cur, 3)
    bref = chunk_rows(CHUNK // 2 - 1)
    blast = chunk_rows(CHUNK - 1)
    qe = (q * jnp.exp(b - bref)).astype(BF16)
    ke = (k * jnp.exp(bref - b)).astype(BF16)
    kd = (k * jnp.exp(blast - b)).astype(BF16)
    qb = (q * jnp.exp(b)).astype(BF16)

    head_sl = [slice(h * HEAD_DIM, (h + 1) * HEAD_DIM) for h in range(HEADS)]
    chunk_sl = [slice(c * CHUNK, (c + 1) * CHUNK) for c in range(CHUNKS_PER_STEP)]
    scores = [lax.dot_general(qe[:, sl], ke[:, sl], NT_DIMS, preferred_element_type=F32)
              for sl in head_sl]
    upd = [[lax.dot_general(vv[rows, sl], kd[rows, sl], TN_DIMS, preferred_element_type=F32)
            for rows in chunk_sl] for sl in head_sl]
    project(n1_next, 1 - cur, 4)
    scores = [jnp.where(causal, sc, 0.0).astype(BF16) for sc in scores]
    intra = [jnp.dot(scores[h], vv[:, head_sl[h]], preferred_element_type=F32)
             for h in range(HEADS)]
    entering = []
    for h in range(HEADS):
        st = st_ref[h]
        per_chunk = []
        for c in range(CHUNKS_PER_STEP):
            per_chunk.append(st.astype(BF16))
            decay = jnp.exp(b[(c + 1) * CHUNK - 1:(c + 1) * CHUNK, head_sl[h]])
            st = st * decay + upd[h][c]
        st_ref[h] = st
        entering.append(per_chunk)
    inter = [jnp.concatenate(
        [lax.dot_general(qb[chunk_sl[c], head_sl[h]], entering[h][c], NT_DIMS,
                         preferred_element_type=F32) for c in range(CHUNKS_PER_STEP)], axis=0)
        for h in range(HEADS)]
    for h in range(HEADS):
        o = _rms(intra[h] + inter[h], og_ref[:, head_sl[h]])
        g_h = gr[:, head_sl[h]]
        mix_ref[:, D_A + h * HEAD_DIM:D_A + (h + 1) * HEAD_DIM] = (
            o * (g_h * _sigmoid(g_h))).astype(BF16)

    h1 = x + jnp.dot(mix_ref[...], wout_ref[...], preferred_element_type=F32)
    project(n1_next, 1 - cur, 5)
    h1_ref[...] = h1
    n2 = _rms(h1, g2_ref[...])
    _store_packed_rows(n2_ref, _pack_bf16_pairs(n2), TS)

    logits = jnp.dot(n2.astype(BF16), wr_ref[...], preferred_element_type=F32)
    logits = jnp.transpose(logits)[0:ROUTER_ROWS, :] + br_ref[...]
    rid = lax.broadcasted_iota(jnp.int32, (ROUTER_ROWS, TS), 0)
    neg = -jnp.inf
    gl = jnp.where(rid < N_GROUPS, logits, neg)
    gmax = jnp.max(gl, axis=0, keepdims=True)
    g_idx = jnp.min(jnp.where(gl == gmax, rid, ROUTER_ROWS), axis=0, keepdims=True)
    g_prob = 1.0 / jnp.sum(jnp.exp(gl - gmax), axis=0, keepdims=True)
    e_lo = EXPERT_ROW0 + g_idx * EXPERTS_PER_GROUP
    el = jnp.where((rid >= e_lo) & (rid < e_lo + EXPERTS_PER_GROUP), logits, neg)
    m1 = jnp.max(el, axis=0, keepdims=True)
    i1 = jnp.min(jnp.where(el == m1, rid, ROUTER_ROWS), axis=0, keepdims=True)
    el2 = jnp.where(rid == i1, neg, el)
    m2 = jnp.max(el2, axis=0, keepdims=True)
    i2 = jnp.min(jnp.where(el2 == m2, rid, ROUTER_ROWS), axis=0, keepdims=True)
    e21 = jnp.exp(m2 - m1)
    w1 = g_prob / (1.0 + e21)
    w2 = g_prob * e21 / (1.0 + e21)

    hit1 = rid == i1
    hit2 = rid == i2
    onehot = jnp.where(hit1 | hit2, 1.0, 0.0)
    earlier = jnp.where(row < col, 1.0, 0.0).astype(BF16)
    seen = carry_ref[...]
    prior = (jnp.dot(onehot.astype(BF16), earlier, preferred_element_type=F32)
             + jnp.concatenate([seen] * (TS // LANES), axis=1))
    r1 = jnp.sum(jnp.where(hit1, prior, 0.0), axis=0, keepdims=True)
    r2 = jnp.sum(jnp.where(hit2, prior, 0.0), axis=0, keepdims=True)
    seen = seen + jnp.sum(onehot, axis=1, keepdims=True)
    carry_ref[...] = seen
    cnt_ref[...] = seen.astype(jnp.int32)

    pad_i = jnp.zeros((META_ROWS - 2 * TOP_K, TS), jnp.int32)
    mi_ref[...] = jnp.concatenate(
        [i1 - EXPERT_ROW0, i2 - EXPERT_ROW0, r1.astype(jnp.int32), r2.astype(jnp.int32), pad_i],
        axis=0)
    mf_ref[...] = jnp.concatenate([w1, w2, jnp.zeros((META_ROWS - TOP_K, TS), F32)], axis=0)


def _mixer(x2, g1, win, vg, ws, bst, lbraw, og, wout, g2, wr, br):
    n_steps = N_TOK // TS
    const2 = lambda i: (0, 0)
    tok = lambda i: (i, 0)
    lane_tok = lambda i: (0, i)
    return pl.pallas_call(
        _mixer_kernel,
        grid=(n_steps,),
        in_specs=[
            pl.BlockSpec((TS, D_MODEL), tok),
            pl.BlockSpec((TS, D_MODEL), lambda i: (jnp.minimum(i + 1, n_steps - 1), 0)),
            pl.BlockSpec((1, D_MODEL), const2),
            pl.BlockSpec((D_MODEL, D_IN), const2),
            pl.BlockSpec((1, D_A), const2),
            pl.BlockSpec((HEADS, GMLP_BLOCK, GMLP_BLOCK), lambda i: (0, 0, 0)),
            pl.BlockSpec((TS, HEADS), const2),
            pl.BlockSpec((2, D_B), const2),
            pl.BlockSpec((1, D_B), const2),
            pl.BlockSpec((D_MODEL, D_MODEL), const2),
            pl.BlockSpec((1, D_MODEL), const2),
            pl.BlockSpec((D_MODEL, LANES), const2),
            pl.BlockSpec((ROUTER_ROWS, 1), const2),
        ],
        out_specs=[
            pl.BlockSpec((TS, D_MODEL), tok),
            pl.BlockSpec((TS * PACK_ROWS, LANES), tok),
            pl.BlockSpec((META_ROWS, TS), lane_tok),
            pl.BlockSpec((META_ROWS, TS), lane_tok),
            pl.BlockSpec((ROUTER_ROWS, LANES), const2),
        ],
        out_shape=[
            jax.ShapeDtypeStruct((N_TOK, D_MODEL), F32),
            jax.ShapeDtypeStruct((N_TOK * PACK_ROWS, LANES), jnp.uint32),
            jax.ShapeDtypeStruct((META_ROWS, N_TOK), jnp.int32),
            jax.ShapeDtypeStruct((META_ROWS, N_TOK), F32),
            jax.ShapeDtypeStruct((ROUTER_ROWS, LANES), jnp.int32),
        ],
        scratch_shapes=[
            pltpu.VMEM((HEADS, HEAD_DIM, HEAD_DIM), F32),
            pltpu.VMEM((ROUTER_ROWS, LANES), F32),
            pltpu.VMEM((HEADS, TS, TS), BF16),
            pltpu.VMEM((TS, D_MODEL), BF16),
            pltpu.VMEM((2, TS, D_IN), F32),
        ],
        compiler_params=pltpu.CompilerParams(
            dimension_semantics=("arbitrary",),
            vmem_limit_bytes=56 * 1024 * 1024,
        ),
        name="mixer",
    )(x2, x2, g1, win, vg, ws, bst, lbraw, og, wout, g2, wr, br)


FILL_UNROLL = 8


def _invert_kernel(dest_ref, gap_lo_ref, gap_hi_ref, inv_ref):
    for e in range(N_EXPERTS):
        first = gap_lo_ref[e] // FILL_UNROLL

        def fill(g, carry):
            for u in range(FILL_UNROLL):
                inv_ref[g * FILL_UNROLL + u] = -1
            return carry

        lax.fori_loop(first, gap_hi_ref[e] // FILL_UNROLL, fill, 0)

    def place(a, carry):
        inv_ref[dest_ref[a]] = a
        return carry

    lax.fori_loop(0, N_ASSIGN, place, 0, unroll=16)


def _invert(dest, gap_lo, gap_hi):
    smem = pl.BlockSpec(memory_space=pltpu.SMEM)
    return pl.pallas_call(
        _invert_kernel,
        in_specs=[smem, smem, smem],
        out_specs=smem,
        out_shape=jax.ShapeDtypeStruct((INV_LEN,), jnp.int32),
        name="invert",
    )(dest, gap_lo, gap_hi)


def _expert_kernel(bexp_ref, nused_ref, src_hbm, dst_hbm, n2_hbm, wg_ref, wu_ref, wd_ref, y_hbm,
                   src_smem, dst_smem, n2_vmem, xbuf, obuf, wg_bf, wu_bf, wd_bf,
                   src_sem, dst_sem, nsem, ssem):
    i = pl.program_id(0)
    nused = nused_ref[0]
    last = nused - 1

    def src_copy(blk, s):
        return pltpu.make_async_copy(src_hbm.at[blk], src_smem.at[pl.ds(s, 1)], src_sem.at[s])

    def dst_copy(blk, s):
        return pltpu.make_async_copy(dst_hbm.at[blk], dst_smem.at[pl.ds(s, 1)], dst_sem.at[s])

    def gather(s):
        for j in range(MOE_BLOCK):
            r = pl.multiple_of(src_smem[s, j], PACK_ROWS)
            xbuf[s, pl.ds(j * PACK_ROWS, PACK_ROWS), :] = n2_vmem[pl.ds(r, PACK_ROWS), :]

    def start_scatter(s):
        for j in range(MOE_BLOCK):
            r = pl.multiple_of(dst_smem[s, j], PACK_ROWS)
            pltpu.make_async_copy(obuf.at[s, pl.ds(j * PACK_ROWS, PACK_ROWS)],
                                  y_hbm.at[pl.ds(r, PACK_ROWS)], ssem.at[s]
                                  ).start(priority=j % 2)

    def wait_scatter(s):
        pltpu.make_async_copy(obuf.at[s], y_hbm.at[pl.ds(0, BLOCK_PACK_ROWS)],
                              ssem.at[s]).wait()

    @pl.when(i == 0)
    def _prologue():
        obuf[...] = jnp.zeros_like(obuf)
        for s in range(N_SLOTS - 1):
            pltpu.make_async_copy(
                obuf.at[s],
                y_hbm.at[pl.ds((N_ASSIGN + s * MOE_BLOCK) * PACK_ROWS, BLOCK_PACK_ROWS)],
                ssem.at[s]).start()
        resident = pltpu.make_async_copy(n2_hbm, n2_vmem, nsem)
        resident.start()
        src_copy(0, 0).start()
        src_copy(0, 0).wait()
        resident.wait()
        gather(0)
        src_copy(jnp.minimum(1, last), 1).start()
        dst_copy(N_BLOCKS, N_SLOTS - 1).start()

    prev = bexp_ref[jnp.maximum(i - 1, 0)]
    changed = (i == 0) | (bexp_ref[i] != prev)

    @pl.when(changed)
    def _cast_weights():
        wg_bf[...] = wg_ref[0].astype(BF16)
        wu_bf[...] = wu_ref[0].astype(BF16)
        wd_bf[...] = wd_ref[0].astype(BF16)

    def active(cur, nxt, prv):
        src_copy(0, nxt).wait()
        dst_copy(0, prv).wait()
        wait_scatter(cur)
        start_scatter(prv)
        gather(nxt)
        src_copy(jnp.minimum(i + 2, last), prv).start()
        dst_copy(i, cur).start()
        xb = _unpack_bf16_pairs(_load_packed_rows(xbuf.at[cur], MOE_BLOCK)).astype(BF16)
        hg = jnp.dot(xb, wg_bf[...], preferred_element_type=F32)
        hu = jnp.dot(xb, wu_bf[...], preferred_element_type=F32)
        hh = (hg * _sigmoid(hg) * hu).astype(BF16)
        y = jnp.dot(hh, wd_bf[...], preferred_element_type=F32)
        _store_packed_rows(obuf.at[cur], _pack_bf16_pairs(y), MOE_BLOCK)

    def epilogue(cur, nxt, prv):
        dst_copy(0, cur).wait()
        start_scatter(cur)
        wait_scatter(nxt)
        wait_scatter(prv)
        wait_scatter(cur)
        src_copy(0, prv).wait()

    for cur in range(N_SLOTS):
        slots = (cur, (cur + 1) % N_SLOTS, (cur + 2) % N_SLOTS)

        @pl.when((i < nused) & (i % N_SLOTS == cur))
        def _():
            active(*slots)

        @pl.when((i == last) & (i % N_SLOTS == cur))
        def _():
            epilogue(*slots)


_EXPERT_W_BYTES = 3 * D_MODEL * D_EXPERT * 4
EXPERT_VMEM_LIMIT = (
    N_TOK * PACK_COLS * 4
    + 2 * _EXPERT_W_BYTES
    + _EXPERT_W_BYTES // 2
    + 2 * N_SLOTS * MOE_BLOCK * PACK_COLS * 4
    + 4 * 1024 * 1024)


def _experts(bexp, nused, src2d, dst2d, n2, w_gate, w_up, w_down):
    def w_blk(i, bexp, nused):
        return (bexp[i], 0, 0)

    any_spec = pl.BlockSpec(memory_space=pl.ANY)
    return pl.pallas_call(
        _expert_kernel,
        grid_spec=pltpu.PrefetchScalarGridSpec(
            num_scalar_prefetch=2,
            grid=(N_BLOCKS,),
            in_specs=[
                any_spec, any_spec, any_spec,
                pl.BlockSpec((1, D_MODEL, D_EXPERT), w_blk),
                pl.BlockSpec((1, D_MODEL, D_EXPERT), w_blk),
                pl.BlockSpec((1, D_EXPERT, D_MODEL), w_blk),
            ],
            out_specs=any_spec,
            scratch_shapes=[
                pltpu.SMEM((N_SLOTS, MOE_BLOCK), jnp.int32),
                pltpu.SMEM((N_SLOTS, MOE_BLOCK), jnp.int32),
                pltpu.VMEM((N_TOK * PACK_ROWS, LANES), jnp.uint32),
                pltpu.VMEM((N_SLOTS, MOE_BLOCK * PACK_ROWS, LANES), jnp.uint32),
                pltpu.VMEM((N_SLOTS, BLOCK_PACK_ROWS, LANES), jnp.uint32),
                pltpu.VMEM((D_MODEL, D_EXPERT), BF16),
                pltpu.VMEM((D_MODEL, D_EXPERT), BF16),
                pltpu.VMEM((D_EXPERT, D_MODEL), BF16),
                pltpu.SemaphoreType.DMA((N_SLOTS,)),
                pltpu.SemaphoreType.DMA((N_SLOTS,)),
                pltpu.SemaphoreType.DMA,
                pltpu.SemaphoreType.DMA((N_SLOTS,)),
            ],
        ),
        out_shape=jax.ShapeDtypeStruct(((N_ASSIGN + DUMMY_ROWS) * PACK_ROWS, LANES), jnp.uint32),
        compiler_params=pltpu.CompilerParams(
            dimension_semantics=("arbitrary",),
            vmem_limit_bytes=EXPERT_VMEM_LIMIT,
        ),
        name="experts",
    )(bexp, nused, src2d, dst2d, n2, w_gate, w_up, w_down)


def _combine_kernel(h1_ref, mf_ref, fg_ref, y0_ref, y1_ref, out_ref):
    w = mf_ref[...]
    y0 = _unpack_bf16_pairs(_load_packed_rows(y0_ref, TD))
    y1 = _unpack_bf16_pairs(_load_packed_rows(y1_ref, TD))
    h = h1_ref[...] + y0 * w[:, 0:1] + y1 * w[:, 1:2]
    out_ref[...] = _rms(h, fg_ref[...])


def _combine(h1, mf, fg, y):
    n_steps = N_TOK // TD
    tok = lambda i: (i, 0)
    return pl.pallas_call(
        _combine_kernel,
        grid=(n_steps,),
        in_specs=[
            pl.BlockSpec((TD, D_MODEL), tok),
            pl.BlockSpec((TD, TOP_K), tok),
            pl.BlockSpec((1, D_MODEL), lambda i: (0, 0)),
            pl.BlockSpec((TD * PACK_ROWS, LANES), tok),
            pl.BlockSpec((TD * PACK_ROWS, LANES), lambda i: (i + n_steps, 0)),
        ],
        out_specs=pl.BlockSpec((TD, D_MODEL), tok),
        out_shape=jax.ShapeDtypeStruct((N_TOK, D_MODEL), F32),
        compiler_params=pltpu.CompilerParams(dimension_semantics=("arbitrary",)),
        name="combine",
    )(h1, mf, fg, y, y)


def kernel(x, norm1_gain, w_in, gmlp_v_gain, gmlp_w_s, gmlp_b_s, hgrn_lower_bounds,
           hgrn_out_gain, w_out, norm2_gain, w_group_router, b_group_router,
           w_expert_router, b_expert_router, w_gate, w_up, w_down, final_gain):
    l = 0
    x2 = x.reshape(N_TOK, D_MODEL)
    bst = jnp.tile(jnp.transpose(gmlp_b_s[l]), (TS // GMLP_BLOCK, 1))
    w_router = jnp.concatenate([w_group_router[l], w_expert_router[l]], axis=1)
    w_router = jnp.pad(w_router, ((0, 0), (0, LANES - w_router.shape[1]))).astype(BF16)
    b_router = jnp.concatenate([b_group_router[l], b_expert_router[l]])
    b_router = jnp.pad(b_router, (0, ROUTER_ROWS - b_router.shape[0])).reshape(ROUTER_ROWS, 1)

    h1, n2, meta_i, meta_f, counts = _mixer(
        x2, norm1_gain[l].reshape(1, D_MODEL), w_in[l].astype(BF16),
        gmlp_v_gain[l].reshape(1, D_A), gmlp_w_s[l], bst,
        hgrn_lower_bounds, hgrn_out_gain[l].reshape(1, D_B), w_out[l].astype(BF16),
        norm2_gain[l].reshape(1, D_MODEL), w_router, b_router)

    cnt = counts[EXPERT_ROW0:EXPERT_ROW0 + N_EXPERTS, 0]
    padded = ((cnt + MOE_BLOCK - 1) // MOE_BLOCK) * MOE_BLOCK
    pend = jnp.cumsum(padded)
    pstart = pend - padded
    eid = meta_i[0:TOP_K]
    rank = meta_i[TOP_K:2 * TOP_K]
    base = jnp.sum(jnp.where(eid[:, :, None] == jnp.arange(N_EXPERTS)[None, None, :],
                             pstart[None, None, :], 0), axis=-1)
    dest = (base + rank).astype(jnp.int32).reshape(N_ASSIGN)
    gap_lo = (pstart + cnt).astype(jnp.int32)
    gap_hi = jnp.concatenate([pstart[1:], jnp.full((1,), INV_LEN)]).astype(jnp.int32)
    inv = _invert(dest, gap_lo, gap_hi)
    pos = jnp.arange(INV_LEN, dtype=jnp.int32)
    blk = jnp.where(pos < SORTED_ROWS, pos // MOE_BLOCK, N_SLOTS - 1)
    dummy = N_ASSIGN + (blk % N_SLOTS) * MOE_BLOCK + pos % MOE_BLOCK
    src2d = (jnp.where(inv >= 0, inv % N_TOK, 0) * PACK_ROWS)[:SORTED_ROWS].reshape(
        N_BLOCKS, 1, MOE_BLOCK)
    dst2d = (jnp.where(inv >= 0, inv, dummy) * PACK_ROWS).reshape(N_BLOCKS + 1, 1, MOE_BLOCK)
    blk_start = jnp.arange(N_BLOCKS, dtype=jnp.int32) * MOE_BLOCK
    bexp = jnp.clip(jnp.sum(blk_start[:, None] >= pend[None, :], axis=1), 0,
                    N_EXPERTS - 1).astype(jnp.int32)
    nused = (pend[-1:] // MOE_BLOCK).astype(jnp.int32)

    y = _experts(bexp, nused, src2d, dst2d, n2, w_gate[l], w_up[l], w_down[l])
    out = _combine(h1, meta_f[0:TOP_K].T, final_gain.reshape(1, D_MODEL), y)
    return out.reshape(BATCH, SEQ, D_MODEL)
```

```python
import functools

import jax
import jax.numpy as jnp
from jax import lax
from jax.experimental import pallas as pl
from jax.experimental.pallas import tpu as pltpu

D_MODEL = 1024
BATCH = 2
SEQ = 8192
N_TOK = BATCH * SEQ
CHUNK = 64
EPS = 1e-6
D_A = 512
HEADS = 4
HEAD_DIM = 128
GMLP_BLOCK = 128
D_B = 512
D_IN = 3072
N_GROUPS = 4
EXPERTS_PER_GROUP = 8
N_EXPERTS = 32
TOP_K = 2
D_EXPERT = 512

LANES = 128
ROUTER_ROWS = 48
EXPERT_ROW0 = N_GROUPS
META_ROWS = 8
PACK_COLS = D_MODEL // 2
PACK_ROWS = PACK_COLS // LANES
HI_MASK = 0xFFFF0000

TS = 256
CHUNKS_PER_STEP = TS // CHUNK
MOE_BLOCK = 256
N_BLOCKS = N_TOK * TOP_K // MOE_BLOCK + N_EXPERTS
SORTED_ROWS = N_BLOCKS * MOE_BLOCK
N_ASSIGN = N_TOK * TOP_K
N_SLOTS = 3
DUMMY_ROWS = N_SLOTS * MOE_BLOCK
BLOCK_PACK_ROWS = MOE_BLOCK * PACK_ROWS
INV_LEN = SORTED_ROWS + MOE_BLOCK
TD = 512

F32 = jnp.float32
BF16 = jnp.bfloat16
NT_DIMS = (((1,), (1,)), ((), ()))
TN_DIMS = (((0,), (0,)), ((), ()))


def _sigmoid(x):
    return 1.0 / (1.0 + jnp.exp(-x))


def _gelu(x):
    return 0.5 * x * (1.0 + lax.erf(x * (2.0 ** -0.5)))


def _rms(x, gain):
    return x * lax.rsqrt(jnp.mean(x * x, axis=-1, keepdims=True) + EPS) * gain


def _pack_bf16_pairs(x):
    lo = pltpu.bitcast(x[:, 0:PACK_COLS].astype(BF16).astype(F32), jnp.uint32) >> 16
    hi = (pltpu.bitcast(x[:, PACK_COLS:D_MODEL].astype(BF16).astype(F32), jnp.uint32)
          & jnp.uint32(HI_MASK))
    return lo | hi


def _unpack_bf16_pairs(words):
    lo = pltpu.bitcast(words << 16, F32)
    hi = pltpu.bitcast(words & jnp.uint32(HI_MASK), F32)
    return jnp.concatenate([lo, hi], axis=1)


def _store_packed_rows(ref, words, rows):
    for c in range(PACK_ROWS):
        ref[pl.ds(c, rows, stride=PACK_ROWS), :] = words[:, c * LANES:(c + 1) * LANES]


def _load_packed_rows(ref, rows):
    return jnp.concatenate(
        [ref[pl.ds(c, rows, stride=PACK_ROWS), :] for c in range(PACK_ROWS)], axis=1)


def _split_bf16(x):
    hi = x.astype(BF16)
    lo = (x - hi.astype(F32)).astype(BF16)
    return hi, lo


def _mixer_kernel(*refs):
    for parity in range(2):
        pl.when(pl.program_id(0) % 2 == parity)(functools.partial(_mixer_step, parity, *refs))


def _mixer_step(cur, x_ref, xn_ref, g1_ref, win_ref, vg_ref, ws_ref, bst_ref, lbraw_ref, og_ref,
                wout_ref, g2_ref, wr_ref, br_ref,
                h1_ref, n2_ref, mi_ref, mf_ref, cnt_ref,
                st_ref, carry_ref, wm_ref, mix_ref, proj_ref):
    i = pl.program_id(0)

    @pl.when(i == 0)
    def _init():
        carry_ref[...] = jnp.zeros_like(carry_ref)
        r = lax.broadcasted_iota(jnp.int32, (GMLP_BLOCK, GMLP_BLOCK), 0)
        c = lax.broadcasted_iota(jnp.int32, (GMLP_BLOCK, GMLP_BLOCK), 1)
        keep = (c // CHUNK) <= (r // CHUNK)
        wm_ref[...] = jnp.zeros_like(wm_ref)
        for g in range(HEADS):
            w = jnp.where(keep, ws_ref[g], 0.0).astype(BF16)
            for p in range(TS // GMLP_BLOCK):
                wm_ref[g, p * GMLP_BLOCK:(p + 1) * GMLP_BLOCK,
                       p * GMLP_BLOCK:(p + 1) * GMLP_BLOCK] = w

    @pl.when(i % (SEQ // TS) == 0)
    def _reset_state():
        st_ref[...] = jnp.zeros_like(st_ref)

    def project(n1, slot, c):
        proj_ref[slot, :, c * D_A:(c + 1) * D_A] = jnp.dot(
            n1, win_ref[:, c * D_A:(c + 1) * D_A], preferred_element_type=F32)

    @pl.when(i == 0)
    def _first_projection():
        n1 = _rms(x_ref[...], g1_ref[...]).astype(BF16)
        for c in range(D_IN // D_A):
            project(n1, 0, c)

    n1_next = _rms(xn_ref[...], g1_ref[...]).astype(BF16)
    x = x_ref[...]

    project(n1_next, 1 - cur, 0)
    gu = _gelu(proj_ref[cur, :, 0:D_A])
    project(n1_next, 1 - cur, 1)
    gv = _gelu(proj_ref[cur, :, D_A:2 * D_A])
    for g in range(HEADS):
        sl = slice(g * HEAD_DIM, (g + 1) * HEAD_DIM)
        vh = _rms(gv[:, sl], vg_ref[:, sl]).astype(BF16)
        mixed = jnp.dot(wm_ref[g], vh, preferred_element_type=F32) + bst_ref[:, g:g + 1]
        mix_ref[:, sl] = (gu[:, sl] * mixed).astype(BF16)

    o2 = 2 * D_A
    qr = proj_ref[cur, :, o2:o2 + D_B]
    fr = proj_ref[cur, :, o2 + D_B:o2 + 2 * D_B]
    vv = proj_ref[cur, :, o2 + 2 * D_B:o2 + 3 * D_B].astype(BF16)
    gr = proj_ref[cur, :, o2 + 3 * D_B:o2 + 4 * D_B]

    lbr = lbraw_ref[...]
    lbm = jnp.max(lbr, axis=0, keepdims=True)
    lbe = jnp.exp(lbr - lbm)
    lb = lbe[0:1, :] / jnp.sum(lbe, axis=0, keepdims=True)

    project(n1_next, 1 - cur, 2)
    q = qr * _sigmoid(qr)
    f = lb + (1.0 - lb) * _sigmoid(fr)
    k = 1.0 - f
    lf = jnp.log(f)

    row = lax.broadcasted_iota(jnp.int32, (TS, TS), 0)
    col = lax.broadcasted_iota(jnp.int32, (TS, TS), 1)
    causal = (row >= col) & ((row // CHUNK) == (col // CHUNK))
    tri = jnp.where(causal, 1.0, 0.0).astype(BF16)
    lf_hi, lf_lo = _split_bf16(lf)
    b = (jnp.dot(tri, lf_hi, preferred_element_type=F32)
         + jnp.dot(tri, lf_lo, preferred_element_type=F32))

    def chunk_rows(r):
        return jnp.concatenate(
            [jnp.broadcast_to(b[c * CHUNK + r:c * CHUNK + r + 1, :], (CHUNK, D_B))
             for c in range(CHUNKS_PER_STEP)], axis=0)

    project(n1_next, 1 - cur, 3)
    bref = chunk_rows(CHUNK // 2 - 1)
    blast = chunk_rows(CHUNK - 1)
    qe = (q * jnp.exp(b - bref)).astype(BF16)
    ke = (k * jnp.exp(bref - b)).astype(BF16)
    kd = (k * jnp.exp(blast - b)).astype(BF16)
    qb = (q * jnp.exp(b)).astype(BF16)

    head_sl = [slice(h * HEAD_DIM, (h + 1) * HEAD_DIM) for h in range(HEADS)]
    chunk_sl = [slice(c * CHUNK, (c + 1) * CHUNK) for c in range(CHUNKS_PER_STEP)]
    scores = [lax.dot_general(qe[:, sl], ke[:, sl], NT_DIMS, preferred_element_type=F32)
              for sl in head_sl]
    upd = [[lax.dot_general(vv[rows, sl], kd[rows, sl], TN_DIMS, preferred_element_type=F32)
            for rows in chunk_sl] for sl in head_sl]
    project(n1_next, 1 - cur, 4)
    scores = [jnp.where(causal, sc, 0.0).astype(BF16) for sc in scores]
    intra = [jnp.dot(scores[h], vv[:, head_sl[h]], preferred_element_type=F32)
             for h in range(HEADS)]
    entering = []
    for h in range(HEADS):
        st = st_ref[h]
        per_chunk = []
        for c in range(CHUNKS_PER_STEP):
            per_chunk.append(st.astype(BF16))
            decay = jnp.exp(b[(c + 1) * CHUNK - 1:(c + 1) * CHUNK, head_sl[h]])
            st = st * decay + upd[h][c]
        st_ref[h] = st
        entering.append(per_chunk)
    inter = [jnp.concatenate(
        [lax.dot_general(qb[chunk_sl[c], head_sl[h]], entering[h][c], NT_DIMS,
                         preferred_element_type=F32) for c in range(CHUNKS_PER_STEP)], axis=0)
        for h in range(HEADS)]
    for h in range(HEADS):
        o = _rms(intra[h] + inter[h], og_ref[:, head_sl[h]])
        g_h = gr[:, head_sl[h]]
        mix_ref[:, D_A + h * HEAD_DIM:D_A + (h + 1) * HEAD_DIM] = (
            o * (g_h * _sigmoid(g_h))).astype(BF16)

    h1 = x + jnp.dot(mix_ref[...], wout_ref[...], preferred_element_type=F32)
    _store_packed_rows(h1_ref, _pack_bf16_pairs(h1), TS)
    n2 = _rms(h1, g2_ref[...])
    _store_packed_rows(n2_ref, _pack_bf16_pairs(n2), TS)

    logits = jnp.dot(n2.astype(BF16), wr_ref[...], preferred_element_type=F32)
    logits = jnp.transpose(logits)[0:ROUTER_ROWS, :] + br_ref[...]
    project(n1_next, 1 - cur, 5)
    rid = lax.broadcasted_iota(jnp.int32, (ROUTER_ROWS, TS), 0)
    neg = -jnp.inf
    gl = jnp.where(rid < N_GROUPS, logits, neg)
    gmax = jnp.max(gl, axis=0, keepdims=True)
    g_idx = jnp.min(jnp.where(gl == gmax, rid, ROUTER_ROWS), axis=0, keepdims=True)
    g_prob = 1.0 / jnp.sum(jnp.exp(gl - gmax), axis=0, keepdims=True)
    e_lo = EXPERT_ROW0 + g_idx * EXPERTS_PER_GROUP
    el = jnp.where((rid >= e_lo) & (rid < e_lo + EXPERTS_PER_GROUP), logits, neg)
    m1 = jnp.max(el, axis=0, keepdims=True)
    i1 = jnp.min(jnp.where(el == m1, rid, ROUTER_ROWS), axis=0, keepdims=True)
    el2 = jnp.where(rid == i1, neg, el)
    m2 = jnp.max(el2, axis=0, keepdims=True)
    i2 = jnp.min(jnp.where(el2 == m2, rid, ROUTER_ROWS), axis=0, keepdims=True)
    e21 = jnp.exp(m2 - m1)
    w1 = g_prob / (1.0 + e21)
    w2 = g_prob * e21 / (1.0 + e21)

    hit1 = rid == i1
    hit2 = rid == i2
    onehot = jnp.where(hit1 | hit2, 1.0, 0.0)
    earlier = jnp.where(row < col, 1.0, 0.0).astype(BF16)
    seen = carry_ref[...]
    prior = (jnp.dot(onehot.astype(BF16), earlier, preferred_element_type=F32)
             + jnp.concatenate([seen] * (TS // LANES), axis=1))
    r1 = jnp.sum(jnp.where(hit1, prior, 0.0), axis=0, keepdims=True)
    r2 = jnp.sum(jnp.where(hit2, prior, 0.0), axis=0, keepdims=True)
    seen = seen + jnp.sum(onehot, axis=1, keepdims=True)
    carry_ref[...] = seen
    cnt_ref[...] = seen.astype(jnp.int32)

    pad_i = jnp.zeros((META_ROWS - 2 * TOP_K, TS), jnp.int32)
    mi_ref[...] = jnp.concatenate(
        [i1 - EXPERT_ROW0, i2 - EXPERT_ROW0, r1.astype(jnp.int32), r2.astype(jnp.int32), pad_i],
        axis=0)
    mf_ref[...] = jnp.concatenate([w1, w2, jnp.zeros((META_ROWS - TOP_K, TS), F32)], axis=0)


def _mixer(x2, g1, win, vg, ws, bst, lbraw, og, wout, g2, wr, br):
    n_steps = N_TOK // TS
    const2 = lambda i: (0, 0)
    tok = lambda i: (i, 0)
    lane_tok = lambda i: (0, i)
    return pl.pallas_call(
        _mixer_kernel,
        grid=(n_steps,),
        in_specs=[
            pl.BlockSpec((TS, D_MODEL), tok),
            pl.BlockSpec((TS, D_MODEL), lambda i: (jnp.minimum(i + 1, n_steps - 1), 0)),
            pl.BlockSpec((1, D_MODEL), const2),
            pl.BlockSpec((D_MODEL, D_IN), const2),
            pl.BlockSpec((1, D_A), const2),
            pl.BlockSpec((HEADS, GMLP_BLOCK, GMLP_BLOCK), lambda i: (0, 0, 0)),
            pl.BlockSpec((TS, HEADS), const2),
            pl.BlockSpec((2, D_B), const2),
            pl.BlockSpec((1, D_B), const2),
            pl.BlockSpec((D_MODEL, D_MODEL), const2),
            pl.BlockSpec((1, D_MODEL), const2),
            pl.BlockSpec((D_MODEL, LANES), const2),
            pl.BlockSpec((ROUTER_ROWS, 1), const2),
        ],
        out_specs=[
            pl.BlockSpec((TS * PACK_ROWS, LANES), tok),
            pl.BlockSpec((TS * PACK_ROWS, LANES), tok),
            pl.BlockSpec((META_ROWS, TS), lane_tok),
            pl.BlockSpec((META_ROWS, TS), lane_tok),
            pl.BlockSpec((ROUTER_ROWS, LANES), const2),
        ],
        out_shape=[
            jax.ShapeDtypeStruct((N_TOK * PACK_ROWS, LANES), jnp.uint32),
            jax.ShapeDtypeStruct((N_TOK * PACK_ROWS, LANES), jnp.uint32),
            jax.ShapeDtypeStruct((META_ROWS, N_TOK), jnp.int32),
            jax.ShapeDtypeStruct((META_ROWS, N_TOK), F32),
            jax.ShapeDtypeStruct((ROUTER_ROWS, LANES), jnp.int32),
        ],
        scratch_shapes=[
            pltpu.VMEM((HEADS, HEAD_DIM, HEAD_DIM), F32),
            pltpu.VMEM((ROUTER_ROWS, LANES), F32),
            pltpu.VMEM((HEADS, TS, TS), BF16),
            pltpu.VMEM((TS, D_MODEL), BF16),
            pltpu.VMEM((2, TS, D_IN), F32),
        ],
        compiler_params=pltpu.CompilerParams(
            dimension_semantics=("arbitrary",),
            vmem_limit_bytes=56 * 1024 * 1024,
        ),
        name="mixer",
    )(x2, x2, g1, win, vg, ws, bst, lbraw, og, wout, g2, wr, br)


FILL_UNROLL = 8


def _invert_kernel(dest_ref, gap_lo_ref, gap_hi_ref, inv_ref):
    for e in range(N_EXPERTS):
        first = gap_lo_ref[e] // FILL_UNROLL

        def fill(g, carry):
            for u in range(FILL_UNROLL):
                inv_ref[g * FILL_UNROLL + u] = -1
            return carry

        lax.fori_loop(first, gap_hi_ref[e] // FILL_UNROLL, fill, 0)

    def place(a, carry):
        inv_ref[dest_ref[a]] = a
        return carry

    lax.fori_loop(0, N_ASSIGN, place, 0, unroll=16)


def _invert(dest, gap_lo, gap_hi):
    smem = pl.BlockSpec(memory_space=pltpu.SMEM)
    return pl.pallas_call(
        _invert_kernel,
        in_specs=[smem, smem, smem],
        out_specs=smem,
        out_shape=jax.ShapeDtypeStruct((INV_LEN,), jnp.int32),
        name="invert",
    )(dest, gap_lo, gap_hi)


def _expert_kernel(bexp_ref, nused_ref, src_hbm, dst_hbm, n2_hbm, wg_ref, wu_ref, wd_ref, y_hbm,
                   src_smem, dst_smem, n2_vmem, xbuf, obuf, wg_bf, wu_bf, wd_bf,
                   src_sem, dst_sem, nsem, ssem):
    i = pl.program_id(0)
    nused = nused_ref[0]
    last = nused - 1

    def src_copy(blk, s):
        return pltpu.make_async_copy(src_hbm.at[blk], src_smem.at[pl.ds(s, 1)], src_sem.at[s])

    def dst_copy(blk, s):
        return pltpu.make_async_copy(dst_hbm.at[blk], dst_smem.at[pl.ds(s, 1)], dst_sem.at[s])

    def gather(s):
        for j in range(MOE_BLOCK):
            r = pl.multiple_of(src_smem[s, j], PACK_ROWS)
            xbuf[s, pl.ds(j * PACK_ROWS, PACK_ROWS), :] = n2_vmem[pl.ds(r, PACK_ROWS), :]

    def start_scatter(s):
        for j in range(MOE_BLOCK):
            r = pl.multiple_of(dst_smem[s, j], PACK_ROWS)
            pltpu.make_async_copy(obuf.at[s, pl.ds(j * PACK_ROWS, PACK_ROWS)],
                                  y_hbm.at[pl.ds(r, PACK_ROWS)], ssem.at[s]
                                  ).start(priority=j % 2)

    def wait_scatter(s):
        pltpu.make_async_copy(obuf.at[s], y_hbm.at[pl.ds(0, BLOCK_PACK_ROWS)],
                              ssem.at[s]).wait()

    @pl.when(i == 0)
    def _prologue():
        obuf[...] = jnp.zeros_like(obuf)
        for s in range(N_SLOTS - 1):
            pltpu.make_async_copy(
                obuf.at[s],
                y_hbm.at[pl.ds((N_ASSIGN + s * MOE_BLOCK) * PACK_ROWS, BLOCK_PACK_ROWS)],
                ssem.at[s]).start()
        resident = pltpu.make_async_copy(n2_hbm, n2_vmem, nsem)
        resident.start()
        src_copy(0, 0).start()
        src_copy(0, 0).wait()
        resident.wait()
        gather(0)
        src_copy(jnp.minimum(1, last), 1).start()
        dst_copy(N_BLOCKS, N_SLOTS - 1).start()

    prev = bexp_ref[jnp.maximum(i - 1, 0)]
    changed = (i == 0) | (bexp_ref[i] != prev)

    @pl.when(changed)
    def _cast_weights():
        wg_bf[...] = wg_ref[0].astype(BF16)
        wu_bf[...] = wu_ref[0].astype(BF16)
        wd_bf[...] = wd_ref[0].astype(BF16)

    def active(cur, nxt, prv):
        src_copy(0, nxt).wait()
        dst_copy(0, prv).wait()
        wait_scatter(cur)
        start_scatter(prv)
        gather(nxt)
        src_copy(jnp.minimum(i + 2, last), prv).start()
        dst_copy(i, cur).start()
        xb = _unpack_bf16_pairs(_load_packed_rows(xbuf.at[cur], MOE_BLOCK)).astype(BF16)
        hg = jnp.dot(xb, wg_bf[...], preferred_element_type=F32)
        hu = jnp.dot(xb, wu_bf[...], preferred_element_type=F32)
        hh = (hg * _sigmoid(hg) * hu).astype(BF16)
        y = jnp.dot(hh, wd_bf[...], preferred_element_type=F32)
        _store_packed_rows(obuf.at[cur], _pack_bf16_pairs(y), MOE_BLOCK)

    def epilogue(cur, nxt, prv):
        dst_copy(0, cur).wait()
        start_scatter(cur)
        wait_scatter(nxt)
        wait_scatter(prv)
        wait_scatter(cur)
        src_copy(0, prv).wait()

    for cur in range(N_SLOTS):
        slots = (cur, (cur + 1) % N_SLOTS, (cur + 2) % N_SLOTS)

        @pl.when((i < nused) & (i % N_SLOTS == cur))
        def _():
            active(*slots)

        @pl.when((i == last) & (i % N_SLOTS == cur))
        def _():
            epilogue(*slots)


_EXPERT_W_BYTES = 3 * D_MODEL * D_EXPERT * 4
EXPERT_VMEM_LIMIT = (
    N_TOK * PACK_COLS * 4
    + 2 * _EXPERT_W_BYTES
    + _EXPERT_W_BYTES // 2
    + 2 * N_SLOTS * MOE_BLOCK * PACK_COLS * 4
    + 4 * 1024 * 1024)


def _experts(bexp, nused, src2d, dst2d, n2, w_gate, w_up, w_down):
    def w_blk(i, bexp, nused):
        return (bexp[i], 0, 0)

    any_spec = pl.BlockSpec(memory_space=pl.ANY)
    return pl.pallas_call(
        _expert_kernel,
        grid_spec=pltpu.PrefetchScalarGridSpec(
            num_scalar_prefetch=2,
            grid=(N_BLOCKS,),
            in_specs=[
                any_spec, any_spec, any_spec,
                pl.BlockSpec((1, D_MODEL, D_EXPERT), w_blk),
                pl.BlockSpec((1, D_MODEL, D_EXPERT), w_blk),
                pl.BlockSpec((1, D_EXPERT, D_MODEL), w_blk),
            ],
            out_specs=any_spec,
            scratch_shapes=[
                pltpu.SMEM((N_SLOTS, MOE_BLOCK), jnp.int32),
                pltpu.SMEM((N_SLOTS, MOE_BLOCK), jnp.int32),
                pltpu.VMEM((N_TOK * PACK_ROWS, LANES), jnp.uint32),
                pltpu.VMEM((N_SLOTS, MOE_BLOCK * PACK_ROWS, LANES), jnp.uint32),
                pltpu.VMEM((N_SLOTS, BLOCK_PACK_ROWS, LANES), jnp.uint32),
                pltpu.VMEM((D_MODEL, D_EXPERT), BF16),
                pltpu.VMEM((D_MODEL, D_EXPERT), BF16),
                pltpu.VMEM((D_EXPERT, D_MODEL), BF16),
                pltpu.SemaphoreType.DMA((N_SLOTS,)),
                pltpu.SemaphoreType.DMA((N_SLOTS,)),
                pltpu.SemaphoreType.DMA,
                pltpu.SemaphoreType.DMA((N_SLOTS,)),
            ],
        ),
        out_shape=jax.ShapeDtypeStruct(((N_ASSIGN + DUMMY_ROWS) * PACK_ROWS, LANES), jnp.uint32),
        compiler_params=pltpu.CompilerParams(
            dimension_semantics=("arbitrary",),
            vmem_limit_bytes=EXPERT_VMEM_LIMIT,
        ),
        name="experts",
    )(bexp, nused, src2d, dst2d, n2, w_gate, w_up, w_down)


def _combine_kernel(h1_ref, mf_ref, fg_ref, y0_ref, y1_ref, out_ref):
    w = mf_ref[...]
    y0 = _unpack_bf16_pairs(_load_packed_rows(y0_ref, TD))
    y1 = _unpack_bf16_pairs(_load_packed_rows(y1_ref, TD))
    h1 = _unpack_bf16_pairs(_load_packed_rows(h1_ref, TD))
    h = h1 + y0 * w[:, 0:1] + y1 * w[:, 1:2]
    out_ref[...] = _rms(h, fg_ref[...])


def _combine(h1, mf, fg, y):
    n_steps = N_TOK // TD
    tok = lambda i: (i, 0)
    return pl.pallas_call(
        _combine_kernel,
        grid=(n_steps,),
        in_specs=[
            pl.BlockSpec((TD * PACK_ROWS, LANES), tok),
            pl.BlockSpec((TD, TOP_K), tok),
            pl.BlockSpec((1, D_MODEL), lambda i: (0, 0)),
            pl.BlockSpec((TD * PACK_ROWS, LANES), tok),
            pl.BlockSpec((TD * PACK_ROWS, LANES), lambda i: (i + n_steps, 0)),
        ],
        out_specs=pl.BlockSpec((TD, D_MODEL), tok),
        out_shape=jax.ShapeDtypeStruct((N_TOK, D_MODEL), F32),
        compiler_params=pltpu.CompilerParams(dimension_semantics=("arbitrary",)),
        name="combine",
    )(h1, mf, fg, y, y)


def kernel(x, norm1_gain, w_in, gmlp_v_gain, gmlp_w_s, gmlp_b_s, hgrn_lower_bounds,
           hgrn_out_gain, w_out, norm2_gain, w_group_router, b_group_router,
           w_expert_router, b_expert_router, w_gate, w_up, w_down, final_gain):
    l = 0
    x2 = x.reshape(N_TOK, D_MODEL)
    bst = jnp.tile(jnp.transpose(gmlp_b_s[l]), (TS // GMLP_BLOCK, 1))
    w_router = jnp.concatenate([w_group_router[l], w_expert_router[l]], axis=1)
    w_router = jnp.pad(w_router, ((0, 0), (0, LANES - w_router.shape[1]))).astype(BF16)
    b_router = jnp.concatenate([b_group_router[l], b_expert_router[l]])
    b_router = jnp.pad(b_router, (0, ROUTER_ROWS - b_router.shape[0])).reshape(ROUTER_ROWS, 1)

    h1, n2, meta_i, meta_f, counts = _mixer(
        x2, norm1_gain[l].reshape(1, D_MODEL), w_in[l].astype(BF16),
        gmlp_v_gain[l].reshape(1, D_A), gmlp_w_s[l], bst,
        hgrn_lower_bounds, hgrn_out_gain[l].reshape(1, D_B), w_out[l].astype(BF16),
        norm2_gain[l].reshape(1, D_MODEL), w_router, b_router)

    cnt = counts[EXPERT_ROW0:EXPERT_ROW0 + N_EXPERTS, 0]
    padded = ((cnt + MOE_BLOCK - 1) // MOE_BLOCK) * MOE_BLOCK
    pend = jnp.cumsum(padded)
    pstart = pend - padded
    eid = meta_i[0:TOP_K]
    rank = meta_i[TOP_K:2 * TOP_K]
    base = jnp.sum(jnp.where(eid[:, :, None] == jnp.arange(N_EXPERTS)[None, None, :],
                             pstart[None, None, :], 0), axis=-1)
    dest = (base + rank).astype(jnp.int32).reshape(N_ASSIGN)
    gap_lo = (pstart + cnt).astype(jnp.int32)
    gap_hi = jnp.concatenate([pstart[1:], jnp.full((1,), INV_LEN)]).astype(jnp.int32)
    inv = _invert(dest, gap_lo, gap_hi)
    pos = jnp.arange(INV_LEN, dtype=jnp.int32)
    blk = jnp.where(pos < SORTED_ROWS, pos // MOE_BLOCK, N_SLOTS - 1)
    dummy = N_ASSIGN + (blk % N_SLOTS) * MOE_BLOCK + pos % MOE_BLOCK
    src2d = (jnp.where(inv >= 0, inv % N_TOK, 0) * PACK_ROWS)[:SORTED_ROWS].reshape(
        N_BLOCKS, 1, MOE_BLOCK)
    dst2d = (jnp.where(inv >= 0, inv, dummy) * PACK_ROWS).reshape(N_BLOCKS + 1, 1, MOE_BLOCK)
    blk_start = jnp.arange(N_BLOCKS, dtype=jnp.int32) * MOE_BLOCK
    bexp = jnp.clip(jnp.sum(blk_start[:, None] >= pend[None, :], axis=1), 0,
                    N_EXPERTS - 1).astype(jnp.int32)
    nused = (pend[-1:] // MOE_BLOCK).astype(jnp.int32)

    y = _experts(bexp, nused, src2d, dst2d, n2, w_gate[l], w_up[l], w_down[l])
    out = _combine(h1, meta_f[0:TOP_K].T, final_gain.reshape(1, D_MODEL), y)
    return out.reshape(BATCH, SEQ, D_MODEL)
```

```python
import functools

import jax
import jax.numpy as jnp
from jax import lax
from jax.experimental import pallas as pl
from jax.experimental.pallas import tpu as pltpu

D_MODEL = 1024
BATCH = 2
SEQ = 8192
N_TOK = BATCH * SEQ
CHUNK = 64
EPS = 1e-6
D_A = 512
HEADS = 4
HEAD_DIM = 128
GMLP_BLOCK = 128
D_B = 512
D_IN = 3072
N_GROUPS = 4
EXPERTS_PER_GROUP = 8
N_EXPERTS = 32
TOP_K = 2
D_EXPERT = 512

LANES = 128
ROUTER_ROWS = 48
EXPERT_ROW0 = N_GROUPS
META_ROWS = 8
PACK_COLS = D_MODEL // 2
PACK_ROWS = PACK_COLS // LANES
HI_MASK = 0xFFFF0000

TS = 256
CHUNKS_PER_STEP = TS // CHUNK
MOE_BLOCK = 256
N_BLOCKS = N_TOK * TOP_K // MOE_BLOCK + N_EXPERTS
SORTED_ROWS = N_BLOCKS * MOE_BLOCK
N_ASSIGN = N_TOK * TOP_K
N_SLOTS = 3
DUMMY_ROWS = N_SLOTS * MOE_BLOCK
BLOCK_PACK_ROWS = MOE_BLOCK * PACK_ROWS
INV_LEN = SORTED_ROWS + MOE_BLOCK
TD = 512

F32 = jnp.float32
BF16 = jnp.bfloat16
NT_DIMS = (((1,), (1,)), ((), ()))
TN_DIMS = (((0,), (0,)), ((), ()))


def _sigmoid(x):
    return 1.0 / (1.0 + jnp.exp(-x))


def _gelu(x):
    return 0.5 * x * (1.0 + lax.erf(x * (2.0 ** -0.5)))


def _rms(x, gain):
    return x * lax.rsqrt(jnp.mean(x * x, axis=-1, keepdims=True) + EPS) * gain


def _pack_bf16_pairs(x):
    lo = pltpu.bitcast(x[:, 0:PACK_COLS].astype(BF16).astype(F32), jnp.uint32) >> 16
    hi = (pltpu.bitcast(x[:, PACK_COLS:D_MODEL].astype(BF16).astype(F32), jnp.uint32)
          & jnp.uint32(HI_MASK))
    return lo | hi


def _unpack_bf16_pairs(words):
    lo = pltpu.bitcast(words << 16, F32)
    hi = pltpu.bitcast(words & jnp.uint32(HI_MASK), F32)
    return jnp.concatenate([lo, hi], axis=1)


def _store_packed_rows(ref, words, rows):
    for c in range(PACK_ROWS):
        ref[pl.ds(c, rows, stride=PACK_ROWS), :] = words[:, c * LANES:(c + 1) * LANES]


def _load_packed_rows(ref, rows):
    return jnp.concatenate(
        [ref[pl.ds(c, rows, stride=PACK_ROWS), :] for c in range(PACK_ROWS)], axis=1)


def _split_bf16(x):
    hi = x.astype(BF16)
    lo = (x - hi.astype(F32)).astype(BF16)
    return hi, lo


def _mixer_kernel(*refs):
    for parity in range(2):
        pl.when(pl.program_id(0) % 2 == parity)(functools.partial(_mixer_step, parity, *refs))


def _mixer_step(cur, x_ref, xn_ref, g1_ref, win_ref, vg_ref, ws_ref, bst_ref, lbraw_ref, og_ref,
                wout_ref, g2_ref, wr_ref, br_ref,
                h1_ref, n2_ref, mi_ref, mf_ref, cnt_ref,
                st_ref, carry_ref, wm_ref, mix_ref, proj_ref, logit_ref):
    i = pl.program_id(0)

    @pl.when(i == 0)
    def _init():
        carry_ref[...] = jnp.zeros_like(carry_ref)
        logit_ref[...] = jnp.zeros_like(logit_ref)
        r = lax.broadcasted_iota(jnp.int32, (GMLP_BLOCK, GMLP_BLOCK), 0)
        c = lax.broadcasted_iota(jnp.int32, (GMLP_BLOCK, GMLP_BLOCK), 1)
        keep = (c // CHUNK) <= (r // CHUNK)
        wm_ref[...] = jnp.zeros_like(wm_ref)
        for g in range(HEADS):
            w = jnp.where(keep, ws_ref[g], 0.0).astype(BF16)
            for p in range(TS // GMLP_BLOCK):
                wm_ref[g, p * GMLP_BLOCK:(p + 1) * GMLP_BLOCK,
                       p * GMLP_BLOCK:(p + 1) * GMLP_BLOCK] = w

    @pl.when(i % (SEQ // TS) == 0)
    def _reset_state():
        st_ref[...] = jnp.zeros_like(st_ref)

    def project(n1, slot, c):
        proj_ref[slot, :, c * D_A:(c + 1) * D_A] = jnp.dot(
            n1, win_ref[:, c * D_A:(c + 1) * D_A], preferred_element_type=F32)

    @pl.when(i == 0)
    def _first_projection():
        n1 = _rms(x_ref[...], g1_ref[...]).astype(BF16)
        for c in range(D_IN // D_A):
            project(n1, 0, c)

    n1_next = _rms(xn_ref[...], g1_ref[...]).astype(BF16)
    x = x_ref[...]

    project(n1_next, 1 - cur, 0)
    gu = _gelu(proj_ref[cur, :, 0:D_A])
    project(n1_next, 1 - cur, 1)
    gv = _gelu(proj_ref[cur, :, D_A:2 * D_A])
    for g in range(HEADS):
        sl = slice(g * HEAD_DIM, (g + 1) * HEAD_DIM)
        vh = _rms(gv[:, sl], vg_ref[:, sl]).astype(BF16)
        mixed = jnp.dot(wm_ref[g], vh, preferred_element_type=F32) + bst_ref[:, g:g + 1]
        mix_ref[:, sl] = (gu[:, sl] * mixed).astype(BF16)

    _route(logit_ref[1 - cur], jnp.maximum(i - 1, 0), jnp.where(i > 0, 1.0, 0.0),
           carry_ref, cnt_ref, mi_ref, mf_ref)

    o2 = 2 * D_A
    qr = proj_ref[cur, :, o2:o2 + D_B]
    fr = proj_ref[cur, :, o2 + D_B:o2 + 2 * D_B]
    vv = proj_ref[cur, :, o2 + 2 * D_B:o2 + 3 * D_B].astype(BF16)
    gr = proj_ref[cur, :, o2 + 3 * D_B:o2 + 4 * D_B]

    lbr = lbraw_ref[...]
    lbm = jnp.max(lbr, axis=0, keepdims=True)
    lbe = jnp.exp(lbr - lbm)
    lb = lbe[0:1, :] / jnp.sum(lbe, axis=0, keepdims=True)

    project(n1_next, 1 - cur, 2)
    q = qr * _sigmoid(qr)
    f = lb + (1.0 - lb) * _sigmoid(fr)
    k = 1.0 - f
    lf = jnp.log(f)

    row = lax.broadcasted_iota(jnp.int32, (TS, TS), 0)
    col = lax.broadcasted_iota(jnp.int32, (TS, TS), 1)
    causal = (row >= col) & ((row // CHUNK) == (col // CHUNK))
    tri = jnp.where(causal, 1.0, 0.0).astype(BF16)
    lf_hi, lf_lo = _split_bf16(lf)
    b = (jnp.dot(tri, lf_hi, preferred_element_type=F32)
         + jnp.dot(tri, lf_lo, preferred_element_type=F32))

    def chunk_rows(r):
        return jnp.concatenate(
            [jnp.broadcast_to(b[c * CHUNK + r:c * CHUNK + r + 1, :], (CHUNK, D_B))
             for c in range(CHUNKS_PER_STEP)], axis=0)

    project(n1_next, 1 - cur, 3)
    bref = chunk_rows(CHUNK // 2 - 1)
    blast = chunk_rows(CHUNK - 1)
    qe = (q * jnp.exp(b - bref)).astype(BF16)
    ke = (k * jnp.exp(bref - b)).astype(BF16)
    kd = (k * jnp.exp(blast - b)).astype(BF16)
    qb = (q * jnp.exp(b)).astype(BF16)

    head_sl = [slice(h * HEAD_DIM, (h + 1) * HEAD_DIM) for h in range(HEADS)]
    chunk_sl = [slice(c * CHUNK, (c + 1) * CHUNK) for c in range(CHUNKS_PER_STEP)]
    scores = [lax.dot_general(qe[:, sl], ke[:, sl], NT_DIMS, preferred_element_type=F32)
              for sl in head_sl]
    upd = [[lax.dot_general(vv[rows, sl], kd[rows, sl], TN_DIMS, preferred_element_type=F32)
            for rows in chunk_sl] for sl in head_sl]
    project(n1_next, 1 - cur, 4)
    scores = [jnp.where(causal, sc, 0.0).astype(BF16) for sc in scores]
    intra = [jnp.dot(scores[h], vv[:, head_sl[h]], preferred_element_type=F32)
             for h in range(HEADS)]
    entering = []
    for h in range(HEADS):
        st = st_ref[h]
        per_chunk = []
        for c in range(CHUNKS_PER_STEP):
            per_chunk.append(st.astype(BF16))
            decay = jnp.exp(b[(c + 1) * CHUNK - 1:(c + 1) * CHUNK, head_sl[h]])
            st = st * decay + upd[h][c]
        st_ref[h] = st
        entering.append(per_chunk)
    inter = [jnp.concatenate(
        [lax.dot_general(qb[chunk_sl[c], head_sl[h]], entering[h][c], NT_DIMS,
                         preferred_element_type=F32) for c in range(CHUNKS_PER_STEP)], axis=0)
        for h in range(HEADS)]
    for h in range(HEADS):
        o = _rms(intra[h] + inter[h], og_ref[:, head_sl[h]])
        g_h = gr[:, head_sl[h]]
        mix_ref[:, D_A + h * HEAD_DIM:D_A + (h + 1) * HEAD_DIM] = (
            o * (g_h * _sigmoid(g_h))).astype(BF16)

    h1 = x + jnp.dot(mix_ref[...], wout_ref[...], preferred_element_type=F32)
    h1_ref[...] = h1
    n2 = _rms(h1, g2_ref[...])
    _store_packed_rows(n2_ref, _pack_bf16_pairs(n2), TS)

    logits = jnp.dot(n2.astype(BF16), wr_ref[...], preferred_element_type=F32)
    logits = jnp.transpose(logits)[0:ROUTER_ROWS, :] + br_ref[...]
    project(n1_next, 1 - cur, 5)
    logit_ref[cur] = logits

    @pl.when(i == N_TOK // TS - 1)
    def _route_last_tile():
        _route(logits, i, 1.0, carry_ref, cnt_ref, mi_ref, mf_ref)


def _route(logits, tile, live, carry_ref, cnt_ref, mi_ref, mf_ref):
    rid = lax.broadcasted_iota(jnp.int32, (ROUTER_ROWS, TS), 0)
    row = lax.broadcasted_iota(jnp.int32, (TS, TS), 0)
    col = lax.broadcasted_iota(jnp.int32, (TS, TS), 1)
    neg = -jnp.inf
    gl = jnp.where(rid < N_GROUPS, logits, neg)
    gmax = jnp.max(gl, axis=0, keepdims=True)
    g_idx = jnp.min(jnp.where(gl == gmax, rid, ROUTER_ROWS), axis=0, keepdims=True)
    g_prob = 1.0 / jnp.sum(jnp.exp(gl - gmax), axis=0, keepdims=True)
    e_lo = EXPERT_ROW0 + g_idx * EXPERTS_PER_GROUP
    el = jnp.where((rid >= e_lo) & (rid < e_lo + EXPERTS_PER_GROUP), logits, neg)
    m1 = jnp.max(el, axis=0, keepdims=True)
    i1 = jnp.min(jnp.where(el == m1, rid, ROUTER_ROWS), axis=0, keepdims=True)
    el2 = jnp.where(rid == i1, neg, el)
    m2 = jnp.max(el2, axis=0, keepdims=True)
    i2 = jnp.min(jnp.where(el2 == m2, rid, ROUTER_ROWS), axis=0, keepdims=True)
    e21 = jnp.exp(m2 - m1)
    w1 = g_prob / (1.0 + e21)
    w2 = g_prob * e21 / (1.0 + e21)

    hit1 = rid == i1
    hit2 = rid == i2
    onehot = jnp.where(hit1 | hit2, live, 0.0)
    earlier = jnp.where(row < col, 1.0, 0.0).astype(BF16)
    seen = carry_ref[...]
    prior = (jnp.dot(onehot.astype(BF16), earlier, preferred_element_type=F32)
             + jnp.concatenate([seen] * (TS // LANES), axis=1))
    r1 = jnp.sum(jnp.where(hit1, prior, 0.0), axis=0, keepdims=True)
    r2 = jnp.sum(jnp.where(hit2, prior, 0.0), axis=0, keepdims=True)
    seen = seen + jnp.sum(onehot, axis=1, keepdims=True)
    carry_ref[...] = seen
    cnt_ref[...] = seen.astype(jnp.int32)

    cols = pl.ds(pl.multiple_of(tile * TS, TS), TS)
    pad_i = jnp.zeros((META_ROWS - 2 * TOP_K, TS), jnp.int32)
    mi_ref[:, cols] = jnp.concatenate(
        [i1 - EXPERT_ROW0, i2 - EXPERT_ROW0, r1.astype(jnp.int32), r2.astype(jnp.int32), pad_i],
        axis=0)
    mf_ref[:, cols] = jnp.concatenate([w1, w2, jnp.zeros((META_ROWS - TOP_K, TS), F32)], axis=0)


def _mixer(x2, g1, win, vg, ws, bst, lbraw, og, wout, g2, wr, br):
    n_steps = N_TOK // TS
    const2 = lambda i: (0, 0)
    tok = lambda i: (i, 0)
    return pl.pallas_call(
        _mixer_kernel,
        grid=(n_steps,),
        in_specs=[
            pl.BlockSpec((TS, D_MODEL), tok),
            pl.BlockSpec((TS, D_MODEL), lambda i: (jnp.minimum(i + 1, n_steps - 1), 0)),
            pl.BlockSpec((1, D_MODEL), const2),
            pl.BlockSpec((D_MODEL, D_IN), const2),
            pl.BlockSpec((1, D_A), const2),
            pl.BlockSpec((HEADS, GMLP_BLOCK, GMLP_BLOCK), lambda i: (0, 0, 0)),
            pl.BlockSpec((TS, HEADS), const2),
            pl.BlockSpec((2, D_B), const2),
            pl.BlockSpec((1, D_B), const2),
            pl.BlockSpec((D_MODEL, D_MODEL), const2),
            pl.BlockSpec((1, D_MODEL), const2),
            pl.BlockSpec((D_MODEL, LANES), const2),
            pl.BlockSpec((ROUTER_ROWS, 1), const2),
        ],
        out_specs=[
            pl.BlockSpec((TS, D_MODEL), tok),
            pl.BlockSpec((TS * PACK_ROWS, LANES), tok),
            pl.BlockSpec((META_ROWS, N_TOK), const2),
            pl.BlockSpec((META_ROWS, N_TOK), const2),
            pl.BlockSpec((ROUTER_ROWS, LANES), const2),
        ],
        out_shape=[
            jax.ShapeDtypeStruct((N_TOK, D_MODEL), F32),
            jax.ShapeDtypeStruct((N_TOK * PACK_ROWS, LANES), jnp.uint32),
            jax.ShapeDtypeStruct((META_ROWS, N_TOK), jnp.int32),
            jax.ShapeDtypeStruct((META_ROWS, N_TOK), F32),
            jax.ShapeDtypeStruct((ROUTER_ROWS, LANES), jnp.int32),
        ],
        scratch_shapes=[
            pltpu.VMEM((HEADS, HEAD_DIM, HEAD_DIM), F32),
            pltpu.VMEM((ROUTER_ROWS, LANES), F32),
            pltpu.VMEM((HEADS, TS, TS), BF16),
            pltpu.VMEM((TS, D_MODEL), BF16),
            pltpu.VMEM((2, TS, D_IN), F32),
            pltpu.VMEM((2, ROUTER_ROWS, TS), F32),
        ],
        compiler_params=pltpu.CompilerParams(
            dimension_semantics=("arbitrary",),
            vmem_limit_bytes=56 * 1024 * 1024,
        ),
        name="mixer",
    )(x2, x2, g1, win, vg, ws, bst, lbraw, og, wout, g2, wr, br)


FILL_UNROLL = 8


def _invert_kernel(dest_ref, gap_lo_ref, gap_hi_ref, inv_ref):
    for e in range(N_EXPERTS):
        first = gap_lo_ref[e] // FILL_UNROLL

        def fill(g, carry):
            for u in range(FILL_UNROLL):
                inv_ref[g * FILL_UNROLL + u] = -1
            return carry

        lax.fori_loop(first, gap_hi_ref[e] // FILL_UNROLL, fill, 0)

    def place(a, carry):
        inv_ref[dest_ref[a]] = a
        return carry

    lax.fori_loop(0, N_ASSIGN, place, 0, unroll=16)


def _invert(dest, gap_lo, gap_hi):
    smem = pl.BlockSpec(memory_space=pltpu.SMEM)
    return pl.pallas_call(
        _invert_kernel,
        in_specs=[smem, smem, smem],
        out_specs=smem,
        out_shape=jax.ShapeDtypeStruct((INV_LEN,), jnp.int32),
        name="invert",
    )(dest, gap_lo, gap_hi)


def _expert_kernel(bexp_ref, nused_ref, src_hbm, dst_hbm, n2_hbm, wg_ref, wu_ref, wd_ref, y_hbm,
                   src_smem, dst_smem, n2_vmem, xb0, xb1, xb2, ob0, ob1, ob2, wg_bf, wu_bf, wd_bf,
                   src_sem, dst_sem, nsem, ssem):
    i = pl.program_id(0)
    nused = nused_ref[0]
    last = nused - 1
    xbuf = (xb0, xb1, xb2)
    obuf = (ob0, ob1, ob2)

    def src_copy(blk, s):
        return pltpu.make_async_copy(src_hbm.at[blk], src_smem.at[pl.ds(s, 1)], src_sem.at[s])

    def dst_copy(blk, s):
        return pltpu.make_async_copy(dst_hbm.at[blk], dst_smem.at[pl.ds(s, 1)], dst_sem.at[s])

    def gather(s):
        for j in range(MOE_BLOCK):
            r = pl.multiple_of(src_smem[s, j], PACK_ROWS)
            xbuf[s][pl.ds(j * PACK_ROWS, PACK_ROWS), :] = n2_vmem[pl.ds(r, PACK_ROWS), :]

    def start_scatter(s):
        for j in range(MOE_BLOCK):
            r = pl.multiple_of(dst_smem[s, j], PACK_ROWS)
            pltpu.make_async_copy(obuf[s].at[pl.ds(j * PACK_ROWS, PACK_ROWS)],
                                  y_hbm.at[pl.ds(r, PACK_ROWS)], ssem.at[s]
                                  ).start(priority=j % 2)

    def wait_scatter(s):
        pltpu.make_async_copy(obuf[s], y_hbm.at[pl.ds(0, BLOCK_PACK_ROWS)],
                              ssem.at[s]).wait()

    @pl.when(i == 0)
    def _prologue():
        for o in obuf:
            o[...] = jnp.zeros_like(o)
        for s in range(N_SLOTS - 1):
            pltpu.make_async_copy(
                obuf[s],
                y_hbm.at[pl.ds((N_ASSIGN + s * MOE_BLOCK) * PACK_ROWS, BLOCK_PACK_ROWS)],
                ssem.at[s]).start()
        resident = pltpu.make_async_copy(n2_hbm, n2_vmem, nsem)
        resident.start()
        src_copy(0, 0).start()
        src_copy(0, 0).wait()
        resident.wait()
        gather(0)
        src_copy(jnp.minimum(1, last), 1).start()
        dst_copy(N_BLOCKS, N_SLOTS - 1).start()

    prev = bexp_ref[jnp.maximum(i - 1, 0)]
    changed = (i == 0) | (bexp_ref[i] != prev)

    @pl.when(changed)
    def _cast_weights():
        wg_bf[...] = wg_ref[0].astype(BF16)
        wu_bf[...] = wu_ref[0].astype(BF16)
        wd_bf[...] = wd_ref[0].astype(BF16)

    def active(cur, nxt, prv):
        src_copy(0, nxt).wait()
        dst_copy(0, prv).wait()
        wait_scatter(cur)
        start_scatter(prv)
        gather(nxt)
        src_copy(jnp.minimum(i + 2, last), prv).start()
        dst_copy(i, cur).start()
        xb = _unpack_bf16_pairs(_load_packed_rows(xbuf[cur], MOE_BLOCK)).astype(BF16)
        hg = jnp.dot(xb, wg_bf[...], preferred_element_type=F32)
        hu = jnp.dot(xb, wu_bf[...], preferred_element_type=F32)
        hh = (hg * _sigmoid(hg) * hu).astype(BF16)
        y = jnp.dot(hh, wd_bf[...], preferred_element_type=F32)
        _store_packed_rows(obuf[cur], _pack_bf16_pairs(y), MOE_BLOCK)

    def epilogue(cur, nxt, prv):
        dst_copy(0, cur).wait()
        start_scatter(cur)
        wait_scatter(nxt)
        wait_scatter(prv)
        wait_scatter(cur)
        src_copy(0, prv).wait()

    for cur in range(N_SLOTS):
        slots = (cur, (cur + 1) % N_SLOTS, (cur + 2) % N_SLOTS)

        @pl.when((i < nused) & (i % N_SLOTS == cur))
        def _():
            active(*slots)

        @pl.when((i == last) & (i % N_SLOTS == cur))
        def _():
            epilogue(*slots)


_EXPERT_W_BYTES = 3 * D_MODEL * D_EXPERT * 4
EXPERT_VMEM_LIMIT = (
    N_TOK * PACK_COLS * 4
    + 2 * _EXPERT_W_BYTES
    + _EXPERT_W_BYTES // 2
    + 2 * N_SLOTS * MOE_BLOCK * PACK_COLS * 4
    + 4 * 1024 * 1024)


def _experts(bexp, nused, src2d, dst2d, n2, w_gate, w_up, w_down):
    def w_blk(i, bexp, nused):
        return (bexp[i], 0, 0)

    any_spec = pl.BlockSpec(memory_space=pl.ANY)
    return pl.pallas_call(
        _expert_kernel,
        grid_spec=pltpu.PrefetchScalarGridSpec(
            num_scalar_prefetch=2,
            grid=(N_BLOCKS,),
            in_specs=[
                any_spec, any_spec, any_spec,
                pl.BlockSpec((1, D_MODEL, D_EXPERT), w_blk),
                pl.BlockSpec((1, D_MODEL, D_EXPERT), w_blk),
                pl.BlockSpec((1, D_EXPERT, D_MODEL), w_blk),
            ],
            out_specs=any_spec,
            scratch_shapes=[
                pltpu.SMEM((N_SLOTS, MOE_BLOCK), jnp.int32),
                pltpu.SMEM((N_SLOTS, MOE_BLOCK), jnp.int32),
                pltpu.VMEM((N_TOK * PACK_ROWS, LANES), jnp.uint32),
            ] + [pltpu.VMEM((BLOCK_PACK_ROWS, LANES), jnp.uint32)] * (2 * N_SLOTS) + [
                pltpu.VMEM((D_MODEL, D_EXPERT), BF16),
                pltpu.VMEM((D_MODEL, D_EXPERT), BF16),
                pltpu.VMEM((D_EXPERT, D_MODEL), BF16),
                pltpu.SemaphoreType.DMA((N_SLOTS,)),
                pltpu.SemaphoreType.DMA((N_SLOTS,)),
                pltpu.SemaphoreType.DMA,
                pltpu.SemaphoreType.DMA((N_SLOTS,)),
            ],
        ),
        out_shape=jax.ShapeDtypeStruct(((N_ASSIGN + DUMMY_ROWS) * PACK_ROWS, LANES), jnp.uint32),
        compiler_params=pltpu.CompilerParams(
            dimension_semantics=("arbitrary",),
            vmem_limit_bytes=EXPERT_VMEM_LIMIT,
        ),
        name="experts",
    )(bexp, nused, src2d, dst2d, n2, w_gate, w_up, w_down)


def _combine_kernel(h1_ref, mf_ref, fg_ref, y0_ref, y1_ref, out_ref):
    w = mf_ref[...]
    y0 = _unpack_bf16_pairs(_load_packed_rows(y0_ref, TD))
    y1 = _unpack_bf16_pairs(_load_packed_rows(y1_ref, TD))
    h = h1_ref[...] + y0 * w[:, 0:1] + y1 * w[:, 1:2]
    out_ref[...] = _rms(h, fg_ref[...])


def _combine(h1, mf, fg, y):
    n_steps = N_TOK // TD
    tok = lambda i: (i, 0)
    return pl.pallas_call(
        _combine_kernel,
        grid=(n_steps,),
        in_specs=[
            pl.BlockSpec((TD, D_MODEL), tok),
            pl.BlockSpec((TD, TOP_K), tok),
            pl.BlockSpec((1, D_MODEL), lambda i: (0, 0)),
            pl.BlockSpec((TD * PACK_ROWS, LANES), tok),
            pl.BlockSpec((TD * PACK_ROWS, LANES), lambda i: (i + n_steps, 0)),
        ],
        out_specs=pl.BlockSpec((TD, D_MODEL), tok),
        out_shape=jax.ShapeDtypeStruct((N_TOK, D_MODEL), F32),
        compiler_params=pltpu.CompilerParams(dimension_semantics=("arbitrary",)),
        name="combine",
    )(h1, mf, fg, y, y)


def kernel(x, norm1_gain, w_in, gmlp_v_gain, gmlp_w_s, gmlp_b_s, hgrn_lower_bounds,
           hgrn_out_gain, w_out, norm2_gain, w_group_router, b_group_router,
           w_expert_router, b_expert_router, w_gate, w_up, w_down, final_gain):
    l = 0
    x2 = x.reshape(N_TOK, D_MODEL)
    bst = jnp.tile(jnp.transpose(gmlp_b_s[l]), (TS // GMLP_BLOCK, 1))
    w_router = jnp.concatenate([w_group_router[l], w_expert_router[l]], axis=1)
    w_router = jnp.pad(w_router, ((0, 0), (0, LANES - w_router.shape[1]))).astype(BF16)
    b_router = jnp.concatenate([b_group_router[l], b_expert_router[l]])
    b_router = jnp.pad(b_router, (0, ROUTER_ROWS - b_router.shape[0])).reshape(ROUTER_ROWS, 1)

    h1, n2, meta_i, meta_f, counts = _mixer(
        x2, norm1_gain[l].reshape(1, D_MODEL), w_in[l].astype(BF16),
        gmlp_v_gain[l].reshape(1, D_A), gmlp_w_s[l], bst,
        hgrn_lower_bounds, hgrn_out_gain[l].reshape(1, D_B), w_out[l].astype(BF16),
        norm2_gain[l].reshape(1, D_MODEL), w_router, b_router)

    cnt = counts[EXPERT_ROW0:EXPERT_ROW0 + N_EXPERTS, 0]
    padded = ((cnt + MOE_BLOCK - 1) // MOE_BLOCK) * MOE_BLOCK
    pend = jnp.cumsum(padded)
    pstart = pend - padded
    eid = meta_i[0:TOP_K]
    rank = meta_i[TOP_K:2 * TOP_K]
    base = jnp.sum(jnp.where(eid[:, :, None] == jnp.arange(N_EXPERTS)[None, None, :],
                             pstart[None, None, :], 0), axis=-1)
    dest = (base + rank).astype(jnp.int32).reshape(N_ASSIGN)
    gap_lo = (pstart + cnt).astype(jnp.int32)
    gap_hi = jnp.concatenate([pstart[1:], jnp.full((1,), INV_LEN)]).astype(jnp.int32)
    inv = _invert(dest, gap_lo, gap_hi)
    pos = jnp.arange(INV_LEN, dtype=jnp.int32)
    blk = jnp.where(pos < SORTED_ROWS, pos // MOE_BLOCK, N_SLOTS - 1)
    dummy = N_ASSIGN + (blk % N_SLOTS) * MOE_BLOCK + pos % MOE_BLOCK
    src2d = (jnp.where(inv >= 0, inv % N_TOK, 0) * PACK_ROWS)[:SORTED_ROWS].reshape(
        N_BLOCKS, 1, MOE_BLOCK)
    dst2d = (jnp.where(inv >= 0, inv, dummy) * PACK_ROWS).reshape(N_BLOCKS + 1, 1, MOE_BLOCK)
    blk_start = jnp.arange(N_BLOCKS, dtype=jnp.int32) * MOE_BLOCK
    bexp = jnp.clip(jnp.sum(blk_start[:, None] >= pend[None, :], axis=1), 0,
                    N_EXPERTS - 1).astype(jnp.int32)
    nused = (pend[-1:] // MOE_BLOCK).astype(jnp.int32)

    y = _experts(bexp, nused, src2d, dst2d, n2, w_gate[l], w_up[l], w_down[l])
    out = _combine(h1, meta_f[0:TOP_K].T, final_gain.reshape(1, D_MODEL), y)
    return out.reshape(BATCH, SEQ, D_MODEL)
```

```python
import functools

import jax
import jax.numpy as jnp
from jax import lax
from jax.experimental import pallas as pl
from jax.experimental.pallas import tpu as pltpu

D_MODEL = 1024
BATCH = 2
SEQ = 8192
N_TOK = BATCH * SEQ
CHUNK = 64
EPS = 1e-6
D_A = 512
HEADS = 4
HEAD_DIM = 128
GMLP_BLOCK = 128
D_B = 512
D_IN = 3072
N_GROUPS = 4
EXPERTS_PER_GROUP = 8
N_EXPERTS = 32
TOP_K = 2
D_EXPERT = 512

LANES = 128
ROUTER_ROWS = 48
EXPERT_ROW0 = N_GROUPS
META_ROWS = 8
PACK_COLS = D_MODEL // 2
PACK_ROWS = PACK_COLS // LANES
HI_MASK = 0xFFFF0000

TS = 256
CHUNKS_PER_STEP = TS // CHUNK
MOE_BLOCK = 128
N_BLOCKS = N_TOK * TOP_K // MOE_BLOCK + N_EXPERTS
SORTED_ROWS = N_BLOCKS * MOE_BLOCK
N_ASSIGN = N_TOK * TOP_K
N_SLOTS = 3
DUMMY_ROWS = N_SLOTS * MOE_BLOCK
BLOCK_PACK_ROWS = MOE_BLOCK * PACK_ROWS
INV_LEN = SORTED_ROWS + MOE_BLOCK
TD = 512

F32 = jnp.float32
BF16 = jnp.bfloat16
NT_DIMS = (((1,), (1,)), ((), ()))
TN_DIMS = (((0,), (0,)), ((), ()))


def _sigmoid(x):
    return 1.0 / (1.0 + jnp.exp(-x))


def _gelu(x):
    return 0.5 * x * (1.0 + lax.erf(x * (2.0 ** -0.5)))


def _rms(x, gain):
    return x * lax.rsqrt(jnp.mean(x * x, axis=-1, keepdims=True) + EPS) * gain


def _pack_bf16_pairs(x):
    lo = pltpu.bitcast(x[:, 0:PACK_COLS].astype(BF16).astype(F32), jnp.uint32) >> 16
    hi = (pltpu.bitcast(x[:, PACK_COLS:D_MODEL].astype(BF16).astype(F32), jnp.uint32)
          & jnp.uint32(HI_MASK))
    return lo | hi


def _unpack_bf16_pairs(words):
    lo = pltpu.bitcast(words << 16, F32)
    hi = pltpu.bitcast(words & jnp.uint32(HI_MASK), F32)
    return jnp.concatenate([lo, hi], axis=1)


def _store_packed_rows(ref, words, rows):
    for c in range(PACK_ROWS):
        ref[pl.ds(c, rows, stride=PACK_ROWS), :] = words[:, c * LANES:(c + 1) * LANES]


def _load_packed_rows(ref, rows):
    return jnp.concatenate(
        [ref[pl.ds(c, rows, stride=PACK_ROWS), :] for c in range(PACK_ROWS)], axis=1)


def _split_bf16(x):
    hi = x.astype(BF16)
    lo = (x - hi.astype(F32)).astype(BF16)
    return hi, lo


def _mixer_kernel(*refs):
    for parity in range(2):
        pl.when(pl.program_id(0) % 2 == parity)(functools.partial(_mixer_step, parity, *refs))


def _mixer_step(cur, x_ref, xn_ref, g1_ref, win_ref, vg_ref, ws_ref, bst_ref, lbraw_ref, og_ref,
                wout_ref, g2_ref, wr_ref, br_ref,
                h1_ref, n2_ref, mi_ref, mf_ref, cnt_ref,
                st_ref, carry_ref, wm_ref, mix_ref, proj_ref, logit_ref):
    i = pl.program_id(0)

    @pl.when(i == 0)
    def _init():
        carry_ref[...] = jnp.zeros_like(carry_ref)
        logit_ref[...] = jnp.zeros_like(logit_ref)
        r = lax.broadcasted_iota(jnp.int32, (GMLP_BLOCK, GMLP_BLOCK), 0)
        c = lax.broadcasted_iota(jnp.int32, (GMLP_BLOCK, GMLP_BLOCK), 1)
        keep = (c // CHUNK) <= (r // CHUNK)
        wm_ref[...] = jnp.zeros_like(wm_ref)
        for g in range(HEADS):
            w = jnp.where(keep, ws_ref[g], 0.0).astype(BF16)
            for p in range(TS // GMLP_BLOCK):
                wm_ref[g, p * GMLP_BLOCK:(p + 1) * GMLP_BLOCK,
                       p * GMLP_BLOCK:(p + 1) * GMLP_BLOCK] = w

    @pl.when(i % (SEQ // TS) == 0)
    def _reset_state():
        st_ref[...] = jnp.zeros_like(st_ref)

    def project(n1, slot, c):
        proj_ref[slot, :, c * D_A:(c + 1) * D_A] = jnp.dot(
            n1, win_ref[:, c * D_A:(c + 1) * D_A], preferred_element_type=F32)

    @pl.when(i == 0)
    def _first_projection():
        n1 = _rms(x_ref[...], g1_ref[...]).astype(BF16)
        for c in range(D_IN // D_A):
            project(n1, 0, c)

    n1_next = _rms(xn_ref[...], g1_ref[...]).astype(BF16)
    x = x_ref[...]

    project(n1_next, 1 - cur, 0)
    gu = _gelu(proj_ref[cur, :, 0:D_A])
    project(n1_next, 1 - cur, 1)
    gv = _gelu(proj_ref[cur, :, D_A:2 * D_A])
    for g in range(HEADS):
        sl = slice(g * HEAD_DIM, (g + 1) * HEAD_DIM)
        vh = _rms(gv[:, sl], vg_ref[:, sl]).astype(BF16)
        mixed = jnp.dot(wm_ref[g], vh, preferred_element_type=F32) + bst_ref[:, g:g + 1]
        mix_ref[:, sl] = (gu[:, sl] * mixed).astype(BF16)

    _route(logit_ref[1 - cur], jnp.maximum(i - 1, 0), jnp.where(i > 0, 1.0, 0.0),
           carry_ref, cnt_ref, mi_ref, mf_ref)

    o2 = 2 * D_A
    qr = proj_ref[cur, :, o2:o2 + D_B]
    fr = proj_ref[cur, :, o2 + D_B:o2 + 2 * D_B]
    vv = proj_ref[cur, :, o2 + 2 * D_B:o2 + 3 * D_B].astype(BF16)
    gr = proj_ref[cur, :, o2 + 3 * D_B:o2 + 4 * D_B]

    lbr = lbraw_ref[...]
    lbm = jnp.max(lbr, axis=0, keepdims=True)
    lbe = jnp.exp(lbr - lbm)
    lb = lbe[0:1, :] / jnp.sum(lbe, axis=0, keepdims=True)

    project(n1_next, 1 - cur, 2)
    q = qr * _sigmoid(qr)
    f = lb + (1.0 - lb) * _sigmoid(fr)
    k = 1.0 - f
    lf = jnp.log(f)

    row = lax.broadcasted_iota(jnp.int32, (TS, TS), 0)
    col = lax.broadcasted_iota(jnp.int32, (TS, TS), 1)
    causal = (row >= col) & ((row // CHUNK) == (col // CHUNK))
    tri = jnp.where(causal, 1.0, 0.0).astype(BF16)
    lf_hi, lf_lo = _split_bf16(lf)
    b = (jnp.dot(tri, lf_hi, preferred_element_type=F32)
         + jnp.dot(tri, lf_lo, preferred_element_type=F32))

    def chunk_rows(r):
        return jnp.concatenate(
            [jnp.broadcast_to(b[c * CHUNK + r:c * CHUNK + r + 1, :], (CHUNK, D_B))
             for c in range(CHUNKS_PER_STEP)], axis=0)

    project(n1_next, 1 - cur, 3)
    bref = chunk_rows(CHUNK // 2 - 1)
    blast = chunk_rows(CHUNK - 1)
    qe = (q * jnp.exp(b - bref)).astype(BF16)
    ke = (k * jnp.exp(bref - b)).astype(BF16)
    kd = (k * jnp.exp(blast - b)).astype(BF16)
    qb = (q * jnp.exp(b)).astype(BF16)

    head_sl = [slice(h * HEAD_DIM, (h + 1) * HEAD_DIM) for h in range(HEADS)]
    chunk_sl = [slice(c * CHUNK, (c + 1) * CHUNK) for c in range(CHUNKS_PER_STEP)]
    scores = [lax.dot_general(qe[:, sl], ke[:, sl], NT_DIMS, preferred_element_type=F32)
              for sl in head_sl]
    upd = [[lax.dot_general(vv[rows, sl], kd[rows, sl], TN_DIMS, preferred_element_type=F32)
            for rows in chunk_sl] for sl in head_sl]
    project(n1_next, 1 - cur, 4)
    scores = [jnp.where(causal, sc, 0.0).astype(BF16) for sc in scores]
    intra = [jnp.dot(scores[h], vv[:, head_sl[h]], preferred_element_type=F32)
             for h in range(HEADS)]
    entering = []
    for h in range(HEADS):
        st = st_ref[h]
        per_chunk = []
        for c in range(CHUNKS_PER_STEP):
            per_chunk.append(st.astype(BF16))
            decay = jnp.exp(b[(c + 1) * CHUNK - 1:(c + 1) * CHUNK, head_sl[h]])
            st = st * decay + upd[h][c]
        st_ref[h] = st
        entering.append(per_chunk)
    inter = [jnp.concatenate(
        [lax.dot_general(qb[chunk_sl[c], head_sl[h]], entering[h][c], NT_DIMS,
                         preferred_element_type=F32) for c in range(CHUNKS_PER_STEP)], axis=0)
        for h in range(HEADS)]
    for h in range(HEADS):
        o = _rms(intra[h] + inter[h], og_ref[:, head_sl[h]])
        g_h = gr[:, head_sl[h]]
        mix_ref[:, D_A + h * HEAD_DIM:D_A + (h + 1) * HEAD_DIM] = (
            o * (g_h * _sigmoid(g_h))).astype(BF16)

    h1 = x + jnp.dot(mix_ref[...], wout_ref[...], preferred_element_type=F32)
    h1_ref[...] = h1
    n2 = _rms(h1, g2_ref[...])
    _store_packed_rows(n2_ref, _pack_bf16_pairs(n2), TS)

    logits = jnp.dot(n2.astype(BF16), wr_ref[...], preferred_element_type=F32)
    logits = jnp.transpose(logits)[0:ROUTER_ROWS, :] + br_ref[...]
    project(n1_next, 1 - cur, 5)
    logit_ref[cur] = logits

    @pl.when(i == N_TOK // TS - 1)
    def _route_last_tile():
        _route(logits, i, 1.0, carry_ref, cnt_ref, mi_ref, mf_ref)


def _route(logits, tile, live, carry_ref, cnt_ref, mi_ref, mf_ref):
    rid = lax.broadcasted_iota(jnp.int32, (ROUTER_ROWS, TS), 0)
    row = lax.broadcasted_iota(jnp.int32, (TS, TS), 0)
    col = lax.broadcasted_iota(jnp.int32, (TS, TS), 1)
    neg = -jnp.inf
    gl = jnp.where(rid < N_GROUPS, logits, neg)
    gmax = jnp.max(gl, axis=0, keepdims=True)
    g_idx = jnp.min(jnp.where(gl == gmax, rid, ROUTER_ROWS), axis=0, keepdims=True)
    g_prob = 1.0 / jnp.sum(jnp.exp(gl - gmax), axis=0, keepdims=True)
    e_lo = EXPERT_ROW0 + g_idx * EXPERTS_PER_GROUP
    el = jnp.where((rid >= e_lo) & (rid < e_lo + EXPERTS_PER_GROUP), logits, neg)
    m1 = jnp.max(el, axis=0, keepdims=True)
    i1 = jnp.min(jnp.where(el == m1, rid, ROUTER_ROWS), axis=0, keepdims=True)
    el2 = jnp.where(rid == i1, neg, el)
    m2 = jnp.max(el2, axis=0, keepdims=True)
    i2 = jnp.min(jnp.where(el2 == m2, rid, ROUTER_ROWS), axis=0, keepdims=True)
    e21 = jnp.exp(m2 - m1)
    w1 = g_prob / (1.0 + e21)
    w2 = g_prob * e21 / (1.0 + e21)

    hit1 = rid == i1
    hit2 = rid == i2
    onehot = jnp.where(hit1 | hit2, live, 0.0)
    earlier = jnp.where(row < col, 1.0, 0.0).astype(BF16)
    seen = carry_ref[...]
    prior = (jnp.dot(onehot.astype(BF16), earlier, preferred_element_type=F32)
             + jnp.concatenate([seen] * (TS // LANES), axis=1))
    r1 = jnp.sum(jnp.where(hit1, prior, 0.0), axis=0, keepdims=True)
    r2 = jnp.sum(jnp.where(hit2, prior, 0.0), axis=0, keepdims=True)
    seen = seen + jnp.sum(onehot, axis=1, keepdims=True)
    carry_ref[...] = seen
    cnt_ref[...] = seen.astype(jnp.int32)

    cols = pl.ds(pl.multiple_of(tile * TS, TS), TS)
    pad_i = jnp.zeros((META_ROWS - 2 * TOP_K, TS), jnp.int32)
    mi_ref[:, cols] = jnp.concatenate(
        [i1 - EXPERT_ROW0, i2 - EXPERT_ROW0, r1.astype(jnp.int32), r2.astype(jnp.int32), pad_i],
        axis=0)
    mf_ref[:, cols] = jnp.concatenate([w1, w2, jnp.zeros((META_ROWS - TOP_K, TS), F32)], axis=0)


def _mixer(x2, g1, win, vg, ws, bst, lbraw, og, wout, g2, wr, br):
    n_steps = N_TOK // TS
    const2 = lambda i: (0, 0)
    tok = lambda i: (i, 0)
    return pl.pallas_call(
        _mixer_kernel,
        grid=(n_steps,),
        in_specs=[
            pl.BlockSpec((TS, D_MODEL), tok),
            pl.BlockSpec((TS, D_MODEL), lambda i: (jnp.minimum(i + 1, n_steps - 1), 0)),
            pl.BlockSpec((1, D_MODEL), const2),
            pl.BlockSpec((D_MODEL, D_IN), const2),
            pl.BlockSpec((1, D_A), const2),
            pl.BlockSpec((HEADS, GMLP_BLOCK, GMLP_BLOCK), lambda i: (0, 0, 0)),
            pl.BlockSpec((TS, HEADS), const2),
            pl.BlockSpec((2, D_B), const2),
            pl.BlockSpec((1, D_B), const2),
            pl.BlockSpec((D_MODEL, D_MODEL), const2),
            pl.BlockSpec((1, D_MODEL), const2),
            pl.BlockSpec((D_MODEL, LANES), const2),
            pl.BlockSpec((ROUTER_ROWS, 1), const2),
        ],
        out_specs=[
            pl.BlockSpec((TS, D_MODEL), tok),
            pl.BlockSpec((TS * PACK_ROWS, LANES), tok),
            pl.BlockSpec((META_ROWS, N_TOK), const2),
            pl.BlockSpec((META_ROWS, N_TOK), const2),
            pl.BlockSpec((ROUTER_ROWS, LANES), const2),
        ],
        out_shape=[
            jax.ShapeDtypeStruct((N_TOK, D_MODEL), F32),
            jax.ShapeDtypeStruct((N_TOK * PACK_ROWS, LANES), jnp.uint32),
            jax.ShapeDtypeStruct((META_ROWS, N_TOK), jnp.int32),
            jax.ShapeDtypeStruct((META_ROWS, N_TOK), F32),
            jax.ShapeDtypeStruct((ROUTER_ROWS, LANES), jnp.int32),
        ],
        scratch_shapes=[
            pltpu.VMEM((HEADS, HEAD_DIM, HEAD_DIM), F32),
            pltpu.VMEM((ROUTER_ROWS, LANES), F32),
            pltpu.VMEM((HEADS, TS, TS), BF16),
            pltpu.VMEM((TS, D_MODEL), BF16),
            pltpu.VMEM((2, TS, D_IN), F32),
            pltpu.VMEM((2, ROUTER_ROWS, TS), F32),
        ],
        compiler_params=pltpu.CompilerParams(
            dimension_semantics=("arbitrary",),
            vmem_limit_bytes=56 * 1024 * 1024,
        ),
        name="mixer",
    )(x2, x2, g1, win, vg, ws, bst, lbraw, og, wout, g2, wr, br)


FILL_UNROLL = 8


def _invert_kernel(dest_ref, gap_lo_ref, gap_hi_ref, inv_ref):
    for e in range(N_EXPERTS):
        first = gap_lo_ref[e] // FILL_UNROLL

        def fill(g, carry):
            for u in range(FILL_UNROLL):
                inv_ref[g * FILL_UNROLL + u] = -1
            return carry

        lax.fori_loop(first, gap_hi_ref[e] // FILL_UNROLL, fill, 0)

    def place(a, carry):
        inv_ref[dest_ref[a]] = a
        return carry

    lax.fori_loop(0, N_ASSIGN, place, 0, unroll=16)


def _invert(dest, gap_lo, gap_hi):
    smem = pl.BlockSpec(memory_space=pltpu.SMEM)
    return pl.pallas_call(
        _invert_kernel,
        in_specs=[smem, smem, smem],
        out_specs=smem,
        out_shape=jax.ShapeDtypeStruct((INV_LEN,), jnp.int32),
        name="invert",
    )(dest, gap_lo, gap_hi)


def _expert_kernel(bexp_ref, nused_ref, src_hbm, dst_hbm, n2_hbm, wg_ref, wu_ref, wd_ref, y_hbm,
                   src_smem, dst_smem, n2_vmem, xb0, xb1, xb2, ob0, ob1, ob2, wg_bf, wu_bf, wd_bf,
                   src_sem, dst_sem, nsem, ssem):
    i = pl.program_id(0)
    nused = nused_ref[0]
    last = nused - 1
    xbuf = (xb0, xb1, xb2)
    obuf = (ob0, ob1, ob2)

    def src_copy(blk, s):
        return pltpu.make_async_copy(src_hbm.at[blk], src_smem.at[pl.ds(s, 1)], src_sem.at[s])

    def dst_copy(blk, s):
        return pltpu.make_async_copy(dst_hbm.at[blk], dst_smem.at[pl.ds(s, 1)], dst_sem.at[s])

    def gather(s):
        for j in range(MOE_BLOCK):
            r = pl.multiple_of(src_smem[s, j], PACK_ROWS)
            xbuf[s][pl.ds(j * PACK_ROWS, PACK_ROWS), :] = n2_vmem[pl.ds(r, PACK_ROWS), :]

    def start_scatter(s):
        for j in range(MOE_BLOCK):
            r = pl.multiple_of(dst_smem[s, j], PACK_ROWS)
            pltpu.make_async_copy(obuf[s].at[pl.ds(j * PACK_ROWS, PACK_ROWS)],
                                  y_hbm.at[pl.ds(r, PACK_ROWS)], ssem.at[s]
                                  ).start(priority=j % 2)

    def wait_scatter(s):
        pltpu.make_async_copy(obuf[s], y_hbm.at[pl.ds(0, BLOCK_PACK_ROWS)],
                              ssem.at[s]).wait()

    @pl.when(i == 0)
    def _prologue():
        for o in obuf:
            o[...] = jnp.zeros_like(o)
        for s in range(N_SLOTS - 1):
            pltpu.make_async_copy(
                obuf[s],
                y_hbm.at[pl.ds((N_ASSIGN + s * MOE_BLOCK) * PACK_ROWS, BLOCK_PACK_ROWS)],
                ssem.at[s]).start()
        resident = pltpu.make_async_copy(n2_hbm, n2_vmem, nsem)
        resident.start()
        src_copy(0, 0).start()
        src_copy(0, 0).wait()
        resident.wait()
        gather(0)
        src_copy(jnp.minimum(1, last), 1).start()
        dst_copy(N_BLOCKS, N_SLOTS - 1).start()

    prev = bexp_ref[jnp.maximum(i - 1, 0)]
    changed = (i == 0) | (bexp_ref[i] != prev)

    @pl.when(changed)
    def _cast_weights():
        wg_bf[...] = wg_ref[0].astype(BF16)
        wu_bf[...] = wu_ref[0].astype(BF16)
        wd_bf[...] = wd_ref[0].astype(BF16)

    def active(cur, nxt, prv):
        src_copy(0, nxt).wait()
        dst_copy(0, prv).wait()
        wait_scatter(cur)
        start_scatter(prv)
        gather(nxt)
        src_copy(jnp.minimum(i + 2, last), prv).start()
        dst_copy(i, cur).start()
        xb = _unpack_bf16_pairs(_load_packed_rows(xbuf[cur], MOE_BLOCK)).astype(BF16)
        hg = jnp.dot(xb, wg_bf[...], preferred_element_type=F32)
        hu = jnp.dot(xb, wu_bf[...], preferred_element_type=F32)
        hh = (hg * _sigmoid(hg) * hu).astype(BF16)
        y = jnp.dot(hh, wd_bf[...], preferred_element_type=F32)
        _store_packed_rows(obuf[cur], _pack_bf16_pairs(y), MOE_BLOCK)

    def epilogue(cur, nxt, prv):
        dst_copy(0, cur).wait()
        start_scatter(cur)
        wait_scatter(nxt)
        wait_scatter(prv)
        wait_scatter(cur)
        src_copy(0, prv).wait()

    for cur in range(N_SLOTS):
        slots = (cur, (cur + 1) % N_SLOTS, (cur + 2) % N_SLOTS)

        @pl.when((i < nused) & (i % N_SLOTS == cur))
        def _():
            active(*slots)

        @pl.when((i == last) & (i % N_SLOTS == cur))
        def _():
            epilogue(*slots)


_EXPERT_W_BYTES = 3 * D_MODEL * D_EXPERT * 4
EXPERT_VMEM_LIMIT = (
    N_TOK * PACK_COLS * 4
    + 2 * _EXPERT_W_BYTES
    + _EXPERT_W_BYTES // 2
    + 2 * N_SLOTS * MOE_BLOCK * PACK_COLS * 4
    + 4 * 1024 * 1024)


def _experts(bexp, nused, src2d, dst2d, n2, w_gate, w_up, w_down):
    def w_blk(i, bexp, nused):
        return (bexp[i], 0, 0)

    any_spec = pl.BlockSpec(memory_space=pl.ANY)
    return pl.pallas_call(
        _expert_kernel,
        grid_spec=pltpu.PrefetchScalarGridSpec(
            num_scalar_prefetch=2,
            grid=(N_BLOCKS,),
            in_specs=[
                any_spec, any_spec, any_spec,
                pl.BlockSpec((1, D_MODEL, D_EXPERT), w_blk),
                pl.BlockSpec((1, D_MODEL, D_EXPERT), w_blk),
                pl.BlockSpec((1, D_EXPERT, D_MODEL), w_blk),
            ],
            out_specs=any_spec,
            scratch_shapes=[
                pltpu.SMEM((N_SLOTS, MOE_BLOCK), jnp.int32),
                pltpu.SMEM((N_SLOTS, MOE_BLOCK), jnp.int32),
                pltpu.VMEM((N_TOK * PACK_ROWS, LANES), jnp.uint32),
            ] + [pltpu.VMEM((BLOCK_PACK_ROWS, LANES), jnp.uint32)] * (2 * N_SLOTS) + [
                pltpu.VMEM((D_MODEL, D_EXPERT), BF16),
                pltpu.VMEM((D_MODEL, D_EXPERT), BF16),
                pltpu.VMEM((D_EXPERT, D_MODEL), BF16),
                pltpu.SemaphoreType.DMA((N_SLOTS,)),
                pltpu.SemaphoreType.DMA((N_SLOTS,)),
                pltpu.SemaphoreType.DMA,
                pltpu.SemaphoreType.DMA((N_SLOTS,)),
            ],
        ),
        out_shape=jax.ShapeDtypeStruct(((N_ASSIGN + DUMMY_ROWS) * PACK_ROWS, LANES), jnp.uint32),
        compiler_params=pltpu.CompilerParams(
            dimension_semantics=("arbitrary",),
            vmem_limit_bytes=EXPERT_VMEM_LIMIT,
        ),
        name="experts",
    )(bexp, nused, src2d, dst2d, n2, w_gate, w_up, w_down)


def _combine_kernel(h1_ref, mf_ref, fg_ref, y0_ref, y1_ref, out_ref):
    w = mf_ref[...]
    y0 = _unpack_bf16_pairs(_load_packed_rows(y0_ref, TD))
    y1 = _unpack_bf16_pairs(_load_packed_rows(y1_ref, TD))
    h = h1_ref[...] + y0 * w[:, 0:1] + y1 * w[:, 1:2]
    out_ref[...] = _rms(h, fg_ref[...])


def _combine(h1, mf, fg, y):
    n_steps = N_TOK // TD
    tok = lambda i: (i, 0)
    return pl.pallas_call(
        _combine_kernel,
        grid=(n_steps,),
        in_specs=[
            pl.BlockSpec((TD, D_MODEL), tok),
            pl.BlockSpec((TD, TOP_K), tok),
            pl.BlockSpec((1, D_MODEL), lambda i: (0, 0)),
            pl.BlockSpec((TD * PACK_ROWS, LANES), tok),
            pl.BlockSpec((TD * PACK_ROWS, LANES), lambda i: (i + n_steps, 0)),
        ],
        out_specs=pl.BlockSpec((TD, D_MODEL), tok),
        out_shape=jax.ShapeDtypeStruct((N_TOK, D_MODEL), F32),
        compiler_params=pltpu.CompilerParams(dimension_semantics=("arbitrary",)),
        name="combine",
    )(h1, mf, fg, y, y)


def kernel(x, norm1_gain, w_in, gmlp_v_gain, gmlp_w_s, gmlp_b_s, hgrn_lower_bounds,
           hgrn_out_gain, w_out, norm2_gain, w_group_router, b_group_router,
           w_expert_router, b_expert_router, w_gate, w_up, w_down, final_gain):
    l = 0
    x2 = x.reshape(N_TOK, D_MODEL)
    bst = jnp.tile(jnp.transpose(gmlp_b_s[l]), (TS // GMLP_BLOCK, 1))
    w_router = jnp.concatenate([w_group_router[l], w_expert_router[l]], axis=1)
    w_router = jnp.pad(w_router, ((0, 0), (0, LANES - w_router.shape[1]))).astype(BF16)
    b_router = jnp.concatenate([b_group_router[l], b_expert_router[l]])
    b_router = jnp.pad(b_router, (0, ROUTER_ROWS - b_router.shape[0])).reshape(ROUTER_ROWS, 1)

    h1, n2, meta_i, meta_f, counts = _mixer(
        x2, norm1_gain[l].reshape(1, D_MODEL), w_in[l].astype(BF16),
        gmlp_v_gain[l].reshape(1, D_A), gmlp_w_s[l], bst,
        hgrn_lower_bounds, hgrn_out_gain[l].reshape(1, D_B), w_out[l].astype(BF16),
        norm2_gain[l].reshape(1, D_MODEL), w_router, b_router)

    cnt = counts[EXPERT_ROW0:EXPERT_ROW0 + N_EXPERTS, 0]
    padded = ((cnt + MOE_BLOCK - 1) // MOE_BLOCK) * MOE_BLOCK
    pend = jnp.cumsum(padded)
    pstart = pend - padded
    eid = meta_i[0:TOP_K]
    rank = meta_i[TOP_K:2 * TOP_K]
    base = jnp.sum(jnp.where(eid[:, :, None] == jnp.arange(N_EXPERTS)[None, None, :],
                             pstart[None, None, :], 0), axis=-1)
    dest = (base + rank).astype(jnp.int32).reshape(N_ASSIGN)
    gap_lo = (pstart + cnt).astype(jnp.int32)
    gap_hi = jnp.concatenate([pstart[1:], jnp.full((1,), INV_LEN)]).astype(jnp.int32)
    inv = _invert(dest, gap_lo, gap_hi)
    pos = jnp.arange(INV_LEN, dtype=jnp.int32)
    blk = jnp.where(pos < SORTED_ROWS, pos // MOE_BLOCK, N_SLOTS - 1)
    dummy = N_ASSIGN + (blk % N_SLOTS) * MOE_BLOCK + pos % MOE_BLOCK
    src2d = (jnp.where(inv >= 0, inv % N_TOK, 0) * PACK_ROWS)[:SORTED_ROWS].reshape(
        N_BLOCKS, 1, MOE_BLOCK)
    dst2d = (jnp.where(inv >= 0, inv, dummy) * PACK_ROWS).reshape(N_BLOCKS + 1, 1, MOE_BLOCK)
    blk_start = jnp.arange(N_BLOCKS, dtype=jnp.int32) * MOE_BLOCK
    bexp = jnp.clip(jnp.sum(blk_start[:, None] >= pend[None, :], axis=1), 0,
                    N_EXPERTS - 1).astype(jnp.int32)
    nused = (pend[-1:] // MOE_BLOCK).astype(jnp.int32)

    y = _experts(bexp, nused, src2d, dst2d, n2, w_gate[l], w_up[l], w_down[l])
    out = _combine(h1, meta_f[0:TOP_K].T, final_gain.reshape(1, D_MODEL), y)
    return out.reshape(BATCH, SEQ, D_MODEL)
```

```python
import functools

import jax
import jax.numpy as jnp
from jax import lax
from jax.experimental import pallas as pl
from jax.experimental.pallas import tpu as pltpu

D_MODEL = 1024
BATCH = 2
SEQ = 8192
N_TOK = BATCH * SEQ
CHUNK = 64
EPS = 1e-6
D_A = 512
HEADS = 4
HEAD_DIM = 128
GMLP_BLOCK = 128
D_B = 512
D_IN = 3072
N_GROUPS = 4
EXPERTS_PER_GROUP = 8
N_EXPERTS = 32
TOP_K = 2
D_EXPERT = 512

LANES = 128
ROUTER_ROWS = 48
EXPERT_ROW0 = N_GROUPS
META_ROWS = 8
PACK_COLS = D_MODEL // 2
PACK_ROWS = PACK_COLS // LANES
HI_MASK = 0xFFFF0000

TS = 256
CHUNKS_PER_STEP = TS // CHUNK
MOE_BLOCK = 256
N_BLOCKS = N_TOK * TOP_K // MOE_BLOCK + N_EXPERTS
SORTED_ROWS = N_BLOCKS * MOE_BLOCK
N_ASSIGN = N_TOK * TOP_K
N_SLOTS = 3
DUMMY_ROWS = N_SLOTS * MOE_BLOCK
BLOCK_PACK_ROWS = MOE_BLOCK * PACK_ROWS
INV_LEN = SORTED_ROWS + MOE_BLOCK
TD = 512

F32 = jnp.float32
BF16 = jnp.bfloat16
NT_DIMS = (((1,), (1,)), ((), ()))
TN_DIMS = (((0,), (0,)), ((), ()))


def _sigmoid(x):
    return 1.0 / (1.0 + jnp.exp(-x))


def _gelu(x):
    return 0.5 * x * (1.0 + lax.erf(x * (2.0 ** -0.5)))


def _rms(x, gain):
    return x * lax.rsqrt(jnp.mean(x * x, axis=-1, keepdims=True) + EPS) * gain


def _pack_bf16_pairs(x):
    lo = pltpu.bitcast(x[:, 0:PACK_COLS].astype(BF16).astype(F32), jnp.uint32) >> 16
    hi = (pltpu.bitcast(x[:, PACK_COLS:D_MODEL].astype(BF16).astype(F32), jnp.uint32)
          & jnp.uint32(HI_MASK))
    return lo | hi


def _unpack_bf16_pairs(words):
    lo = pltpu.bitcast(words << 16, F32)
    hi = pltpu.bitcast(words & jnp.uint32(HI_MASK), F32)
    return jnp.concatenate([lo, hi], axis=1)


def _store_packed_rows(ref, words, rows):
    for c in range(PACK_ROWS):
        ref[pl.ds(c, rows, stride=PACK_ROWS), :] = words[:, c * LANES:(c + 1) * LANES]


def _load_packed_rows(ref, rows):
    return jnp.concatenate(
        [ref[pl.ds(c, rows, stride=PACK_ROWS), :] for c in range(PACK_ROWS)], axis=1)


def _split_bf16(x):
    hi = x.astype(BF16)
    lo = (x - hi.astype(F32)).astype(BF16)
    return hi, lo


def _mixer_kernel(*refs):
    for parity in range(2):
        pl.when(pl.program_id(0) % 2 == parity)(functools.partial(_mixer_step, parity, *refs))


def _mixer_step(cur, x_ref, xn_ref, g1_ref, win_ref, vg_ref, ws_ref, bst_ref, lbraw_ref, og_ref,
                wout_ref, g2_ref, wr_ref, br_ref,
                h1_ref, n2_ref, mi_ref, mf_ref, cnt_ref,
                st_ref, carry_ref, wm_ref, mix_ref, proj_ref, logit_ref):
    i = pl.program_id(0)

    @pl.when(i == 0)
    def _init():
        carry_ref[...] = jnp.zeros_like(carry_ref)
        logit_ref[...] = jnp.zeros_like(logit_ref)
        r = lax.broadcasted_iota(jnp.int32, (GMLP_BLOCK, GMLP_BLOCK), 0)
        c = lax.broadcasted_iota(jnp.int32, (GMLP_BLOCK, GMLP_BLOCK), 1)
        keep = (c // CHUNK) <= (r // CHUNK)
        wm_ref[...] = jnp.zeros_like(wm_ref)
        for g in range(HEADS):
            w = jnp.where(keep, ws_ref[g], 0.0).astype(BF16)
            for p in range(TS // GMLP_BLOCK):
                wm_ref[g, p * GMLP_BLOCK:(p + 1) * GMLP_BLOCK,
                       p * GMLP_BLOCK:(p + 1) * GMLP_BLOCK] = w

    @pl.when(i % (SEQ // TS) == 0)
    def _reset_state():
        st_ref[...] = jnp.zeros_like(st_ref)

    def project(n1, slot, c):
        proj_ref[slot, :, c * D_A:(c + 1) * D_A] = jnp.dot(
            n1, win_ref[:, c * D_A:(c + 1) * D_A], preferred_element_type=F32)

    @pl.when(i == 0)
    def _first_projection():
        n1 = _rms(x_ref[...], g1_ref[...]).astype(BF16)
        for c in range(D_IN // D_A):
            project(n1, 0, c)

    n1_next = _rms(xn_ref[...], g1_ref[...]).astype(BF16)
    x = x_ref[...]

    project(n1_next, 1 - cur, 0)
    gu = _gelu(proj_ref[cur, :, 0:D_A])
    project(n1_next, 1 - cur, 1)
    gv = _gelu(proj_ref[cur, :, D_A:2 * D_A])
    for g in range(HEADS):
        sl = slice(g * HEAD_DIM, (g + 1) * HEAD_DIM)
        vh = _rms(gv[:, sl], vg_ref[:, sl]).astype(BF16)
        mixed = jnp.dot(wm_ref[g], vh, preferred_element_type=F32) + bst_ref[:, g:g + 1]
        mix_ref[:, sl] = (gu[:, sl] * mixed).astype(BF16)

    _route(logit_ref[1 - cur], jnp.maximum(i - 1, 0), jnp.where(i > 0, 1.0, 0.0),
           carry_ref, cnt_ref, mi_ref, mf_ref)

    o2 = 2 * D_A
    qr = proj_ref[cur, :, o2:o2 + D_B]
    fr = proj_ref[cur, :, o2 + D_B:o2 + 2 * D_B]
    vv = proj_ref[cur, :, o2 + 2 * D_B:o2 + 3 * D_B].astype(BF16)
    gr = proj_ref[cur, :, o2 + 3 * D_B:o2 + 4 * D_B]

    lbr = lbraw_ref[...]
    lbm = jnp.max(lbr, axis=0, keepdims=True)
    lbe = jnp.exp(lbr - lbm)
    lb = lbe[0:1, :] / jnp.sum(lbe, axis=0, keepdims=True)

    project(n1_next, 1 - cur, 2)
    q = qr * _sigmoid(qr)
    f = lb + (1.0 - lb) * _sigmoid(fr)
    k = 1.0 - f
    lf = jnp.log(f)

    row = lax.broadcasted_iota(jnp.int32, (TS, TS), 0)
    col = lax.broadcasted_iota(jnp.int32, (TS, TS), 1)
    causal = (row >= col) & ((row // CHUNK) == (col // CHUNK))
    tri = jnp.where(causal, 1.0, 0.0).astype(BF16)
    lf_hi, lf_lo = _split_bf16(lf)
    b = (jnp.dot(tri, lf_hi, preferred_element_type=F32)
         + jnp.dot(tri, lf_lo, preferred_element_type=F32))

    def chunk_rows(r):
        return jnp.concatenate(
            [jnp.broadcast_to(b[c * CHUNK + r:c * CHUNK + r + 1, :], (CHUNK, D_B))
             for c in range(CHUNKS_PER_STEP)], axis=0)

    project(n1_next, 1 - cur, 3)
    bref = chunk_rows(CHUNK // 2 - 1)
    blast = chunk_rows(CHUNK - 1)
    qe = (q * jnp.exp(b - bref)).astype(BF16)
    ke = (k * jnp.exp(bref - b)).astype(BF16)
    kd = (k * jnp.exp(blast - b)).astype(BF16)
    qb = (q * jnp.exp(b)).astype(BF16)

    head_sl = [slice(h * HEAD_DIM, (h + 1) * HEAD_DIM) for h in range(HEADS)]
    chunk_sl = [slice(c * CHUNK, (c + 1) * CHUNK) for c in range(CHUNKS_PER_STEP)]
    scores = [lax.dot_general(qe[:, sl], ke[:, sl], NT_DIMS, preferred_element_type=F32)
              for sl in head_sl]
    upd = [[lax.dot_general(vv[rows, sl], kd[rows, sl], TN_DIMS, preferred_element_type=F32)
            for rows in chunk_sl] for sl in head_sl]
    project(n1_next, 1 - cur, 4)
    scores = [jnp.where(causal, sc, 0.0).astype(BF16) for sc in scores]
    intra = [jnp.dot(scores[h], vv[:, head_sl[h]], preferred_element_type=F32)
             for h in range(HEADS)]
    entering = []
    for h in range(HEADS):
        st = st_ref[h]
        per_chunk = []
        for c in range(CHUNKS_PER_STEP):
            per_chunk.append(st.astype(BF16))
            decay = jnp.exp(b[(c + 1) * CHUNK - 1:(c + 1) * CHUNK, head_sl[h]])
            st = st * decay + upd[h][c]
        st_ref[h] = st
        entering.append(per_chunk)
    inter = [jnp.concatenate(
        [lax.dot_general(qb[chunk_sl[c], head_sl[h]], entering[h][c], NT_DIMS,
                         preferred_element_type=F32) for c in range(CHUNKS_PER_STEP)], axis=0)
        for h in range(HEADS)]
    for h in range(HEADS):
        o = _rms(intra[h] + inter[h], og_ref[:, head_sl[h]])
        g_h = gr[:, head_sl[h]]
        mix_ref[:, D_A + h * HEAD_DIM:D_A + (h + 1) * HEAD_DIM] = (
            o * (g_h * _sigmoid(g_h))).astype(BF16)

    h1 = x + jnp.dot(mix_ref[...], wout_ref[...], preferred_element_type=F32)
    h1_ref[...] = h1
    n2 = _rms(h1, g2_ref[...])
    _store_packed_rows(n2_ref, _pack_bf16_pairs(n2), TS)

    logits = jnp.dot(n2.astype(BF16), wr_ref[...], preferred_element_type=F32)
    logits = jnp.transpose(logits)[0:ROUTER_ROWS, :] + br_ref[...]
    project(n1_next, 1 - cur, 5)
    logit_ref[cur] = logits

    @pl.when(i == N_TOK // TS - 1)
    def _route_last_tile():
        _route(logits, i, 1.0, carry_ref, cnt_ref, mi_ref, mf_ref)


def _route(logits, tile, live, carry_ref, cnt_ref, mi_ref, mf_ref):
    rid = lax.broadcasted_iota(jnp.int32, (ROUTER_ROWS, TS), 0)
    row = lax.broadcasted_iota(jnp.int32, (TS, TS), 0)
    col = lax.broadcasted_iota(jnp.int32, (TS, TS), 1)
    neg = -jnp.inf
    gl = jnp.where(rid < N_GROUPS, logits, neg)
    gmax = jnp.max(gl, axis=0, keepdims=True)
    g_idx = jnp.min(jnp.where(gl == gmax, rid, ROUTER_ROWS), axis=0, keepdims=True)
    g_prob = 1.0 / jnp.sum(jnp.exp(gl - gmax), axis=0, keepdims=True)
    e_lo = EXPERT_ROW0 + g_idx * EXPERTS_PER_GROUP
    el = jnp.where((rid >= e_lo) & (rid < e_lo + EXPERTS_PER_GROUP), logits, neg)
    m1 = jnp.max(el, axis=0, keepdims=True)
    i1 = jnp.min(jnp.where(el == m1, rid, ROUTER_ROWS), axis=0, keepdims=True)
    el2 = jnp.where(rid == i1, neg, el)
    m2 = jnp.max(el2, axis=0, keepdims=True)
    i2 = jnp.min(jnp.where(el2 == m2, rid, ROUTER_ROWS), axis=0, keepdims=True)
    e21 = jnp.exp(m2 - m1)
    w1 = g_prob / (1.0 + e21)
    w2 = g_prob * e21 / (1.0 + e21)

    hit1 = rid == i1
    hit2 = rid == i2
    onehot = jnp.where(hit1 | hit2, live, 0.0)
    earlier = jnp.where(row < col, 1.0, 0.0).astype(BF16)
    seen = carry_ref[...]
    prior = (jnp.dot(onehot.astype(BF16), earlier, preferred_element_type=F32)
             + jnp.concatenate([seen] * (TS // LANES), axis=1))
    r1 = jnp.sum(jnp.where(hit1, prior, 0.0), axis=0, keepdims=True)
    r2 = jnp.sum(jnp.where(hit2, prior, 0.0), axis=0, keepdims=True)
    seen = seen + jnp.sum(onehot, axis=1, keepdims=True)
    carry_ref[...] = seen
    cnt_ref[...] = seen.astype(jnp.int32)

    cols = pl.ds(pl.multiple_of(tile * TS, TS), TS)
    pad_i = jnp.zeros((META_ROWS - 2 * TOP_K, TS), jnp.int32)
    mi_ref[:, cols] = jnp.concatenate(
        [i1 - EXPERT_ROW0, i2 - EXPERT_ROW0, r1.astype(jnp.int32), r2.astype(jnp.int32), pad_i],
        axis=0)
    mf_ref[:, cols] = jnp.concatenate([w1, w2, jnp.zeros((META_ROWS - TOP_K, TS), F32)], axis=0)


def _mixer(x2, g1, win, vg, ws, bst, lbraw, og, wout, g2, wr, br):
    n_steps = N_TOK // TS
    const2 = lambda i: (0, 0)
    tok = lambda i: (i, 0)
    return pl.pallas_call(
        _mixer_kernel,
        grid=(n_steps,),
        in_specs=[
            pl.BlockSpec((TS, D_MODEL), tok),
            pl.BlockSpec((TS, D_MODEL), lambda i: (jnp.minimum(i + 1, n_steps - 1), 0)),
            pl.BlockSpec((1, D_MODEL), const2),
            pl.BlockSpec((D_MODEL, D_IN), const2),
            pl.BlockSpec((1, D_A), const2),
            pl.BlockSpec((HEADS, GMLP_BLOCK, GMLP_BLOCK), lambda i: (0, 0, 0)),
            pl.BlockSpec((TS, HEADS), const2),
            pl.BlockSpec((2, D_B), const2),
            pl.BlockSpec((1, D_B), const2),
            pl.BlockSpec((D_MODEL, D_MODEL), const2),
            pl.BlockSpec((1, D_MODEL), const2),
            pl.BlockSpec((D_MODEL, LANES), const2),
            pl.BlockSpec((ROUTER_ROWS, 1), const2),
        ],
        out_specs=[
            pl.BlockSpec((TS, D_MODEL), tok),
            pl.BlockSpec((TS * PACK_ROWS, LANES), tok),
            pl.BlockSpec((META_ROWS, N_TOK), const2),
            pl.BlockSpec((META_ROWS, N_TOK), const2),
            pl.BlockSpec((ROUTER_ROWS, LANES), const2),
        ],
        out_shape=[
            jax.ShapeDtypeStruct((N_TOK, D_MODEL), F32),
            jax.ShapeDtypeStruct((N_TOK * PACK_ROWS, LANES), jnp.uint32),
            jax.ShapeDtypeStruct((META_ROWS, N_TOK), jnp.int32),
            jax.ShapeDtypeStruct((META_ROWS, N_TOK), F32),
            jax.ShapeDtypeStruct((ROUTER_ROWS, LANES), jnp.int32),
        ],
        scratch_shapes=[
            pltpu.VMEM((HEADS, HEAD_DIM, HEAD_DIM), F32),
            pltpu.VMEM((ROUTER_ROWS, LANES), F32),
            pltpu.VMEM((HEADS, TS, TS), BF16),
            pltpu.VMEM((TS, D_MODEL), BF16),
            pltpu.VMEM((2, TS, D_IN), F32),
            pltpu.VMEM((2, ROUTER_ROWS, TS), F32),
        ],
        compiler_params=pltpu.CompilerParams(
            dimension_semantics=("arbitrary",),
            vmem_limit_bytes=56 * 1024 * 1024,
        ),
        name="mixer",
    )(x2, x2, g1, win, vg, ws, bst, lbraw, og, wout, g2, wr, br)


FILL_UNROLL = 8
DEST_BITS = 16
DEST_MASK = (1 << DEST_BITS) - 1
assert INV_LEN <= 1 << DEST_BITS and TOP_K == 2


def _invert_kernel(dest_ref, gap_lo_ref, gap_hi_ref, inv_ref):
    for e in range(N_EXPERTS):
        first = gap_lo_ref[e] // FILL_UNROLL

        def fill(g, carry):
            for u in range(FILL_UNROLL):
                inv_ref[g * FILL_UNROLL + u] = -1
            return carry

        lax.fori_loop(first, gap_hi_ref[e] // FILL_UNROLL, fill, 0)

    def place(t, carry):
        pair = dest_ref[t]
        inv_ref[pair & DEST_MASK] = t
        inv_ref[lax.shift_right_logical(pair, DEST_BITS)] = N_TOK + t
        return carry

    lax.fori_loop(0, N_TOK, place, 0, unroll=8)


def _invert(dest, gap_lo, gap_hi):
    smem = pl.BlockSpec(memory_space=pltpu.SMEM)
    return pl.pallas_call(
        _invert_kernel,
        in_specs=[smem, smem, smem],
        out_specs=smem,
        out_shape=jax.ShapeDtypeStruct((INV_LEN,), jnp.int32),
        name="invert",
    )(dest, gap_lo, gap_hi)


def _expert_kernel(bexp_ref, nused_ref, src_hbm, dst_hbm, n2_hbm, wg_ref, wu_ref, wd_ref, y_hbm,
                   src_smem, dst_smem, n2_vmem, xb0, xb1, xb2, ob0, ob1, ob2, wg_bf, wu_bf, wd_bf,
                   src_sem, dst_sem, nsem, ssem):
    i = pl.program_id(0)
    nused = nused_ref[0]
    last = nused - 1
    xbuf = (xb0, xb1, xb2)
    obuf = (ob0, ob1, ob2)

    def src_copy(blk, s):
        return pltpu.make_async_copy(src_hbm.at[blk], src_smem.at[pl.ds(s, 1)], src_sem.at[s])

    def dst_copy(blk, s):
        return pltpu.make_async_copy(dst_hbm.at[blk], dst_smem.at[pl.ds(s, 1)], dst_sem.at[s])

    def gather(s):
        for j in range(MOE_BLOCK):
            r = pl.multiple_of(src_smem[s, j], PACK_ROWS)
            xbuf[s][pl.ds(j * PACK_ROWS, PACK_ROWS), :] = n2_vmem[pl.ds(r, PACK_ROWS), :]

    def start_scatter(s):
        for j in range(MOE_BLOCK):
            r = pl.multiple_of(dst_smem[s, j], PACK_ROWS)
            pltpu.make_async_copy(obuf[s].at[pl.ds(j * PACK_ROWS, PACK_ROWS)],
                                  y_hbm.at[pl.ds(r, PACK_ROWS)], ssem.at[s]
                                  ).start(priority=j % 2)

    def wait_scatter(s):
        pltpu.make_async_copy(obuf[s], y_hbm.at[pl.ds(0, BLOCK_PACK_ROWS)],
                              ssem.at[s]).wait()

    @pl.when(i == 0)
    def _prologue():
        for o in obuf:
            o[...] = jnp.zeros_like(o)
        for s in range(N_SLOTS - 1):
            pltpu.make_async_copy(
                obuf[s],
                y_hbm.at[pl.ds((N_ASSIGN + s * MOE_BLOCK) * PACK_ROWS, BLOCK_PACK_ROWS)],
                ssem.at[s]).start()
        resident = pltpu.make_async_copy(n2_hbm, n2_vmem, nsem)
        resident.start()
        src_copy(0, 0).start()
        src_copy(0, 0).wait()
        resident.wait()
        gather(0)
        src_copy(jnp.minimum(1, last), 1).start()
        dst_copy(N_BLOCKS, N_SLOTS - 1).start()

    prev = bexp_ref[jnp.maximum(i - 1, 0)]
    changed = (i == 0) | (bexp_ref[i] != prev)

    @pl.when(changed)
    def _cast_weights():
        wg_bf[...] = wg_ref[0].astype(BF16)
        wu_bf[...] = wu_ref[0].astype(BF16)
        wd_bf[...] = wd_ref[0].astype(BF16)

    def active(cur, nxt, prv):
        src_copy(0, nxt).wait()
        dst_copy(0, prv).wait()
        wait_scatter(cur)
        start_scatter(prv)
        gather(nxt)
        src_copy(jnp.minimum(i + 2, last), prv).start()
        dst_copy(i, cur).start()
        xb = _unpack_bf16_pairs(_load_packed_rows(xbuf[cur], MOE_BLOCK)).astype(BF16)
        hg = jnp.dot(xb, wg_bf[...], preferred_element_type=F32)
        hu = jnp.dot(xb, wu_bf[...], preferred_element_type=F32)
        hh = (hg * _sigmoid(hg) * hu).astype(BF16)
        y = jnp.dot(hh, wd_bf[...], preferred_element_type=F32)
        _store_packed_rows(obuf[cur], _pack_bf16_pairs(y), MOE_BLOCK)

    def epilogue(cur, nxt, prv):
        dst_copy(0, cur).wait()
        start_scatter(cur)
        wait_scatter(nxt)
        wait_scatter(prv)
        wait_scatter(cur)
        src_copy(0, prv).wait()

    for cur in range(N_SLOTS):
        slots = (cur, (cur + 1) % N_SLOTS, (cur + 2) % N_SLOTS)

        @pl.when((i < nused) & (i % N_SLOTS == cur))
        def _():
            active(*slots)

        @pl.when((i == last) & (i % N_SLOTS == cur))
        def _():
            epilogue(*slots)


_EXPERT_W_BYTES = 3 * D_MODEL * D_EXPERT * 4
EXPERT_VMEM_LIMIT = (
    N_TOK * PACK_COLS * 4
    + 2 * _EXPERT_W_BYTES
    + _EXPERT_W_BYTES // 2
    + 2 * N_SLOTS * MOE_BLOCK * PACK_COLS * 4
    + 4 * 1024 * 1024)


def _experts(bexp, nused, src2d, dst2d, n2, w_gate, w_up, w_down):
    def w_blk(i, bexp, nused):
        return (bexp[i], 0, 0)

    any_spec = pl.BlockSpec(memory_space=pl.ANY)
    return pl.pallas_call(
        _expert_kernel,
        grid_spec=pltpu.PrefetchScalarGridSpec(
            num_scalar_prefetch=2,
            grid=(N_BLOCKS,),
            in_specs=[
                any_spec, any_spec, any_spec,
                pl.BlockSpec((1, D_MODEL, D_EXPERT), w_blk),
                pl.BlockSpec((1, D_MODEL, D_EXPERT), w_blk),
                pl.BlockSpec((1, D_EXPERT, D_MODEL), w_blk),
            ],
            out_specs=any_spec,
            scratch_shapes=[
                pltpu.SMEM((N_SLOTS, MOE_BLOCK), jnp.int32),
                pltpu.SMEM((N_SLOTS, MOE_BLOCK), jnp.int32),
                pltpu.VMEM((N_TOK * PACK_ROWS, LANES), jnp.uint32),
            ] + [pltpu.VMEM((BLOCK_PACK_ROWS, LANES), jnp.uint32)] * (2 * N_SLOTS) + [
                pltpu.VMEM((D_MODEL, D_EXPERT), BF16),
                pltpu.VMEM((D_MODEL, D_EXPERT), BF16),
                pltpu.VMEM((D_EXPERT, D_MODEL), BF16),
                pltpu.SemaphoreType.DMA((N_SLOTS,)),
                pltpu.SemaphoreType.DMA((N_SLOTS,)),
                pltpu.SemaphoreType.DMA,
                pltpu.SemaphoreType.DMA((N_SLOTS,)),
            ],
        ),
        out_shape=jax.ShapeDtypeStruct(((N_ASSIGN + DUMMY_ROWS) * PACK_ROWS, LANES), jnp.uint32),
        compiler_params=pltpu.CompilerParams(
            dimension_semantics=("arbitrary",),
            vmem_limit_bytes=EXPERT_VMEM_LIMIT,
        ),
        name="experts",
    )(bexp, nused, src2d, dst2d, n2, w_gate, w_up, w_down)


def _combine_kernel(h1_ref, mf_ref, fg_ref, y0_ref, y1_ref, out_ref):
    w = mf_ref[...]
    y0 = _unpack_bf16_pairs(_load_packed_rows(y0_ref, TD))
    y1 = _unpack_bf16_pairs(_load_packed_rows(y1_ref, TD))
    h = h1_ref[...] + y0 * w[:, 0:1] + y1 * w[:, 1:2]
    out_ref[...] = _rms(h, fg_ref[...])


def _combine(h1, mf, fg, y):
    n_steps = N_TOK // TD
    tok = lambda i: (i, 0)
    return pl.pallas_call(
        _combine_kernel,
        grid=(n_steps,),
        in_specs=[
            pl.BlockSpec((TD, D_MODEL), tok),
            pl.BlockSpec((TD, TOP_K), tok),
            pl.BlockSpec((1, D_MODEL), lambda i: (0, 0)),
            pl.BlockSpec((TD * PACK_ROWS, LANES), tok),
            pl.BlockSpec((TD * PACK_ROWS, LANES), lambda i: (i + n_steps, 0)),
        ],
        out_specs=pl.BlockSpec((TD, D_MODEL), tok),
        out_shape=jax.ShapeDtypeStruct((N_TOK, D_MODEL), F32),
        compiler_params=pltpu.CompilerParams(dimension_semantics=("arbitrary",)),
        name="combine",
    )(h1, mf, fg, y, y)


def kernel(x, norm1_gain, w_in, gmlp_v_gain, gmlp_w_s, gmlp_b_s, hgrn_lower_bounds,
           hgrn_out_gain, w_out, norm2_gain, w_group_router, b_group_router,
           w_expert_router, b_expert_router, w_gate, w_up, w_down, final_gain):
    l = 0
    x2 = x.reshape(N_TOK, D_MODEL)
    bst = jnp.tile(jnp.transpose(gmlp_b_s[l]), (TS // GMLP_BLOCK, 1))
    w_router = jnp.concatenate([w_group_router[l], w_expert_router[l]], axis=1)
    w_router = jnp.pad(w_router, ((0, 0), (0, LANES - w_router.shape[1]))).astype(BF16)
    b_router = jnp.concatenate([b_group_router[l], b_expert_router[l]])
    b_router = jnp.pad(b_router, (0, ROUTER_ROWS - b_router.shape[0])).reshape(ROUTER_ROWS, 1)

    h1, n2, meta_i, meta_f, counts = _mixer(
        x2, norm1_gain[l].reshape(1, D_MODEL), w_in[l].astype(BF16),
        gmlp_v_gain[l].reshape(1, D_A), gmlp_w_s[l], bst,
        hgrn_lower_bounds, hgrn_out_gain[l].reshape(1, D_B), w_out[l].astype(BF16),
        norm2_gain[l].reshape(1, D_MODEL), w_router, b_router)

    cnt = counts[EXPERT_ROW0:EXPERT_ROW0 + N_EXPERTS, 0]
    padded = ((cnt + MOE_BLOCK - 1) // MOE_BLOCK) * MOE_BLOCK
    pend = jnp.cumsum(padded)
    pstart = pend - padded
    eid = meta_i[0:TOP_K]
    rank = meta_i[TOP_K:2 * TOP_K]
    base = jnp.sum(jnp.where(eid[:, :, None] == jnp.arange(N_EXPERTS)[None, None, :],
                             pstart[None, None, :], 0), axis=-1)
    dest = (base + rank).astype(jnp.int32)
    dest = dest[0] | (dest[1] << DEST_BITS)
    gap_lo = (pstart + cnt).astype(jnp.int32)
    gap_hi = jnp.concatenate([pstart[1:], jnp.full((1,), INV_LEN)]).astype(jnp.int32)
    inv = _invert(dest, gap_lo, gap_hi)
    pos = jnp.arange(INV_LEN, dtype=jnp.int32)
    blk = jnp.where(pos < SORTED_ROWS, pos // MOE_BLOCK, N_SLOTS - 1)
    dummy = N_ASSIGN + (blk % N_SLOTS) * MOE_BLOCK + pos % MOE_BLOCK
    src2d = (jnp.where(inv >= 0, inv % N_TOK, 0) * PACK_ROWS)[:SORTED_ROWS].reshape(
        N_BLOCKS, 1, MOE_BLOCK)
    dst2d = (jnp.where(inv >= 0, inv, dummy) * PACK_ROWS).reshape(N_BLOCKS + 1, 1, MOE_BLOCK)
    blk_start = jnp.arange(N_BLOCKS, dtype=jnp.int32) * MOE_BLOCK
    bexp = jnp.clip(jnp.sum(blk_start[:, None] >= pend[None, :], axis=1), 0,
                    N_EXPERTS - 1).astype(jnp.int32)
    nused = (pend[-1:] // MOE_BLOCK).astype(jnp.int32)

    y = _experts(bexp, nused, src2d, dst2d, n2, w_gate[l], w_up[l], w_down[l])
    out = _combine(h1, meta_f[0:TOP_K].T, final_gain.reshape(1, D_MODEL), y)
    return out.reshape(BATCH, SEQ, D_MODEL)
```

```python
import functools

import jax
import jax.numpy as jnp
from jax import lax
from jax.experimental import pallas as pl
from jax.experimental.pallas import tpu as pltpu

D_MODEL = 1024
BATCH = 2
SEQ = 8192
N_TOK = BATCH * SEQ
CHUNK = 64
EPS = 1e-6
D_A = 512
HEADS = 4
HEAD_DIM = 128
GMLP_BLOCK = 128
D_B = 512
D_IN = 3072
N_GROUPS = 4
EXPERTS_PER_GROUP = 8
N_EXPERTS = 32
TOP_K = 2
D_EXPERT = 512

LANES = 128
ROUTER_ROWS = 48
EXPERT_ROW0 = N_GROUPS
META_ROWS = 8
PACK_COLS = D_MODEL // 2
PACK_ROWS = PACK_COLS // LANES
HI_MASK = 0xFFFF0000

TS = 256
CHUNKS_PER_STEP = TS // CHUNK
MOE_BLOCK = 256
N_BLOCKS = N_TOK * TOP_K // MOE_BLOCK + N_EXPERTS
SORTED_ROWS = N_BLOCKS * MOE_BLOCK
N_ASSIGN = N_TOK * TOP_K
N_SLOTS = 3
DUMMY_ROWS = N_SLOTS * MOE_BLOCK
BLOCK_PACK_ROWS = MOE_BLOCK * PACK_ROWS
INV_LEN = SORTED_ROWS + MOE_BLOCK
TD = 1024

F32 = jnp.float32
BF16 = jnp.bfloat16
NT_DIMS = (((1,), (1,)), ((), ()))
TN_DIMS = (((0,), (0,)), ((), ()))


def _sigmoid(x):
    return 1.0 / (1.0 + jnp.exp(-x))


def _gelu(x):
    return 0.5 * x * (1.0 + lax.erf(x * (2.0 ** -0.5)))


def _rms(x, gain):
    return x * lax.rsqrt(jnp.mean(x * x, axis=-1, keepdims=True) + EPS) * gain


def _pack_bf16_pairs(x):
    lo = pltpu.bitcast(x[:, 0:PACK_COLS].astype(BF16).astype(F32), jnp.uint32) >> 16
    hi = (pltpu.bitcast(x[:, PACK_COLS:D_MODEL].astype(BF16).astype(F32), jnp.uint32)
          & jnp.uint32(HI_MASK))
    return lo | hi


def _unpack_bf16_pairs(words):
    lo = pltpu.bitcast(words << 16, F32)
    hi = pltpu.bitcast(words & jnp.uint32(HI_MASK), F32)
    return jnp.concatenate([lo, hi], axis=1)


def _store_packed_rows(ref, words, rows):
    for c in range(PACK_ROWS):
        ref[pl.ds(c, rows, stride=PACK_ROWS), :] = words[:, c * LANES:(c + 1) * LANES]


def _load_packed_rows(ref, rows):
    return jnp.concatenate(
        [ref[pl.ds(c, rows, stride=PACK_ROWS), :] for c in range(PACK_ROWS)], axis=1)


def _split_bf16(x):
    hi = x.astype(BF16)
    lo = (x - hi.astype(F32)).astype(BF16)
    return hi, lo


def _mixer_kernel(*refs):
    for parity in range(2):
        pl.when(pl.program_id(0) % 2 == parity)(functools.partial(_mixer_step, parity, *refs))


def _mixer_step(cur, x_hbm, xn_ref, g1_ref, win_ref, vg_ref, ws_ref, bst_ref, lbraw_ref, og_ref,
                wout_ref, g2_ref, wr_ref, br_ref,
                h1_ref, n2_ref, mi_ref, mf_ref, cnt_ref,
                st_ref, carry_ref, wm_ref, mix_ref, proj_ref, logit_ref, xkeep_ref, xsem):
    i = pl.program_id(0)

    @pl.when(i == 0)
    def _init():
        carry_ref[...] = jnp.zeros_like(carry_ref)
        logit_ref[...] = jnp.zeros_like(logit_ref)
        r = lax.broadcasted_iota(jnp.int32, (GMLP_BLOCK, GMLP_BLOCK), 0)
        c = lax.broadcasted_iota(jnp.int32, (GMLP_BLOCK, GMLP_BLOCK), 1)
        keep = (c // CHUNK) <= (r // CHUNK)
        wm_ref[...] = jnp.zeros_like(wm_ref)
        for g in range(HEADS):
            w = jnp.where(keep, ws_ref[g], 0.0).astype(BF16)
            for p in range(TS // GMLP_BLOCK):
                wm_ref[g, p * GMLP_BLOCK:(p + 1) * GMLP_BLOCK,
                       p * GMLP_BLOCK:(p + 1) * GMLP_BLOCK] = w

    @pl.when(i % (SEQ // TS) == 0)
    def _reset_state():
        st_ref[...] = jnp.zeros_like(st_ref)

    def project(n1, slot, c):
        proj_ref[slot, :, c * D_A:(c + 1) * D_A] = jnp.dot(
            n1, win_ref[:, c * D_A:(c + 1) * D_A], preferred_element_type=F32)

    @pl.when(i == 0)
    def _first_projection():
        first = pltpu.make_async_copy(x_hbm.at[pl.ds(0, TS)], xkeep_ref.at[0], xsem)
        first.start()
        first.wait()
        n1 = _rms(xkeep_ref[0], g1_ref[...]).astype(BF16)
        for c in range(D_IN // D_A):
            project(n1, 0, c)

    x_next = xn_ref[...]
    n1_next = _rms(x_next, g1_ref[...]).astype(BF16)
    xkeep_ref[1 - cur] = x_next

    project(n1_next, 1 - cur, 0)
    gu = _gelu(proj_ref[cur, :, 0:D_A])
    project(n1_next, 1 - cur, 1)
    gv = _gelu(proj_ref[cur, :, D_A:2 * D_A])
    for g in range(HEADS):
        sl = slice(g * HEAD_DIM, (g + 1) * HEAD_DIM)
        vh = _rms(gv[:, sl], vg_ref[:, sl]).astype(BF16)
        mixed = jnp.dot(wm_ref[g], vh, preferred_element_type=F32) + bst_ref[:, g:g + 1]
        mix_ref[:, sl] = (gu[:, sl] * mixed).astype(BF16)

    _route(logit_ref[1 - cur], jnp.maximum(i - 1, 0), jnp.where(i > 0, 1.0, 0.0),
           carry_ref, cnt_ref, mi_ref, mf_ref)

    o2 = 2 * D_A
    qr = proj_ref[cur, :, o2:o2 + D_B]
    fr = proj_ref[cur, :, o2 + D_B:o2 + 2 * D_B]
    vv = proj_ref[cur, :, o2 + 2 * D_B:o2 + 3 * D_B].astype(BF16)
    gr = proj_ref[cur, :, o2 + 3 * D_B:o2 + 4 * D_B]

    lbr = lbraw_ref[...]
    lbm = jnp.max(lbr, axis=0, keepdims=True)
    lbe = jnp.exp(lbr - lbm)
    lb = lbe[0:1, :] / jnp.sum(lbe, axis=0, keepdims=True)

    project(n1_next, 1 - cur, 2)
    q = qr * _sigmoid(qr)
    f = lb + (1.0 - lb) * _sigmoid(fr)
    k = 1.0 - f
    lf = jnp.log(f)

    row = lax.broadcasted_iota(jnp.int32, (TS, TS), 0)
    col = lax.broadcasted_iota(jnp.int32, (TS, TS), 1)
    causal = (row >= col) & ((row // CHUNK) == (col // CHUNK))
    tri = jnp.where(causal, 1.0, 0.0).astype(BF16)
    lf_hi, lf_lo = _split_bf16(lf)
    b = (jnp.dot(tri, lf_hi, preferred_element_type=F32)
         + jnp.dot(tri, lf_lo, preferred_element_type=F32))

    def chunk_rows(r):
        return jnp.concatenate(
            [jnp.broadcast_to(b[c * CHUNK + r:c * CHUNK + r + 1, :], (CHUNK, D_B))
             for c in range(CHUNKS_PER_STEP)], axis=0)

    project(n1_next, 1 - cur, 3)
    bref = chunk_rows(CHUNK // 2 - 1)
    blast = chunk_rows(CHUNK - 1)
    qe = (q * jnp.exp(b - bref)).astype(BF16)
    ke = (k * jnp.exp(bref - b)).astype(BF16)
    kd = (k * jnp.exp(blast - b)).astype(BF16)
    qb = (q * jnp.exp(b)).astype(BF16)

    head_sl = [slice(h * HEAD_DIM, (h + 1) * HEAD_DIM) for h in range(HEADS)]
    chunk_sl = [slice(c * CHUNK, (c + 1) * CHUNK) for c in range(CHUNKS_PER_STEP)]
    scores = [lax.dot_general(qe[:, sl], ke[:, sl], NT_DIMS, preferred_element_type=F32)
              for sl in head_sl]
    upd = [[lax.dot_general(vv[rows, sl], kd[rows, sl], TN_DIMS, preferred_element_type=F32)
            for rows in chunk_sl] for sl in head_sl]
    project(n1_next, 1 - cur, 4)
    scores = [jnp.where(causal, sc, 0.0).astype(BF16) for sc in scores]
    intra = [jnp.dot(scores[h], vv[:, head_sl[h]], preferred_element_type=F32)
             for h in range(HEADS)]
    entering = []
    for h in range(HEADS):
        st = st_ref[h]
        per_chunk = []
        for c in range(CHUNKS_PER_STEP):
            per_chunk.append(st.astype(BF16))
            decay = jnp.exp(b[(c + 1) * CHUNK - 1:(c + 1) * CHUNK, head_sl[h]])
            st = st * decay + upd[h][c]
        st_ref[h] = st
        entering.append(per_chunk)
    inter = [jnp.concatenate(
        [lax.dot_general(qb[chunk_sl[c], head_sl[h]], entering[h][c], NT_DIMS,
                         preferred_element_type=F32) for c in range(CHUNKS_PER_STEP)], axis=0)
        for h in range(HEADS)]
    for h in range(HEADS):
        o = _rms(intra[h] + inter[h], og_ref[:, head_sl[h]])
        g_h = gr[:, head_sl[h]]
        mix_ref[:, D_A + h * HEAD_DIM:D_A + (h + 1) * HEAD_DIM] = (
            o * (g_h * _sigmoid(g_h))).astype(BF16)

    h1 = xkeep_ref[cur] + jnp.dot(mix_ref[...], wout_ref[...], preferred_element_type=F32)
    h1_ref[...] = h1
    n2 = _rms(h1, g2_ref[...])
    _store_packed_rows(n2_ref, _pack_bf16_pairs(n2), TS)

    logits = jnp.dot(n2.astype(BF16), wr_ref[...], preferred_element_type=F32)
    logits = jnp.transpose(logits)[0:ROUTER_ROWS, :] + br_ref[...]
    project(n1_next, 1 - cur, 5)
    logit_ref[cur] = logits

    @pl.when(i == N_TOK // TS - 1)
    def _route_last_tile():
        _route(logits, i, 1.0, carry_ref, cnt_ref, mi_ref, mf_ref)


def _route(logits, tile, live, carry_ref, cnt_ref, mi_ref, mf_ref):
    rid = lax.broadcasted_iota(jnp.int32, (ROUTER_ROWS, TS), 0)
    row = lax.broadcasted_iota(jnp.int32, (TS, TS), 0)
    col = lax.broadcasted_iota(jnp.int32, (TS, TS), 1)
    neg = -jnp.inf
    gl = jnp.where(rid < N_GROUPS, logits, neg)
    gmax = jnp.max(gl, axis=0, keepdims=True)
    g_idx = jnp.min(jnp.where(gl == gmax, rid, ROUTER_ROWS), axis=0, keepdims=True)
    g_prob = 1.0 / jnp.sum(jnp.exp(gl - gmax), axis=0, keepdims=True)
    e_lo = EXPERT_ROW0 + g_idx * EXPERTS_PER_GROUP
    el = jnp.where((rid >= e_lo) & (rid < e_lo + EXPERTS_PER_GROUP), logits, neg)
    m1 = jnp.max(el, axis=0, keepdims=True)
    i1 = jnp.min(jnp.where(el == m1, rid, ROUTER_ROWS), axis=0, keepdims=True)
    el2 = jnp.where(rid == i1, neg, el)
    m2 = jnp.max(el2, axis=0, keepdims=True)
    i2 = jnp.min(jnp.where(el2 == m2, rid, ROUTER_ROWS), axis=0, keepdims=True)
    e21 = jnp.exp(m2 - m1)
    w1 = g_prob / (1.0 + e21)
    w2 = g_prob * e21 / (1.0 + e21)

    hit1 = rid == i1
    hit2 = rid == i2
    onehot = jnp.where(hit1 | hit2, live, 0.0)
    earlier = jnp.where(row < col, 1.0, 0.0).astype(BF16)
    seen = carry_ref[...]
    prior = (jnp.dot(onehot.astype(BF16), earlier, preferred_element_type=F32)
             + jnp.concatenate([seen] * (TS // LANES), axis=1))
    r1 = jnp.sum(jnp.where(hit1, prior, 0.0), axis=0, keepdims=True)
    r2 = jnp.sum(jnp.where(hit2, prior, 0.0), axis=0, keepdims=True)
    seen = seen + jnp.sum(onehot, axis=1, keepdims=True)
    carry_ref[...] = seen
    cnt_ref[...] = seen.astype(jnp.int32)

    cols = pl.ds(pl.multiple_of(tile * TS, TS), TS)
    pad_i = jnp.zeros((META_ROWS - 2 * TOP_K, TS), jnp.int32)
    mi_ref[:, cols] = jnp.concatenate(
        [i1 - EXPERT_ROW0, i2 - EXPERT_ROW0, r1.astype(jnp.int32), r2.astype(jnp.int32), pad_i],
        axis=0)
    mf_ref[:, cols] = jnp.concatenate([w1, w2, jnp.zeros((META_ROWS - TOP_K, TS), F32)], axis=0)


def _mixer(x2, g1, win, vg, ws, bst, lbraw, og, wout, g2, wr, br):
    n_steps = N_TOK // TS
    const2 = lambda i: (0, 0)
    tok = lambda i: (i, 0)
    return pl.pallas_call(
        _mixer_kernel,
        grid=(n_steps,),
        in_specs=[
            pl.BlockSpec(memory_space=pl.ANY),
            pl.BlockSpec((TS, D_MODEL), lambda i: (jnp.minimum(i + 1, n_steps - 1), 0)),
            pl.BlockSpec((1, D_MODEL), const2),
            pl.BlockSpec((D_MODEL, D_IN), const2),
            pl.BlockSpec((1, D_A), const2),
            pl.BlockSpec((HEADS, GMLP_BLOCK, GMLP_BLOCK), lambda i: (0, 0, 0)),
            pl.BlockSpec((TS, HEADS), const2),
            pl.BlockSpec((2, D_B), const2),
            pl.BlockSpec((1, D_B), const2),
            pl.BlockSpec((D_MODEL, D_MODEL), const2),
            pl.BlockSpec((1, D_MODEL), const2),
            pl.BlockSpec((D_MODEL, LANES), const2),
            pl.BlockSpec((ROUTER_ROWS, 1), const2),
        ],
        out_specs=[
            pl.BlockSpec((TS, D_MODEL), tok),
            pl.BlockSpec((TS * PACK_ROWS, LANES), tok),
            pl.BlockSpec((META_ROWS, N_TOK), const2),
            pl.BlockSpec((META_ROWS, N_TOK), const2),
            pl.BlockSpec((ROUTER_ROWS, LANES), const2),
        ],
        out_shape=[
            jax.ShapeDtypeStruct((N_TOK, D_MODEL), F32),
            jax.ShapeDtypeStruct((N_TOK * PACK_ROWS, LANES), jnp.uint32),
            jax.ShapeDtypeStruct((META_ROWS, N_TOK), jnp.int32),
            jax.ShapeDtypeStruct((META_ROWS, N_TOK), F32),
            jax.ShapeDtypeStruct((ROUTER_ROWS, LANES), jnp.int32),
        ],
        scratch_shapes=[
            pltpu.VMEM((HEADS, HEAD_DIM, HEAD_DIM), F32),
            pltpu.VMEM((ROUTER_ROWS, LANES), F32),
            pltpu.VMEM((HEADS, TS, TS), BF16),
            pltpu.VMEM((TS, D_MODEL), BF16),
            pltpu.VMEM((2, TS, D_IN), F32),
            pltpu.VMEM((2, ROUTER_ROWS, TS), F32),
            pltpu.VMEM((2, TS, D_MODEL), F32),
            pltpu.SemaphoreType.DMA,
        ],
        compiler_params=pltpu.CompilerParams(
            dimension_semantics=("arbitrary",),
            vmem_limit_bytes=56 * 1024 * 1024,
        ),
        name="mixer",
    )(x2, x2, g1, win, vg, ws, bst, lbraw, og, wout, g2, wr, br)


FILL_UNROLL = 8

def _invert_kernel(dest_ref, gap_lo_ref, gap_hi_ref, inv_ref):
    for e in range(N_EXPERTS):
        first = gap_lo_ref[e] // FILL_UNROLL

        def fill(g, carry):
            for u in range(FILL_UNROLL):
                inv_ref[g * FILL_UNROLL + u] = -1
            return carry

        lax.fori_loop(first, gap_hi_ref[e] // FILL_UNROLL, fill, 0)

    def place(a, carry):
        inv_ref[dest_ref[a]] = a
        return carry

    lax.fori_loop(0, N_ASSIGN, place, 0, unroll=16)


def _invert(dest, gap_lo, gap_hi):
    smem = pl.BlockSpec(memory_space=pltpu.SMEM)
    return pl.pallas_call(
        _invert_kernel,
        in_specs=[smem, smem, smem],
        out_specs=smem,
        out_shape=jax.ShapeDtypeStruct((INV_LEN,), jnp.int32),
        name="invert",
    )(dest, gap_lo, gap_hi)


def _expert_kernel(bexp_ref, nused_ref, src_hbm, dst_hbm, n2_hbm, wg_ref, wu_ref, wd_ref, y_hbm,
                   src_smem, dst_smem, n2_vmem, xb0, xb1, xb2, ob0, ob1, ob2, wg_bf, wu_bf, wd_bf,
                   src_sem, dst_sem, nsem, ssem):
    i = pl.program_id(0)
    nused = nused_ref[0]
    last = nused - 1
    xbuf = (xb0, xb1, xb2)
    obuf = (ob0, ob1, ob2)

    def src_copy(blk, s):
        return pltpu.make_async_copy(src_hbm.at[blk], src_smem.at[pl.ds(s, 1)], src_sem.at[s])

    def dst_copy(blk, s):
        return pltpu.make_async_copy(dst_hbm.at[blk], dst_smem.at[pl.ds(s, 1)], dst_sem.at[s])

    def gather(s):
        for j in range(MOE_BLOCK):
            r = pl.multiple_of(src_smem[s, j], PACK_ROWS)
            xbuf[s][pl.ds(j * PACK_ROWS, PACK_ROWS), :] = n2_vmem[pl.ds(r, PACK_ROWS), :]

    def start_scatter(s):
        for j in range(MOE_BLOCK):
            r = pl.multiple_of(dst_smem[s, j], PACK_ROWS)
            pltpu.make_async_copy(obuf[s].at[pl.ds(j * PACK_ROWS, PACK_ROWS)],
                                  y_hbm.at[pl.ds(r, PACK_ROWS)], ssem.at[s]
                                  ).start(priority=j % 2)

    def wait_scatter(s):
        pltpu.make_async_copy(obuf[s], y_hbm.at[pl.ds(0, BLOCK_PACK_ROWS)],
                              ssem.at[s]).wait()

    @pl.when(i == 0)
    def _prologue():
        for o in obuf:
            o[...] = jnp.zeros_like(o)
        for s in range(N_SLOTS - 1):
            pltpu.make_async_copy(
                obuf[s],
                y_hbm.at[pl.ds((N_ASSIGN + s * MOE_BLOCK) * PACK_ROWS, BLOCK_PACK_ROWS)],
                ssem.at[s]).start()
        resident = pltpu.make_async_copy(n2_hbm, n2_vmem, nsem)
        resident.start()
        src_copy(0, 0).start()
        src_copy(0, 0).wait()
        resident.wait()
        gather(0)
        src_copy(jnp.minimum(1, last), 1).start()
        dst_copy(N_BLOCKS, N_SLOTS - 1).start()

    prev = bexp_ref[jnp.maximum(i - 1, 0)]
    changed = (i == 0) | (bexp_ref[i] != prev)

    @pl.when(changed)
    def _cast_weights():
        wg_bf[...] = wg_ref[0].astype(BF16)
        wu_bf[...] = wu_ref[0].astype(BF16)
        wd_bf[...] = wd_ref[0].astype(BF16)

    def active(cur, nxt, prv):
        src_copy(0, nxt).wait()
        dst_copy(0, prv).wait()
        wait_scatter(cur)
        start_scatter(prv)
        gather(nxt)
        src_copy(jnp.minimum(i + 2, last), prv).start()
        dst_copy(i, cur).start()
        xb = _unpack_bf16_pairs(_load_packed_rows(xbuf[cur], MOE_BLOCK)).astype(BF16)
        hg = jnp.dot(xb, wg_bf[...], preferred_element_type=F32)
        hu = jnp.dot(xb, wu_bf[...], preferred_element_type=F32)
        hh = (hg * _sigmoid(hg) * hu).astype(BF16)
        y = jnp.dot(hh, wd_bf[...], preferred_element_type=F32)
        _store_packed_rows(obuf[cur], _pack_bf16_pairs(y), MOE_BLOCK)

    def epilogue(cur, nxt, prv):
        dst_copy(0, cur).wait()
        start_scatter(cur)
        wait_scatter(nxt)
        wait_scatter(prv)
        wait_scatter(cur)
        src_copy(0, prv).wait()

    for cur in range(N_SLOTS):
        slots = (cur, (cur + 1) % N_SLOTS, (cur + 2) % N_SLOTS)

        @pl.when((i < nused) & (i % N_SLOTS == cur))
        def _():
            active(*slots)

        @pl.when((i == last) & (i % N_SLOTS == cur))
        def _():
            epilogue(*slots)


_EXPERT_W_BYTES = 3 * D_MODEL * D_EXPERT * 4
EXPERT_VMEM_LIMIT = (
    N_TOK * PACK_COLS * 4
    + 2 * _EXPERT_W_BYTES
    + _EXPERT_W_BYTES // 2
    + 2 * N_SLOTS * MOE_BLOCK * PACK_COLS * 4
    + 4 * 1024 * 1024)


def _experts(bexp, nused, src2d, dst2d, n2, w_gate, w_up, w_down):
    def w_blk(i, bexp, nused):
        return (bexp[i], 0, 0)

    any_spec = pl.BlockSpec(memory_space=pl.ANY)
    return pl.pallas_call(
        _expert_kernel,
        grid_spec=pltpu.PrefetchScalarGridSpec(
            num_scalar_prefetch=2,
            grid=(N_BLOCKS,),
            in_specs=[
                any_spec, any_spec, any_spec,
                pl.BlockSpec((1, D_MODEL, D_EXPERT), w_blk),
                pl.BlockSpec((1, D_MODEL, D_EXPERT), w_blk),
                pl.BlockSpec((1, D_EXPERT, D_MODEL), w_blk),
            ],
            out_specs=any_spec,
            scratch_shapes=[
                pltpu.SMEM((N_SLOTS, MOE_BLOCK), jnp.int32),
                pltpu.SMEM((N_SLOTS, MOE_BLOCK), jnp.int32),
                pltpu.VMEM((N_TOK * PACK_ROWS, LANES), jnp.uint32),
            ] + [pltpu.VMEM((BLOCK_PACK_ROWS, LANES), jnp.uint32)] * (2 * N_SLOTS) + [
                pltpu.VMEM((D_MODEL, D_EXPERT), BF16),
                pltpu.VMEM((D_MODEL, D_EXPERT), BF16),
                pltpu.VMEM((D_EXPERT, D_MODEL), BF16),
                pltpu.SemaphoreType.DMA((N_SLOTS,)),
                pltpu.SemaphoreType.DMA((N_SLOTS,)),
                pltpu.SemaphoreType.DMA,
                pltpu.SemaphoreType.DMA((N_SLOTS,)),
            ],
        ),
        out_shape=jax.ShapeDtypeStruct(((N_ASSIGN + DUMMY_ROWS) * PACK_ROWS, LANES), jnp.uint32),
        compiler_params=pltpu.CompilerParams(
            dimension_semantics=("arbitrary",),
            vmem_limit_bytes=EXPERT_VMEM_LIMIT,
        ),
        name="experts",
    )(bexp, nused, src2d, dst2d, n2, w_gate, w_up, w_down)


def _combine_kernel(h1_ref, mf_ref, fg_ref, y0_ref, y1_ref, out_ref):
    w = mf_ref[...]
    y0 = _unpack_bf16_pairs(_load_packed_rows(y0_ref, TD))
    y1 = _unpack_bf16_pairs(_load_packed_rows(y1_ref, TD))
    h = h1_ref[...] + y0 * w[:, 0:1] + y1 * w[:, 1:2]
    out_ref[...] = _rms(h, fg_ref[...])


COMBINE_VMEM_LIMIT = 2 * (2 * (2 * TD * D_MODEL * 4) + 2 * (2 * TD * PACK_COLS * 4))


def _combine(h1, mf, fg, y):
    n_steps = N_TOK // TD
    tok = lambda i: (i, 0)
    return pl.pallas_call(
        _combine_kernel,
        grid=(n_steps,),
        in_specs=[
            pl.BlockSpec((TD, D_MODEL), tok),
            pl.BlockSpec((TD, TOP_K), tok),
            pl.BlockSpec((1, D_MODEL), lambda i: (0, 0)),
            pl.BlockSpec((TD * PACK_ROWS, LANES), tok),
            pl.BlockSpec((TD * PACK_ROWS, LANES), lambda i: (i + n_steps, 0)),
        ],
        out_specs=pl.BlockSpec((TD, D_MODEL), tok),
        out_shape=jax.ShapeDtypeStruct((N_TOK, D_MODEL), F32),
        compiler_params=pltpu.CompilerParams(
            dimension_semantics=("arbitrary",),
            vmem_limit_bytes=COMBINE_VMEM_LIMIT,
        ),
        name="combine",
    )(h1, mf, fg, y, y)


def kernel(x, norm1_gain, w_in, gmlp_v_gain, gmlp_w_s, gmlp_b_s, hgrn_lower_bounds,
           hgrn_out_gain, w_out, norm2_gain, w_group_router, b_group_router,
           w_expert_router, b_expert_router, w_gate, w_up, w_down, final_gain):
    l = 0
    x2 = x.reshape(N_TOK, D_MODEL)
    bst = jnp.tile(jnp.transpose(gmlp_b_s[l]), (TS // GMLP_BLOCK, 1))
    w_router = jnp.concatenate([w_group_router[l], w_expert_router[l]], axis=1)
    w_router = jnp.pad(w_router, ((0, 0), (0, LANES - w_router.shape[1]))).astype(BF16)
    b_router = jnp.concatenate([b_group_router[l], b_expert_router[l]])
    b_router = jnp.pad(b_router, (0, ROUTER_ROWS - b_router.shape[0])).reshape(ROUTER_ROWS, 1)

    h1, n2, meta_i, meta_f, counts = _mixer(
        x2, norm1_gain[l].reshape(1, D_MODEL), w_in[l].astype(BF16),
        gmlp_v_gain[l].reshape(1, D_A), gmlp_w_s[l], bst,
        hgrn_lower_bounds, hgrn_out_gain[l].reshape(1, D_B), w_out[l].astype(BF16),
        norm2_gain[l].reshape(1, D_MODEL), w_router, b_router)

    cnt = counts[EXPERT_ROW0:EXPERT_ROW0 + N_EXPERTS, 0]
    padded = ((cnt + MOE_BLOCK - 1) // MOE_BLOCK) * MOE_BLOCK
    pend = jnp.cumsum(padded)
    pstart = pend - padded
    eid = meta_i[0:TOP_K]
    rank = meta_i[TOP_K:2 * TOP_K]
    base = jnp.sum(jnp.where(eid[:, :, None] == jnp.arange(N_EXPERTS)[None, None, :],
                             pstart[None, None, :], 0), axis=-1)
    dest = (base + rank).astype(jnp.int32).reshape(N_ASSIGN)
    gap_lo = (pstart + cnt).astype(jnp.int32)
    gap_hi = jnp.concatenate([pstart[1:], jnp.full((1,), INV_LEN)]).astype(jnp.int32)
    inv = _invert(dest, gap_lo, gap_hi)
    pos = jnp.arange(INV_LEN, dtype=jnp.int32)
    blk = jnp.where(pos < SORTED_ROWS, pos // MOE_BLOCK, N_SLOTS - 1)
    dummy = N_ASSIGN + (blk % N_SLOTS) * MOE_BLOCK + pos % MOE_BLOCK
    src2d = (jnp.where(inv >= 0, inv % N_TOK, 0) * PACK_ROWS)[:SORTED_ROWS].reshape(
        N_BLOCKS, 1, MOE_BLOCK)
    dst2d = (jnp.where(inv >= 0, inv, dummy) * PACK_ROWS).reshape(N_BLOCKS + 1, 1, MOE_BLOCK)
    blk_start = jnp.arange(N_BLOCKS, dtype=jnp.int32) * MOE_BLOCK
    bexp = jnp.clip(jnp.sum(blk_start[:, None] >= pend[None, :], axis=1), 0,
                    N_EXPERTS - 1).astype(jnp.int32)
    nused = (pend[-1:] // MOE_BLOCK).astype(jnp.int32)

    y = _experts(bexp, nused, src2d, dst2d, n2, w_gate[l], w_up[l], w_down[l])
    out = _combine(h1, meta_f[0:TOP_K].T, final_gain.reshape(1, D_MODEL), y)
    return out.reshape(BATCH, SEQ, D_MODEL)
```

```python
import functools

import jax
import jax.numpy as jnp
from jax import lax
from jax.experimental import pallas as pl
from jax.experimental.pallas import tpu as pltpu

D_MODEL = 1024
BATCH = 2
SEQ = 8192
N_TOK = BATCH * SEQ
CHUNK = 64
EPS = 1e-6
D_A = 512
HEADS = 4
HEAD_DIM = 128
GMLP_BLOCK = 128
D_B = 512
D_IN = 3072
N_GROUPS = 4
EXPERTS_PER_GROUP = 8
N_EXPERTS = 32
TOP_K = 2
D_EXPERT = 512

LANES = 128
ROUTER_ROWS = 48
EXPERT_ROW0 = N_GROUPS
META_ROWS = 8
PACK_COLS = D_MODEL // 2
PACK_ROWS = PACK_COLS // LANES
HI_MASK = 0xFFFF0000

TS = 256
CHUNKS_PER_STEP = TS // CHUNK
MOE_BLOCK = 256
N_BLOCKS = N_TOK * TOP_K // MOE_BLOCK + N_EXPERTS
SORTED_ROWS = N_BLOCKS * MOE_BLOCK
N_ASSIGN = N_TOK * TOP_K
N_SLOTS = 3
DUMMY_ROWS = N_SLOTS * MOE_BLOCK
BLOCK_PACK_ROWS = MOE_BLOCK * PACK_ROWS
INV_LEN = SORTED_ROWS + MOE_BLOCK
TD = 1024

F32 = jnp.float32
BF16 = jnp.bfloat16
NT_DIMS = (((1,), (1,)), ((), ()))
TN_DIMS = (((0,), (0,)), ((), ()))


def _sigmoid(x):
    return 1.0 / (1.0 + jnp.exp(-x))


def _gelu(x):
    return 0.5 * x * (1.0 + lax.erf(x * (2.0 ** -0.5)))


def _rms(x, gain):
    return x * lax.rsqrt(jnp.mean(x * x, axis=-1, keepdims=True) + EPS) * gain


def _pack_bf16_pairs(x):
    lo = pltpu.bitcast(x[:, 0:PACK_COLS].astype(BF16).astype(F32), jnp.uint32) >> 16
    hi = (pltpu.bitcast(x[:, PACK_COLS:D_MODEL].astype(BF16).astype(F32), jnp.uint32)
          & jnp.uint32(HI_MASK))
    return lo | hi


def _unpack_bf16_pairs(words):
    lo = pltpu.bitcast(words << 16, F32)
    hi = pltpu.bitcast(words & jnp.uint32(HI_MASK), F32)
    return jnp.concatenate([lo, hi], axis=1)


def _store_packed_rows(ref, words, rows):
    for c in range(PACK_ROWS):
        ref[pl.ds(c, rows, stride=PACK_ROWS), :] = words[:, c * LANES:(c + 1) * LANES]


def _load_packed_rows(ref, rows):
    return jnp.concatenate(
        [ref[pl.ds(c, rows, stride=PACK_ROWS), :] for c in range(PACK_ROWS)], axis=1)


def _split_bf16(x):
    hi = x.astype(BF16)
    lo = (x - hi.astype(F32)).astype(BF16)
    return hi, lo


def _mixer_kernel(*refs):
    for parity in range(2):
        pl.when(pl.program_id(0) % 2 == parity)(functools.partial(_mixer_step, parity, *refs))


def _mixer_step(cur, x_hbm, xn_ref, g1_ref, win_ref, vg_ref, ws_ref, bst_ref, lbraw_ref, og_ref,
                wout_ref, g2_ref, wr_ref, br_ref,
                h1_ref, n2_ref, mi_ref, mf_ref, cnt_ref,
                st_ref, carry_ref, wm_ref, mix_ref, proj_ref, logit_ref, xkeep_ref, xsem):
    i = pl.program_id(0)

    @pl.when(i == 0)
    def _init():
        carry_ref[...] = jnp.zeros_like(carry_ref)
        logit_ref[...] = jnp.zeros_like(logit_ref)
        r = lax.broadcasted_iota(jnp.int32, (GMLP_BLOCK, GMLP_BLOCK), 0)
        c = lax.broadcasted_iota(jnp.int32, (GMLP_BLOCK, GMLP_BLOCK), 1)
        keep = (c // CHUNK) <= (r // CHUNK)
        wm_ref[...] = jnp.zeros_like(wm_ref)
        for g in range(HEADS):
            w = jnp.where(keep, ws_ref[g], 0.0).astype(BF16)
            for p in range(TS // GMLP_BLOCK):
                wm_ref[g, p * GMLP_BLOCK:(p + 1) * GMLP_BLOCK,
                       p * GMLP_BLOCK:(p + 1) * GMLP_BLOCK] = w

    @pl.when(i % (SEQ // TS) == 0)
    def _reset_state():
        st_ref[...] = jnp.zeros_like(st_ref)

    def project(n1, slot, c):
        proj_ref[slot, :, c * D_A:(c + 1) * D_A] = jnp.dot(
            n1, win_ref[:, c * D_A:(c + 1) * D_A], preferred_element_type=F32)

    @pl.when(i == 0)
    def _first_projection():
        first = pltpu.make_async_copy(x_hbm.at[pl.ds(0, TS)], xkeep_ref.at[0], xsem)
        first.start()
        first.wait()
        n1 = _rms(xkeep_ref[0], g1_ref[...]).astype(BF16)
        for c in range(D_IN // D_A):
            project(n1, 0, c)

    x_next = xn_ref[...]
    n1_next = _rms(x_next, g1_ref[...]).astype(BF16)
    xkeep_ref[1 - cur] = x_next

    project(n1_next, 1 - cur, 0)
    gu = _gelu(proj_ref[cur, :, 0:D_A])
    project(n1_next, 1 - cur, 1)
    gv = _gelu(proj_ref[cur, :, D_A:2 * D_A])
    for g in range(HEADS):
        sl = slice(g * HEAD_DIM, (g + 1) * HEAD_DIM)
        vh = _rms(gv[:, sl], vg_ref[:, sl]).astype(BF16)
        mixed = jnp.dot(wm_ref[g], vh, preferred_element_type=F32) + bst_ref[:, g:g + 1]
        mix_ref[:, sl] = (gu[:, sl] * mixed).astype(BF16)

    _route(logit_ref[1 - cur], jnp.maximum(i - 1, 0), jnp.where(i > 0, 1.0, 0.0),
           carry_ref, cnt_ref, mi_ref, mf_ref)

    o2 = 2 * D_A
    qr = proj_ref[cur, :, o2:o2 + D_B]
    fr = proj_ref[cur, :, o2 + D_B:o2 + 2 * D_B]
    vv = proj_ref[cur, :, o2 + 2 * D_B:o2 + 3 * D_B].astype(BF16)
    gr = proj_ref[cur, :, o2 + 3 * D_B:o2 + 4 * D_B]

    lbr = lbraw_ref[...]
    lbm = jnp.max(lbr, axis=0, keepdims=True)
    lbe = jnp.exp(lbr - lbm)
    lb = lbe[0:1, :] / jnp.sum(lbe, axis=0, keepdims=True)

    project(n1_next, 1 - cur, 2)
    q = qr * _sigmoid(qr)
    f = lb + (1.0 - lb) * _sigmoid(fr)
    k = 1.0 - f
    lf = jnp.log(f)

    row = lax.broadcasted_iota(jnp.int32, (TS, TS), 0)
    col = lax.broadcasted_iota(jnp.int32, (TS, TS), 1)
    causal = (row >= col) & ((row // CHUNK) == (col // CHUNK))
    tri = jnp.where(causal, 1.0, 0.0).astype(BF16)
    lf_hi, lf_lo = _split_bf16(lf)
    b = (jnp.dot(tri, lf_hi, preferred_element_type=F32)
         + jnp.dot(tri, lf_lo, preferred_element_type=F32))

    def chunk_rows(r):
        return jnp.concatenate(
            [jnp.broadcast_to(b[c * CHUNK + r:c * CHUNK + r + 1, :], (CHUNK, D_B))
             for c in range(CHUNKS_PER_STEP)], axis=0)

    project(n1_next, 1 - cur, 3)
    bref = chunk_rows(CHUNK // 2 - 1)
    blast = chunk_rows(CHUNK - 1)
    qe = (q * jnp.exp(b - bref)).astype(BF16)
    ke = (k * jnp.exp(bref - b)).astype(BF16)
    kd = (k * jnp.exp(blast - b)).astype(BF16)
    qb = (q * jnp.exp(b)).astype(BF16)

    head_sl = [slice(h * HEAD_DIM, (h + 1) * HEAD_DIM) for h in range(HEADS)]
    chunk_sl = [slice(c * CHUNK, (c + 1) * CHUNK) for c in range(CHUNKS_PER_STEP)]
    scores = [lax.dot_general(qe[:, sl], ke[:, sl], NT_DIMS, preferred_element_type=F32)
              for sl in head_sl]
    upd = [[lax.dot_general(vv[rows, sl], kd[rows, sl], TN_DIMS, preferred_element_type=F32)
            for rows in chunk_sl] for sl in head_sl]
    project(n1_next, 1 - cur, 4)
    scores = [jnp.where(causal, sc, 0.0).astype(BF16) for sc in scores]
    intra = [jnp.dot(scores[h], vv[:, head_sl[h]], preferred_element_type=F32)
             for h in range(HEADS)]
    entering = []
    for h in range(HEADS):
        st = st_ref[h]
        per_chunk = []
        for c in range(CHUNKS_PER_STEP):
            per_chunk.append(st.astype(BF16))
            decay = jnp.exp(b[(c + 1) * CHUNK - 1:(c + 1) * CHUNK, head_sl[h]])
            st = st * decay + upd[h][c]
        st_ref[h] = st
        entering.append(per_chunk)
    inter = [jnp.concatenate(
        [lax.dot_general(qb[chunk_sl[c], head_sl[h]], entering[h][c], NT_DIMS,
                         preferred_element_type=F32) for c in range(CHUNKS_PER_STEP)], axis=0)
        for h in range(HEADS)]
    for h in range(HEADS):
        o = _rms(intra[h] + inter[h], og_ref[:, head_sl[h]])
        g_h = gr[:, head_sl[h]]
        mix_ref[:, D_A + h * HEAD_DIM:D_A + (h + 1) * HEAD_DIM] = (
            o * (g_h * _sigmoid(g_h))).astype(BF16)

    h1 = xkeep_ref[cur] + jnp.dot(mix_ref[...], wout_ref[...], preferred_element_type=F32)
    h1_ref[...] = h1
    n2 = _rms(h1, g2_ref[...])
    _store_packed_rows(n2_ref, _pack_bf16_pairs(n2), TS)

    logits = jnp.dot(n2.astype(BF16), wr_ref[...], preferred_element_type=F32)
    logits = jnp.transpose(logits)[0:ROUTER_ROWS, :] + br_ref[...]
    project(n1_next, 1 - cur, 5)
    logit_ref[cur] = logits

    @pl.when(i == N_TOK // TS - 1)
    def _route_last_tile():
        _route(logits, i, 1.0, carry_ref, cnt_ref, mi_ref, mf_ref)


def _route(logits, tile, live, carry_ref, cnt_ref, mi_ref, mf_ref):
    rid = lax.broadcasted_iota(jnp.int32, (ROUTER_ROWS, TS), 0)
    row = lax.broadcasted_iota(jnp.int32, (TS, TS), 0)
    col = lax.broadcasted_iota(jnp.int32, (TS, TS), 1)
    neg = -jnp.inf
    gl = jnp.where(rid < N_GROUPS, logits, neg)
    gmax = jnp.max(gl, axis=0, keepdims=True)
    g_idx = jnp.min(jnp.where(gl == gmax, rid, ROUTER_ROWS), axis=0, keepdims=True)
    g_prob = 1.0 / jnp.sum(jnp.exp(gl - gmax), axis=0, keepdims=True)
    e_lo = EXPERT_ROW0 + g_idx * EXPERTS_PER_GROUP
    el = jnp.where((rid >= e_lo) & (rid < e_lo + EXPERTS_PER_GROUP), logits, neg)
    m1 = jnp.max(el, axis=0, keepdims=True)
    i1 = jnp.min(jnp.where(el == m1, rid, ROUTER_ROWS), axis=0, keepdims=True)
    el2 = jnp.where(rid == i1, neg, el)
    m2 = jnp.max(el2, axis=0, keepdims=True)
    i2 = jnp.min(jnp.where(el2 == m2, rid, ROUTER_ROWS), axis=0, keepdims=True)
    e21 = jnp.exp(m2 - m1)
    w1 = g_prob / (1.0 + e21)
    w2 = g_prob * e21 / (1.0 + e21)

    hit1 = rid == i1
    hit2 = rid == i2
    onehot = jnp.where(hit1 | hit2, live, 0.0)
    earlier = jnp.where(row < col, 1.0, 0.0).astype(BF16)
    seen = carry_ref[...]
    prior = (jnp.dot(onehot.astype(BF16), earlier, preferred_element_type=F32)
             + jnp.concatenate([seen] * (TS // LANES), axis=1))
    r1 = jnp.sum(jnp.where(hit1, prior, 0.0), axis=0, keepdims=True)
    r2 = jnp.sum(jnp.where(hit2, prior, 0.0), axis=0, keepdims=True)
    seen = seen + jnp.sum(onehot, axis=1, keepdims=True)
    carry_ref[...] = seen
    cnt_ref[...] = seen.astype(jnp.int32)

    cols = pl.ds(pl.multiple_of(tile * TS, TS), TS)
    pad_i = jnp.zeros((META_ROWS - 2 * TOP_K, TS), jnp.int32)
    mi_ref[:, cols] = jnp.concatenate(
        [i1 - EXPERT_ROW0, i2 - EXPERT_ROW0, r1.astype(jnp.int32), r2.astype(jnp.int32), pad_i],
        axis=0)
    mf_ref[:, cols] = jnp.concatenate([w1, w2, jnp.zeros((META_ROWS - TOP_K, TS), F32)], axis=0)


def _mixer(x2, g1, win, vg, ws, bst, lbraw, og, wout, g2, wr, br):
    n_steps = N_TOK // TS
    const2 = lambda i: (0, 0)
    tok = lambda i: (i, 0)
    return pl.pallas_call(
        _mixer_kernel,
        grid=(n_steps,),
        in_specs=[
            pl.BlockSpec(memory_space=pl.ANY),
            pl.BlockSpec((TS, D_MODEL), lambda i: (jnp.minimum(i + 1, n_steps - 1), 0)),
            pl.BlockSpec((1, D_MODEL), const2),
            pl.BlockSpec((D_MODEL, D_IN), const2),
            pl.BlockSpec((1, D_A), const2),
            pl.BlockSpec((HEADS, GMLP_BLOCK, GMLP_BLOCK), lambda i: (0, 0, 0)),
            pl.BlockSpec((TS, HEADS), const2),
            pl.BlockSpec((2, D_B), const2),
            pl.BlockSpec((1, D_B), const2),
            pl.BlockSpec((D_MODEL, D_MODEL), const2),
            pl.BlockSpec((1, D_MODEL), const2),
            pl.BlockSpec((D_MODEL, LANES), const2),
            pl.BlockSpec((ROUTER_ROWS, 1), const2),
        ],
        out_specs=[
            pl.BlockSpec((TS, D_MODEL), tok),
            pl.BlockSpec((TS * PACK_ROWS, LANES), tok),
            pl.BlockSpec((META_ROWS, N_TOK), const2),
            pl.BlockSpec((META_ROWS, N_TOK), const2),
            pl.BlockSpec((ROUTER_ROWS, LANES), const2),
        ],
        out_shape=[
            jax.ShapeDtypeStruct((N_TOK, D_MODEL), F32),
            jax.ShapeDtypeStruct((N_TOK * PACK_ROWS, LANES), jnp.uint32),
            jax.ShapeDtypeStruct((META_ROWS, N_TOK), jnp.int32),
            jax.ShapeDtypeStruct((META_ROWS, N_TOK), F32),
            jax.ShapeDtypeStruct((ROUTER_ROWS, LANES), jnp.int32),
        ],
        scratch_shapes=[
            pltpu.VMEM((HEADS, HEAD_DIM, HEAD_DIM), F32),
            pltpu.VMEM((ROUTER_ROWS, LANES), F32),
            pltpu.VMEM((HEADS, TS, TS), BF16),
            pltpu.VMEM((TS, D_MODEL), BF16),
            pltpu.VMEM((2, TS, D_IN), F32),
            pltpu.VMEM((2, ROUTER_ROWS, TS), F32),
            pltpu.VMEM((2, TS, D_MODEL), F32),
            pltpu.SemaphoreType.DMA,
        ],
        compiler_params=pltpu.CompilerParams(
            dimension_semantics=("arbitrary",),
            vmem_limit_bytes=56 * 1024 * 1024,
        ),
        name="mixer",
    )(x2, x2, g1, win, vg, ws, bst, lbraw, og, wout, g2, wr, br)


FILL_UNROLL = 8

def _invert_kernel(dest_ref, gap_lo_ref, gap_hi_ref, inv_ref):
    for e in range(N_EXPERTS):
        first = gap_lo_ref[e] // FILL_UNROLL

        def fill(g, carry):
            for u in range(FILL_UNROLL):
                inv_ref[g * FILL_UNROLL + u] = -1
            return carry

        lax.fori_loop(first, gap_hi_ref[e] // FILL_UNROLL, fill, 0)

    def place(a, carry):
        inv_ref[dest_ref[a]] = a
        return carry

    lax.fori_loop(0, N_ASSIGN, place, 0, unroll=16)


def _invert(dest, gap_lo, gap_hi):
    smem = pl.BlockSpec(memory_space=pltpu.SMEM)
    return pl.pallas_call(
        _invert_kernel,
        in_specs=[smem, smem, smem],
        out_specs=smem,
        out_shape=jax.ShapeDtypeStruct((INV_LEN,), jnp.int32),
        name="invert",
    )(dest, gap_lo, gap_hi)


def _expert_kernel(bexp_ref, nused_ref, src_hbm, dst_hbm, n2_hbm, wg_ref, wu_ref, wd_ref, y_hbm,
                   src_smem, dst_smem, n2_vmem, xb0, xb1, xb2, ob0, ob1, ob2, wg_bf, wu_bf, wd_bf,
                   src_sem, dst_sem, nsem, ssem):
    i = pl.program_id(0)
    nused = nused_ref[0]
    last = nused - 1
    xbuf = (xb0, xb1, xb2)
    obuf = (ob0, ob1, ob2)

    def src_copy(blk, s):
        return pltpu.make_async_copy(src_hbm.at[blk], src_smem.at[pl.ds(s, 1)], src_sem.at[s])

    def dst_copy(blk, s):
        return pltpu.make_async_copy(dst_hbm.at[blk], dst_smem.at[pl.ds(s, 1)], dst_sem.at[s])

    def gather(s):
        for j in range(MOE_BLOCK):
            r = pl.multiple_of(src_smem[s, j], PACK_ROWS)
            xbuf[s][pl.ds(j * PACK_ROWS, PACK_ROWS), :] = n2_vmem[pl.ds(r, PACK_ROWS), :]

    def start_scatter(s):
        for j in range(MOE_BLOCK):
            r = pl.multiple_of(dst_smem[s, j], PACK_ROWS)
            pltpu.make_async_copy(obuf[s].at[pl.ds(j * PACK_ROWS, PACK_ROWS)],
                                  y_hbm.at[pl.ds(r, PACK_ROWS)], ssem.at[s]
                                  ).start(priority=j % 2)

    def wait_scatter(s):
        pltpu.make_async_copy(obuf[s], y_hbm.at[pl.ds(0, BLOCK_PACK_ROWS)],
                              ssem.at[s]).wait()

    @pl.when(i == 0)
    def _prologue():
        for o in obuf:
            o[...] = jnp.zeros_like(o)
        for s in range(N_SLOTS - 1):
            pltpu.make_async_copy(
                obuf[s],
                y_hbm.at[pl.ds((N_ASSIGN + s * MOE_BLOCK) * PACK_ROWS, BLOCK_PACK_ROWS)],
                ssem.at[s]).start()
        resident = pltpu.make_async_copy(n2_hbm, n2_vmem, nsem)
        resident.start()
        src_copy(0, 0).start()
        src_copy(0, 0).wait()
        resident.wait()
        gather(0)
        for b in (1, 2):
            src_copy(jnp.minimum(b, last), b).start()
        dst_copy(N_BLOCKS, N_SLOTS - 1).start()
        dst_copy(0, 0).start()

    prev = bexp_ref[jnp.maximum(i - 1, 0)]
    changed = (i == 0) | (bexp_ref[i] != prev)

    @pl.when(changed)
    def _cast_weights():
        wg_bf[...] = wg_ref[0].astype(BF16)
        wu_bf[...] = wu_ref[0].astype(BF16)
        wd_bf[...] = wd_ref[0].astype(BF16)

    def active(cur, nxt, prv):
        src_copy(0, nxt).wait()
        dst_copy(0, prv).wait()
        wait_scatter(cur)
        start_scatter(prv)
        gather(nxt)
        src_copy(jnp.minimum(i + 3, last), cur).start()
        dst_copy(jnp.minimum(i + 1, last), nxt).start()
        xb = _unpack_bf16_pairs(_load_packed_rows(xbuf[cur], MOE_BLOCK)).astype(BF16)
        hg = jnp.dot(xb, wg_bf[...], preferred_element_type=F32)
        hu = jnp.dot(xb, wu_bf[...], preferred_element_type=F32)
        hh = (hg * _sigmoid(hg) * hu).astype(BF16)
        y = jnp.dot(hh, wd_bf[...], preferred_element_type=F32)
        _store_packed_rows(obuf[cur], _pack_bf16_pairs(y), MOE_BLOCK)

    def epilogue(cur, nxt, prv):
        dst_copy(0, cur).wait()
        start_scatter(cur)
        wait_scatter(nxt)
        wait_scatter(prv)
        wait_scatter(cur)
        src_copy(0, prv).wait()
        src_copy(0, cur).wait()
        dst_copy(0, nxt).wait()

    for cur in range(N_SLOTS):
        slots = (cur, (cur + 1) % N_SLOTS, (cur + 2) % N_SLOTS)

        @pl.when((i < nused) & (i % N_SLOTS == cur))
        def _():
            active(*slots)

        @pl.when((i == last) & (i % N_SLOTS == cur))
        def _():
            epilogue(*slots)


_EXPERT_W_BYTES = 3 * D_MODEL * D_EXPERT * 4
EXPERT_VMEM_LIMIT = (
    N_TOK * PACK_COLS * 4
    + 2 * _EXPERT_W_BYTES
    + _EXPERT_W_BYTES // 2
    + 2 * N_SLOTS * MOE_BLOCK * PACK_COLS * 4
    + 4 * 1024 * 1024)


def _experts(bexp, nused, src2d, dst2d, n2, w_gate, w_up, w_down):
    def w_blk(i, bexp, nused):
        return (bexp[i], 0, 0)

    any_spec = pl.BlockSpec(memory_space=pl.ANY)
    return pl.pallas_call(
        _expert_kernel,
        grid_spec=pltpu.PrefetchScalarGridSpec(
            num_scalar_prefetch=2,
            grid=(N_BLOCKS,),
            in_specs=[
                any_spec, any_spec, any_spec,
                pl.BlockSpec((1, D_MODEL, D_EXPERT), w_blk),
                pl.BlockSpec((1, D_MODEL, D_EXPERT), w_blk),
                pl.BlockSpec((1, D_EXPERT, D_MODEL), w_blk),
            ],
            out_specs=any_spec,
            scratch_shapes=[
                pltpu.SMEM((N_SLOTS, MOE_BLOCK), jnp.int32),
                pltpu.SMEM((N_SLOTS, MOE_BLOCK), jnp.int32),
                pltpu.VMEM((N_TOK * PACK_ROWS, LANES), jnp.uint32),
            ] + [pltpu.VMEM((BLOCK_PACK_ROWS, LANES), jnp.uint32)] * (2 * N_SLOTS) + [
                pltpu.VMEM((D_MODEL, D_EXPERT), BF16),
                pltpu.VMEM((D_MODEL, D_EXPERT), BF16),
                pltpu.VMEM((D_EXPERT, D_MODEL), BF16),
                pltpu.SemaphoreType.DMA((N_SLOTS,)),
                pltpu.SemaphoreType.DMA((N_SLOTS,)),
                pltpu.SemaphoreType.DMA,
                pltpu.SemaphoreType.DMA((N_SLOTS,)),
            ],
        ),
        out_shape=jax.ShapeDtypeStruct(((N_ASSIGN + DUMMY_ROWS) * PACK_ROWS, LANES), jnp.uint32),
        compiler_params=pltpu.CompilerParams(
            dimension_semantics=("arbitrary",),
            vmem_limit_bytes=EXPERT_VMEM_LIMIT,
        ),
        name="experts",
    )(bexp, nused, src2d, dst2d, n2, w_gate, w_up, w_down)


def _combine_kernel(h1_ref, mf_ref, fg_ref, y0_ref, y1_ref, out_ref):
    w = mf_ref[...]
    y0 = _unpack_bf16_pairs(_load_packed_rows(y0_ref, TD))
    y1 = _unpack_bf16_pairs(_load_packed_rows(y1_ref, TD))
    h = h1_ref[...] + y0 * w[:, 0:1] + y1 * w[:, 1:2]
    out_ref[...] = _rms(h, fg_ref[...])


COMBINE_VMEM_LIMIT = 2 * (2 * (2 * TD * D_MODEL * 4) + 2 * (2 * TD * PACK_COLS * 4))


def _combine(h1, mf, fg, y):
    n_steps = N_TOK // TD
    tok = lambda i: (i, 0)
    return pl.pallas_call(
        _combine_kernel,
        grid=(n_steps,),
        in_specs=[
            pl.BlockSpec((TD, D_MODEL), tok),
            pl.BlockSpec((TD, TOP_K), tok),
            pl.BlockSpec((1, D_MODEL), lambda i: (0, 0)),
            pl.BlockSpec((TD * PACK_ROWS, LANES), tok),
            pl.BlockSpec((TD * PACK_ROWS, LANES), lambda i: (i + n_steps, 0)),
        ],
        out_specs=pl.BlockSpec((TD, D_MODEL), tok),
        out_shape=jax.ShapeDtypeStruct((N_TOK, D_MODEL), F32),
        compiler_params=pltpu.CompilerParams(
            dimension_semantics=("arbitrary",),
            vmem_limit_bytes=COMBINE_VMEM_LIMIT,
        ),
        name="combine",
    )(h1, mf, fg, y, y)


def kernel(x, norm1_gain, w_in, gmlp_v_gain, gmlp_w_s, gmlp_b_s, hgrn_lower_bounds,
           hgrn_out_gain, w_out, norm2_gain, w_group_router, b_group_router,
           w_expert_router, b_expert_router, w_gate, w_up, w_down, final_gain):
    l = 0
    x2 = x.reshape(N_TOK, D_MODEL)
    bst = jnp.tile(jnp.transpose(gmlp_b_s[l]), (TS // GMLP_BLOCK, 1))
    w_router = jnp.concatenate([w_group_router[l], w_expert_router[l]], axis=1)
    w_router = jnp.pad(w_router, ((0, 0), (0, LANES - w_router.shape[1]))).astype(BF16)
    b_router = jnp.concatenate([b_group_router[l], b_expert_router[l]])
    b_router = jnp.pad(b_router, (0, ROUTER_ROWS - b_router.shape[0])).reshape(ROUTER_ROWS, 1)

    h1, n2, meta_i, meta_f, counts = _mixer(
        x2, norm1_gain[l].reshape(1, D_MODEL), w_in[l].astype(BF16),
        gmlp_v_gain[l].reshape(1, D_A), gmlp_w_s[l], bst,
        hgrn_lower_bounds, hgrn_out_gain[l].reshape(1, D_B), w_out[l].astype(BF16),
        norm2_gain[l].reshape(1, D_MODEL), w_router, b_router)

    cnt = counts[EXPERT_ROW0:EXPERT_ROW0 + N_EXPERTS, 0]
    padded = ((cnt + MOE_BLOCK - 1) // MOE_BLOCK) * MOE_BLOCK
    pend = jnp.cumsum(padded)
    pstart = pend - padded
    eid = meta_i[0:TOP_K]
    rank = meta_i[TOP_K:2 * TOP_K]
    base = jnp.sum(jnp.where(eid[:, :, None] == jnp.arange(N_EXPERTS)[None, None, :],
                             pstart[None, None, :], 0), axis=-1)
    dest = (base + rank).astype(jnp.int32).reshape(N_ASSIGN)
    gap_lo = (pstart + cnt).astype(jnp.int32)
    gap_hi = jnp.concatenate([pstart[1:], jnp.full((1,), INV_LEN)]).astype(jnp.int32)
    inv = _invert(dest, gap_lo, gap_hi)
    pos = jnp.arange(INV_LEN, dtype=jnp.int32)
    blk = jnp.where(pos < SORTED_ROWS, pos // MOE_BLOCK, N_SLOTS - 1)
    dummy = N_ASSIGN + (blk % N_SLOTS) * MOE_BLOCK + pos % MOE_BLOCK
    src2d = (jnp.where(inv >= 0, inv % N_TOK, 0) * PACK_ROWS)[:SORTED_ROWS].reshape(
        N_BLOCKS, 1, MOE_BLOCK)
    dst2d = (jnp.where(inv >= 0, inv, dummy) * PACK_ROWS).reshape(N_BLOCKS + 1, 1, MOE_BLOCK)
    blk_start = jnp.arange(N_BLOCKS, dtype=jnp.int32) * MOE_BLOCK
    bexp = jnp.clip(jnp.sum(blk_start[:, None] >= pend[None, :], axis=1), 0,
                    N_EXPERTS - 1).astype(jnp.int32)
    nused = (pend[-1:] // MOE_BLOCK).astype(jnp.int32)

    y = _experts(bexp, nused, src2d, dst2d, n2, w_gate[l], w_up[l], w_down[l])
    out = _combine(h1, meta_f[0:TOP_K].T, final_gain.reshape(1, D_MODEL), y)
    return out.reshape(BATCH, SEQ, D_MODEL)
```

```python
import functools

import jax
import jax.numpy as jnp
from jax import lax
from jax.experimental import pallas as pl
from jax.experimental.pallas import tpu as pltpu

D_MODEL = 1024
BATCH = 2
SEQ = 8192
N_TOK = BATCH * SEQ
CHUNK = 64
EPS = 1e-6
D_A = 512
HEADS = 4
HEAD_DIM = 128
GMLP_BLOCK = 128
D_B = 512
D_IN = 3072
N_GROUPS = 4
EXPERTS_PER_GROUP = 8
N_EXPERTS = 32
TOP_K = 2
D_EXPERT = 512

LANES = 128
ROUTER_ROWS = 48
EXPERT_ROW0 = N_GROUPS
META_ROWS = 8
PACK_COLS = D_MODEL // 2
PACK_ROWS = PACK_COLS // LANES
HI_MASK = 0xFFFF0000

TS = 256
CHUNKS_PER_STEP = TS // CHUNK
MOE_BLOCK = 256
N_BLOCKS = N_TOK * TOP_K // MOE_BLOCK + N_EXPERTS
SORTED_ROWS = N_BLOCKS * MOE_BLOCK
N_ASSIGN = N_TOK * TOP_K
N_SLOTS = 3
DUMMY_ROWS = N_SLOTS * MOE_BLOCK
BLOCK_PACK_ROWS = MOE_BLOCK * PACK_ROWS
INV_LEN = SORTED_ROWS + MOE_BLOCK
TD = 1024

F32 = jnp.float32
BF16 = jnp.bfloat16
NT_DIMS = (((1,), (1,)), ((), ()))
TN_DIMS = (((0,), (0,)), ((), ()))


def _sigmoid(x):
    return 1.0 / (1.0 + jnp.exp(-x))


def _gelu(x):
    return 0.5 * x * (1.0 + lax.erf(x * (2.0 ** -0.5)))


def _rms(x, gain):
    return x * lax.rsqrt(jnp.mean(x * x, axis=-1, keepdims=True) + EPS) * gain


def _pack_bf16_pairs(x):
    lo = pltpu.bitcast(x[:, 0:PACK_COLS].astype(BF16).astype(F32), jnp.uint32) >> 16
    hi = (pltpu.bitcast(x[:, PACK_COLS:D_MODEL].astype(BF16).astype(F32), jnp.uint32)
          & jnp.uint32(HI_MASK))
    return lo | hi


def _unpack_bf16_pairs(words):
    lo = pltpu.bitcast(words << 16, F32)
    hi = pltpu.bitcast(words & jnp.uint32(HI_MASK), F32)
    return jnp.concatenate([lo, hi], axis=1)


def _store_packed_rows(ref, words, rows):
    for c in range(PACK_ROWS):
        ref[pl.ds(c, rows, stride=PACK_ROWS), :] = words[:, c * LANES:(c + 1) * LANES]


def _load_packed_rows(ref, rows):
    return jnp.concatenate(
        [ref[pl.ds(c, rows, stride=PACK_ROWS), :] for c in range(PACK_ROWS)], axis=1)


def _split_bf16(x):
    hi = x.astype(BF16)
    lo = (x - hi.astype(F32)).astype(BF16)
    return hi, lo


def _mixer_kernel(*refs):
    for parity in range(2):
        pl.when(pl.program_id(0) % 2 == parity)(functools.partial(_mixer_step, parity, *refs))


def _mixer_step(cur, x_hbm, xn_ref, g1_ref, win_ref, vg_ref, ws_ref, bst_ref, lbraw_ref, og_ref,
                wout_ref, g2_ref, wr_ref, br_ref,
                h1_ref, n2_ref, mi_ref, mf_ref, cnt_ref,
                st_ref, carry_ref, wm_ref, mix_ref, proj_ref, logit_ref, xkeep_ref, xsem):
    i = pl.program_id(0)

    @pl.when(i == 0)
    def _init():
        carry_ref[...] = jnp.zeros_like(carry_ref)
        logit_ref[...] = jnp.zeros_like(logit_ref)
        r = lax.broadcasted_iota(jnp.int32, (GMLP_BLOCK, GMLP_BLOCK), 0)
        c = lax.broadcasted_iota(jnp.int32, (GMLP_BLOCK, GMLP_BLOCK), 1)
        keep = (c // CHUNK) <= (r // CHUNK)
        wm_ref[...] = jnp.zeros_like(wm_ref)
        for g in range(HEADS):
            w = jnp.where(keep, ws_ref[g], 0.0).astype(BF16)
            for p in range(TS // GMLP_BLOCK):
                wm_ref[g, p * GMLP_BLOCK:(p + 1) * GMLP_BLOCK,
                       p * GMLP_BLOCK:(p + 1) * GMLP_BLOCK] = w

    @pl.when(i % (SEQ // TS) == 0)
    def _reset_state():
        st_ref[...] = jnp.zeros_like(st_ref)

    def project(n1, slot, c):
        proj_ref[slot, :, c * D_A:(c + 1) * D_A] = jnp.dot(
            n1, win_ref[:, c * D_A:(c + 1) * D_A], preferred_element_type=F32)

    @pl.when(i == 0)
    def _first_projection():
        first = pltpu.make_async_copy(x_hbm.at[pl.ds(0, TS)], xkeep_ref.at[0], xsem)
        first.start()
        first.wait()
        n1 = _rms(xkeep_ref[0], g1_ref[...]).astype(BF16)
        for c in range(D_IN // D_A):
            project(n1, 0, c)

    x_next = xn_ref[...]
    n1_next = _rms(x_next, g1_ref[...]).astype(BF16)
    xkeep_ref[1 - cur] = x_next

    project(n1_next, 1 - cur, 0)
    gu = _gelu(proj_ref[cur, :, 0:D_A])
    project(n1_next, 1 - cur, 1)
    gv = _gelu(proj_ref[cur, :, D_A:2 * D_A])
    for g in range(HEADS):
        sl = slice(g * HEAD_DIM, (g + 1) * HEAD_DIM)
        vh = _rms(gv[:, sl], vg_ref[:, sl]).astype(BF16)
        mixed = jnp.dot(wm_ref[g], vh, preferred_element_type=F32) + bst_ref[:, g:g + 1]
        mix_ref[:, sl] = (gu[:, sl] * mixed).astype(BF16)

    _route(logit_ref[1 - cur], jnp.maximum(i - 1, 0), jnp.where(i > 0, 1.0, 0.0),
           carry_ref, cnt_ref, mi_ref, mf_ref)

    o2 = 2 * D_A
    qr = proj_ref[cur, :, o2:o2 + D_B]
    fr = proj_ref[cur, :, o2 + D_B:o2 + 2 * D_B]
    vv = proj_ref[cur, :, o2 + 2 * D_B:o2 + 3 * D_B].astype(BF16)
    gr = proj_ref[cur, :, o2 + 3 * D_B:o2 + 4 * D_B]

    lbr = lbraw_ref[...]
    lbm = jnp.max(lbr, axis=0, keepdims=True)
    lbe = jnp.exp(lbr - lbm)
    lb = lbe[0:1, :] / jnp.sum(lbe, axis=0, keepdims=True)

    project(n1_next, 1 - cur, 2)
    q = qr * _sigmoid(qr)
    f = lb + (1.0 - lb) * _sigmoid(fr)
    k = 1.0 - f
    lf = jnp.log(f)

    row = lax.broadcasted_iota(jnp.int32, (TS, TS), 0)
    col = lax.broadcasted_iota(jnp.int32, (TS, TS), 1)
    causal = (row >= col) & ((row // CHUNK) == (col // CHUNK))
    tri = jnp.where(causal, 1.0, 0.0).astype(BF16)
    lf_hi, lf_lo = _split_bf16(lf)
    b = (jnp.dot(tri, lf_hi, preferred_element_type=F32)
         + jnp.dot(tri, lf_lo, preferred_element_type=F32))

    def chunk_rows(r):
        return jnp.concatenate(
            [jnp.broadcast_to(b[c * CHUNK + r:c * CHUNK + r + 1, :], (CHUNK, D_B))
             for c in range(CHUNKS_PER_STEP)], axis=0)

    project(n1_next, 1 - cur, 3)
    bref = chunk_rows(CHUNK // 2 - 1)
    blast = chunk_rows(CHUNK - 1)
    qe = (q * jnp.exp(b - bref)).astype(BF16)
    ke = (k * jnp.exp(bref - b)).astype(BF16)
    kd = (k * jnp.exp(blast - b)).astype(BF16)
    qb = (q * jnp.exp(b)).astype(BF16)

    head_sl = [slice(h * HEAD_DIM, (h + 1) * HEAD_DIM) for h in range(HEADS)]
    chunk_sl = [slice(c * CHUNK, (c + 1) * CHUNK) for c in range(CHUNKS_PER_STEP)]
    scores = [lax.dot_general(qe[:, sl], ke[:, sl], NT_DIMS, preferred_element_type=F32)
              for sl in head_sl]
    upd = [[lax.dot_general(vv[rows, sl], kd[rows, sl], TN_DIMS, preferred_element_type=F32)
            for rows in chunk_sl] for sl in head_sl]
    project(n1_next, 1 - cur, 4)
    scores = [jnp.where(causal, sc, 0.0).astype(BF16) for sc in scores]
    intra = [jnp.dot(scores[h], vv[:, head_sl[h]], preferred_element_type=F32)
             for h in range(HEADS)]
    entering = []
    for h in range(HEADS):
        st = st_ref[h]
        per_chunk = []
        for c in range(CHUNKS_PER_STEP):
            per_chunk.append(st.astype(BF16))
            decay = jnp.exp(b[(c + 1) * CHUNK - 1:(c + 1) * CHUNK, head_sl[h]])
            st = st * decay + upd[h][c]
        st_ref[h] = st
        entering.append(per_chunk)
    inter = [jnp.concatenate(
        [lax.dot_general(qb[chunk_sl[c], head_sl[h]], entering[h][c], NT_DIMS,
                         preferred_element_type=F32) for c in range(CHUNKS_PER_STEP)], axis=0)
        for h in range(HEADS)]
    for h in range(HEADS):
        o = _rms(intra[h] + inter[h], og_ref[:, head_sl[h]])
        g_h = gr[:, head_sl[h]]
        mix_ref[:, D_A + h * HEAD_DIM:D_A + (h + 1) * HEAD_DIM] = (
            o * (g_h * _sigmoid(g_h))).astype(BF16)

    h1 = xkeep_ref[cur] + jnp.dot(mix_ref[...], wout_ref[...], preferred_element_type=F32)
    h1_ref[...] = h1
    n2 = _rms(h1, g2_ref[...])
    _store_packed_rows(n2_ref, _pack_bf16_pairs(n2), TS)

    logits = jnp.dot(n2.astype(BF16), wr_ref[...], preferred_element_type=F32)
    logits = jnp.transpose(logits)[0:ROUTER_ROWS, :] + br_ref[...]
    project(n1_next, 1 - cur, 5)
    logit_ref[cur] = logits

    @pl.when(i == N_TOK // TS - 1)
    def _route_last_tile():
        _route(logits, i, 1.0, carry_ref, cnt_ref, mi_ref, mf_ref)


def _route(logits, tile, live, carry_ref, cnt_ref, mi_ref, mf_ref):
    rid = lax.broadcasted_iota(jnp.int32, (ROUTER_ROWS, TS), 0)
    row = lax.broadcasted_iota(jnp.int32, (TS, TS), 0)
    col = lax.broadcasted_iota(jnp.int32, (TS, TS), 1)
    neg = -jnp.inf
    gl = jnp.where(rid < N_GROUPS, logits, neg)
    gmax = jnp.max(gl, axis=0, keepdims=True)
    g_idx = jnp.min(jnp.where(gl == gmax, rid, ROUTER_ROWS), axis=0, keepdims=True)
    g_prob = 1.0 / jnp.sum(jnp.exp(gl - gmax), axis=0, keepdims=True)
    e_lo = EXPERT_ROW0 + g_idx * EXPERTS_PER_GROUP
    el = jnp.where((rid >= e_lo) & (rid < e_lo + EXPERTS_PER_GROUP), logits, neg)
    m1 = jnp.max(el, axis=0, keepdims=True)
    i1 = jnp.min(jnp.where(el == m1, rid, ROUTER_ROWS), axis=0, keepdims=True)
    el2 = jnp.where(rid == i1, neg, el)
    m2 = jnp.max(el2, axis=0, keepdims=True)
    i2 = jnp.min(jnp.where(el2 == m2, rid, ROUTER_ROWS), axis=0, keepdims=True)
    e21 = jnp.exp(m2 - m1)
    w1 = g_prob / (1.0 + e21)
    w2 = g_prob * e21 / (1.0 + e21)

    hit1 = rid == i1
    hit2 = rid == i2
    onehot = jnp.where(hit1 | hit2, live, 0.0)
    earlier = jnp.where(row < col, 1.0, 0.0).astype(BF16)
    seen = carry_ref[...]
    prior = (jnp.dot(onehot.astype(BF16), earlier, preferred_element_type=F32)
             + jnp.concatenate([seen] * (TS // LANES), axis=1))
    r1 = jnp.sum(jnp.where(hit1, prior, 0.0), axis=0, keepdims=True)
    r2 = jnp.sum(jnp.where(hit2, prior, 0.0), axis=0, keepdims=True)
    seen = seen + jnp.sum(onehot, axis=1, keepdims=True)
    carry_ref[...] = seen
    cnt_ref[...] = seen.astype(jnp.int32)

    cols = pl.ds(pl.multiple_of(tile * TS, TS), TS)
    pad_i = jnp.zeros((META_ROWS - 2 * TOP_K, TS), jnp.int32)
    mi_ref[:, cols] = jnp.concatenate(
        [i1 - EXPERT_ROW0, i2 - EXPERT_ROW0, r1.astype(jnp.int32), r2.astype(jnp.int32), pad_i],
        axis=0)
    mf_ref[:, cols] = jnp.concatenate([w1, w2, jnp.zeros((META_ROWS - TOP_K, TS), F32)], axis=0)


def _mixer(x2, g1, win, vg, ws, bst, lbraw, og, wout, g2, wr, br):
    n_steps = N_TOK // TS
    const2 = lambda i: (0, 0)
    tok = lambda i: (i, 0)
    return pl.pallas_call(
        _mixer_kernel,
        grid=(n_steps,),
        in_specs=[
            pl.BlockSpec(memory_space=pl.ANY),
            pl.BlockSpec((TS, D_MODEL), lambda i: (jnp.minimum(i + 1, n_steps - 1), 0)),
            pl.BlockSpec((1, D_MODEL), const2),
            pl.BlockSpec((D_MODEL, D_IN), const2),
            pl.BlockSpec((1, D_A), const2),
            pl.BlockSpec((HEADS, GMLP_BLOCK, GMLP_BLOCK), lambda i: (0, 0, 0)),
            pl.BlockSpec((TS, HEADS), const2),
            pl.BlockSpec((2, D_B), const2),
            pl.BlockSpec((1, D_B), const2),
            pl.BlockSpec((D_MODEL, D_MODEL), const2),
            pl.BlockSpec((1, D_MODEL), const2),
            pl.BlockSpec((D_MODEL, LANES), const2),
            pl.BlockSpec((ROUTER_ROWS, 1), const2),
        ],
        out_specs=[
            pl.BlockSpec((TS, D_MODEL), tok),
            pl.BlockSpec((TS * PACK_ROWS, LANES), tok),
            pl.BlockSpec((META_ROWS, N_TOK), const2),
            pl.BlockSpec((META_ROWS, N_TOK), const2),
            pl.BlockSpec((ROUTER_ROWS, LANES), const2),
        ],
        out_shape=[
            jax.ShapeDtypeStruct((N_TOK, D_MODEL), F32),
            jax.ShapeDtypeStruct((N_TOK * PACK_ROWS, LANES), jnp.uint32),
            jax.ShapeDtypeStruct((META_ROWS, N_TOK), jnp.int32),
            jax.ShapeDtypeStruct((META_ROWS, N_TOK), F32),
            jax.ShapeDtypeStruct((ROUTER_ROWS, LANES), jnp.int32),
        ],
        scratch_shapes=[
            pltpu.VMEM((HEADS, HEAD_DIM, HEAD_DIM), F32),
            pltpu.VMEM((ROUTER_ROWS, LANES), F32),
            pltpu.VMEM((HEADS, TS, TS), BF16),
            pltpu.VMEM((TS, D_MODEL), BF16),
            pltpu.VMEM((2, TS, D_IN), F32),
            pltpu.VMEM((2, ROUTER_ROWS, TS), F32),
            pltpu.VMEM((2, TS, D_MODEL), F32),
            pltpu.SemaphoreType.DMA,
        ],
        compiler_params=pltpu.CompilerParams(
            dimension_semantics=("arbitrary",),
            vmem_limit_bytes=56 * 1024 * 1024,
        ),
        name="mixer",
    )(x2, x2, g1, win, vg, ws, bst, lbraw, og, wout, g2, wr, br)


FILL_UNROLL = 8

def _invert_kernel(dest_ref, gap_lo_ref, gap_hi_ref, inv_ref):
    for e in range(N_EXPERTS):
        first = gap_lo_ref[e] // FILL_UNROLL

        def fill(g, carry):
            for u in range(FILL_UNROLL):
                inv_ref[g * FILL_UNROLL + u] = -1
            return carry

        lax.fori_loop(first, gap_hi_ref[e] // FILL_UNROLL, fill, 0)

    def place(a, carry):
        inv_ref[dest_ref[a]] = a
        return carry

    lax.fori_loop(0, N_ASSIGN, place, 0, unroll=16)


def _invert(dest, gap_lo, gap_hi):
    smem = pl.BlockSpec(memory_space=pltpu.SMEM)
    return pl.pallas_call(
        _invert_kernel,
        in_specs=[smem, smem, smem],
        out_specs=smem,
        out_shape=jax.ShapeDtypeStruct((INV_LEN,), jnp.int32),
        name="invert",
    )(dest, gap_lo, gap_hi)


def _expert_kernel(bexp_ref, nused_ref, src_ref, dst_ref, n2_hbm, wg_ref, wu_ref, wd_ref, y_hbm,
                   n2_vmem, xb0, xb1, xb2, ob0, ob1, ob2, wg_bf, wu_bf, wd_bf, nsem, ssem):
    i = pl.program_id(0)
    nused = nused_ref[0]
    last = nused - 1
    xbuf = (xb0, xb1, xb2)
    obuf = (ob0, ob1, ob2)

    def gather(s, blk):
        for j in range(MOE_BLOCK):
            r = pl.multiple_of(src_ref[blk, j], PACK_ROWS)
            xbuf[s][pl.ds(j * PACK_ROWS, PACK_ROWS), :] = n2_vmem[pl.ds(r, PACK_ROWS), :]

    def start_scatter(s, blk):
        for j in range(MOE_BLOCK):
            r = pl.multiple_of(dst_ref[blk, j], PACK_ROWS)
            pltpu.make_async_copy(obuf[s].at[pl.ds(j * PACK_ROWS, PACK_ROWS)],
                                  y_hbm.at[pl.ds(r, PACK_ROWS)], ssem.at[s]
                                  ).start(priority=j % 2)

    def wait_scatter(s):
        pltpu.make_async_copy(obuf[s], y_hbm.at[pl.ds(0, BLOCK_PACK_ROWS)],
                              ssem.at[s]).wait()

    @pl.when(i == 0)
    def _prologue():
        for o in obuf:
            o[...] = jnp.zeros_like(o)
        for s in range(N_SLOTS - 1):
            pltpu.make_async_copy(
                obuf[s],
                y_hbm.at[pl.ds((N_ASSIGN + s * MOE_BLOCK) * PACK_ROWS, BLOCK_PACK_ROWS)],
                ssem.at[s]).start()
        resident = pltpu.make_async_copy(n2_hbm, n2_vmem, nsem)
        resident.start()
        resident.wait()
        gather(0, 0)

    prev = bexp_ref[jnp.maximum(i - 1, 0)]
    changed = (i == 0) | (bexp_ref[i] != prev)

    @pl.when(changed)
    def _cast_weights():
        wg_bf[...] = wg_ref[0].astype(BF16)
        wu_bf[...] = wu_ref[0].astype(BF16)
        wd_bf[...] = wd_ref[0].astype(BF16)

    def active(cur, nxt, prv):
        wait_scatter(cur)
        start_scatter(prv, jnp.where(i == 0, N_BLOCKS, i - 1))
        gather(nxt, jnp.minimum(i + 1, last))
        xb = _unpack_bf16_pairs(_load_packed_rows(xbuf[cur], MOE_BLOCK)).astype(BF16)
        hg = jnp.dot(xb, wg_bf[...], preferred_element_type=F32)
        hu = jnp.dot(xb, wu_bf[...], preferred_element_type=F32)
        hh = (hg * _sigmoid(hg) * hu).astype(BF16)
        y = jnp.dot(hh, wd_bf[...], preferred_element_type=F32)
        _store_packed_rows(obuf[cur], _pack_bf16_pairs(y), MOE_BLOCK)

    def epilogue(cur, nxt, prv):
        start_scatter(cur, i)
        wait_scatter(nxt)
        wait_scatter(prv)
        wait_scatter(cur)

    for cur in range(N_SLOTS):
        slots = (cur, (cur + 1) % N_SLOTS, (cur + 2) % N_SLOTS)

        @pl.when((i < nused) & (i % N_SLOTS == cur))
        def _():
            active(*slots)

        @pl.when((i == last) & (i % N_SLOTS == cur))
        def _():
            epilogue(*slots)


_EXPERT_W_BYTES = 3 * D_MODEL * D_EXPERT * 4
EXPERT_VMEM_LIMIT = (
    N_TOK * PACK_COLS * 4
    + 2 * _EXPERT_W_BYTES
    + _EXPERT_W_BYTES // 2
    + 2 * N_SLOTS * MOE_BLOCK * PACK_COLS * 4
    + 4 * 1024 * 1024)


def _experts(bexp, nused, src2d, dst2d, n2, w_gate, w_up, w_down):
    def w_blk(i, bexp, nused, src2d, dst2d):
        return (bexp[i], 0, 0)

    any_spec = pl.BlockSpec(memory_space=pl.ANY)
    return pl.pallas_call(
        _expert_kernel,
        grid_spec=pltpu.PrefetchScalarGridSpec(
            num_scalar_prefetch=4,
            grid=(N_BLOCKS,),
            in_specs=[
                any_spec,
                pl.BlockSpec((1, D_MODEL, D_EXPERT), w_blk),
                pl.BlockSpec((1, D_MODEL, D_EXPERT), w_blk),
                pl.BlockSpec((1, D_EXPERT, D_MODEL), w_blk),
            ],
            out_specs=any_spec,
            scratch_shapes=[
                pltpu.VMEM((N_TOK * PACK_ROWS, LANES), jnp.uint32),
            ] + [pltpu.VMEM((BLOCK_PACK_ROWS, LANES), jnp.uint32)] * (2 * N_SLOTS) + [
                pltpu.VMEM((D_MODEL, D_EXPERT), BF16),
                pltpu.VMEM((D_MODEL, D_EXPERT), BF16),
                pltpu.VMEM((D_EXPERT, D_MODEL), BF16),
                pltpu.SemaphoreType.DMA,
                pltpu.SemaphoreType.DMA((N_SLOTS,)),
            ],
        ),
        out_shape=jax.ShapeDtypeStruct(((N_ASSIGN + DUMMY_ROWS) * PACK_ROWS, LANES), jnp.uint32),
        compiler_params=pltpu.CompilerParams(
            dimension_semantics=("arbitrary",),
            vmem_limit_bytes=EXPERT_VMEM_LIMIT,
        ),
        name="experts",
    )(bexp, nused, src2d, dst2d, n2, w_gate, w_up, w_down)


def _combine_kernel(h1_ref, mf_ref, fg_ref, y0_ref, y1_ref, out_ref):
    w = mf_ref[...]
    y0 = _unpack_bf16_pairs(_load_packed_rows(y0_ref, TD))
    y1 = _unpack_bf16_pairs(_load_packed_rows(y1_ref, TD))
    h = h1_ref[...] + y0 * w[:, 0:1] + y1 * w[:, 1:2]
    out_ref[...] = _rms(h, fg_ref[...])


COMBINE_VMEM_LIMIT = 2 * (2 * (2 * TD * D_MODEL * 4) + 2 * (2 * TD * PACK_COLS * 4))


def _combine(h1, mf, fg, y):
    n_steps = N_TOK // TD
    tok = lambda i: (i, 0)
    return pl.pallas_call(
        _combine_kernel,
        grid=(n_steps,),
        in_specs=[
            pl.BlockSpec((TD, D_MODEL), tok),
            pl.BlockSpec((TD, TOP_K), tok),
            pl.BlockSpec((1, D_MODEL), lambda i: (0, 0)),
            pl.BlockSpec((TD * PACK_ROWS, LANES), tok),
            pl.BlockSpec((TD * PACK_ROWS, LANES), lambda i: (i + n_steps, 0)),
        ],
        out_specs=pl.BlockSpec((TD, D_MODEL), tok),
        out_shape=jax.ShapeDtypeStruct((N_TOK, D_MODEL), F32),
        compiler_params=pltpu.CompilerParams(
            dimension_semantics=("arbitrary",),
            vmem_limit_bytes=COMBINE_VMEM_LIMIT,
        ),
        name="combine",
    )(h1, mf, fg, y, y)


def kernel(x, norm1_gain, w_in, gmlp_v_gain, gmlp_w_s, gmlp_b_s, hgrn_lower_bounds,
           hgrn_out_gain, w_out, norm2_gain, w_group_router, b_group_router,
           w_expert_router, b_expert_router, w_gate, w_up, w_down, final_gain):
    l = 0
    x2 = x.reshape(N_TOK, D_MODEL)
    bst = jnp.tile(jnp.transpose(gmlp_b_s[l]), (TS // GMLP_BLOCK, 1))
    w_router = jnp.concatenate([w_group_router[l], w_expert_router[l]], axis=1)
    w_router = jnp.pad(w_router, ((0, 0), (0, LANES - w_router.shape[1]))).astype(BF16)
    b_router = jnp.concatenate([b_group_router[l], b_expert_router[l]])
    b_router = jnp.pad(b_router, (0, ROUTER_ROWS - b_router.shape[0])).reshape(ROUTER_ROWS, 1)

    h1, n2, meta_i, meta_f, counts = _mixer(
        x2, norm1_gain[l].reshape(1, D_MODEL), w_in[l].astype(BF16),
        gmlp_v_gain[l].reshape(1, D_A), gmlp_w_s[l], bst,
        hgrn_lower_bounds, hgrn_out_gain[l].reshape(1, D_B), w_out[l].astype(BF16),
        norm2_gain[l].reshape(1, D_MODEL), w_router, b_router)

    cnt = counts[EXPERT_ROW0:EXPERT_ROW0 + N_EXPERTS, 0]
    padded = ((cnt + MOE_BLOCK - 1) // MOE_BLOCK) * MOE_BLOCK
    pend = jnp.cumsum(padded)
    pstart = pend - padded
    eid = meta_i[0:TOP_K]
    rank = meta_i[TOP_K:2 * TOP_K]
    base = jnp.sum(jnp.where(eid[:, :, None] == jnp.arange(N_EXPERTS)[None, None, :],
                             pstart[None, None, :], 0), axis=-1)
    dest = (base + rank).astype(jnp.int32).reshape(N_ASSIGN)
    gap_lo = (pstart + cnt).astype(jnp.int32)
    gap_hi = jnp.concatenate([pstart[1:], jnp.full((1,), INV_LEN)]).astype(jnp.int32)
    inv = _invert(dest, gap_lo, gap_hi)
    pos = jnp.arange(INV_LEN, dtype=jnp.int32)
    blk = jnp.where(pos < SORTED_ROWS, pos // MOE_BLOCK, N_SLOTS - 1)
    dummy = N_ASSIGN + (blk % N_SLOTS) * MOE_BLOCK + pos % MOE_BLOCK
    src2d = (jnp.where(inv >= 0, inv % N_TOK, 0) * PACK_ROWS)[:SORTED_ROWS].reshape(
        N_BLOCKS, MOE_BLOCK)
    dst2d = (jnp.where(inv >= 0, inv, dummy) * PACK_ROWS).reshape(N_BLOCKS + 1, MOE_BLOCK)
    blk_start = jnp.arange(N_BLOCKS, dtype=jnp.int32) * MOE_BLOCK
    bexp = jnp.clip(jnp.sum(blk_start[:, None] >= pend[None, :], axis=1), 0,
                    N_EXPERTS - 1).astype(jnp.int32)
    nused = (pend[-1:] // MOE_BLOCK).astype(jnp.int32)

    y = _experts(bexp, nused, src2d, dst2d, n2, w_gate[l], w_up[l], w_down[l])
    out = _combine(h1, meta_f[0:TOP_K].T, final_gain.reshape(1, D_MODEL), y)
    return out.reshape(BATCH, SEQ, D_MODEL)
```

```python
import functools

import jax
import jax.numpy as jnp
from jax import lax
from jax.experimental import pallas as pl
from jax.experimental.pallas import tpu as pltpu

D_MODEL = 1024
BATCH = 2
SEQ = 8192
N_TOK = BATCH * SEQ
CHUNK = 64
EPS = 1e-6
D_A = 512
HEADS = 4
HEAD_DIM = 128
GMLP_BLOCK = 128
D_B = 512
D_IN = 3072
N_GROUPS = 4
EXPERTS_PER_GROUP = 8
N_EXPERTS = 32
TOP_K = 2
D_EXPERT = 512

LANES = 128
ROUTER_ROWS = 48
EXPERT_ROW0 = N_GROUPS
META_ROWS = 8
PACK_COLS = D_MODEL // 2
PACK_ROWS = PACK_COLS // LANES
HI_MASK = 0xFFFF0000

TS = 256
CHUNKS_PER_STEP = TS // CHUNK
MOE_BLOCK = 256
N_BLOCKS = N_TOK * TOP_K // MOE_BLOCK + N_EXPERTS
SORTED_ROWS = N_BLOCKS * MOE_BLOCK
N_ASSIGN = N_TOK * TOP_K
N_SLOTS = 3
DUMMY_ROWS = N_SLOTS * MOE_BLOCK
BLOCK_PACK_ROWS = MOE_BLOCK * PACK_ROWS
INV_LEN = SORTED_ROWS + MOE_BLOCK
TD = 1024

F32 = jnp.float32
BF16 = jnp.bfloat16
NT_DIMS = (((1,), (1,)), ((), ()))
TN_DIMS = (((0,), (0,)), ((), ()))


def _sigmoid(x):
    return 1.0 / (1.0 + jnp.exp(-x))


def _gelu(x):
    return 0.5 * x * (1.0 + lax.erf(x * (2.0 ** -0.5)))


def _rms(x, gain):
    return x * lax.rsqrt(jnp.mean(x * x, axis=-1, keepdims=True) + EPS) * gain


def _pack_bf16_pairs(x):
    lo = pltpu.bitcast(x[:, 0:PACK_COLS].astype(BF16).astype(F32), jnp.uint32) >> 16
    hi = (pltpu.bitcast(x[:, PACK_COLS:D_MODEL].astype(BF16).astype(F32), jnp.uint32)
          & jnp.uint32(HI_MASK))
    return lo | hi


def _unpack_bf16_pairs(words):
    lo = pltpu.bitcast(words << 16, F32)
    hi = pltpu.bitcast(words & jnp.uint32(HI_MASK), F32)
    return jnp.concatenate([lo, hi], axis=1)


def _store_packed_rows(ref, words, rows):
    for c in range(PACK_ROWS):
        ref[pl.ds(c, rows, stride=PACK_ROWS), :] = words[:, c * LANES:(c + 1) * LANES]


def _load_packed_rows(ref, rows):
    return jnp.concatenate(
        [ref[pl.ds(c, rows, stride=PACK_ROWS), :] for c in range(PACK_ROWS)], axis=1)


def _split_bf16(x):
    hi = x.astype(BF16)
    lo = (x - hi.astype(F32)).astype(BF16)
    return hi, lo


def _mixer_kernel(*refs):
    for parity in range(2):
        pl.when(pl.program_id(0) % 2 == parity)(functools.partial(_mixer_step, parity, *refs))


def _mixer_step(cur, x_hbm, xn_ref, g1_ref, win_ref, vg_ref, ws_ref, bst_ref, lbraw_ref, og_ref,
                wout_ref, g2_ref, wr_ref, br_ref,
                h1_ref, n2_ref, mi_ref, mf_ref, cnt_ref,
                st_ref, carry_ref, wm_ref, mix_ref, proj_ref, logit_ref, xkeep_ref, xsem):
    i = pl.program_id(0)

    @pl.when(i == 0)
    def _init():
        carry_ref[...] = jnp.zeros_like(carry_ref)
        logit_ref[...] = jnp.zeros_like(logit_ref)
        r = lax.broadcasted_iota(jnp.int32, (GMLP_BLOCK, GMLP_BLOCK), 0)
        c = lax.broadcasted_iota(jnp.int32, (GMLP_BLOCK, GMLP_BLOCK), 1)
        keep = (c // CHUNK) <= (r // CHUNK)
        wm_ref[...] = jnp.zeros_like(wm_ref)
        for g in range(HEADS):
            w = jnp.where(keep, ws_ref[g], 0.0).astype(BF16)
            for p in range(TS // GMLP_BLOCK):
                wm_ref[g, p * GMLP_BLOCK:(p + 1) * GMLP_BLOCK,
                       p * GMLP_BLOCK:(p + 1) * GMLP_BLOCK] = w

    @pl.when(i % (SEQ // TS) == 0)
    def _reset_state():
        st_ref[...] = jnp.zeros_like(st_ref)

    def project(n1, slot, c):
        proj_ref[slot, :, c * D_A:(c + 1) * D_A] = jnp.dot(
            n1, win_ref[:, c * D_A:(c + 1) * D_A], preferred_element_type=F32)

    @pl.when(i == 0)
    def _first_projection():
        first = pltpu.make_async_copy(x_hbm.at[pl.ds(0, TS)], xkeep_ref.at[0], xsem)
        first.start()
        first.wait()
        n1 = _rms(xkeep_ref[0], g1_ref[...]).astype(BF16)
        for c in range(D_IN // D_A):
            project(n1, 0, c)

    x_next = xn_ref[...]
    n1_next = _rms(x_next, g1_ref[...]).astype(BF16)
    xkeep_ref[1 - cur] = x_next

    project(n1_next, 1 - cur, 0)
    gu = _gelu(proj_ref[cur, :, 0:D_A])
    project(n1_next, 1 - cur, 1)
    gv = _gelu(proj_ref[cur, :, D_A:2 * D_A])
    for g in range(HEADS):
        sl = slice(g * HEAD_DIM, (g + 1) * HEAD_DIM)
        vh = _rms(gv[:, sl], vg_ref[:, sl]).astype(BF16)
        mixed = jnp.dot(wm_ref[g], vh, preferred_element_type=F32) + bst_ref[:, g:g + 1]
        mix_ref[:, sl] = (gu[:, sl] * mixed).astype(BF16)

    _route(logit_ref[1 - cur], jnp.maximum(i - 1, 0), jnp.where(i > 0, 1.0, 0.0),
           carry_ref, cnt_ref, mi_ref, mf_ref)

    o2 = 2 * D_A
    qr = proj_ref[cur, :, o2:o2 + D_B]
    fr = proj_ref[cur, :, o2 + D_B:o2 + 2 * D_B]
    vv = proj_ref[cur, :, o2 + 2 * D_B:o2 + 3 * D_B].astype(BF16)
    gr = proj_ref[cur, :, o2 + 3 * D_B:o2 + 4 * D_B]

    lbr = lbraw_ref[...]
    lbm = jnp.max(lbr, axis=0, keepdims=True)
    lbe = jnp.exp(lbr - lbm)
    lb = lbe[0:1, :] / jnp.sum(lbe, axis=0, keepdims=True)

    project(n1_next, 1 - cur, 2)
    q = qr * _sigmoid(qr)
    f = lb + (1.0 - lb) * _sigmoid(fr)
    k = 1.0 - f
    lf = jnp.log(f)

    row = lax.broadcasted_iota(jnp.int32, (TS, TS), 0)
    col = lax.broadcasted_iota(jnp.int32, (TS, TS), 1)
    causal = (row >= col) & ((row // CHUNK) == (col // CHUNK))
    tri = jnp.where(causal, 1.0, 0.0).astype(BF16)
    lf_hi, lf_lo = _split_bf16(lf)
    b = (jnp.dot(tri, lf_hi, preferred_element_type=F32)
         + jnp.dot(tri, lf_lo, preferred_element_type=F32))

    def chunk_rows(r):
        return jnp.concatenate(
            [jnp.broadcast_to(b[c * CHUNK + r:c * CHUNK + r + 1, :], (CHUNK, D_B))
             for c in range(CHUNKS_PER_STEP)], axis=0)

    project(n1_next, 1 - cur, 3)
    bref = chunk_rows(CHUNK // 2 - 1)
    blast = chunk_rows(CHUNK - 1)
    qe = (q * jnp.exp(b - bref)).astype(BF16)
    ke = (k * jnp.exp(bref - b)).astype(BF16)
    kd = (k * jnp.exp(blast - b)).astype(BF16)
    qb = (q * jnp.exp(b)).astype(BF16)

    head_sl = [slice(h * HEAD_DIM, (h + 1) * HEAD_DIM) for h in range(HEADS)]
    chunk_sl = [slice(c * CHUNK, (c + 1) * CHUNK) for c in range(CHUNKS_PER_STEP)]
    scores = [lax.dot_general(qe[:, sl], ke[:, sl], NT_DIMS, preferred_element_type=F32)
              for sl in head_sl]
    upd = [[lax.dot_general(vv[rows, sl], kd[rows, sl], TN_DIMS, preferred_element_type=F32)
            for rows in chunk_sl] for sl in head_sl]
    project(n1_next, 1 - cur, 4)
    scores = [jnp.where(causal, sc, 0.0).astype(BF16) for sc in scores]
    intra = [jnp.dot(scores[h], vv[:, head_sl[h]], preferred_element_type=F32)
             for h in range(HEADS)]
    entering = []
    for h in range(HEADS):
        st = st_ref[h]
        per_chunk = []
        for c in range(CHUNKS_PER_STEP):
            per_chunk.append(st.astype(BF16))
            decay = jnp.exp(b[(c + 1) * CHUNK - 1:(c + 1) * CHUNK, head_sl[h]])
            st = st * decay + upd[h][c]
        st_ref[h] = st
        entering.append(per_chunk)
    inter = [jnp.concatenate(
        [lax.dot_general(qb[chunk_sl[c], head_sl[h]], entering[h][c], NT_DIMS,
                         preferred_element_type=F32) for c in range(CHUNKS_PER_STEP)], axis=0)
        for h in range(HEADS)]
    for h in range(HEADS):
        o = _rms(intra[h] + inter[h], og_ref[:, head_sl[h]])
        g_h = gr[:, head_sl[h]]
        mix_ref[:, D_A + h * HEAD_DIM:D_A + (h + 1) * HEAD_DIM] = (
            o * (g_h * _sigmoid(g_h))).astype(BF16)

    h1 = xkeep_ref[cur] + jnp.dot(mix_ref[...], wout_ref[...], preferred_element_type=F32)
    h1_ref[...] = h1
    n2 = _rms(h1, g2_ref[...])
    _store_packed_rows(n2_ref, _pack_bf16_pairs(n2), TS)

    logits = jnp.dot(n2.astype(BF16), wr_ref[...], preferred_element_type=F32)
    logits = jnp.transpose(logits)[0:ROUTER_ROWS, :] + br_ref[...]
    project(n1_next, 1 - cur, 5)
    logit_ref[cur] = logits

    @pl.when(i == N_TOK // TS - 1)
    def _route_last_tile():
        _route(logits, i, 1.0, carry_ref, cnt_ref, mi_ref, mf_ref)


def _route(logits, tile, live, carry_ref, cnt_ref, mi_ref, mf_ref):
    rid = lax.broadcasted_iota(jnp.int32, (ROUTER_ROWS, TS), 0)
    row = lax.broadcasted_iota(jnp.int32, (TS, TS), 0)
    col = lax.broadcasted_iota(jnp.int32, (TS, TS), 1)
    neg = -jnp.inf
    gl = jnp.where(rid < N_GROUPS, logits, neg)
    gmax = jnp.max(gl, axis=0, keepdims=True)
    g_idx = jnp.min(jnp.where(gl == gmax, rid, ROUTER_ROWS), axis=0, keepdims=True)
    g_prob = 1.0 / jnp.sum(jnp.exp(gl - gmax), axis=0, keepdims=True)
    e_lo = EXPERT_ROW0 + g_idx * EXPERTS_PER_GROUP
    el = jnp.where((rid >= e_lo) & (rid < e_lo + EXPERTS_PER_GROUP), logits, neg)
    m1 = jnp.max(el, axis=0, keepdims=True)
    i1 = jnp.min(jnp.where(el == m1, rid, ROUTER_ROWS), axis=0, keepdims=True)
    el2 = jnp.where(rid == i1, neg, el)
    m2 = jnp.max(el2, axis=0, keepdims=True)
    i2 = jnp.min(jnp.where(el2 == m2, rid, ROUTER_ROWS), axis=0, keepdims=True)
    e21 = jnp.exp(m2 - m1)
    w1 = g_prob / (1.0 + e21)
    w2 = g_prob * e21 / (1.0 + e21)

    hit1 = rid == i1
    hit2 = rid == i2
    onehot = jnp.where(hit1 | hit2, live, 0.0)
    earlier = jnp.where(row < col, 1.0, 0.0).astype(BF16)
    seen = carry_ref[...]
    prior = (jnp.dot(onehot.astype(BF16), earlier, preferred_element_type=F32)
             + jnp.concatenate([seen] * (TS // LANES), axis=1))
    r1 = jnp.sum(jnp.where(hit1, prior, 0.0), axis=0, keepdims=True)
    r2 = jnp.sum(jnp.where(hit2, prior, 0.0), axis=0, keepdims=True)
    seen = seen + jnp.sum(onehot, axis=1, keepdims=True)
    carry_ref[...] = seen
    cnt_ref[...] = seen.astype(jnp.int32)

    cols = pl.ds(pl.multiple_of(tile * TS, TS), TS)
    pad_i = jnp.zeros((META_ROWS - 2 * TOP_K, TS), jnp.int32)
    mi_ref[:, cols] = jnp.concatenate(
        [i1 - EXPERT_ROW0, i2 - EXPERT_ROW0, r1.astype(jnp.int32), r2.astype(jnp.int32), pad_i],
        axis=0)
    mf_ref[:, cols] = jnp.concatenate([w1, w2, jnp.zeros((META_ROWS - TOP_K, TS), F32)], axis=0)


MIXER_VMEM_LIMIT = (
    2 * (D_MODEL * D_IN + D_MODEL * D_MODEL) * 2
    + 2 * TS * D_MODEL * 4 * 2
    + 2 * TS * PACK_COLS * 4
    + 2 * 2 * META_ROWS * N_TOK * 4
    + 2 * TS * D_IN * 4 + 2 * TS * D_MODEL * 4
    + HEADS * TS * TS * 2 + TS * D_MODEL * 2
    + 16 * 1024 * 1024)


def _mixer(x2, g1, win, vg, ws, bst, lbraw, og, wout, g2, wr, br):
    n_steps = N_TOK // TS
    const2 = lambda i: (0, 0)
    tok = lambda i: (i, 0)
    return pl.pallas_call(
        _mixer_kernel,
        grid=(n_steps,),
        in_specs=[
            pl.BlockSpec(memory_space=pl.ANY),
            pl.BlockSpec((TS, D_MODEL), lambda i: (jnp.minimum(i + 1, n_steps - 1), 0)),
            pl.BlockSpec((1, D_MODEL), const2),
            pl.BlockSpec((D_MODEL, D_IN), const2),
            pl.BlockSpec((1, D_A), const2),
            pl.BlockSpec((HEADS, GMLP_BLOCK, GMLP_BLOCK), lambda i: (0, 0, 0)),
            pl.BlockSpec((TS, HEADS), const2),
            pl.BlockSpec((2, D_B), const2),
            pl.BlockSpec((1, D_B), const2),
            pl.BlockSpec((D_MODEL, D_MODEL), const2),
            pl.BlockSpec((1, D_MODEL), const2),
            pl.BlockSpec((D_MODEL, LANES), const2),
            pl.BlockSpec((ROUTER_ROWS, 1), const2),
        ],
        out_specs=[
            pl.BlockSpec((TS, D_MODEL), tok),
            pl.BlockSpec((TS * PACK_ROWS, LANES), tok),
            pl.BlockSpec((META_ROWS, N_TOK), const2),
            pl.BlockSpec((META_ROWS, N_TOK), const2),
            pl.BlockSpec((ROUTER_ROWS, LANES), const2),
        ],
        out_shape=[
            jax.ShapeDtypeStruct((N_TOK, D_MODEL), F32),
            jax.ShapeDtypeStruct((N_TOK * PACK_ROWS, LANES), jnp.uint32),
            jax.ShapeDtypeStruct((META_ROWS, N_TOK), jnp.int32),
            jax.ShapeDtypeStruct((META_ROWS, N_TOK), F32),
            jax.ShapeDtypeStruct((ROUTER_ROWS, LANES), jnp.int32),
        ],
        scratch_shapes=[
            pltpu.VMEM((HEADS, HEAD_DIM, HEAD_DIM), F32),
            pltpu.VMEM((ROUTER_ROWS, LANES), F32),
            pltpu.VMEM((HEADS, TS, TS), BF16),
            pltpu.VMEM((TS, D_MODEL), BF16),
            pltpu.VMEM((2, TS, D_IN), F32),
            pltpu.VMEM((2, ROUTER_ROWS, TS), F32),
            pltpu.VMEM((2, TS, D_MODEL), F32),
            pltpu.SemaphoreType.DMA,
        ],
        compiler_params=pltpu.CompilerParams(
            dimension_semantics=("arbitrary",),
            vmem_limit_bytes=MIXER_VMEM_LIMIT,
        ),
        name="mixer",
    )(x2, x2, g1, win, vg, ws, bst, lbraw, og, wout, g2, wr, br)


FILL_UNROLL = 8

def _invert(dest_ref, gap_lo_ref, gap_hi_ref, inv_ref):
    for e in range(N_EXPERTS):
        first = gap_lo_ref[e] // FILL_UNROLL

        def fill(g, carry):
            for u in range(FILL_UNROLL):
                inv_ref[g * FILL_UNROLL + u] = -1
            return carry

        lax.fori_loop(first, gap_hi_ref[e] // FILL_UNROLL, fill, 0)

    def place(a, carry):
        inv_ref[dest_ref[a]] = a
        return carry

    lax.fori_loop(0, N_ASSIGN, place, 0, unroll=16)


def _expert_kernel(bexp_ref, nused_ref, dest_ref, gap_lo_ref, gap_hi_ref,
                   n2_hbm, wg_ref, wu_ref, wd_ref, y_hbm,
                   inv_ref, n2_vmem, xb0, xb1, xb2, ob0, ob1, ob2, wg_bf, wu_bf, wd_bf, nsem, ssem):
    i = pl.program_id(0)
    nused = nused_ref[0]
    last = nused - 1
    xbuf = (xb0, xb1, xb2)
    obuf = (ob0, ob1, ob2)

    def gather(s, blk):
        for j in range(MOE_BLOCK):
            tok = inv_ref[blk * MOE_BLOCK + j] & (N_TOK - 1)
            r = pl.multiple_of(tok * PACK_ROWS, PACK_ROWS)
            xbuf[s][pl.ds(j * PACK_ROWS, PACK_ROWS), :] = n2_vmem[pl.ds(r, PACK_ROWS), :]

    def start_scatter(s, blk):
        for j in range(MOE_BLOCK):
            a = inv_ref[blk * MOE_BLOCK + j]
            row = jnp.where(a >= 0, a, N_ASSIGN + s * MOE_BLOCK + j)
            r = pl.multiple_of(row * PACK_ROWS, PACK_ROWS)
            pltpu.make_async_copy(obuf[s].at[pl.ds(j * PACK_ROWS, PACK_ROWS)],
                                  y_hbm.at[pl.ds(r, PACK_ROWS)], ssem.at[s]
                                  ).start(priority=j % 2)

    def wait_scatter(s):
        pltpu.make_async_copy(obuf[s], y_hbm.at[pl.ds(0, BLOCK_PACK_ROWS)],
                              ssem.at[s]).wait()

    @pl.when(i == 0)
    def _prologue():
        for o in obuf:
            o[...] = jnp.zeros_like(o)
        for s in range(N_SLOTS - 1):
            pltpu.make_async_copy(
                obuf[s],
                y_hbm.at[pl.ds((N_ASSIGN + s * MOE_BLOCK) * PACK_ROWS, BLOCK_PACK_ROWS)],
                ssem.at[s]).start()
        resident = pltpu.make_async_copy(n2_hbm, n2_vmem, nsem)
        resident.start()
        _invert(dest_ref, gap_lo_ref, gap_hi_ref, inv_ref)
        resident.wait()
        gather(0, 0)

    prev = bexp_ref[jnp.maximum(i - 1, 0)]
    changed = (i == 0) | (bexp_ref[i] != prev)

    @pl.when(changed)
    def _cast_weights():
        wg_bf[...] = wg_ref[0].astype(BF16)
        wu_bf[...] = wu_ref[0].astype(BF16)
        wd_bf[...] = wd_ref[0].astype(BF16)

    def active(cur, nxt, prv):
        wait_scatter(cur)
        start_scatter(prv, jnp.where(i == 0, N_BLOCKS, i - 1))
        gather(nxt, jnp.minimum(i + 1, last))
        xb = _unpack_bf16_pairs(_load_packed_rows(xbuf[cur], MOE_BLOCK)).astype(BF16)
        hg = jnp.dot(xb, wg_bf[...], preferred_element_type=F32)
        hu = jnp.dot(xb, wu_bf[...], preferred_element_type=F32)
        hh = (hg * _sigmoid(hg) * hu).astype(BF16)
        y = jnp.dot(hh, wd_bf[...], preferred_element_type=F32)
        _store_packed_rows(obuf[cur], _pack_bf16_pairs(y), MOE_BLOCK)

    def epilogue(cur, nxt, prv):
        start_scatter(cur, i)
        wait_scatter(nxt)
        wait_scatter(prv)
        wait_scatter(cur)

    for cur in range(N_SLOTS):
        slots = (cur, (cur + 1) % N_SLOTS, (cur + 2) % N_SLOTS)

        @pl.when((i < nused) & (i % N_SLOTS == cur))
        def _():
            active(*slots)

        @pl.when((i == last) & (i % N_SLOTS == cur))
        def _():
            epilogue(*slots)


_EXPERT_W_BYTES = 3 * D_MODEL * D_EXPERT * 4
EXPERT_VMEM_LIMIT = (
    N_TOK * PACK_COLS * 4
    + 2 * _EXPERT_W_BYTES
    + _EXPERT_W_BYTES // 2
    + 2 * N_SLOTS * MOE_BLOCK * PACK_COLS * 4
    + 4 * 1024 * 1024)


def _experts(bexp, nused, dest, gap_lo, gap_hi, n2, w_gate, w_up, w_down):
    def w_blk(i, bexp, nused, dest, gap_lo, gap_hi):
        return (bexp[i], 0, 0)

    any_spec = pl.BlockSpec(memory_space=pl.ANY)
    return pl.pallas_call(
        _expert_kernel,
        grid_spec=pltpu.PrefetchScalarGridSpec(
            num_scalar_prefetch=5,
            grid=(N_BLOCKS,),
            in_specs=[
                any_spec,
                pl.BlockSpec((1, D_MODEL, D_EXPERT), w_blk),
                pl.BlockSpec((1, D_MODEL, D_EXPERT), w_blk),
                pl.BlockSpec((1, D_EXPERT, D_MODEL), w_blk),
            ],
            out_specs=any_spec,
            scratch_shapes=[
                pltpu.SMEM((INV_LEN,), jnp.int32),
                pltpu.VMEM((N_TOK * PACK_ROWS, LANES), jnp.uint32),
            ] + [pltpu.VMEM((BLOCK_PACK_ROWS, LANES), jnp.uint32)] * (2 * N_SLOTS) + [
                pltpu.VMEM((D_MODEL, D_EXPERT), BF16),
                pltpu.VMEM((D_MODEL, D_EXPERT), BF16),
                pltpu.VMEM((D_EXPERT, D_MODEL), BF16),
                pltpu.SemaphoreType.DMA,
                pltpu.SemaphoreType.DMA((N_SLOTS,)),
            ],
        ),
        out_shape=jax.ShapeDtypeStruct(((N_ASSIGN + DUMMY_ROWS) * PACK_ROWS, LANES), jnp.uint32),
        compiler_params=pltpu.CompilerParams(
            dimension_semantics=("arbitrary",),
            vmem_limit_bytes=EXPERT_VMEM_LIMIT,
        ),
        name="experts",
    )(bexp, nused, dest, gap_lo, gap_hi, n2, w_gate, w_up, w_down)


def _combine_kernel(h1_ref, mf_ref, fg_ref, y0_ref, y1_ref, out_ref):
    w = mf_ref[...]
    y0 = _unpack_bf16_pairs(_load_packed_rows(y0_ref, TD))
    y1 = _unpack_bf16_pairs(_load_packed_rows(y1_ref, TD))
    h = h1_ref[...] + y0 * w[:, 0:1] + y1 * w[:, 1:2]
    out_ref[...] = _rms(h, fg_ref[...])


COMBINE_VMEM_LIMIT = 2 * (2 * (2 * TD * D_MODEL * 4) + 2 * (2 * TD * PACK_COLS * 4))


def _combine(h1, mf, fg, y):
    n_steps = N_TOK // TD
    tok = lambda i: (i, 0)
    return pl.pallas_call(
        _combine_kernel,
        grid=(n_steps,),
        in_specs=[
            pl.BlockSpec((TD, D_MODEL), tok),
            pl.BlockSpec((TD, TOP_K), tok),
            pl.BlockSpec((1, D_MODEL), lambda i: (0, 0)),
            pl.BlockSpec((TD * PACK_ROWS, LANES), tok),
            pl.BlockSpec((TD * PACK_ROWS, LANES), lambda i: (i + n_steps, 0)),
        ],
        out_specs=pl.BlockSpec((TD, D_MODEL), tok),
        out_shape=jax.ShapeDtypeStruct((N_TOK, D_MODEL), F32),
        compiler_params=pltpu.CompilerParams(
            dimension_semantics=("arbitrary",),
            vmem_limit_bytes=COMBINE_VMEM_LIMIT,
        ),
        name="combine",
    )(h1, mf, fg, y, y)


def kernel(x, norm1_gain, w_in, gmlp_v_gain, gmlp_w_s, gmlp_b_s, hgrn_lower_bounds,
           hgrn_out_gain, w_out, norm2_gain, w_group_router, b_group_router,
           w_expert_router, b_expert_router, w_gate, w_up, w_down, final_gain):
    l = 0
    x2 = x.reshape(N_TOK, D_MODEL)
    bst = jnp.tile(jnp.transpose(gmlp_b_s[l]), (TS // GMLP_BLOCK, 1))
    w_router = jnp.concatenate([w_group_router[l], w_expert_router[l]], axis=1)
    w_router = jnp.pad(w_router, ((0, 0), (0, LANES - w_router.shape[1]))).astype(BF16)
    b_router = jnp.concatenate([b_group_router[l], b_expert_router[l]])
    b_router = jnp.pad(b_router, (0, ROUTER_ROWS - b_router.shape[0])).reshape(ROUTER_ROWS, 1)

    h1, n2, meta_i, meta_f, counts = _mixer(
        x2, norm1_gain[l].reshape(1, D_MODEL), w_in[l].astype(BF16),
        gmlp_v_gain[l].reshape(1, D_A), gmlp_w_s[l], bst,
        hgrn_lower_bounds, hgrn_out_gain[l].reshape(1, D_B), w_out[l].astype(BF16),
        norm2_gain[l].reshape(1, D_MODEL), w_router, b_router)

    cnt = counts[EXPERT_ROW0:EXPERT_ROW0 + N_EXPERTS, 0]
    padded = ((cnt + MOE_BLOCK - 1) // MOE_BLOCK) * MOE_BLOCK
    pend = jnp.cumsum(padded)
    pstart = pend - padded
    eid = meta_i[0:TOP_K]
    rank = meta_i[TOP_K:2 * TOP_K]
    base = jnp.sum(jnp.where(eid[:, :, None] == jnp.arange(N_EXPERTS)[None, None, :],
                             pstart[None, None, :], 0), axis=-1)
    dest = (base + rank).astype(jnp.int32).reshape(N_ASSIGN)
    gap_lo = (pstart + cnt).astype(jnp.int32)
    gap_hi = jnp.concatenate([pstart[1:], jnp.full((1,), INV_LEN)]).astype(jnp.int32)
    blk_start = jnp.arange(N_BLOCKS, dtype=jnp.int32) * MOE_BLOCK
    bexp = jnp.clip(jnp.sum(blk_start[:, None] >= pend[None, :], axis=1), 0,
                    N_EXPERTS - 1).astype(jnp.int32)
    nused = (pend[-1:] // MOE_BLOCK).astype(jnp.int32)

    y = _experts(bexp, nused, dest, gap_lo, gap_hi, n2, w_gate[l], w_up[l], w_down[l])
    out = _combine(h1, meta_f[0:TOP_K].T, final_gain.reshape(1, D_MODEL), y)
    return out.reshape(BATCH, SEQ, D_MODEL)
```

```python
import functools

import jax
import jax.numpy as jnp
from jax import lax
from jax.experimental import pallas as pl
from jax.experimental.pallas import tpu as pltpu

D_MODEL = 1024
BATCH = 2
SEQ = 8192
N_TOK = BATCH * SEQ
CHUNK = 64
EPS = 1e-6
D_A = 512
HEADS = 4
HEAD_DIM = 128
GMLP_BLOCK = 128
D_B = 512
D_IN = 3072
N_GROUPS = 4
EXPERTS_PER_GROUP = 8
N_EXPERTS = 32
TOP_K = 2
D_EXPERT = 512

LANES = 128
ROUTER_ROWS = 48
EXPERT_ROW0 = N_GROUPS
META_ROWS = 8
PACK_COLS = D_MODEL // 2
PACK_ROWS = PACK_COLS // LANES
HI_MASK = 0xFFFF0000

TS = 256
CHUNKS_PER_STEP = TS // CHUNK
MOE_BLOCK = 256
N_BLOCKS = N_TOK * TOP_K // MOE_BLOCK + N_EXPERTS
SORTED_ROWS = N_BLOCKS * MOE_BLOCK
N_ASSIGN = N_TOK * TOP_K
N_SLOTS = 3
DUMMY_ROWS = N_SLOTS * MOE_BLOCK
BLOCK_PACK_ROWS = MOE_BLOCK * PACK_ROWS
INV_LEN = SORTED_ROWS + MOE_BLOCK
TD = 1024

F32 = jnp.float32
BF16 = jnp.bfloat16
NT_DIMS = (((1,), (1,)), ((), ()))
TN_DIMS = (((0,), (0,)), ((), ()))


def _sigmoid(x):
    return 1.0 / (1.0 + jnp.exp(-x))


def _gelu(x):
    return 0.5 * x * (1.0 + lax.erf(x * (2.0 ** -0.5)))


def _rms(x, gain):
    return x * lax.rsqrt(jnp.mean(x * x, axis=-1, keepdims=True) + EPS) * gain


def _pack_bf16_pairs(x):
    lo = pltpu.bitcast(x[:, 0:PACK_COLS].astype(BF16).astype(F32), jnp.uint32) >> 16
    hi = (pltpu.bitcast(x[:, PACK_COLS:D_MODEL].astype(BF16).astype(F32), jnp.uint32)
          & jnp.uint32(HI_MASK))
    return lo | hi


def _unpack_bf16_pairs(words):
    lo = pltpu.bitcast(words << 16, F32)
    hi = pltpu.bitcast(words & jnp.uint32(HI_MASK), F32)
    return jnp.concatenate([lo, hi], axis=1)


def _store_packed_rows(ref, words, rows):
    for c in range(PACK_ROWS):
        ref[pl.ds(c, rows, stride=PACK_ROWS), :] = words[:, c * LANES:(c + 1) * LANES]


def _load_packed_rows(ref, rows):
    return jnp.concatenate(
        [ref[pl.ds(c, rows, stride=PACK_ROWS), :] for c in range(PACK_ROWS)], axis=1)


def _split_bf16(x):
    hi = x.astype(BF16)
    lo = (x - hi.astype(F32)).astype(BF16)
    return hi, lo


def _mixer_kernel(*refs):
    for parity in range(2):
        pl.when(pl.program_id(0) % 2 == parity)(functools.partial(_mixer_step, parity, *refs))


def _mixer_step(cur, x_hbm, xn_ref, g1_ref, win_hbm, vg_ref, ws_ref, bst_ref, lbraw_ref, og_ref,
                wout_hbm, g2_ref, wr_ref, br_ref,
                h1_ref, n2_ref, mi_ref, mf_ref, cnt_ref,
                st_ref, carry_ref, wm_ref, mix_ref, proj_ref, logit_ref, xkeep_ref, xsem,
                win_ref, wout_ref, wstage_ref, wsem):
    i = pl.program_id(0)

    @pl.when(i == 0)
    def _init():
        carry_ref[...] = jnp.zeros_like(carry_ref)
        logit_ref[...] = jnp.zeros_like(logit_ref)
        n_in = D_IN // D_A
        pieces = ([(win_hbm, win_ref, c) for c in range(n_in)]
                  + [(wout_hbm, wout_ref, c) for c in range(D_MODEL // D_A)])

        def fetch(k):
            src, _, c = pieces[k]
            return pltpu.make_async_copy(src.at[:, pl.ds(c * D_A, D_A)], wstage_ref.at[k % 2],
                                         wsem.at[k % 2])

        fetch(0).start()
        for k, (_, dst, c) in enumerate(pieces):
            if k + 1 < len(pieces):
                fetch(k + 1).start()
            fetch(k).wait()
            dst[:, c * D_A:(c + 1) * D_A] = wstage_ref[k % 2].astype(BF16)
        r = lax.broadcasted_iota(jnp.int32, (GMLP_BLOCK, GMLP_BLOCK), 0)
        c = lax.broadcasted_iota(jnp.int32, (GMLP_BLOCK, GMLP_BLOCK), 1)
        keep = (c // CHUNK) <= (r // CHUNK)
        wm_ref[...] = jnp.zeros_like(wm_ref)
        for g in range(HEADS):
            w = jnp.where(keep, ws_ref[g], 0.0).astype(BF16)
            for p in range(TS // GMLP_BLOCK):
                wm_ref[g, p * GMLP_BLOCK:(p + 1) * GMLP_BLOCK,
                       p * GMLP_BLOCK:(p + 1) * GMLP_BLOCK] = w

    @pl.when(i % (SEQ // TS) == 0)
    def _reset_state():
        st_ref[...] = jnp.zeros_like(st_ref)

    def project(n1, slot, c):
        proj_ref[slot, :, c * D_A:(c + 1) * D_A] = jnp.dot(
            n1, win_ref[:, c * D_A:(c + 1) * D_A], preferred_element_type=F32)

    @pl.when(i == 0)
    def _first_projection():
        first = pltpu.make_async_copy(x_hbm.at[pl.ds(0, TS)], xkeep_ref.at[0], xsem)
        first.start()
        first.wait()
        n1 = _rms(xkeep_ref[0], g1_ref[...]).astype(BF16)
        for c in range(D_IN // D_A):
            project(n1, 0, c)

    x_next = xn_ref[...]
    n1_next = _rms(x_next, g1_ref[...]).astype(BF16)
    xkeep_ref[1 - cur] = x_next

    project(n1_next, 1 - cur, 0)
    gu = _gelu(proj_ref[cur, :, 0:D_A])
    project(n1_next, 1 - cur, 1)
    gv = _gelu(proj_ref[cur, :, D_A:2 * D_A])
    for g in range(HEADS):
        sl = slice(g * HEAD_DIM, (g + 1) * HEAD_DIM)
        vh = _rms(gv[:, sl], vg_ref[:, sl]).astype(BF16)
        mixed = jnp.dot(wm_ref[g], vh, preferred_element_type=F32) + bst_ref[:, g:g + 1]
        mix_ref[:, sl] = (gu[:, sl] * mixed).astype(BF16)

    _route(logit_ref[1 - cur], jnp.maximum(i - 1, 0), jnp.where(i > 0, 1.0, 0.0),
           carry_ref, cnt_ref, mi_ref, mf_ref)

    o2 = 2 * D_A
    qr = proj_ref[cur, :, o2:o2 + D_B]
    fr = proj_ref[cur, :, o2 + D_B:o2 + 2 * D_B]
    vv = proj_ref[cur, :, o2 + 2 * D_B:o2 + 3 * D_B].astype(BF16)
    gr = proj_ref[cur, :, o2 + 3 * D_B:o2 + 4 * D_B]

    lbr = lbraw_ref[...]
    lbm = jnp.max(lbr, axis=0, keepdims=True)
    lbe = jnp.exp(lbr - lbm)
    lb = lbe[0:1, :] / jnp.sum(lbe, axis=0, keepdims=True)

    project(n1_next, 1 - cur, 2)
    q = qr * _sigmoid(qr)
    f = lb + (1.0 - lb) * _sigmoid(fr)
    k = 1.0 - f
    lf = jnp.log(f)

    row = lax.broadcasted_iota(jnp.int32, (TS, TS), 0)
    col = lax.broadcasted_iota(jnp.int32, (TS, TS), 1)
    causal = (row >= col) & ((row // CHUNK) == (col // CHUNK))
    tri = jnp.where(causal, 1.0, 0.0).astype(BF16)
    lf_hi, lf_lo = _split_bf16(lf)
    b = (jnp.dot(tri, lf_hi, preferred_element_type=F32)
         + jnp.dot(tri, lf_lo, preferred_element_type=F32))

    def chunk_rows(r):
        return jnp.concatenate(
            [jnp.broadcast_to(b[c * CHUNK + r:c * CHUNK + r + 1, :], (CHUNK, D_B))
             for c in range(CHUNKS_PER_STEP)], axis=0)

    project(n1_next, 1 - cur, 3)
    bref = chunk_rows(CHUNK // 2 - 1)
    blast = chunk_rows(CHUNK - 1)
    qe = (q * jnp.exp(b - bref)).astype(BF16)
    ke = (k * jnp.exp(bref - b)).astype(BF16)
    kd = (k * jnp.exp(blast - b)).astype(BF16)
    qb = (q * jnp.exp(b)).astype(BF16)

    head_sl = [slice(h * HEAD_DIM, (h + 1) * HEAD_DIM) for h in range(HEADS)]
    chunk_sl = [slice(c * CHUNK, (c + 1) * CHUNK) for c in range(CHUNKS_PER_STEP)]
    scores = [lax.dot_general(qe[:, sl], ke[:, sl], NT_DIMS, preferred_element_type=F32)
              for sl in head_sl]
    upd = [[lax.dot_general(vv[rows, sl], kd[rows, sl], TN_DIMS, preferred_element_type=F32)
            for rows in chunk_sl] for sl in head_sl]
    project(n1_next, 1 - cur, 4)
    scores = [jnp.where(causal, sc, 0.0).astype(BF16) for sc in scores]
    intra = [jnp.dot(scores[h], vv[:, head_sl[h]], preferred_element_type=F32)
             for h in range(HEADS)]
    entering = []
    for h in range(HEADS):
        st = st_ref[h]
        per_chunk = []
        for c in range(CHUNKS_PER_STEP):
            per_chunk.append(st.astype(BF16))
            decay = jnp.exp(b[(c + 1) * CHUNK - 1:(c + 1) * CHUNK, head_sl[h]])
            st = st * decay + upd[h][c]
        st_ref[h] = st
        entering.append(per_chunk)
    inter = [jnp.concatenate(
        [lax.dot_general(qb[chunk_sl[c], head_sl[h]], entering[h][c], NT_DIMS,
                         preferred_element_type=F32) for c in range(CHUNKS_PER_STEP)], axis=0)
        for h in range(HEADS)]
    for h in range(HEADS):
        o = _rms(intra[h] + inter[h], og_ref[:, head_sl[h]])
        g_h = gr[:, head_sl[h]]
        mix_ref[:, D_A + h * HEAD_DIM:D_A + (h + 1) * HEAD_DIM] = (
            o * (g_h * _sigmoid(g_h))).astype(BF16)

    h1 = xkeep_ref[cur] + jnp.dot(mix_ref[...], wout_ref[...], preferred_element_type=F32)
    h1_ref[...] = h1
    n2 = _rms(h1, g2_ref[...])
    _store_packed_rows(n2_ref, _pack_bf16_pairs(n2), TS)

    logits = jnp.dot(n2.astype(BF16), wr_ref[...], preferred_element_type=F32)
    logits = jnp.transpose(logits)[0:ROUTER_ROWS, :] + br_ref[...]
    project(n1_next, 1 - cur, 5)
    logit_ref[cur] = logits

    @pl.when(i == N_TOK // TS - 1)
    def _route_last_tile():
        _route(logits, i, 1.0, carry_ref, cnt_ref, mi_ref, mf_ref)


def _route(logits, tile, live, carry_ref, cnt_ref, mi_ref, mf_ref):
    rid = lax.broadcasted_iota(jnp.int32, (ROUTER_ROWS, TS), 0)
    row = lax.broadcasted_iota(jnp.int32, (TS, TS), 0)
    col = lax.broadcasted_iota(jnp.int32, (TS, TS), 1)
    neg = -jnp.inf
    gl = jnp.where(rid < N_GROUPS, logits, neg)
    gmax = jnp.max(gl, axis=0, keepdims=True)
    g_idx = jnp.min(jnp.where(gl == gmax, rid, ROUTER_ROWS), axis=0, keepdims=True)
    g_prob = 1.0 / jnp.sum(jnp.exp(gl - gmax), axis=0, keepdims=True)
    e_lo = EXPERT_ROW0 + g_idx * EXPERTS_PER_GROUP
    el = jnp.where((rid >= e_lo) & (rid < e_lo + EXPERTS_PER_GROUP), logits, neg)
    m1 = jnp.max(el, axis=0, keepdims=True)
    i1 = jnp.min(jnp.where(el == m1, rid, ROUTER_ROWS), axis=0, keepdims=True)
    el2 = jnp.where(rid == i1, neg, el)
    m2 = jnp.max(el2, axis=0, keepdims=True)
    i2 = jnp.min(jnp.where(el2 == m2, rid, ROUTER_ROWS), axis=0, keepdims=True)
    e21 = jnp.exp(m2 - m1)
    w1 = g_prob / (1.0 + e21)
    w2 = g_prob * e21 / (1.0 + e21)

    hit1 = rid == i1
    hit2 = rid == i2
    onehot = jnp.where(hit1 | hit2, live, 0.0)
    earlier = jnp.where(row < col, 1.0, 0.0).astype(BF16)
    seen = carry_ref[...]
    prior = (jnp.dot(onehot.astype(BF16), earlier, preferred_element_type=F32)
             + jnp.concatenate([seen] * (TS // LANES), axis=1))
    r1 = jnp.sum(jnp.where(hit1, prior, 0.0), axis=0, keepdims=True)
    r2 = jnp.sum(jnp.where(hit2, prior, 0.0), axis=0, keepdims=True)
    seen = seen + jnp.sum(onehot, axis=1, keepdims=True)
    carry_ref[...] = seen
    cnt_ref[...] = seen.astype(jnp.int32)

    cols = pl.ds(pl.multiple_of(tile * TS, TS), TS)
    pad_i = jnp.zeros((META_ROWS - 2 * TOP_K, TS), jnp.int32)
    mi_ref[:, cols] = jnp.concatenate(
        [i1 - EXPERT_ROW0, i2 - EXPERT_ROW0, r1.astype(jnp.int32), r2.astype(jnp.int32), pad_i],
        axis=0)
    mf_ref[:, cols] = jnp.concatenate([w1, w2, jnp.zeros((META_ROWS - TOP_K, TS), F32)], axis=0)


MIXER_VMEM_LIMIT = (
    (D_MODEL * D_IN + D_MODEL * D_MODEL) * 2
    + 2 * D_MODEL * D_A * 4
    + 2 * TS * D_MODEL * 4 * 2
    + 2 * TS * PACK_COLS * 4
    + 2 * 2 * META_ROWS * N_TOK * 4
    + 2 * TS * D_IN * 4 + 2 * TS * D_MODEL * 4
    + HEADS * TS * TS * 2 + TS * D_MODEL * 2
    + 16 * 1024 * 1024)


def _mixer(x2, g1, win, vg, ws, bst, lbraw, og, wout, g2, wr, br):
    n_steps = N_TOK // TS
    const2 = lambda i: (0, 0)
    tok = lambda i: (i, 0)
    return pl.pallas_call(
        _mixer_kernel,
        grid=(n_steps,),
        in_specs=[
            pl.BlockSpec(memory_space=pl.ANY),
            pl.BlockSpec((TS, D_MODEL), lambda i: (jnp.minimum(i + 1, n_steps - 1), 0)),
            pl.BlockSpec((1, D_MODEL), const2),
            pl.BlockSpec(memory_space=pl.ANY),
            pl.BlockSpec((1, D_A), const2),
            pl.BlockSpec((HEADS, GMLP_BLOCK, GMLP_BLOCK), lambda i: (0, 0, 0)),
            pl.BlockSpec((TS, HEADS), const2),
            pl.BlockSpec((2, D_B), const2),
            pl.BlockSpec((1, D_B), const2),
            pl.BlockSpec(memory_space=pl.ANY),
            pl.BlockSpec((1, D_MODEL), const2),
            pl.BlockSpec((D_MODEL, LANES), const2),
            pl.BlockSpec((ROUTER_ROWS, 1), const2),
        ],
        out_specs=[
            pl.BlockSpec((TS, D_MODEL), tok),
            pl.BlockSpec((TS * PACK_ROWS, LANES), tok),
            pl.BlockSpec((META_ROWS, N_TOK), const2),
            pl.BlockSpec((META_ROWS, N_TOK), const2),
            pl.BlockSpec((ROUTER_ROWS, LANES), const2),
        ],
        out_shape=[
            jax.ShapeDtypeStruct((N_TOK, D_MODEL), F32),
            jax.ShapeDtypeStruct((N_TOK * PACK_ROWS, LANES), jnp.uint32),
            jax.ShapeDtypeStruct((META_ROWS, N_TOK), jnp.int32),
            jax.ShapeDtypeStruct((META_ROWS, N_TOK), F32),
            jax.ShapeDtypeStruct((ROUTER_ROWS, LANES), jnp.int32),
        ],
        scratch_shapes=[
            pltpu.VMEM((HEADS, HEAD_DIM, HEAD_DIM), F32),
            pltpu.VMEM((ROUTER_ROWS, LANES), F32),
            pltpu.VMEM((HEADS, TS, TS), BF16),
            pltpu.VMEM((TS, D_MODEL), BF16),
            pltpu.VMEM((2, TS, D_IN), F32),
            pltpu.VMEM((2, ROUTER_ROWS, TS), F32),
            pltpu.VMEM((2, TS, D_MODEL), F32),
            pltpu.SemaphoreType.DMA,
            pltpu.VMEM((D_MODEL, D_IN), BF16),
            pltpu.VMEM((D_MODEL, D_MODEL), BF16),
            pltpu.VMEM((2, D_MODEL, D_A), F32),
            pltpu.SemaphoreType.DMA((2,)),
        ],
        compiler_params=pltpu.CompilerParams(
            dimension_semantics=("arbitrary",),
            vmem_limit_bytes=MIXER_VMEM_LIMIT,
        ),
        name="mixer",
    )(x2, x2, g1, win, vg, ws, bst, lbraw, og, wout, g2, wr, br)


FILL_UNROLL = 8

def _invert(dest_ref, gap_lo_ref, gap_hi_ref, inv_ref):
    for e in range(N_EXPERTS):
        first = gap_lo_ref[e] // FILL_UNROLL

        def fill(g, carry):
            for u in range(FILL_UNROLL):
                inv_ref[g * FILL_UNROLL + u] = -1
            return carry

        lax.fori_loop(first, gap_hi_ref[e] // FILL_UNROLL, fill, 0)

    def place(a, carry):
        inv_ref[dest_ref[a]] = a
        return carry

    lax.fori_loop(0, N_ASSIGN, place, 0, unroll=16)


def _expert_kernel(bexp_ref, nused_ref, dest_ref, gap_lo_ref, gap_hi_ref,
                   n2_hbm, wg_ref, wu_ref, wd_ref, y_hbm,
                   inv_ref, n2_vmem, xb0, xb1, xb2, ob0, ob1, ob2, wg_bf, wu_bf, wd_bf, nsem, ssem):
    i = pl.program_id(0)
    nused = nused_ref[0]
    last = nused - 1
    xbuf = (xb0, xb1, xb2)
    obuf = (ob0, ob1, ob2)

    def gather(s, blk):
        for j in range(MOE_BLOCK):
            tok = inv_ref[blk * MOE_BLOCK + j] & (N_TOK - 1)
            r = pl.multiple_of(tok * PACK_ROWS, PACK_ROWS)
            xbuf[s][pl.ds(j * PACK_ROWS, PACK_ROWS), :] = n2_vmem[pl.ds(r, PACK_ROWS), :]

    def start_scatter(s, blk):
        for j in range(MOE_BLOCK):
            a = inv_ref[blk * MOE_BLOCK + j]
            row = jnp.where(a >= 0, a, N_ASSIGN + s * MOE_BLOCK + j)
            r = pl.multiple_of(row * PACK_ROWS, PACK_ROWS)
            pltpu.make_async_copy(obuf[s].at[pl.ds(j * PACK_ROWS, PACK_ROWS)],
                                  y_hbm.at[pl.ds(r, PACK_ROWS)], ssem.at[s]
                                  ).start(priority=j % 2)

    def wait_scatter(s):
        pltpu.make_async_copy(obuf[s], y_hbm.at[pl.ds(0, BLOCK_PACK_ROWS)],
                              ssem.at[s]).wait()

    @pl.when(i == 0)
    def _prologue():
        for o in obuf:
            o[...] = jnp.zeros_like(o)
        for s in range(N_SLOTS - 1):
            pltpu.make_async_copy(
                obuf[s],
                y_hbm.at[pl.ds((N_ASSIGN + s * MOE_BLOCK) * PACK_ROWS, BLOCK_PACK_ROWS)],
                ssem.at[s]).start()
        resident = pltpu.make_async_copy(n2_hbm, n2_vmem, nsem)
        resident.start()
        _invert(dest_ref, gap_lo_ref, gap_hi_ref, inv_ref)
        resident.wait()
        gather(0, 0)

    prev = bexp_ref[jnp.maximum(i - 1, 0)]
    changed = (i == 0) | (bexp_ref[i] != prev)

    @pl.when(changed)
    def _cast_weights():
        wg_bf[...] = wg_ref[0].astype(BF16)
        wu_bf[...] = wu_ref[0].astype(BF16)
        wd_bf[...] = wd_ref[0].astype(BF16)

    def active(cur, nxt, prv):
        wait_scatter(cur)
        start_scatter(prv, jnp.where(i == 0, N_BLOCKS, i - 1))
        gather(nxt, jnp.minimum(i + 1, last))
        xb = _unpack_bf16_pairs(_load_packed_rows(xbuf[cur], MOE_BLOCK)).astype(BF16)
        hg = jnp.dot(xb, wg_bf[...], preferred_element_type=F32)
        hu = jnp.dot(xb, wu_bf[...], preferred_element_type=F32)
        hh = (hg * _sigmoid(hg) * hu).astype(BF16)
        y = jnp.dot(hh, wd_bf[...], preferred_element_type=F32)
        _store_packed_rows(obuf[cur], _pack_bf16_pairs(y), MOE_BLOCK)

    def epilogue(cur, nxt, prv):
        start_scatter(cur, i)
        wait_scatter(nxt)
        wait_scatter(prv)
        wait_scatter(cur)

    for cur in range(N_SLOTS):
        slots = (cur, (cur + 1) % N_SLOTS, (cur + 2) % N_SLOTS)

        @pl.when((i < nused) & (i % N_SLOTS == cur))
        def _():
            active(*slots)

        @pl.when((i == last) & (i % N_SLOTS == cur))
        def _():
            epilogue(*slots)


_EXPERT_W_BYTES = 3 * D_MODEL * D_EXPERT * 4
EXPERT_VMEM_LIMIT = (
    N_TOK * PACK_COLS * 4
    + 2 * _EXPERT_W_BYTES
    + _EXPERT_W_BYTES // 2
    + 2 * N_SLOTS * MOE_BLOCK * PACK_COLS * 4
    + 4 * 1024 * 1024)


def _experts(bexp, nused, dest, gap_lo, gap_hi, n2, w_gate, w_up, w_down):
    def w_blk(i, bexp, nused, dest, gap_lo, gap_hi):
        return (bexp[i], 0, 0)

    any_spec = pl.BlockSpec(memory_space=pl.ANY)
    return pl.pallas_call(
        _expert_kernel,
        grid_spec=pltpu.PrefetchScalarGridSpec(
            num_scalar_prefetch=5,
            grid=(N_BLOCKS,),
            in_specs=[
                any_spec,
                pl.BlockSpec((1, D_MODEL, D_EXPERT), w_blk),
                pl.BlockSpec((1, D_MODEL, D_EXPERT), w_blk),
                pl.BlockSpec((1, D_EXPERT, D_MODEL), w_blk),
            ],
            out_specs=any_spec,
            scratch_shapes=[
                pltpu.SMEM((INV_LEN,), jnp.int32),
                pltpu.VMEM((N_TOK * PACK_ROWS, LANES), jnp.uint32),
            ] + [pltpu.VMEM((BLOCK_PACK_ROWS, LANES), jnp.uint32)] * (2 * N_SLOTS) + [
                pltpu.VMEM((D_MODEL, D_EXPERT), BF16),
                pltpu.VMEM((D_MODEL, D_EXPERT), BF16),
                pltpu.VMEM((D_EXPERT, D_MODEL), BF16),
                pltpu.SemaphoreType.DMA,
                pltpu.SemaphoreType.DMA((N_SLOTS,)),
            ],
        ),
        out_shape=jax.ShapeDtypeStruct(((N_ASSIGN + DUMMY_ROWS) * PACK_ROWS, LANES), jnp.uint32),
        compiler_params=pltpu.CompilerParams(
            dimension_semantics=("arbitrary",),
            vmem_limit_bytes=EXPERT_VMEM_LIMIT,
        ),
        name="experts",
    )(bexp, nused, dest, gap_lo, gap_hi, n2, w_gate, w_up, w_down)


def _combine_kernel(h1_ref, mf_ref, fg_ref, y0_ref, y1_ref, out_ref):
    rows = mf_ref[...]
    sel = (lax.broadcasted_iota(jnp.int32, (META_ROWS, LANES), 0)
           == lax.broadcasted_iota(jnp.int32, (META_ROWS, LANES), 1)).astype(BF16)
    w = sum(lax.dot_general(part, sel, TN_DIMS, preferred_element_type=F32)
            for part in _split_bf16(rows))
    y0 = _unpack_bf16_pairs(_load_packed_rows(y0_ref, TD))
    y1 = _unpack_bf16_pairs(_load_packed_rows(y1_ref, TD))
    h = h1_ref[...] + y0 * w[:, 0:1] + y1 * w[:, 1:2]
    out_ref[...] = _rms(h, fg_ref[...])


COMBINE_VMEM_LIMIT = 2 * (2 * (2 * TD * D_MODEL * 4) + 2 * (2 * TD * PACK_COLS * 4))


def _combine(h1, mf, fg, y):
    n_steps = N_TOK // TD
    tok = lambda i: (i, 0)
    return pl.pallas_call(
        _combine_kernel,
        grid=(n_steps,),
        in_specs=[
            pl.BlockSpec((TD, D_MODEL), tok),
            pl.BlockSpec((META_ROWS, TD), lambda i: (0, i)),
            pl.BlockSpec((1, D_MODEL), lambda i: (0, 0)),
            pl.BlockSpec((TD * PACK_ROWS, LANES), tok),
            pl.BlockSpec((TD * PACK_ROWS, LANES), lambda i: (i + n_steps, 0)),
        ],
        out_specs=pl.BlockSpec((TD, D_MODEL), tok),
        out_shape=jax.ShapeDtypeStruct((N_TOK, D_MODEL), F32),
        compiler_params=pltpu.CompilerParams(
            dimension_semantics=("arbitrary",),
            vmem_limit_bytes=COMBINE_VMEM_LIMIT,
        ),
        name="combine",
    )(h1, mf, fg, y, y)


def kernel(x, norm1_gain, w_in, gmlp_v_gain, gmlp_w_s, gmlp_b_s, hgrn_lower_bounds,
           hgrn_out_gain, w_out, norm2_gain, w_group_router, b_group_router,
           w_expert_router, b_expert_router, w_gate, w_up, w_down, final_gain):
    l = 0
    x2 = x.reshape(N_TOK, D_MODEL)
    bst = jnp.tile(jnp.transpose(gmlp_b_s[l]), (TS // GMLP_BLOCK, 1))
    w_router = jnp.concatenate([w_group_router[l], w_expert_router[l]], axis=1)
    w_router = jnp.pad(w_router, ((0, 0), (0, LANES - w_router.shape[1]))).astype(BF16)
    b_router = jnp.concatenate([b_group_router[l], b_expert_router[l]])
    b_router = jnp.pad(b_router, (0, ROUTER_ROWS - b_router.shape[0])).reshape(ROUTER_ROWS, 1)

    h1, n2, meta_i, meta_f, counts = _mixer(
        x2, norm1_gain[l].reshape(1, D_MODEL), w_in[l],
        gmlp_v_gain[l].reshape(1, D_A), gmlp_w_s[l], bst,
        hgrn_lower_bounds, hgrn_out_gain[l].reshape(1, D_B), w_out[l],
        norm2_gain[l].reshape(1, D_MODEL), w_router, b_router)

    cnt = counts[EXPERT_ROW0:EXPERT_ROW0 + N_EXPERTS, 0]
    padded = ((cnt + MOE_BLOCK - 1) // MOE_BLOCK) * MOE_BLOCK
    pend = jnp.cumsum(padded)
    pstart = pend - padded
    eid = meta_i[0:TOP_K]
    rank = meta_i[TOP_K:2 * TOP_K]
    base = jnp.sum(jnp.where(eid[:, :, None] == jnp.arange(N_EXPERTS)[None, None, :],
                             pstart[None, None, :], 0), axis=-1)
    dest = (base + rank).astype(jnp.int32).reshape(N_ASSIGN)
    gap_lo = (pstart + cnt).astype(jnp.int32)
    gap_hi = jnp.concatenate([pstart[1:], jnp.full((1,), INV_LEN)]).astype(jnp.int32)
    blk_start = jnp.arange(N_BLOCKS, dtype=jnp.int32) * MOE_BLOCK
    bexp = jnp.clip(jnp.sum(blk_start[:, None] >= pend[None, :], axis=1), 0,
                    N_EXPERTS - 1).astype(jnp.int32)
    nused = (pend[-1:] // MOE_BLOCK).astype(jnp.int32)

    y = _experts(bexp, nused, dest, gap_lo, gap_hi, n2, w_gate[l], w_up[l], w_down[l])
    out = _combine(h1, meta_f, final_gain.reshape(1, D_MODEL), y)
    return out.reshape(BATCH, SEQ, D_MODEL)
```

```python
import functools

import jax
import jax.numpy as jnp
from jax import lax
from jax.experimental import pallas as pl
from jax.experimental.pallas import tpu as pltpu

D_MODEL = 1024
BATCH = 2
SEQ = 8192
N_TOK = BATCH * SEQ
CHUNK = 64
EPS = 1e-6
D_A = 512
HEADS = 4
HEAD_DIM = 128
GMLP_BLOCK = 128
D_B = 512
D_IN = 3072
N_GROUPS = 4
EXPERTS_PER_GROUP = 8
N_EXPERTS = 32
TOP_K = 2
D_EXPERT = 512

LANES = 128
ROUTER_ROWS = 48
EXPERT_ROW0 = N_GROUPS
META_ROWS = 8
PACK_COLS = D_MODEL // 2
PACK_ROWS = PACK_COLS // LANES
HI_MASK = 0xFFFF0000

TS = 256
CHUNKS_PER_STEP = TS // CHUNK
MOE_BLOCK = 256
N_BLOCKS = N_TOK * TOP_K // MOE_BLOCK + N_EXPERTS
SORTED_ROWS = N_BLOCKS * MOE_BLOCK
N_ASSIGN = N_TOK * TOP_K
N_SLOTS = 3
DUMMY_ROWS = N_SLOTS * MOE_BLOCK
BLOCK_PACK_ROWS = MOE_BLOCK * PACK_ROWS
INV_LEN = SORTED_ROWS + MOE_BLOCK
TD = 1024

F32 = jnp.float32
BF16 = jnp.bfloat16
NT_DIMS = (((1,), (1,)), ((), ()))
TN_DIMS = (((0,), (0,)), ((), ()))


def _sigmoid(x):
    return 1.0 / (1.0 + jnp.exp(-x))


def _gelu(x):
    return 0.5 * x * (1.0 + lax.erf(x * (2.0 ** -0.5)))


def _rms(x, gain):
    return x * lax.rsqrt(jnp.mean(x * x, axis=-1, keepdims=True) + EPS) * gain


def _pack_bf16_pairs(x):
    lo = pltpu.bitcast(x[:, 0:PACK_COLS].astype(BF16).astype(F32), jnp.uint32) >> 16
    hi = (pltpu.bitcast(x[:, PACK_COLS:D_MODEL].astype(BF16).astype(F32), jnp.uint32)
          & jnp.uint32(HI_MASK))
    return lo | hi


def _unpack_bf16_pairs(words):
    lo = pltpu.bitcast(words << 16, F32)
    hi = pltpu.bitcast(words & jnp.uint32(HI_MASK), F32)
    return jnp.concatenate([lo, hi], axis=1)


def _store_packed_rows(ref, words, rows):
    for c in range(PACK_ROWS):
        ref[pl.ds(c, rows, stride=PACK_ROWS), :] = words[:, c * LANES:(c + 1) * LANES]


def _load_packed_rows(ref, rows):
    return jnp.concatenate(
        [ref[pl.ds(c, rows, stride=PACK_ROWS), :] for c in range(PACK_ROWS)], axis=1)


def _split_bf16(x):
    hi = x.astype(BF16)
    lo = (x - hi.astype(F32)).astype(BF16)
    return hi, lo


def _mixer_kernel(*refs):
    for parity in range(2):
        pl.when(pl.program_id(0) % 2 == parity)(functools.partial(_mixer_step, parity, *refs))


def _mixer_step(cur, x_hbm, xn_ref, g1_ref, win_hbm, vg_ref, ws_ref, bst_ref, lbraw_ref, og_ref,
                wout_hbm, g2_ref, wr_ref, br_ref,
                h1_ref, n2_ref, mi_ref, mf_ref, cnt_ref,
                st_ref, carry_ref, wm_ref, mix_ref, proj_ref, logit_ref, xkeep_ref, xsem,
                win_ref, wout_ref, wstage_ref, wsem):
    i = pl.program_id(0)

    @pl.when(i == 0)
    def _init():
        carry_ref[...] = jnp.zeros_like(carry_ref)
        logit_ref[...] = jnp.zeros_like(logit_ref)
        n_in = D_IN // D_A
        pieces = ([(win_hbm, win_ref, c) for c in range(n_in)]
                  + [(wout_hbm, wout_ref, c) for c in range(D_MODEL // D_A)])

        def fetch(k):
            src, _, c = pieces[k]
            return pltpu.make_async_copy(src.at[:, pl.ds(c * D_A, D_A)], wstage_ref.at[k % 2],
                                         wsem.at[k % 2])

        fetch(0).start()
        for k, (_, dst, c) in enumerate(pieces):
            if k + 1 < len(pieces):
                fetch(k + 1).start()
            fetch(k).wait()
            dst[:, c * D_A:(c + 1) * D_A] = wstage_ref[k % 2].astype(BF16)
        r = lax.broadcasted_iota(jnp.int32, (GMLP_BLOCK, GMLP_BLOCK), 0)
        c = lax.broadcasted_iota(jnp.int32, (GMLP_BLOCK, GMLP_BLOCK), 1)
        keep = (c // CHUNK) <= (r // CHUNK)
        wm_ref[...] = jnp.zeros_like(wm_ref)
        for g in range(HEADS):
            w = jnp.where(keep, ws_ref[g], 0.0).astype(BF16)
            for p in range(TS // GMLP_BLOCK):
                wm_ref[g, p * GMLP_BLOCK:(p + 1) * GMLP_BLOCK,
                       p * GMLP_BLOCK:(p + 1) * GMLP_BLOCK] = w

    @pl.when(i % (SEQ // TS) == 0)
    def _reset_state():
        st_ref[...] = jnp.zeros_like(st_ref)

    def project(n1, slot, c):
        proj_ref[slot, :, c * D_A:(c + 1) * D_A] = jnp.dot(
            n1, win_ref[:, c * D_A:(c + 1) * D_A], preferred_element_type=F32)

    @pl.when(i == 0)
    def _first_projection():
        first = pltpu.make_async_copy(x_hbm.at[pl.ds(0, TS)], xkeep_ref.at[0], xsem)
        first.start()
        first.wait()
        n1 = _rms(xkeep_ref[0], g1_ref[...]).astype(BF16)
        for c in range(D_IN // D_A):
            project(n1, 0, c)

    x_next = xn_ref[...]
    n1_next = _rms(x_next, g1_ref[...]).astype(BF16)
    xkeep_ref[1 - cur] = x_next

    project(n1_next, 1 - cur, 0)
    gu = _gelu(proj_ref[cur, :, 0:D_A])
    project(n1_next, 1 - cur, 1)
    gv = _gelu(proj_ref[cur, :, D_A:2 * D_A])
    for g in range(HEADS):
        sl = slice(g * HEAD_DIM, (g + 1) * HEAD_DIM)
        vh = _rms(gv[:, sl], vg_ref[:, sl]).astype(BF16)
        mixed = jnp.dot(wm_ref[g], vh, preferred_element_type=F32) + bst_ref[:, g:g + 1]
        mix_ref[:, sl] = (gu[:, sl] * mixed).astype(BF16)

    _route(logit_ref[1 - cur], jnp.maximum(i - 1, 0), jnp.where(i > 0, 1.0, 0.0),
           carry_ref, cnt_ref, mi_ref, mf_ref)

    o2 = 2 * D_A
    qr = proj_ref[cur, :, o2:o2 + D_B]
    fr = proj_ref[cur, :, o2 + D_B:o2 + 2 * D_B]
    vv = proj_ref[cur, :, o2 + 2 * D_B:o2 + 3 * D_B].astype(BF16)
    gr = proj_ref[cur, :, o2 + 3 * D_B:o2 + 4 * D_B]

    lbr = lbraw_ref[...]
    lbm = jnp.max(lbr, axis=0, keepdims=True)
    lbe = jnp.exp(lbr - lbm)
    lb = lbe[0:1, :] / jnp.sum(lbe, axis=0, keepdims=True)

    project(n1_next, 1 - cur, 2)
    q = qr * _sigmoid(qr)
    f = lb + (1.0 - lb) * _sigmoid(fr)
    k = 1.0 - f
    lf = jnp.log(f)

    row = lax.broadcasted_iota(jnp.int32, (TS, TS), 0)
    col = lax.broadcasted_iota(jnp.int32, (TS, TS), 1)
    causal = (row >= col) & ((row // CHUNK) == (col // CHUNK))
    tri = jnp.where(causal, 1.0, 0.0).astype(BF16)
    lf_hi, lf_lo = _split_bf16(lf)
    b = (jnp.dot(tri, lf_hi, preferred_element_type=F32)
         + jnp.dot(tri, lf_lo, preferred_element_type=F32))

    def chunk_rows(r):
        return jnp.concatenate(
            [jnp.broadcast_to(b[c * CHUNK + r:c * CHUNK + r + 1, :], (CHUNK, D_B))
             for c in range(CHUNKS_PER_STEP)], axis=0)

    project(n1_next, 1 - cur, 3)
    bref = chunk_rows(CHUNK // 2 - 1)
    blast = chunk_rows(CHUNK - 1)
    qe = (q * jnp.exp(b - bref)).astype(BF16)
    ke = (k * jnp.exp(bref - b)).astype(BF16)
    kd = (k * jnp.exp(blast - b)).astype(BF16)
    qb = (q * jnp.exp(b)).astype(BF16)

    head_sl = [slice(h * HEAD_DIM, (h + 1) * HEAD_DIM) for h in range(HEADS)]
    chunk_sl = [slice(c * CHUNK, (c + 1) * CHUNK) for c in range(CHUNKS_PER_STEP)]
    scores = [lax.dot_general(qe[:, sl], ke[:, sl], NT_DIMS, preferred_element_type=F32)
              for sl in head_sl]
    upd = [[lax.dot_general(vv[rows, sl], kd[rows, sl], TN_DIMS, preferred_element_type=F32)
            for rows in chunk_sl] for sl in head_sl]
    project(n1_next, 1 - cur, 4)
    scores = [jnp.where(causal, sc, 0.0).astype(BF16) for sc in scores]
    intra = [jnp.dot(scores[h], vv[:, head_sl[h]], preferred_element_type=F32)
             for h in range(HEADS)]
    entering = []
    for h in range(HEADS):
        st = st_ref[h]
        per_chunk = []
        for c in range(CHUNKS_PER_STEP):
            per_chunk.append(st.astype(BF16))
            decay = jnp.exp(b[(c + 1) * CHUNK - 1:(c + 1) * CHUNK, head_sl[h]])
            st = st * decay + upd[h][c]
        st_ref[h] = st
        entering.append(per_chunk)
    inter = [jnp.concatenate(
        [lax.dot_general(qb[chunk_sl[c], head_sl[h]], entering[h][c], NT_DIMS,
                         preferred_element_type=F32) for c in range(CHUNKS_PER_STEP)], axis=0)
        for h in range(HEADS)]
    for h in range(HEADS):
        o = _rms(intra[h] + inter[h], og_ref[:, head_sl[h]])
        g_h = gr[:, head_sl[h]]
        mix_ref[:, D_A + h * HEAD_DIM:D_A + (h + 1) * HEAD_DIM] = (
            o * (g_h * _sigmoid(g_h))).astype(BF16)

    h1 = xkeep_ref[cur] + jnp.dot(mix_ref[...], wout_ref[...], preferred_element_type=F32)
    h1_ref[...] = h1
    n2 = _rms(h1, g2_ref[...])
    _store_packed_rows(n2_ref, _pack_bf16_pairs(n2), TS)

    logits = jnp.dot(n2.astype(BF16), wr_ref[...], preferred_element_type=F32)
    logits = jnp.transpose(logits)[0:ROUTER_ROWS, :] + br_ref[...]
    project(n1_next, 1 - cur, 5)
    logit_ref[cur] = logits

    @pl.when(i == N_TOK // TS - 1)
    def _route_last_tile():
        _route(logits, i, 1.0, carry_ref, cnt_ref, mi_ref, mf_ref)


def _route(logits, tile, live, carry_ref, cnt_ref, mi_ref, mf_ref):
    rid = lax.broadcasted_iota(jnp.int32, (ROUTER_ROWS, TS), 0)
    row = lax.broadcasted_iota(jnp.int32, (TS, TS), 0)
    col = lax.broadcasted_iota(jnp.int32, (TS, TS), 1)
    neg = -jnp.inf
    gl = jnp.where(rid < N_GROUPS, logits, neg)
    gmax = jnp.max(gl, axis=0, keepdims=True)
    g_idx = jnp.min(jnp.where(gl == gmax, rid, ROUTER_ROWS), axis=0, keepdims=True)
    g_prob = 1.0 / jnp.sum(jnp.exp(gl - gmax), axis=0, keepdims=True)
    e_lo = EXPERT_ROW0 + g_idx * EXPERTS_PER_GROUP
    el = jnp.where((rid >= e_lo) & (rid < e_lo + EXPERTS_PER_GROUP), logits, neg)
    m1 = jnp.max(el, axis=0, keepdims=True)
    i1 = jnp.min(jnp.where(el == m1, rid, ROUTER_ROWS), axis=0, keepdims=True)
    el2 = jnp.where(rid == i1, neg, el)
    m2 = jnp.max(el2, axis=0, keepdims=True)
    i2 = jnp.min(jnp.where(el2 == m2, rid, ROUTER_ROWS), axis=0, keepdims=True)
    e21 = jnp.exp(m2 - m1)
    w1 = g_prob / (1.0 + e21)
    w2 = g_prob * e21 / (1.0 + e21)

    hit1 = rid == i1
    hit2 = rid == i2
    onehot = jnp.where(hit1 | hit2, live, 0.0)
    earlier = jnp.where(row < col, 1.0, 0.0).astype(BF16)
    seen = carry_ref[...]
    prior = (jnp.dot(onehot.astype(BF16), earlier, preferred_element_type=F32)
             + jnp.concatenate([seen] * (TS // LANES), axis=1))
    r1 = jnp.sum(jnp.where(hit1, prior, 0.0), axis=0, keepdims=True)
    r2 = jnp.sum(jnp.where(hit2, prior, 0.0), axis=0, keepdims=True)
    seen = seen + jnp.sum(onehot, axis=1, keepdims=True)
    carry_ref[...] = seen
    cnt_ref[...] = seen.astype(jnp.int32)

    cols = pl.ds(pl.multiple_of(tile * TS, TS), TS)
    pad_i = jnp.zeros((META_ROWS - 2 * TOP_K, TS), jnp.int32)
    mi_ref[:, cols] = jnp.concatenate(
        [i1 - EXPERT_ROW0, i2 - EXPERT_ROW0, r1.astype(jnp.int32), r2.astype(jnp.int32), pad_i],
        axis=0)
    mf_ref[:, cols] = jnp.concatenate([w1, w2, jnp.zeros((META_ROWS - TOP_K, TS), F32)], axis=0)


MIXER_VMEM_LIMIT = (
    (D_MODEL * D_IN + D_MODEL * D_MODEL) * 2
    + 2 * D_MODEL * D_A * 4
    + 2 * TS * D_MODEL * 4 * 2
    + 2 * TS * PACK_COLS * 4
    + 2 * 2 * META_ROWS * N_TOK * 4
    + 2 * TS * D_IN * 4 + 2 * TS * D_MODEL * 4
    + HEADS * TS * TS * 2 + TS * D_MODEL * 2
    + 16 * 1024 * 1024)


def _mixer(x2, g1, win, vg, ws, bst, lbraw, og, wout, g2, wr, br):
    n_steps = N_TOK // TS
    const2 = lambda i: (0, 0)
    tok = lambda i: (i, 0)
    return pl.pallas_call(
        _mixer_kernel,
        grid=(n_steps,),
        in_specs=[
            pl.BlockSpec(memory_space=pl.ANY),
            pl.BlockSpec((TS, D_MODEL), lambda i: (jnp.minimum(i + 1, n_steps - 1), 0)),
            pl.BlockSpec((1, D_MODEL), const2),
            pl.BlockSpec(memory_space=pl.ANY),
            pl.BlockSpec((1, D_A), const2),
            pl.BlockSpec((HEADS, GMLP_BLOCK, GMLP_BLOCK), lambda i: (0, 0, 0)),
            pl.BlockSpec((TS, HEADS), const2),
            pl.BlockSpec((2, D_B), const2),
            pl.BlockSpec((1, D_B), const2),
            pl.BlockSpec(memory_space=pl.ANY),
            pl.BlockSpec((1, D_MODEL), const2),
            pl.BlockSpec((D_MODEL, LANES), const2),
            pl.BlockSpec((ROUTER_ROWS, 1), const2),
        ],
        out_specs=[
            pl.BlockSpec((TS, D_MODEL), tok),
            pl.BlockSpec((TS * PACK_ROWS, LANES), tok),
            pl.BlockSpec((META_ROWS, N_TOK), const2),
            pl.BlockSpec((META_ROWS, N_TOK), const2),
            pl.BlockSpec((ROUTER_ROWS, LANES), const2),
        ],
        out_shape=[
            jax.ShapeDtypeStruct((N_TOK, D_MODEL), F32),
            jax.ShapeDtypeStruct((N_TOK * PACK_ROWS, LANES), jnp.uint32),
            jax.ShapeDtypeStruct((META_ROWS, N_TOK), jnp.int32),
            jax.ShapeDtypeStruct((META_ROWS, N_TOK), F32),
            jax.ShapeDtypeStruct((ROUTER_ROWS, LANES), jnp.int32),
        ],
        scratch_shapes=[
            pltpu.VMEM((HEADS, HEAD_DIM, HEAD_DIM), F32),
            pltpu.VMEM((ROUTER_ROWS, LANES), F32),
            pltpu.VMEM((HEADS, TS, TS), BF16),
            pltpu.VMEM((TS, D_MODEL), BF16),
            pltpu.VMEM((2, TS, D_IN), F32),
            pltpu.VMEM((2, ROUTER_ROWS, TS), F32),
            pltpu.VMEM((2, TS, D_MODEL), F32),
            pltpu.SemaphoreType.DMA,
            pltpu.VMEM((D_MODEL, D_IN), BF16),
            pltpu.VMEM((D_MODEL, D_MODEL), BF16),
            pltpu.VMEM((2, D_MODEL, D_A), F32),
            pltpu.SemaphoreType.DMA((2,)),
        ],
        compiler_params=pltpu.CompilerParams(
            dimension_semantics=("arbitrary",),
            vmem_limit_bytes=MIXER_VMEM_LIMIT,
        ),
        name="mixer",
    )(x2, x2, g1, win, vg, ws, bst, lbraw, og, wout, g2, wr, br)


FILL_UNROLL = 8

def _invert(dest_ref, gap_lo_ref, gap_hi_ref, inv_ref):
    for e in range(N_EXPERTS):
        first = gap_lo_ref[e] // FILL_UNROLL

        def fill(g, carry):
            for u in range(FILL_UNROLL):
                inv_ref[g * FILL_UNROLL + u] = -1
            return carry

        lax.fori_loop(first, gap_hi_ref[e] // FILL_UNROLL, fill, 0)

    def place(a, carry):
        inv_ref[dest_ref[a]] = a
        return carry

    lax.fori_loop(0, N_ASSIGN, place, 0, unroll=16)


def _expert_kernel(bexp_ref, nused_ref, dest_ref, gap_lo_ref, gap_hi_ref,
                   n2_hbm, wg_ref, wu_ref, wd_ref, y_hbm,
                   inv_ref, n2_vmem, xb0, xb1, xb2, ob0, ob1, ob2, wg_bf, wu_bf, wd_bf, nsem, ssem):
    i = pl.program_id(0)
    nused = nused_ref[0]
    last = nused - 1
    xbuf = (xb0, xb1, xb2)
    obuf = (ob0, ob1, ob2)

    def gather(s, blk):
        for j in range(MOE_BLOCK):
            tok = inv_ref[blk * MOE_BLOCK + j] & (N_TOK - 1)
            r = pl.multiple_of(tok * PACK_ROWS, PACK_ROWS)
            xbuf[s][pl.ds(j * PACK_ROWS, PACK_ROWS), :] = n2_vmem[pl.ds(r, PACK_ROWS), :]

    def start_scatter(s, blk):
        for j in range(MOE_BLOCK):
            a = inv_ref[blk * MOE_BLOCK + j]
            row = jnp.where(a >= 0, a, N_ASSIGN + s * MOE_BLOCK + j)
            r = pl.multiple_of(row * PACK_ROWS, PACK_ROWS)
            pltpu.make_async_copy(obuf[s].at[pl.ds(j * PACK_ROWS, PACK_ROWS)],
                                  y_hbm.at[pl.ds(r, PACK_ROWS)], ssem.at[s]).start()

    def wait_scatter(s):
        pltpu.make_async_copy(obuf[s], y_hbm.at[pl.ds(0, BLOCK_PACK_ROWS)],
                              ssem.at[s]).wait()

    @pl.when(i == 0)
    def _prologue():
        for o in obuf:
            o[...] = jnp.zeros_like(o)
        for s in range(N_SLOTS - 1):
            pltpu.make_async_copy(
                obuf[s],
                y_hbm.at[pl.ds((N_ASSIGN + s * MOE_BLOCK) * PACK_ROWS, BLOCK_PACK_ROWS)],
                ssem.at[s]).start()
        resident = pltpu.make_async_copy(n2_hbm, n2_vmem, nsem)
        resident.start()
        _invert(dest_ref, gap_lo_ref, gap_hi_ref, inv_ref)
        resident.wait()
        gather(0, 0)

    prev = bexp_ref[jnp.maximum(i - 1, 0)]
    changed = (i == 0) | (bexp_ref[i] != prev)

    @pl.when(changed)
    def _cast_weights():
        wg_bf[...] = wg_ref[0].astype(BF16)
        wu_bf[...] = wu_ref[0].astype(BF16)
        wd_bf[...] = wd_ref[0].astype(BF16)

    def active(cur, nxt, prv):
        wait_scatter(cur)
        start_scatter(prv, jnp.where(i == 0, N_BLOCKS, i - 1))
        gather(nxt, jnp.minimum(i + 1, last))
        xb = _unpack_bf16_pairs(_load_packed_rows(xbuf[cur], MOE_BLOCK)).astype(BF16)
        hg = jnp.dot(xb, wg_bf[...], preferred_element_type=F32)
        hu = jnp.dot(xb, wu_bf[...], preferred_element_type=F32)
        hh = (hg * _sigmoid(hg) * hu).astype(BF16)
        y = jnp.dot(hh, wd_bf[...], preferred_element_type=F32)
        _store_packed_rows(obuf[cur], _pack_bf16_pairs(y), MOE_BLOCK)

    def epilogue(cur, nxt, prv):
        start_scatter(cur, i)
        wait_scatter(nxt)
        wait_scatter(prv)
        wait_scatter(cur)

    for cur in range(N_SLOTS):
        slots = (cur, (cur + 1) % N_SLOTS, (cur + 2) % N_SLOTS)

        @pl.when((i < nused) & (i % N_SLOTS == cur))
        def _():
            active(*slots)

        @pl.when((i == last) & (i % N_SLOTS == cur))
        def _():
            epilogue(*slots)


_EXPERT_W_BYTES = 3 * D_MODEL * D_EXPERT * 4
EXPERT_VMEM_LIMIT = (
    N_TOK * PACK_COLS * 4
    + 2 * _EXPERT_W_BYTES
    + _EXPERT_W_BYTES // 2
    + 2 * N_SLOTS * MOE_BLOCK * PACK_COLS * 4
    + 4 * 1024 * 1024)


def _experts(bexp, nused, dest, gap_lo, gap_hi, n2, w_gate, w_up, w_down):
    def w_blk(i, bexp, nused, dest, gap_lo, gap_hi):
        return (bexp[i], 0, 0)

    any_spec = pl.BlockSpec(memory_space=pl.ANY)
    return pl.pallas_call(
        _expert_kernel,
        grid_spec=pltpu.PrefetchScalarGridSpec(
            num_scalar_prefetch=5,
            grid=(N_BLOCKS,),
            in_specs=[
                any_spec,
                pl.BlockSpec((1, D_MODEL, D_EXPERT), w_blk),
                pl.BlockSpec((1, D_MODEL, D_EXPERT), w_blk),
                pl.BlockSpec((1, D_EXPERT, D_MODEL), w_blk),
            ],
            out_specs=any_spec,
            scratch_shapes=[
                pltpu.SMEM((INV_LEN,), jnp.int32),
                pltpu.VMEM((N_TOK * PACK_ROWS, LANES), jnp.uint32),
            ] + [pltpu.VMEM((BLOCK_PACK_ROWS, LANES), jnp.uint32)] * (2 * N_SLOTS) + [
                pltpu.VMEM((D_MODEL, D_EXPERT), BF16),
                pltpu.VMEM((D_MODEL, D_EXPERT), BF16),
                pltpu.VMEM((D_EXPERT, D_MODEL), BF16),
                pltpu.SemaphoreType.DMA,
                pltpu.SemaphoreType.DMA((N_SLOTS,)),
            ],
        ),
        out_shape=jax.ShapeDtypeStruct(((N_ASSIGN + DUMMY_ROWS) * PACK_ROWS, LANES), jnp.uint32),
        compiler_params=pltpu.CompilerParams(
            dimension_semantics=("arbitrary",),
            vmem_limit_bytes=EXPERT_VMEM_LIMIT,
        ),
        name="experts",
    )(bexp, nused, dest, gap_lo, gap_hi, n2, w_gate, w_up, w_down)


def _combine_kernel(h1_ref, mf_ref, fg_ref, y0_ref, y1_ref, out_ref):
    w = mf_ref[...]
    y0 = _unpack_bf16_pairs(_load_packed_rows(y0_ref, TD))
    y1 = _unpack_bf16_pairs(_load_packed_rows(y1_ref, TD))
    h = h1_ref[...] + y0 * w[:, 0:1] + y1 * w[:, 1:2]
    out_ref[...] = _rms(h, fg_ref[...])


COMBINE_VMEM_LIMIT = 2 * (2 * (2 * TD * D_MODEL * 4) + 2 * (2 * TD * PACK_COLS * 4))


def _combine(h1, mf, fg, y):
    n_steps = N_TOK // TD
    tok = lambda i: (i, 0)
    return pl.pallas_call(
        _combine_kernel,
        grid=(n_steps,),
        in_specs=[
            pl.BlockSpec((TD, D_MODEL), tok),
            pl.BlockSpec((TD, TOP_K), tok),
            pl.BlockSpec((1, D_MODEL), lambda i: (0, 0)),
            pl.BlockSpec((TD * PACK_ROWS, LANES), tok),
            pl.BlockSpec((TD * PACK_ROWS, LANES), lambda i: (i + n_steps, 0)),
        ],
        out_specs=pl.BlockSpec((TD, D_MODEL), tok),
        out_shape=jax.ShapeDtypeStruct((N_TOK, D_MODEL), F32),
        compiler_params=pltpu.CompilerParams(
            dimension_semantics=("arbitrary",),
            vmem_limit_bytes=COMBINE_VMEM_LIMIT,
        ),
        name="combine",
    )(h1, mf, fg, y, y)


def kernel(x, norm1_gain, w_in, gmlp_v_gain, gmlp_w_s, gmlp_b_s, hgrn_lower_bounds,
           hgrn_out_gain, w_out, norm2_gain, w_group_router, b_group_router,
           w_expert_router, b_expert_router, w_gate, w_up, w_down, final_gain):
    l = 0
    x2 = x.reshape(N_TOK, D_MODEL)
    bst = jnp.tile(jnp.transpose(gmlp_b_s[l]), (TS // GMLP_BLOCK, 1))
    w_router = jnp.concatenate([w_group_router[l], w_expert_router[l]], axis=1)
    w_router = jnp.pad(w_router, ((0, 0), (0, LANES - w_router.shape[1]))).astype(BF16)
    b_router = jnp.concatenate([b_group_router[l], b_expert_router[l]])
    b_router = jnp.pad(b_router, (0, ROUTER_ROWS - b_router.shape[0])).reshape(ROUTER_ROWS, 1)

    h1, n2, meta_i, meta_f, counts = _mixer(
        x2, norm1_gain[l].reshape(1, D_MODEL), w_in[l],
        gmlp_v_gain[l].reshape(1, D_A), gmlp_w_s[l], bst,
        hgrn_lower_bounds, hgrn_out_gain[l].reshape(1, D_B), w_out[l],
        norm2_gain[l].reshape(1, D_MODEL), w_router, b_router)

    cnt = counts[EXPERT_ROW0:EXPERT_ROW0 + N_EXPERTS, 0]
    padded = ((cnt + MOE_BLOCK - 1) // MOE_BLOCK) * MOE_BLOCK
    pend = jnp.cumsum(padded)
    pstart = pend - padded
    eid = meta_i[0:TOP_K]
    rank = meta_i[TOP_K:2 * TOP_K]
    base = jnp.sum(jnp.where(eid[:, :, None] == jnp.arange(N_EXPERTS)[None, None, :],
                             pstart[None, None, :], 0), axis=-1)
    dest = (base + rank).astype(jnp.int32).reshape(N_ASSIGN)
    gap_lo = (pstart + cnt).astype(jnp.int32)
    gap_hi = jnp.concatenate([pstart[1:], jnp.full((1,), INV_LEN)]).astype(jnp.int32)
    blk_start = jnp.arange(N_BLOCKS, dtype=jnp.int32) * MOE_BLOCK
    bexp = jnp.clip(jnp.sum(blk_start[:, None] >= pend[None, :], axis=1), 0,
                    N_EXPERTS - 1).astype(jnp.int32)
    nused = (pend[-1:] // MOE_BLOCK).astype(jnp.int32)

    y = _experts(bexp, nused, dest, gap_lo, gap_hi, n2, w_gate[l], w_up[l], w_down[l])
    out = _combine(h1, meta_f[0:TOP_K].T, final_gain.reshape(1, D_MODEL), y)
    return out.reshape(BATCH, SEQ, D_MODEL)
```

```python
import functools

import jax
import jax.numpy as jnp
from jax import lax
from jax.experimental import pallas as pl
from jax.experimental.pallas import tpu as pltpu

D_MODEL = 1024
BATCH = 2
SEQ = 8192
N_TOK = BATCH * SEQ
CHUNK = 64
EPS = 1e-6
D_A = 512
HEADS = 4
HEAD_DIM = 128
GMLP_BLOCK = 128
D_B = 512
D_IN = 3072
N_GROUPS = 4
EXPERTS_PER_GROUP = 8
N_EXPERTS = 32
TOP_K = 2
D_EXPERT = 512

LANES = 128
ROUTER_ROWS = 48
EXPERT_ROW0 = N_GROUPS
META_ROWS = 8
PACK_COLS = D_MODEL // 2
PACK_ROWS = PACK_COLS // LANES
HI_MASK = 0xFFFF0000

TS = 256
CHUNKS_PER_STEP = TS // CHUNK
MOE_BLOCK = 256
N_BLOCKS = N_TOK * TOP_K // MOE_BLOCK + N_EXPERTS
SORTED_ROWS = N_BLOCKS * MOE_BLOCK
N_ASSIGN = N_TOK * TOP_K
N_SLOTS = 3
DUMMY_ROWS = N_SLOTS * MOE_BLOCK
BLOCK_PACK_ROWS = MOE_BLOCK * PACK_ROWS
INV_LEN = SORTED_ROWS + MOE_BLOCK
TD = 1024

F32 = jnp.float32
BF16 = jnp.bfloat16
NT_DIMS = (((1,), (1,)), ((), ()))
TN_DIMS = (((0,), (0,)), ((), ()))


def _sigmoid(x):
    return 1.0 / (1.0 + jnp.exp(-x))


def _gelu(x):
    return 0.5 * x * (1.0 + lax.erf(x * (2.0 ** -0.5)))


def _rms(x, gain):
    return x * lax.rsqrt(jnp.mean(x * x, axis=-1, keepdims=True) + EPS) * gain


def _pack_bf16_pairs(x):
    lo = pltpu.bitcast(x[:, 0:PACK_COLS].astype(BF16).astype(F32), jnp.uint32) >> 16
    hi = (pltpu.bitcast(x[:, PACK_COLS:D_MODEL].astype(BF16).astype(F32), jnp.uint32)
          & jnp.uint32(HI_MASK))
    return lo | hi


def _unpack_bf16_pairs(words):
    lo = pltpu.bitcast(words << 16, F32)
    hi = pltpu.bitcast(words & jnp.uint32(HI_MASK), F32)
    return jnp.concatenate([lo, hi], axis=1)


def _store_packed_rows(ref, words, rows):
    for c in range(PACK_ROWS):
        ref[pl.ds(c, rows, stride=PACK_ROWS), :] = words[:, c * LANES:(c + 1) * LANES]


def _load_packed_rows(ref, rows):
    return jnp.concatenate(
        [ref[pl.ds(c, rows, stride=PACK_ROWS), :] for c in range(PACK_ROWS)], axis=1)


def _split_bf16(x):
    hi = x.astype(BF16)
    lo = (x - hi.astype(F32)).astype(BF16)
    return hi, lo


def _mixer_kernel(*refs):
    for parity in range(2):
        pl.when(pl.program_id(0) % 2 == parity)(functools.partial(_mixer_step, parity, *refs))


def _mixer_step(cur, x_hbm, xn_ref, g1_ref, win_hbm, vg_ref, ws_ref, bst_ref, lbraw_ref, og_ref,
                wout_hbm, g2_ref, wr_ref, br_ref,
                h1_ref, n2_ref, mi_ref, mf_ref, cnt_ref,
                st_ref, carry_ref, wm_ref, mix_ref, proj_ref, logit_ref, xkeep_ref, xsem,
                win_ref, wout_ref, wstage_ref, wsem):
    i = pl.program_id(0)

    @pl.when(i == 0)
    def _init():
        carry_ref[...] = jnp.zeros_like(carry_ref)
        logit_ref[...] = jnp.zeros_like(logit_ref)
        n_in = D_IN // D_A
        pieces = ([(win_hbm, win_ref, c) for c in range(n_in)]
                  + [(wout_hbm, wout_ref, c) for c in range(D_MODEL // D_A)])

        def fetch(k):
            src, _, c = pieces[k]
            return pltpu.make_async_copy(src.at[:, pl.ds(c * D_A, D_A)], wstage_ref.at[k % 2],
                                         wsem.at[k % 2])

        fetch(0).start()
        for k, (_, dst, c) in enumerate(pieces):
            if k + 1 < len(pieces):
                fetch(k + 1).start()
            fetch(k).wait()
            dst[:, c * D_A:(c + 1) * D_A] = wstage_ref[k % 2].astype(BF16)
        r = lax.broadcasted_iota(jnp.int32, (GMLP_BLOCK, GMLP_BLOCK), 0)
        c = lax.broadcasted_iota(jnp.int32, (GMLP_BLOCK, GMLP_BLOCK), 1)
        keep = (c // CHUNK) <= (r // CHUNK)
        wm_ref[...] = jnp.zeros_like(wm_ref)
        for g in range(HEADS):
            w = jnp.where(keep, ws_ref[g], 0.0).astype(BF16)
            for p in range(TS // GMLP_BLOCK):
                wm_ref[g, p * GMLP_BLOCK:(p + 1) * GMLP_BLOCK,
                       p * GMLP_BLOCK:(p + 1) * GMLP_BLOCK] = w

    @pl.when(i % (SEQ // TS) == 0)
    def _reset_state():
        st_ref[...] = jnp.zeros_like(st_ref)

    def project(n1, slot, c):
        proj_ref[slot, :, c * D_A:(c + 1) * D_A] = jnp.dot(
            n1, win_ref[:, c * D_A:(c + 1) * D_A], preferred_element_type=F32)

    @pl.when(i == 0)
    def _first_projection():
        first = pltpu.make_async_copy(x_hbm.at[pl.ds(0, TS)], xkeep_ref.at[0], xsem)
        first.start()
        first.wait()
        n1 = _rms(xkeep_ref[0], g1_ref[...]).astype(BF16)
        for c in range(D_IN // D_A):
            project(n1, 0, c)

    x_next = xn_ref[...]
    n1_next = _rms(x_next, g1_ref[...]).astype(BF16)
    xkeep_ref[1 - cur] = x_next

    project(n1_next, 1 - cur, 0)
    gu = _gelu(proj_ref[cur, :, 0:D_A])
    project(n1_next, 1 - cur, 1)
    gv = _gelu(proj_ref[cur, :, D_A:2 * D_A])
    for g in range(HEADS):
        sl = slice(g * HEAD_DIM, (g + 1) * HEAD_DIM)
        vh = _rms(gv[:, sl], vg_ref[:, sl]).astype(BF16)
        mixed = jnp.dot(wm_ref[g], vh, preferred_element_type=F32) + bst_ref[:, g:g + 1]
        mix_ref[:, sl] = (gu[:, sl] * mixed).astype(BF16)

    _route(logit_ref[1 - cur], jnp.maximum(i - 1, 0), jnp.where(i > 0, 1.0, 0.0),
           carry_ref, cnt_ref, mi_ref, mf_ref)

    o2 = 2 * D_A
    qr = proj_ref[cur, :, o2:o2 + D_B]
    fr = proj_ref[cur, :, o2 + D_B:o2 + 2 * D_B]
    vv = proj_ref[cur, :, o2 + 2 * D_B:o2 + 3 * D_B].astype(BF16)
    gr = proj_ref[cur, :, o2 + 3 * D_B:o2 + 4 * D_B]

    lbr = lbraw_ref[...]
    lbm = jnp.max(lbr, axis=0, keepdims=True)
    lbe = jnp.exp(lbr - lbm)
    lb = lbe[0:1, :] / jnp.sum(lbe, axis=0, keepdims=True)

    project(n1_next, 1 - cur, 2)
    q = qr * _sigmoid(qr)
    f = lb + (1.0 - lb) * _sigmoid(fr)
    k = 1.0 - f
    lf = jnp.log(f)

    row = lax.broadcasted_iota(jnp.int32, (TS, TS), 0)
    col = lax.broadcasted_iota(jnp.int32, (TS, TS), 1)
    causal = (row >= col) & ((row // CHUNK) == (col // CHUNK))
    tri = jnp.where(causal, 1.0, 0.0).astype(BF16)
    lf_hi, lf_lo = _split_bf16(lf)
    b = (jnp.dot(tri, lf_hi, preferred_element_type=F32)
         + jnp.dot(tri, lf_lo, preferred_element_type=F32))

    def chunk_rows(r):
        return jnp.concatenate(
            [jnp.broadcast_to(b[c * CHUNK + r:c * CHUNK + r + 1, :], (CHUNK, D_B))
             for c in range(CHUNKS_PER_STEP)], axis=0)

    project(n1_next, 1 - cur, 3)
    bref = chunk_rows(CHUNK // 2 - 1)
    blast = chunk_rows(CHUNK - 1)
    qe = (q * jnp.exp(b - bref)).astype(BF16)
    ke = (k * jnp.exp(bref - b)).astype(BF16)
    kd = (k * jnp.exp(blast - b)).astype(BF16)
    qb = (q * jnp.exp(b)).astype(BF16)

    head_sl = [slice(h * HEAD_DIM, (h + 1) * HEAD_DIM) for h in range(HEADS)]
    chunk_sl = [slice(c * CHUNK, (c + 1) * CHUNK) for c in range(CHUNKS_PER_STEP)]
    scores = [lax.dot_general(qe[:, sl], ke[:, sl], NT_DIMS, preferred_element_type=F32)
              for sl in head_sl]
    upd = [[lax.dot_general(vv[rows, sl], kd[rows, sl], TN_DIMS, preferred_element_type=F32)
            for rows in chunk_sl] for sl in head_sl]
    project(n1_next, 1 - cur, 4)
    scores = [jnp.where(causal, sc, 0.0).astype(BF16) for sc in scores]
    intra = [jnp.dot(scores[h], vv[:, head_sl[h]], preferred_element_type=F32)
             for h in range(HEADS)]
    entering = []
    for h in range(HEADS):
        st = st_ref[h]
        per_chunk = []
        for c in range(CHUNKS_PER_STEP):
            per_chunk.append(st.astype(BF16))
            decay = jnp.exp(b[(c + 1) * CHUNK - 1:(c + 1) * CHUNK, head_sl[h]])
            st = st * decay + upd[h][c]
        st_ref[h] = st
        entering.append(per_chunk)
    inter = [jnp.concatenate(
        [lax.dot_general(qb[chunk_sl[c], head_sl[h]], entering[h][c], NT_DIMS,
                         preferred_element_type=F32) for c in range(CHUNKS_PER_STEP)], axis=0)
        for h in range(HEADS)]
    for h in range(HEADS):
        o = _rms(intra[h] + inter[h], og_ref[:, head_sl[h]])
        g_h = gr[:, head_sl[h]]
        mix_ref[:, D_A + h * HEAD_DIM:D_A + (h + 1) * HEAD_DIM] = (
            o * (g_h * _sigmoid(g_h))).astype(BF16)

    h1 = xkeep_ref[cur] + jnp.dot(mix_ref[...], wout_ref[...], preferred_element_type=F32)
    h1_ref[...] = h1
    n2 = _rms(h1, g2_ref[...])
    _store_packed_rows(n2_ref, _pack_bf16_pairs(n2), TS)

    logits = jnp.dot(n2.astype(BF16), wr_ref[...], preferred_element_type=F32)
    logits = jnp.transpose(logits)[0:ROUTER_ROWS, :] + br_ref[...]
    project(n1_next, 1 - cur, 5)
    logit_ref[cur] = logits

    @pl.when(i == N_TOK // TS - 1)
    def _route_last_tile():
        _route(logits, i, 1.0, carry_ref, cnt_ref, mi_ref, mf_ref)


def _route(logits, tile, live, carry_ref, cnt_ref, mi_ref, mf_ref):
    rid = lax.broadcasted_iota(jnp.int32, (ROUTER_ROWS, TS), 0)
    row = lax.broadcasted_iota(jnp.int32, (TS, TS), 0)
    col = lax.broadcasted_iota(jnp.int32, (TS, TS), 1)
    neg = -jnp.inf
    gl = jnp.where(rid < N_GROUPS, logits, neg)
    gmax = jnp.max(gl, axis=0, keepdims=True)
    g_idx = jnp.min(jnp.where(gl == gmax, rid, ROUTER_ROWS), axis=0, keepdims=True)
    g_prob = 1.0 / jnp.sum(jnp.exp(gl - gmax), axis=0, keepdims=True)
    e_lo = EXPERT_ROW0 + g_idx * EXPERTS_PER_GROUP
    el = jnp.where((rid >= e_lo) & (rid < e_lo + EXPERTS_PER_GROUP), logits, neg)
    m1 = jnp.max(el, axis=0, keepdims=True)
    i1 = jnp.min(jnp.where(el == m1, rid, ROUTER_ROWS), axis=0, keepdims=True)
    el2 = jnp.where(rid == i1, neg, el)
    m2 = jnp.max(el2, axis=0, keepdims=True)
    i2 = jnp.min(jnp.where(el2 == m2, rid, ROUTER_ROWS), axis=0, keepdims=True)
    e21 = jnp.exp(m2 - m1)
    w1 = g_prob / (1.0 + e21)
    w2 = g_prob * e21 / (1.0 + e21)

    hit1 = rid == i1
    hit2 = rid == i2
    onehot = jnp.where(hit1 | hit2, live, 0.0)
    earlier = jnp.where(row < col, 1.0, 0.0).astype(BF16)
    seen = carry_ref[...]
    prior = (jnp.dot(onehot.astype(BF16), earlier, preferred_element_type=F32)
             + jnp.concatenate([seen] * (TS // LANES), axis=1))
    r1 = jnp.sum(jnp.where(hit1, prior, 0.0), axis=0, keepdims=True)
    r2 = jnp.sum(jnp.where(hit2, prior, 0.0), axis=0, keepdims=True)
    seen = seen + jnp.sum(onehot, axis=1, keepdims=True)
    carry_ref[...] = seen
    cnt_ref[...] = seen.astype(jnp.int32)

    cols = pl.ds(pl.multiple_of(tile * TS, TS), TS)
    pad_i = jnp.zeros((META_ROWS - 2 * TOP_K, TS), jnp.int32)
    mi_ref[:, cols] = jnp.concatenate(
        [i1 - EXPERT_ROW0, i2 - EXPERT_ROW0, r1.astype(jnp.int32), r2.astype(jnp.int32), pad_i],
        axis=0)
    mf_ref[:, cols] = jnp.concatenate([w1, w2, jnp.zeros((META_ROWS - TOP_K, TS), F32)], axis=0)


MIXER_VMEM_LIMIT = (
    (D_MODEL * D_IN + D_MODEL * D_MODEL) * 2
    + 2 * D_MODEL * D_A * 4
    + 2 * TS * D_MODEL * 4 * 2
    + 2 * TS * PACK_COLS * 4
    + 2 * 2 * META_ROWS * N_TOK * 4
    + 2 * TS * D_IN * 4 + 2 * TS * D_MODEL * 4
    + HEADS * TS * TS * 2 + TS * D_MODEL * 2
    + 16 * 1024 * 1024)


def _mixer(x2, g1, win, vg, ws, bst, lbraw, og, wout, g2, wr, br):
    n_steps = N_TOK // TS
    const2 = lambda i: (0, 0)
    tok = lambda i: (i, 0)
    return pl.pallas_call(
        _mixer_kernel,
        grid=(n_steps,),
        in_specs=[
            pl.BlockSpec(memory_space=pl.ANY),
            pl.BlockSpec((TS, D_MODEL), lambda i: (jnp.minimum(i + 1, n_steps - 1), 0)),
            pl.BlockSpec((1, D_MODEL), const2),
            pl.BlockSpec(memory_space=pl.ANY),
            pl.BlockSpec((1, D_A), const2),
            pl.BlockSpec((HEADS, GMLP_BLOCK, GMLP_BLOCK), lambda i: (0, 0, 0)),
            pl.BlockSpec((TS, HEADS), const2),
            pl.BlockSpec((2, D_B), const2),
            pl.BlockSpec((1, D_B), const2),
            pl.BlockSpec(memory_space=pl.ANY),
            pl.BlockSpec((1, D_MODEL), const2),
            pl.BlockSpec((D_MODEL, LANES), const2),
            pl.BlockSpec((ROUTER_ROWS, 1), const2),
        ],
        out_specs=[
            pl.BlockSpec((TS, D_MODEL), tok),
            pl.BlockSpec((TS * PACK_ROWS, LANES), tok),
            pl.BlockSpec((META_ROWS, N_TOK), const2),
            pl.BlockSpec((META_ROWS, N_TOK), const2),
            pl.BlockSpec((ROUTER_ROWS, LANES), const2),
        ],
        out_shape=[
            jax.ShapeDtypeStruct((N_TOK, D_MODEL), F32),
            jax.ShapeDtypeStruct((N_TOK * PACK_ROWS, LANES), jnp.uint32),
            jax.ShapeDtypeStruct((META_ROWS, N_TOK), jnp.int32),
            jax.ShapeDtypeStruct((META_ROWS, N_TOK), F32),
            jax.ShapeDtypeStruct((ROUTER_ROWS, LANES), jnp.int32),
        ],
        scratch_shapes=[
            pltpu.VMEM((HEADS, HEAD_DIM, HEAD_DIM), F32),
            pltpu.VMEM((ROUTER_ROWS, LANES), F32),
            pltpu.VMEM((HEADS, TS, TS), BF16),
            pltpu.VMEM((TS, D_MODEL), BF16),
            pltpu.VMEM((2, TS, D_IN), F32),
            pltpu.VMEM((2, ROUTER_ROWS, TS), F32),
            pltpu.VMEM((2, TS, D_MODEL), F32),
            pltpu.SemaphoreType.DMA,
            pltpu.VMEM((D_MODEL, D_IN), BF16),
            pltpu.VMEM((D_MODEL, D_MODEL), BF16),
            pltpu.VMEM((2, D_MODEL, D_A), F32),
            pltpu.SemaphoreType.DMA((2,)),
        ],
        compiler_params=pltpu.CompilerParams(
            dimension_semantics=("arbitrary",),
            vmem_limit_bytes=MIXER_VMEM_LIMIT,
        ),
        name="mixer",
    )(x2, x2, g1, win, vg, ws, bst, lbraw, og, wout, g2, wr, br)


FILL_UNROLL = 8

def _invert(dest_ref, gap_lo_ref, gap_hi_ref, inv_ref):
    for e in range(N_EXPERTS):
        first = gap_lo_ref[e] // FILL_UNROLL

        def fill(g, carry):
            for u in range(FILL_UNROLL):
                inv_ref[g * FILL_UNROLL + u] = -1
            return carry

        lax.fori_loop(first, gap_hi_ref[e] // FILL_UNROLL, fill, 0)

    def place(a, carry):
        inv_ref[dest_ref[a]] = a
        return carry

    lax.fori_loop(0, N_ASSIGN, place, 0, unroll=16)


def _expert_kernel(bexp_ref, nused_ref, dest_ref, gap_lo_ref, gap_hi_ref,
                   n2_hbm, wg_ref, wu_ref, wd_ref, y_hbm,
                   inv_ref, n2_vmem, xb0, xb1, xb2, ob0, ob1, ob2, wg_bf, wu_bf, wd_bf, nsem, ssem):
    i = pl.program_id(0)
    nused = nused_ref[0]
    last = nused - 1
    xbuf = (xb0, xb1, xb2)
    obuf = (ob0, ob1, ob2)

    def gather(s, blk):
        for j in range(MOE_BLOCK):
            tok = inv_ref[blk * MOE_BLOCK + j] & (N_TOK - 1)
            r = pl.multiple_of(tok * PACK_ROWS, PACK_ROWS)
            xbuf[s][pl.ds(j * PACK_ROWS, PACK_ROWS), :] = n2_vmem[pl.ds(r, PACK_ROWS), :]

    def start_scatter(s, blk):
        for j in range(MOE_BLOCK):
            a = inv_ref[blk * MOE_BLOCK + j]
            row = jnp.where(a >= 0, a, N_ASSIGN + s * MOE_BLOCK + j)
            r = pl.multiple_of(row * PACK_ROWS, PACK_ROWS)
            pltpu.make_async_copy(obuf[s].at[pl.ds(j * PACK_ROWS, PACK_ROWS)],
                                  y_hbm.at[pl.ds(r, PACK_ROWS)], ssem.at[s]).start()

    def wait_scatter(s):
        pltpu.make_async_copy(obuf[s], y_hbm.at[pl.ds(0, BLOCK_PACK_ROWS)],
                              ssem.at[s]).wait()

    @pl.when(i == 0)
    def _prologue():
        for o in obuf:
            o[...] = jnp.zeros_like(o)
        for s in range(N_SLOTS - 1):
            pltpu.make_async_copy(
                obuf[s],
                y_hbm.at[pl.ds((N_ASSIGN + s * MOE_BLOCK) * PACK_ROWS, BLOCK_PACK_ROWS)],
                ssem.at[s]).start()
        resident = pltpu.make_async_copy(n2_hbm, n2_vmem, nsem)
        resident.start()
        _invert(dest_ref, gap_lo_ref, gap_hi_ref, inv_ref)
        resident.wait()
        gather(0, 0)

    prev = bexp_ref[jnp.maximum(i - 1, 0)]
    changed = (i == 0) | (bexp_ref[i] != prev)

    @pl.when(changed)
    def _cast_weights():
        wg_bf[...] = wg_ref[0].astype(BF16)
        wu_bf[...] = wu_ref[0].astype(BF16)
        wd_bf[...] = wd_ref[0].astype(BF16)

    def active(cur, nxt, prv):
        wait_scatter(cur)
        start_scatter(prv, jnp.where(i == 0, N_BLOCKS, i - 1))
        gather(nxt, jnp.minimum(i + 1, last))
        xb = _unpack_bf16_pairs(_load_packed_rows(xbuf[cur], MOE_BLOCK)).astype(BF16)
        hg = jnp.dot(xb, wg_bf[...], preferred_element_type=F32)
        hu = jnp.dot(xb, wu_bf[...], preferred_element_type=F32)
        hh = (hg * _sigmoid(hg) * hu).astype(BF16)
        y = jnp.dot(hh, wd_bf[...], preferred_element_type=F32)
        _store_packed_rows(obuf[cur], _pack_bf16_pairs(y), MOE_BLOCK)

    def epilogue(cur, nxt, prv):
        start_scatter(cur, i)
        wait_scatter(nxt)
        wait_scatter(prv)
        wait_scatter(cur)

    for cur in range(N_SLOTS):
        slots = (cur, (cur + 1) % N_SLOTS, (cur + 2) % N_SLOTS)

        @pl.when((i < nused) & (i % N_SLOTS == cur))
        def _():
            active(*slots)

        @pl.when((i == last) & (i % N_SLOTS == cur))
        def _():
            epilogue(*slots)


_EXPERT_W_BYTES = 3 * D_MODEL * D_EXPERT * 4
EXPERT_VMEM_LIMIT = (
    N_TOK * PACK_COLS * 4
    + 2 * _EXPERT_W_BYTES
    + _EXPERT_W_BYTES // 2
    + 2 * N_SLOTS * MOE_BLOCK * PACK_COLS * 4
    + 4 * 1024 * 1024)


def _experts(bexp, nused, dest, gap_lo, gap_hi, n2, w_gate, w_up, w_down):
    def w_blk(i, bexp, nused, dest, gap_lo, gap_hi):
        return (bexp[i], 0, 0)

    any_spec = pl.BlockSpec(memory_space=pl.ANY)
    return pl.pallas_call(
        _expert_kernel,
        grid_spec=pltpu.PrefetchScalarGridSpec(
            num_scalar_prefetch=5,
            grid=(N_BLOCKS,),
            in_specs=[
                any_spec,
                pl.BlockSpec((1, D_MODEL, D_EXPERT), w_blk),
                pl.BlockSpec((1, D_MODEL, D_EXPERT), w_blk),
                pl.BlockSpec((1, D_EXPERT, D_MODEL), w_blk),
            ],
            out_specs=any_spec,
            scratch_shapes=[
                pltpu.SMEM((INV_LEN,), jnp.int32),
                pltpu.VMEM((N_TOK * PACK_ROWS, LANES), jnp.uint32),
            ] + [pltpu.VMEM((BLOCK_PACK_ROWS, LANES), jnp.uint32)] * (2 * N_SLOTS) + [
                pltpu.VMEM((D_MODEL, D_EXPERT), BF16),
                pltpu.VMEM((D_MODEL, D_EXPERT), BF16),
                pltpu.VMEM((D_EXPERT, D_MODEL), BF16),
                pltpu.SemaphoreType.DMA,
                pltpu.SemaphoreType.DMA((N_SLOTS,)),
            ],
        ),
        out_shape=jax.ShapeDtypeStruct(((N_ASSIGN + DUMMY_ROWS) * PACK_ROWS, LANES), jnp.uint32),
        compiler_params=pltpu.CompilerParams(
            dimension_semantics=("arbitrary",),
            vmem_limit_bytes=EXPERT_VMEM_LIMIT,
        ),
        name="experts",
    )(bexp, nused, dest, gap_lo, gap_hi, n2, w_gate, w_up, w_down)


def _combine_kernel(h1_ref, mf_ref, fg_ref, y0_ref, y1_ref, out_ref):
    rows = mf_ref[...]
    sel = (lax.broadcasted_iota(jnp.int32, (META_ROWS, LANES), 0)
           == lax.broadcasted_iota(jnp.int32, (META_ROWS, LANES), 1)).astype(BF16)
    w = sum(lax.dot_general(part, sel, TN_DIMS, preferred_element_type=F32)
            for part in _split_bf16(rows))
    y0 = _unpack_bf16_pairs(_load_packed_rows(y0_ref, TD))
    y1 = _unpack_bf16_pairs(_load_packed_rows(y1_ref, TD))
    h = h1_ref[...] + y0 * w[:, 0:1] + y1 * w[:, 1:2]
    out_ref[...] = _rms(h, fg_ref[...])


COMBINE_VMEM_LIMIT = 2 * (2 * (2 * TD * D_MODEL * 4) + 2 * (2 * TD * PACK_COLS * 4))


def _combine(h1, mf, fg, y):
    n_steps = N_TOK // TD
    tok = lambda i: (i, 0)
    return pl.pallas_call(
        _combine_kernel,
        grid=(n_steps,),
        in_specs=[
            pl.BlockSpec((TD, D_MODEL), tok),
            pl.BlockSpec((META_ROWS, TD), lambda i: (0, i)),
            pl.BlockSpec((1, D_MODEL), lambda i: (0, 0)),
            pl.BlockSpec((TD * PACK_ROWS, LANES), tok),
            pl.BlockSpec((TD * PACK_ROWS, LANES), lambda i: (i + n_steps, 0)),
        ],
        out_specs=pl.BlockSpec((TD, D_MODEL), tok),
        out_shape=jax.ShapeDtypeStruct((N_TOK, D_MODEL), F32),
        compiler_params=pltpu.CompilerParams(
            dimension_semantics=("arbitrary",),
            vmem_limit_bytes=COMBINE_VMEM_LIMIT,
        ),
        name="combine",
    )(h1, mf, fg, y, y)


def kernel(x, norm1_gain, w_in, gmlp_v_gain, gmlp_w_s, gmlp_b_s, hgrn_lower_bounds,
           hgrn_out_gain, w_out, norm2_gain, w_group_router, b_group_router,
           w_expert_router, b_expert_router, w_gate, w_up, w_down, final_gain):
    l = 0
    x2 = x.reshape(N_TOK, D_MODEL)
    bst = jnp.tile(jnp.transpose(gmlp_b_s[l]), (TS // GMLP_BLOCK, 1))
    w_router = jnp.concatenate([w_group_router[l], w_expert_router[l]], axis=1)
    w_router = jnp.pad(w_router, ((0, 0), (0, LANES - w_router.shape[1]))).astype(BF16)
    b_router = jnp.concatenate([b_group_router[l], b_expert_router[l]])
    b_router = jnp.pad(b_router, (0, ROUTER_ROWS - b_router.shape[0])).reshape(ROUTER_ROWS, 1)

    h1, n2, meta_i, meta_f, counts = _mixer(
        x2, norm1_gain[l].reshape(1, D_MODEL), w_in[l],
        gmlp_v_gain[l].reshape(1, D_A), gmlp_w_s[l], bst,
        hgrn_lower_bounds, hgrn_out_gain[l].reshape(1, D_B), w_out[l],
        norm2_gain[l].reshape(1, D_MODEL), w_router, b_router)

    cnt = counts[EXPERT_ROW0:EXPERT_ROW0 + N_EXPERTS, 0]
    padded = ((cnt + MOE_BLOCK - 1) // MOE_BLOCK) * MOE_BLOCK
    pend = jnp.cumsum(padded)
    pstart = pend - padded
    eid = meta_i[0:TOP_K]
    rank = meta_i[TOP_K:2 * TOP_K]
    base = jnp.sum(jnp.where(eid[:, :, None] == jnp.arange(N_EXPERTS)[None, None, :],
                             pstart[None, None, :], 0), axis=-1)
    dest = (base + rank).astype(jnp.int32).reshape(N_ASSIGN)
    gap_lo = (pstart + cnt).astype(jnp.int32)
    gap_hi = jnp.concatenate([pstart[1:], jnp.full((1,), INV_LEN)]).astype(jnp.int32)
    blk_start = jnp.arange(N_BLOCKS, dtype=jnp.int32) * MOE_BLOCK
    bexp = jnp.clip(jnp.sum(blk_start[:, None] >= pend[None, :], axis=1), 0,
                    N_EXPERTS - 1).astype(jnp.int32)
    nused = (pend[-1:] // MOE_BLOCK).astype(jnp.int32)

    y = _experts(bexp, nused, dest, gap_lo, gap_hi, n2, w_gate[l], w_up[l], w_down[l])
    out = _combine(h1, meta_f, final_gain.reshape(1, D_MODEL), y)
    return out.reshape(BATCH, SEQ, D_MODEL)
```

```python
import functools

import jax
import jax.numpy as jnp
from jax import lax
from jax.experimental import pallas as pl
from jax.experimental.pallas import tpu as pltpu

D_MODEL = 1024
BATCH = 2
SEQ = 8192
N_TOK = BATCH * SEQ
CHUNK = 64
EPS = 1e-6
D_A = 512
HEADS = 4
HEAD_DIM = 128
GMLP_BLOCK = 128
D_B = 512
D_IN = 3072
N_GROUPS = 4
EXPERTS_PER_GROUP = 8
N_EXPERTS = 32
TOP_K = 2
D_EXPERT = 512

LANES = 128
ROUTER_ROWS = 48
EXPERT_ROW0 = N_GROUPS
META_ROWS = 8
PACK_COLS = D_MODEL // 2
PACK_ROWS = PACK_COLS // LANES
HI_MASK = 0xFFFF0000

TS = 256
CHUNKS_PER_STEP = TS // CHUNK
MOE_BLOCK = 256
N_BLOCKS = N_TOK * TOP_K // MOE_BLOCK + N_EXPERTS
SORTED_ROWS = N_BLOCKS * MOE_BLOCK
N_ASSIGN = N_TOK * TOP_K
N_SLOTS = 3
DUMMY_ROWS = N_SLOTS * MOE_BLOCK
BLOCK_PACK_ROWS = MOE_BLOCK * PACK_ROWS
INV_LEN = SORTED_ROWS + MOE_BLOCK
TD = 1024

F32 = jnp.float32
BF16 = jnp.bfloat16
NT_DIMS = (((1,), (1,)), ((), ()))
TN_DIMS = (((0,), (0,)), ((), ()))


def _sigmoid(x):
    return 1.0 / (1.0 + jnp.exp(-x))


def _gelu(x):
    return 0.5 * x * (1.0 + lax.erf(x * (2.0 ** -0.5)))


def _rms(x, gain):
    return x * lax.rsqrt(jnp.mean(x * x, axis=-1, keepdims=True) + EPS) * gain


def _pack_bf16_pairs(x):
    lo = pltpu.bitcast(x[:, 0:PACK_COLS].astype(BF16).astype(F32), jnp.uint32) >> 16
    hi = (pltpu.bitcast(x[:, PACK_COLS:D_MODEL].astype(BF16).astype(F32), jnp.uint32)
          & jnp.uint32(HI_MASK))
    return lo | hi


def _unpack_bf16_pairs(words):
    lo = pltpu.bitcast(words << 16, F32)
    hi = pltpu.bitcast(words & jnp.uint32(HI_MASK), F32)
    return jnp.concatenate([lo, hi], axis=1)


def _store_packed_rows(ref, words, rows):
    for c in range(PACK_ROWS):
        ref[pl.ds(c, rows, stride=PACK_ROWS), :] = words[:, c * LANES:(c + 1) * LANES]


def _load_packed_rows(ref, rows):
    return jnp.concatenate(
        [ref[pl.ds(c, rows, stride=PACK_ROWS), :] for c in range(PACK_ROWS)], axis=1)


def _split_bf16(x):
    hi = x.astype(BF16)
    lo = (x - hi.astype(F32)).astype(BF16)
    return hi, lo


def _mixer_kernel(*refs):
    for parity in range(2):
        pl.when(pl.program_id(0) % 2 == parity)(functools.partial(_mixer_step, parity, *refs))


def _mixer_step(cur, x_hbm, xn_ref, g1_ref, win_hbm, vg_ref, ws_ref, bst_ref, lbraw_ref, og_ref,
                wout_hbm, g2_ref, wr_ref, br_ref,
                h1_ref, n2_ref, mi_ref, mf_ref, cnt_ref,
                st_ref, carry_ref, wm_ref, mix_ref, proj_ref, logit_ref, xkeep_ref, xsem,
                win_ref, wout_ref, wstage_ref, wsem):
    i = pl.program_id(0)

    @pl.when(i == 0)
    def _init():
        carry_ref[...] = jnp.zeros_like(carry_ref)
        logit_ref[...] = jnp.zeros_like(logit_ref)
        n_in = D_IN // D_A
        pieces = ([(win_hbm, win_ref, c) for c in range(n_in)]
                  + [(wout_hbm, wout_ref, c) for c in range(D_MODEL // D_A)])

        def fetch(k):
            src, _, c = pieces[k]
            return pltpu.make_async_copy(src.at[:, pl.ds(c * D_A, D_A)], wstage_ref.at[k % 2],
                                         wsem.at[k % 2])

        fetch(0).start()
        for k, (_, dst, c) in enumerate(pieces):
            if k + 1 < len(pieces):
                fetch(k + 1).start()
            fetch(k).wait()
            dst[:, c * D_A:(c + 1) * D_A] = wstage_ref[k % 2].astype(BF16)
        r = lax.broadcasted_iota(jnp.int32, (GMLP_BLOCK, GMLP_BLOCK), 0)
        c = lax.broadcasted_iota(jnp.int32, (GMLP_BLOCK, GMLP_BLOCK), 1)
        keep = (c // CHUNK) <= (r // CHUNK)
        wm_ref[...] = jnp.zeros_like(wm_ref)
        for g in range(HEADS):
            w = jnp.where(keep, ws_ref[g], 0.0).astype(BF16)
            for p in range(TS // GMLP_BLOCK):
                wm_ref[g, p * GMLP_BLOCK:(p + 1) * GMLP_BLOCK,
                       p * GMLP_BLOCK:(p + 1) * GMLP_BLOCK] = w

    @pl.when(i % (SEQ // TS) == 0)
    def _reset_state():
        st_ref[...] = jnp.zeros_like(st_ref)

    def project(n1, slot, c):
        proj_ref[slot, :, c * D_A:(c + 1) * D_A] = jnp.dot(
            n1, win_ref[:, c * D_A:(c + 1) * D_A], preferred_element_type=F32)

    @pl.when(i == 0)
    def _first_projection():
        first = pltpu.make_async_copy(x_hbm.at[pl.ds(0, TS)], xkeep_ref.at[0], xsem)
        first.start()
        first.wait()
        n1 = _rms(xkeep_ref[0], g1_ref[...]).astype(BF16)
        for c in range(D_IN // D_A):
            project(n1, 0, c)

    x_next = xn_ref[...]
    n1_next = _rms(x_next, g1_ref[...]).astype(BF16)
    xkeep_ref[1 - cur] = x_next

    project(n1_next, 1 - cur, 0)
    gu = _gelu(proj_ref[cur, :, 0:D_A])
    project(n1_next, 1 - cur, 1)
    gv = _gelu(proj_ref[cur, :, D_A:2 * D_A])
    for g in range(HEADS):
        sl = slice(g * HEAD_DIM, (g + 1) * HEAD_DIM)
        vh = _rms(gv[:, sl], vg_ref[:, sl]).astype(BF16)
        mixed = jnp.dot(wm_ref[g], vh, preferred_element_type=F32) + bst_ref[:, g:g + 1]
        mix_ref[:, sl] = (gu[:, sl] * mixed).astype(BF16)

    _route(logit_ref[1 - cur], jnp.maximum(i - 1, 0), jnp.where(i > 0, 1.0, 0.0),
           carry_ref, cnt_ref, mi_ref, mf_ref)

    o2 = 2 * D_A
    qr = proj_ref[cur, :, o2:o2 + D_B]
    fr = proj_ref[cur, :, o2 + D_B:o2 + 2 * D_B]
    vv = proj_ref[cur, :, o2 + 2 * D_B:o2 + 3 * D_B].astype(BF16)
    gr = proj_ref[cur, :, o2 + 3 * D_B:o2 + 4 * D_B]

    lbr = lbraw_ref[...]
    lbm = jnp.max(lbr, axis=0, keepdims=True)
    lbe = jnp.exp(lbr - lbm)
    lb = lbe[0:1, :] / jnp.sum(lbe, axis=0, keepdims=True)

    project(n1_next, 1 - cur, 2)
    q = qr * _sigmoid(qr)
    f = lb + (1.0 - lb) * _sigmoid(fr)
    k = 1.0 - f
    lf = jnp.log(f)

    row = lax.broadcasted_iota(jnp.int32, (TS, TS), 0)
    col = lax.broadcasted_iota(jnp.int32, (TS, TS), 1)
    causal = (row >= col) & ((row // CHUNK) == (col // CHUNK))
    tri = jnp.where(causal, 1.0, 0.0).astype(BF16)
    lf_hi, lf_lo = _split_bf16(lf)
    b = (jnp.dot(tri, lf_hi, preferred_element_type=F32)
         + jnp.dot(tri, lf_lo, preferred_element_type=F32))

    def chunk_rows(r):
        return jnp.concatenate(
            [jnp.broadcast_to(b[c * CHUNK + r:c * CHUNK + r + 1, :], (CHUNK, D_B))
             for c in range(CHUNKS_PER_STEP)], axis=0)

    project(n1_next, 1 - cur, 3)
    bref = chunk_rows(CHUNK // 2 - 1)
    blast = chunk_rows(CHUNK - 1)
    qe = (q * jnp.exp(b - bref)).astype(BF16)
    ke = (k * jnp.exp(bref - b)).astype(BF16)
    kd = (k * jnp.exp(blast - b)).astype(BF16)
    qb = (q * jnp.exp(b)).astype(BF16)

    head_sl = [slice(h * HEAD_DIM, (h + 1) * HEAD_DIM) for h in range(HEADS)]
    chunk_sl = [slice(c * CHUNK, (c + 1) * CHUNK) for c in range(CHUNKS_PER_STEP)]
    scores = [lax.dot_general(qe[:, sl], ke[:, sl], NT_DIMS, preferred_element_type=F32)
              for sl in head_sl]
    upd = [[lax.dot_general(vv[rows, sl], kd[rows, sl], TN_DIMS, preferred_element_type=F32)
            for rows in chunk_sl] for sl in head_sl]
    project(n1_next, 1 - cur, 4)
    scores = [jnp.where(causal, sc, 0.0).astype(BF16) for sc in scores]
    intra = [jnp.dot(scores[h], vv[:, head_sl[h]], preferred_element_type=F32)
             for h in range(HEADS)]
    entering = []
    for h in range(HEADS):
        st = st_ref[h]
        per_chunk = []
        for c in range(CHUNKS_PER_STEP):
            per_chunk.append(st.astype(BF16))
            decay = jnp.exp(b[(c + 1) * CHUNK - 1:(c + 1) * CHUNK, head_sl[h]])
            st = st * decay + upd[h][c]
        st_ref[h] = st
        entering.append(per_chunk)
    inter = [jnp.concatenate(
        [lax.dot_general(qb[chunk_sl[c], head_sl[h]], entering[h][c], NT_DIMS,
                         preferred_element_type=F32) for c in range(CHUNKS_PER_STEP)], axis=0)
        for h in range(HEADS)]
    for h in range(HEADS):
        o = _rms(intra[h] + inter[h], og_ref[:, head_sl[h]])
        g_h = gr[:, head_sl[h]]
        mix_ref[:, D_A + h * HEAD_DIM:D_A + (h + 1) * HEAD_DIM] = (
            o * (g_h * _sigmoid(g_h))).astype(BF16)

    h1 = xkeep_ref[cur] + jnp.dot(mix_ref[...], wout_ref[...], preferred_element_type=F32)
    h1_ref[...] = h1
    n2 = _rms(h1, g2_ref[...])
    _store_packed_rows(n2_ref, _pack_bf16_pairs(n2), TS)

    logits = jnp.dot(n2.astype(BF16), wr_ref[...], preferred_element_type=F32)
    logits = jnp.transpose(logits)[0:ROUTER_ROWS, :] + br_ref[...]
    project(n1_next, 1 - cur, 5)
    logit_ref[cur] = logits

    @pl.when(i == N_TOK // TS - 1)
    def _route_last_tile():
        _route(logits, i, 1.0, carry_ref, cnt_ref, mi_ref, mf_ref)


def _route(logits, tile, live, carry_ref, cnt_ref, mi_ref, mf_ref):
    rid = lax.broadcasted_iota(jnp.int32, (ROUTER_ROWS, TS), 0)
    row = lax.broadcasted_iota(jnp.int32, (TS, TS), 0)
    col = lax.broadcasted_iota(jnp.int32, (TS, TS), 1)
    neg = -jnp.inf
    gl = jnp.where(rid < N_GROUPS, logits, neg)
    gmax = jnp.max(gl, axis=0, keepdims=True)
    g_idx = jnp.min(jnp.where(gl == gmax, rid, ROUTER_ROWS), axis=0, keepdims=True)
    g_prob = 1.0 / jnp.sum(jnp.exp(gl - gmax), axis=0, keepdims=True)
    e_lo = EXPERT_ROW0 + g_idx * EXPERTS_PER_GROUP
    el = jnp.where((rid >= e_lo) & (rid < e_lo + EXPERTS_PER_GROUP), logits, neg)
    m1 = jnp.max(el, axis=0, keepdims=True)
    i1 = jnp.min(jnp.where(el == m1, rid, ROUTER_ROWS), axis=0, keepdims=True)
    el2 = jnp.where(rid == i1, neg, el)
    m2 = jnp.max(el2, axis=0, keepdims=True)
    i2 = jnp.min(jnp.where(el2 == m2, rid, ROUTER_ROWS), axis=0, keepdims=True)
    e21 = jnp.exp(m2 - m1)
    w1 = g_prob / (1.0 + e21)
    w2 = g_prob * e21 / (1.0 + e21)

    hit1 = rid == i1
    hit2 = rid == i2
    onehot = jnp.where(hit1 | hit2, live, 0.0)
    earlier = jnp.where(row < col, 1.0, 0.0).astype(BF16)
    seen = carry_ref[...]
    prior = (jnp.dot(onehot.astype(BF16), earlier, preferred_element_type=F32)
             + jnp.concatenate([seen] * (TS // LANES), axis=1))
    r1 = jnp.sum(jnp.where(hit1, prior, 0.0), axis=0, keepdims=True)
    r2 = jnp.sum(jnp.where(hit2, prior, 0.0), axis=0, keepdims=True)
    seen = seen + jnp.sum(onehot, axis=1, keepdims=True)
    carry_ref[...] = seen
    cnt_ref[...] = seen.astype(jnp.int32)

    cols = pl.ds(pl.multiple_of(tile * TS, TS), TS)
    pad_i = jnp.zeros((META_ROWS - 2 * TOP_K, TS), jnp.int32)
    mi_ref[:, cols] = jnp.concatenate(
        [i1 - EXPERT_ROW0, i2 - EXPERT_ROW0, r1.astype(jnp.int32), r2.astype(jnp.int32), pad_i],
        axis=0)
    mf_ref[:, cols] = jnp.concatenate([w1, w2, jnp.zeros((META_ROWS - TOP_K, TS), F32)], axis=0)


MIXER_VMEM_LIMIT = (
    (D_MODEL * D_IN + D_MODEL * D_MODEL) * 2
    + 2 * D_MODEL * D_A * 4
    + 2 * TS * D_MODEL * 4 * 2
    + 2 * TS * PACK_COLS * 4
    + 2 * 2 * META_ROWS * N_TOK * 4
    + 2 * TS * D_IN * 4 + 2 * TS * D_MODEL * 4
    + HEADS * TS * TS * 2 + TS * D_MODEL * 2
    + 16 * 1024 * 1024)


def _mixer(x2, g1, win, vg, ws, bst, lbraw, og, wout, g2, wr, br):
    n_steps = N_TOK // TS
    const2 = lambda i: (0, 0)
    tok = lambda i: (i, 0)
    return pl.pallas_call(
        _mixer_kernel,
        grid=(n_steps,),
        in_specs=[
            pl.BlockSpec(memory_space=pl.ANY),
            pl.BlockSpec((TS, D_MODEL), lambda i: (jnp.minimum(i + 1, n_steps - 1), 0)),
            pl.BlockSpec((1, D_MODEL), const2),
            pl.BlockSpec(memory_space=pl.ANY),
            pl.BlockSpec((1, D_A), const2),
            pl.BlockSpec((HEADS, GMLP_BLOCK, GMLP_BLOCK), lambda i: (0, 0, 0)),
            pl.BlockSpec((TS, HEADS), const2),
            pl.BlockSpec((2, D_B), const2),
            pl.BlockSpec((1, D_B), const2),
            pl.BlockSpec(memory_space=pl.ANY),
            pl.BlockSpec((1, D_MODEL), const2),
            pl.BlockSpec((D_MODEL, LANES), const2),
            pl.BlockSpec((ROUTER_ROWS, 1), const2),
        ],
        out_specs=[
            pl.BlockSpec((TS, D_MODEL), tok),
            pl.BlockSpec((TS * PACK_ROWS, LANES), tok),
            pl.BlockSpec((META_ROWS, N_TOK), const2),
            pl.BlockSpec((META_ROWS, N_TOK), const2),
            pl.BlockSpec((ROUTER_ROWS, LANES), const2),
        ],
        out_shape=[
            jax.ShapeDtypeStruct((N_TOK, D_MODEL), F32),
            jax.ShapeDtypeStruct((N_TOK * PACK_ROWS, LANES), jnp.uint32),
            jax.ShapeDtypeStruct((META_ROWS, N_TOK), jnp.int32),
            jax.ShapeDtypeStruct((META_ROWS, N_TOK), F32),
            jax.ShapeDtypeStruct((ROUTER_ROWS, LANES), jnp.int32),
        ],
        scratch_shapes=[
            pltpu.VMEM((HEADS, HEAD_DIM, HEAD_DIM), F32),
            pltpu.VMEM((ROUTER_ROWS, LANES), F32),
            pltpu.VMEM((HEADS, TS, TS), BF16),
            pltpu.VMEM((TS, D_MODEL), BF16),
            pltpu.VMEM((2, TS, D_IN), F32),
            pltpu.VMEM((2, ROUTER_ROWS, TS), F32),
            pltpu.VMEM((2, TS, D_MODEL), F32),
            pltpu.SemaphoreType.DMA,
            pltpu.VMEM((D_MODEL, D_IN), BF16),
            pltpu.VMEM((D_MODEL, D_MODEL), BF16),
            pltpu.VMEM((2, D_MODEL, D_A), F32),
            pltpu.SemaphoreType.DMA((2,)),
        ],
        compiler_params=pltpu.CompilerParams(
            dimension_semantics=("arbitrary",),
            vmem_limit_bytes=MIXER_VMEM_LIMIT,
        ),
        name="mixer",
    )(x2, x2, g1, win, vg, ws, bst, lbraw, og, wout, g2, wr, br)


FILL_UNROLL = 8

def _invert(dest_ref, gap_lo_ref, gap_hi_ref, inv_ref):
    for e in range(N_EXPERTS):
        first = gap_lo_ref[e] // FILL_UNROLL

        def fill(g, carry):
            for u in range(FILL_UNROLL):
                inv_ref[g * FILL_UNROLL + u] = -1
            return carry

        lax.fori_loop(first, gap_hi_ref[e] // FILL_UNROLL, fill, 0)

    def place(a, carry):
        inv_ref[dest_ref[a]] = a
        return carry

    lax.fori_loop(0, N_ASSIGN, place, 0, unroll=16)


def _expert_kernel(bexp_ref, nused_ref, dest_ref, gap_lo_ref, gap_hi_ref,
                   n2_hbm, wg_ref, wu_ref, wd_ref, y_hbm,
                   inv_ref, n2_vmem, xb0, xb1, xb2, ob0, ob1, ob2, wg_bf, wu_bf, wd_bf, nsem, ssem):
    i = pl.program_id(0)
    nused = nused_ref[0]
    last = nused - 1
    xbuf = (xb0, xb1, xb2)
    obuf = (ob0, ob1, ob2)

    def gather(s, blk):
        for j in range(MOE_BLOCK):
            tok = inv_ref[blk * MOE_BLOCK + j] & (N_TOK - 1)
            r = pl.multiple_of(tok * PACK_ROWS, PACK_ROWS)
            xbuf[s][pl.ds(j * PACK_ROWS, PACK_ROWS), :] = n2_vmem[pl.ds(r, PACK_ROWS), :]

    def start_scatter(s, blk):
        for j in range(MOE_BLOCK):
            a = inv_ref[blk * MOE_BLOCK + j]
            row = jnp.where(a >= 0, a, N_ASSIGN + s * MOE_BLOCK + j)
            r = pl.multiple_of(row * PACK_ROWS, PACK_ROWS)
            pltpu.make_async_copy(obuf[s].at[pl.ds(j * PACK_ROWS, PACK_ROWS)],
                                  y_hbm.at[pl.ds(r, PACK_ROWS)], ssem.at[s]).start()

    def wait_scatter(s):
        pltpu.make_async_copy(obuf[s], y_hbm.at[pl.ds(0, BLOCK_PACK_ROWS)],
                              ssem.at[s]).wait()

    @pl.when(i == 0)
    def _prologue():
        for o in obuf:
            o[...] = jnp.zeros_like(o)
        for s in range(N_SLOTS - 1):
            pltpu.make_async_copy(
                obuf[s],
                y_hbm.at[pl.ds((N_ASSIGN + s * MOE_BLOCK) * PACK_ROWS, BLOCK_PACK_ROWS)],
                ssem.at[s]).start()
        resident = pltpu.make_async_copy(n2_hbm, n2_vmem, nsem)
        resident.start()
        _invert(dest_ref, gap_lo_ref, gap_hi_ref, inv_ref)
        resident.wait()
        gather(0, 0)

    prev = bexp_ref[jnp.maximum(i - 1, 0)]
    changed = (i == 0) | (bexp_ref[i] != prev)

    @pl.when(changed)
    def _cast_weights():
        wg_bf[...] = wg_ref[0].astype(BF16)
        wu_bf[...] = wu_ref[0].astype(BF16)
        wd_bf[...] = wd_ref[0].astype(BF16)

    def active(cur, nxt, prv):
        wait_scatter(cur)
        start_scatter(prv, jnp.where(i == 0, N_BLOCKS, i - 1))
        gather(nxt, jnp.minimum(i + 1, last))
        xb = _unpack_bf16_pairs(_load_packed_rows(xbuf[cur], MOE_BLOCK)).astype(BF16)
        y = None
        for h in range(2):
            cols = slice(h * D_EXPERT // 2, (h + 1) * D_EXPERT // 2)
            hg = jnp.dot(xb, wg_bf[:, cols], preferred_element_type=F32)
            hu = jnp.dot(xb, wu_bf[:, cols], preferred_element_type=F32)
            hh = (hg * _sigmoid(hg) * hu).astype(BF16)
            part = jnp.dot(hh, wd_bf[cols, :], preferred_element_type=F32)
            y = part if y is None else y + part
        _store_packed_rows(obuf[cur], _pack_bf16_pairs(y), MOE_BLOCK)

    def epilogue(cur, nxt, prv):
        start_scatter(cur, i)
        wait_scatter(nxt)
        wait_scatter(prv)
        wait_scatter(cur)

    for cur in range(N_SLOTS):
        slots = (cur, (cur + 1) % N_SLOTS, (cur + 2) % N_SLOTS)

        @pl.when((i < nused) & (i % N_SLOTS == cur))
        def _():
            active(*slots)

        @pl.when((i == last) & (i % N_SLOTS == cur))
        def _():
            epilogue(*slots)


_EXPERT_W_BYTES = 3 * D_MODEL * D_EXPERT * 4
EXPERT_VMEM_LIMIT = (
    N_TOK * PACK_COLS * 4
    + 2 * _EXPERT_W_BYTES
    + _EXPERT_W_BYTES // 2
    + 2 * N_SLOTS * MOE_BLOCK * PACK_COLS * 4
    + 4 * 1024 * 1024)


def _experts(bexp, nused, dest, gap_lo, gap_hi, n2, w_gate, w_up, w_down):
    def w_blk(i, bexp, nused, dest, gap_lo, gap_hi):
        return (bexp[i], 0, 0)

    any_spec = pl.BlockSpec(memory_space=pl.ANY)
    return pl.pallas_call(
        _expert_kernel,
        grid_spec=pltpu.PrefetchScalarGridSpec(
            num_scalar_prefetch=5,
            grid=(N_BLOCKS,),
            in_specs=[
                any_spec,
                pl.BlockSpec((1, D_MODEL, D_EXPERT), w_blk),
                pl.BlockSpec((1, D_MODEL, D_EXPERT), w_blk),
                pl.BlockSpec((1, D_EXPERT, D_MODEL), w_blk),
            ],
            out_specs=any_spec,
            scratch_shapes=[
                pltpu.SMEM((INV_LEN,), jnp.int32),
                pltpu.VMEM((N_TOK * PACK_ROWS, LANES), jnp.uint32),
            ] + [pltpu.VMEM((BLOCK_PACK_ROWS, LANES), jnp.uint32)] * (2 * N_SLOTS) + [
                pltpu.VMEM((D_MODEL, D_EXPERT), BF16),
                pltpu.VMEM((D_MODEL, D_EXPERT), BF16),
                pltpu.VMEM((D_EXPERT, D_MODEL), BF16),
                pltpu.SemaphoreType.DMA,
                pltpu.SemaphoreType.DMA((N_SLOTS,)),
            ],
        ),
        out_shape=jax.ShapeDtypeStruct(((N_ASSIGN + DUMMY_ROWS) * PACK_ROWS, LANES), jnp.uint32),
        compiler_params=pltpu.CompilerParams(
            dimension_semantics=("arbitrary",),
            vmem_limit_bytes=EXPERT_VMEM_LIMIT,
        ),
        name="experts",
    )(bexp, nused, dest, gap_lo, gap_hi, n2, w_gate, w_up, w_down)


def _combine_kernel(h1_ref, mf_ref, fg_ref, y0_ref, y1_ref, out_ref):
    rows = mf_ref[...]
    sel = (lax.broadcasted_iota(jnp.int32, (META_ROWS, LANES), 0)
           == lax.broadcasted_iota(jnp.int32, (META_ROWS, LANES), 1)).astype(BF16)
    w = sum(lax.dot_general(part, sel, TN_DIMS, preferred_element_type=F32)
            for part in _split_bf16(rows))
    y0 = _unpack_bf16_pairs(_load_packed_rows(y0_ref, TD))
    y1 = _unpack_bf16_pairs(_load_packed_rows(y1_ref, TD))
    h = h1_ref[...] + y0 * w[:, 0:1] + y1 * w[:, 1:2]
    out_ref[...] = _rms(h, fg_ref[...])


COMBINE_VMEM_LIMIT = 2 * (2 * (2 * TD * D_MODEL * 4) + 2 * (2 * TD * PACK_COLS * 4))


def _combine(h1, mf, fg, y):
    n_steps = N_TOK // TD
    tok = lambda i: (i, 0)
    return pl.pallas_call(
        _combine_kernel,
        grid=(n_steps,),
        in_specs=[
            pl.BlockSpec((TD, D_MODEL), tok),
            pl.BlockSpec((META_ROWS, TD), lambda i: (0, i)),
            pl.BlockSpec((1, D_MODEL), lambda i: (0, 0)),
            pl.BlockSpec((TD * PACK_ROWS, LANES), tok),
            pl.BlockSpec((TD * PACK_ROWS, LANES), lambda i: (i + n_steps, 0)),
        ],
        out_specs=pl.BlockSpec((TD, D_MODEL), tok),
        out_shape=jax.ShapeDtypeStruct((N_TOK, D_MODEL), F32),
        compiler_params=pltpu.CompilerParams(
            dimension_semantics=("arbitrary",),
            vmem_limit_bytes=COMBINE_VMEM_LIMIT,
        ),
        name="combine",
    )(h1, mf, fg, y, y)


def kernel(x, norm1_gain, w_in, gmlp_v_gain, gmlp_w_s, gmlp_b_s, hgrn_lower_bounds,
           hgrn_out_gain, w_out, norm2_gain, w_group_router, b_group_router,
           w_expert_router, b_expert_router, w_gate, w_up, w_down, final_gain):
    l = 0
    x2 = x.reshape(N_TOK, D_MODEL)
    bst = jnp.tile(jnp.transpose(gmlp_b_s[l]), (TS // GMLP_BLOCK, 1))
    w_router = jnp.concatenate([w_group_router[l], w_expert_router[l]], axis=1)
    w_router = jnp.pad(w_router, ((0, 0), (0, LANES - w_router.shape[1]))).astype(BF16)
    b_router = jnp.concatenate([b_group_router[l], b_expert_router[l]])
    b_router = jnp.pad(b_router, (0, ROUTER_ROWS - b_router.shape[0])).reshape(ROUTER_ROWS, 1)

    h1, n2, meta_i, meta_f, counts = _mixer(
        x2, norm1_gain[l].reshape(1, D_MODEL), w_in[l],
        gmlp_v_gain[l].reshape(1, D_A), gmlp_w_s[l], bst,
        hgrn_lower_bounds, hgrn_out_gain[l].reshape(1, D_B), w_out[l],
        norm2_gain[l].reshape(1, D_MODEL), w_router, b_router)

    cnt = counts[EXPERT_ROW0:EXPERT_ROW0 + N_EXPERTS, 0]
    padded = ((cnt + MOE_BLOCK - 1) // MOE_BLOCK) * MOE_BLOCK
    pend = jnp.cumsum(padded)
    pstart = pend - padded
    eid = meta_i[0:TOP_K]
    rank = meta_i[TOP_K:2 * TOP_K]
    base = jnp.sum(jnp.where(eid[:, :, None] == jnp.arange(N_EXPERTS)[None, None, :],
                             pstart[None, None, :], 0), axis=-1)
    dest = (base + rank).astype(jnp.int32).reshape(N_ASSIGN)
    gap_lo = (pstart + cnt).astype(jnp.int32)
    gap_hi = jnp.concatenate([pstart[1:], jnp.full((1,), INV_LEN)]).astype(jnp.int32)
    blk_start = jnp.arange(N_BLOCKS, dtype=jnp.int32) * MOE_BLOCK
    bexp = jnp.clip(jnp.sum(blk_start[:, None] >= pend[None, :], axis=1), 0,
                    N_EXPERTS - 1).astype(jnp.int32)
    nused = (pend[-1:] // MOE_BLOCK).astype(jnp.int32)

    y = _experts(bexp, nused, dest, gap_lo, gap_hi, n2, w_gate[l], w_up[l], w_down[l])
    out = _combine(h1, meta_f, final_gain.reshape(1, D_MODEL), y)
    return out.reshape(BATCH, SEQ, D_MODEL)
```

```python
import functools

import jax
import jax.numpy as jnp
from jax import lax
from jax.experimental import pallas as pl
from jax.experimental.pallas import tpu as pltpu

D_MODEL = 1024
BATCH = 2
SEQ = 8192
N_TOK = BATCH * SEQ
CHUNK = 64
EPS = 1e-6
D_A = 512
HEADS = 4
HEAD_DIM = 128
GMLP_BLOCK = 128
D_B = 512
D_IN = 3072
N_GROUPS = 4
EXPERTS_PER_GROUP = 8
N_EXPERTS = 32
TOP_K = 2
D_EXPERT = 512

LANES = 128
ROUTER_ROWS = 48
EXPERT_ROW0 = N_GROUPS
META_ROWS = 8
PACK_COLS = D_MODEL // 2
PACK_ROWS = PACK_COLS // LANES
HI_MASK = 0xFFFF0000

TS = 256
CHUNKS_PER_STEP = TS // CHUNK
MOE_BLOCK = 256
N_BLOCKS = N_TOK * TOP_K // MOE_BLOCK + N_EXPERTS
SORTED_ROWS = N_BLOCKS * MOE_BLOCK
N_ASSIGN = N_TOK * TOP_K
N_SLOTS = 2
DUMMY_ROWS = N_SLOTS * MOE_BLOCK
BLOCK_PACK_ROWS = MOE_BLOCK * PACK_ROWS
INV_LEN = SORTED_ROWS + MOE_BLOCK
TD = 1024

F32 = jnp.float32
BF16 = jnp.bfloat16
NT_DIMS = (((1,), (1,)), ((), ()))
TN_DIMS = (((0,), (0,)), ((), ()))


def _sigmoid(x):
    return 1.0 / (1.0 + jnp.exp(-x))


def _gelu(x):
    return 0.5 * x * (1.0 + lax.erf(x * (2.0 ** -0.5)))


def _rms(x, gain):
    return x * lax.rsqrt(jnp.mean(x * x, axis=-1, keepdims=True) + EPS) * gain


def _pack_bf16_pairs(x):
    lo = pltpu.bitcast(x[:, 0:PACK_COLS].astype(BF16).astype(F32), jnp.uint32) >> 16
    hi = (pltpu.bitcast(x[:, PACK_COLS:D_MODEL].astype(BF16).astype(F32), jnp.uint32)
          & jnp.uint32(HI_MASK))
    return lo | hi


def _unpack_bf16_pairs(words):
    lo = pltpu.bitcast(words << 16, F32)
    hi = pltpu.bitcast(words & jnp.uint32(HI_MASK), F32)
    return jnp.concatenate([lo, hi], axis=1)


def _store_packed_rows(ref, words, rows):
    for c in range(PACK_ROWS):
        ref[pl.ds(c, rows, stride=PACK_ROWS), :] = words[:, c * LANES:(c + 1) * LANES]


def _load_packed_rows(ref, rows):
    return jnp.concatenate(
        [ref[pl.ds(c, rows, stride=PACK_ROWS), :] for c in range(PACK_ROWS)], axis=1)


def _split_bf16(x):
    hi = x.astype(BF16)
    lo = (x - hi.astype(F32)).astype(BF16)
    return hi, lo


def _mixer_kernel(*refs):
    for parity in range(2):
        pl.when(pl.program_id(0) % 2 == parity)(functools.partial(_mixer_step, parity, *refs))


def _mixer_step(cur, x_hbm, xn_ref, g1_ref, win_hbm, vg_ref, ws_ref, bst_ref, lbraw_ref, og_ref,
                wout_hbm, g2_ref, wr_ref, br_ref,
                h1_ref, n2_ref, mi_ref, mf_ref, cnt_ref,
                st_ref, carry_ref, wm_ref, mix_ref, proj_ref, logit_ref, xkeep_ref, xsem,
                win_ref, wout_ref, wstage_ref, wsem):
    i = pl.program_id(0)

    @pl.when(i == 0)
    def _init():
        carry_ref[...] = jnp.zeros_like(carry_ref)
        logit_ref[...] = jnp.zeros_like(logit_ref)
        n_in = D_IN // D_A
        pieces = ([(win_hbm, win_ref, c) for c in range(n_in)]
                  + [(wout_hbm, wout_ref, c) for c in range(D_MODEL // D_A)])

        def fetch(k):
            src, _, c = pieces[k]
            return pltpu.make_async_copy(src.at[:, pl.ds(c * D_A, D_A)], wstage_ref.at[k % 2],
                                         wsem.at[k % 2])

        fetch(0).start()
        for k, (_, dst, c) in enumerate(pieces):
            if k + 1 < len(pieces):
                fetch(k + 1).start()
            fetch(k).wait()
            dst[:, c * D_A:(c + 1) * D_A] = wstage_ref[k % 2].astype(BF16)
        r = lax.broadcasted_iota(jnp.int32, (GMLP_BLOCK, GMLP_BLOCK), 0)
        c = lax.broadcasted_iota(jnp.int32, (GMLP_BLOCK, GMLP_BLOCK), 1)
        keep = (c // CHUNK) <= (r // CHUNK)
        wm_ref[...] = jnp.zeros_like(wm_ref)
        for g in range(HEADS):
            w = jnp.where(keep, ws_ref[g], 0.0).astype(BF16)
            for p in range(TS // GMLP_BLOCK):
                wm_ref[g, p * GMLP_BLOCK:(p + 1) * GMLP_BLOCK,
                       p * GMLP_BLOCK:(p + 1) * GMLP_BLOCK] = w

    @pl.when(i % (SEQ // TS) == 0)
    def _reset_state():
        st_ref[...] = jnp.zeros_like(st_ref)

    def project(n1, slot, c):
        proj_ref[slot, :, c * D_A:(c + 1) * D_A] = jnp.dot(
            n1, win_ref[:, c * D_A:(c + 1) * D_A], preferred_element_type=F32)

    @pl.when(i == 0)
    def _first_projection():
        first = pltpu.make_async_copy(x_hbm.at[pl.ds(0, TS)], xkeep_ref.at[0], xsem)
        first.start()
        first.wait()
        n1 = _rms(xkeep_ref[0], g1_ref[...]).astype(BF16)
        for c in range(D_IN // D_A):
            project(n1, 0, c)

    x_next = xn_ref[...]
    n1_next = _rms(x_next, g1_ref[...]).astype(BF16)
    xkeep_ref[1 - cur] = x_next

    project(n1_next, 1 - cur, 0)
    gu = _gelu(proj_ref[cur, :, 0:D_A])
    project(n1_next, 1 - cur, 1)
    gv = _gelu(proj_ref[cur, :, D_A:2 * D_A])
    for g in range(HEADS):
        sl = slice(g * HEAD_DIM, (g + 1) * HEAD_DIM)
        vh = _rms(gv[:, sl], vg_ref[:, sl]).astype(BF16)
        mixed = jnp.dot(wm_ref[g], vh, preferred_element_type=F32) + bst_ref[:, g:g + 1]
        mix_ref[:, sl] = (gu[:, sl] * mixed).astype(BF16)

    _route(logit_ref[1 - cur], jnp.maximum(i - 1, 0), jnp.where(i > 0, 1.0, 0.0),
           carry_ref, cnt_ref, mi_ref, mf_ref)

    o2 = 2 * D_A
    qr = proj_ref[cur, :, o2:o2 + D_B]
    fr = proj_ref[cur, :, o2 + D_B:o2 + 2 * D_B]
    vv = proj_ref[cur, :, o2 + 2 * D_B:o2 + 3 * D_B].astype(BF16)
    gr = proj_ref[cur, :, o2 + 3 * D_B:o2 + 4 * D_B]

    lbr = lbraw_ref[...]
    lbm = jnp.max(lbr, axis=0, keepdims=True)
    lbe = jnp.exp(lbr - lbm)
    lb = lbe[0:1, :] / jnp.sum(lbe, axis=0, keepdims=True)

    project(n1_next, 1 - cur, 2)
    q = qr * _sigmoid(qr)
    f = lb + (1.0 - lb) * _sigmoid(fr)
    k = 1.0 - f
    lf = jnp.log(f)

    row = lax.broadcasted_iota(jnp.int32, (TS, TS), 0)
    col = lax.broadcasted_iota(jnp.int32, (TS, TS), 1)
    causal = (row >= col) & ((row // CHUNK) == (col // CHUNK))
    tri = jnp.where(causal, 1.0, 0.0).astype(BF16)
    lf_hi, lf_lo = _split_bf16(lf)
    b = (jnp.dot(tri, lf_hi, preferred_element_type=F32)
         + jnp.dot(tri, lf_lo, preferred_element_type=F32))

    def chunk_rows(r):
        return jnp.concatenate(
            [jnp.broadcast_to(b[c * CHUNK + r:c * CHUNK + r + 1, :], (CHUNK, D_B))
             for c in range(CHUNKS_PER_STEP)], axis=0)

    project(n1_next, 1 - cur, 3)
    bref = chunk_rows(CHUNK // 2 - 1)
    blast = chunk_rows(CHUNK - 1)
    qe = (q * jnp.exp(b - bref)).astype(BF16)
    ke = (k * jnp.exp(bref - b)).astype(BF16)
    kd = (k * jnp.exp(blast - b)).astype(BF16)
    qb = (q * jnp.exp(b)).astype(BF16)

    head_sl = [slice(h * HEAD_DIM, (h + 1) * HEAD_DIM) for h in range(HEADS)]
    chunk_sl = [slice(c * CHUNK, (c + 1) * CHUNK) for c in range(CHUNKS_PER_STEP)]
    scores = [lax.dot_general(qe[:, sl], ke[:, sl], NT_DIMS, preferred_element_type=F32)
              for sl in head_sl]
    upd = [[lax.dot_general(vv[rows, sl], kd[rows, sl], TN_DIMS, preferred_element_type=F32)
            for rows in chunk_sl] for sl in head_sl]
    project(n1_next, 1 - cur, 4)
    scores = [jnp.where(causal, sc, 0.0).astype(BF16) for sc in scores]
    intra = [jnp.dot(scores[h], vv[:, head_sl[h]], preferred_element_type=F32)
             for h in range(HEADS)]
    entering = []
    for h in range(HEADS):
        st = st_ref[h]
        per_chunk = []
        for c in range(CHUNKS_PER_STEP):
            per_chunk.append(st.astype(BF16))
            decay = jnp.exp(b[(c + 1) * CHUNK - 1:(c + 1) * CHUNK, head_sl[h]])
            st = st * decay + upd[h][c]
        st_ref[h] = st
        entering.append(per_chunk)
    inter = [jnp.concatenate(
        [lax.dot_general(qb[chunk_sl[c], head_sl[h]], entering[h][c], NT_DIMS,
                         preferred_element_type=F32) for c in range(CHUNKS_PER_STEP)], axis=0)
        for h in range(HEADS)]
    for h in range(HEADS):
        o = _rms(intra[h] + inter[h], og_ref[:, head_sl[h]])
        g_h = gr[:, head_sl[h]]
        mix_ref[:, D_A + h * HEAD_DIM:D_A + (h + 1) * HEAD_DIM] = (
            o * (g_h * _sigmoid(g_h))).astype(BF16)

    h1 = xkeep_ref[cur] + jnp.dot(mix_ref[...], wout_ref[...], preferred_element_type=F32)
    h1_ref[...] = h1
    n2 = _rms(h1, g2_ref[...])
    _store_packed_rows(n2_ref, _pack_bf16_pairs(n2), TS)

    logits = jnp.dot(n2.astype(BF16), wr_ref[...], preferred_element_type=F32)
    logits = jnp.transpose(logits)[0:ROUTER_ROWS, :] + br_ref[...]
    project(n1_next, 1 - cur, 5)
    logit_ref[cur] = logits

    @pl.when(i == N_TOK // TS - 1)
    def _route_last_tile():
        _route(logits, i, 1.0, carry_ref, cnt_ref, mi_ref, mf_ref)


def _route(logits, tile, live, carry_ref, cnt_ref, mi_ref, mf_ref):
    rid = lax.broadcasted_iota(jnp.int32, (ROUTER_ROWS, TS), 0)
    row = lax.broadcasted_iota(jnp.int32, (TS, TS), 0)
    col = lax.broadcasted_iota(jnp.int32, (TS, TS), 1)
    neg = -jnp.inf
    gl = jnp.where(rid < N_GROUPS, logits, neg)
    gmax = jnp.max(gl, axis=0, keepdims=True)
    g_idx = jnp.min(jnp.where(gl == gmax, rid, ROUTER_ROWS), axis=0, keepdims=True)
    g_prob = 1.0 / jnp.sum(jnp.exp(gl - gmax), axis=0, keepdims=True)
    e_lo = EXPERT_ROW0 + g_idx * EXPERTS_PER_GROUP
    el = jnp.where((rid >= e_lo) & (rid < e_lo + EXPERTS_PER_GROUP), logits, neg)
    m1 = jnp.max(el, axis=0, keepdims=True)
    i1 = jnp.min(jnp.where(el == m1, rid, ROUTER_ROWS), axis=0, keepdims=True)
    el2 = jnp.where(rid == i1, neg, el)
    m2 = jnp.max(el2, axis=0, keepdims=True)
    i2 = jnp.min(jnp.where(el2 == m2, rid, ROUTER_ROWS), axis=0, keepdims=True)
    e21 = jnp.exp(m2 - m1)
    w1 = g_prob / (1.0 + e21)
    w2 = g_prob * e21 / (1.0 + e21)

    hit1 = rid == i1
    hit2 = rid == i2
    onehot = jnp.where(hit1 | hit2, live, 0.0)
    earlier = jnp.where(row < col, 1.0, 0.0).astype(BF16)
    seen = carry_ref[...]
    prior = (jnp.dot(onehot.astype(BF16), earlier, preferred_element_type=F32)
             + jnp.concatenate([seen] * (TS // LANES), axis=1))
    r1 = jnp.sum(jnp.where(hit1, prior, 0.0), axis=0, keepdims=True)
    r2 = jnp.sum(jnp.where(hit2, prior, 0.0), axis=0, keepdims=True)
    seen = seen + jnp.sum(onehot, axis=1, keepdims=True)
    carry_ref[...] = seen
    cnt_ref[...] = seen.astype(jnp.int32)

    cols = pl.ds(pl.multiple_of(tile * TS, TS), TS)
    pad_i = jnp.zeros((META_ROWS - 2 * TOP_K, TS), jnp.int32)
    mi_ref[:, cols] = jnp.concatenate(
        [i1 - EXPERT_ROW0, i2 - EXPERT_ROW0, r1.astype(jnp.int32), r2.astype(jnp.int32), pad_i],
        axis=0)
    mf_ref[:, cols] = jnp.concatenate([w1, w2, jnp.zeros((META_ROWS - TOP_K, TS), F32)], axis=0)


MIXER_VMEM_LIMIT = (
    (D_MODEL * D_IN + D_MODEL * D_MODEL) * 2
    + 2 * D_MODEL * D_A * 4
    + 2 * TS * D_MODEL * 4 * 2
    + 2 * TS * PACK_COLS * 4
    + 2 * 2 * META_ROWS * N_TOK * 4
    + 2 * TS * D_IN * 4 + 2 * TS * D_MODEL * 4
    + HEADS * TS * TS * 2 + TS * D_MODEL * 2
    + 16 * 1024 * 1024)


def _mixer(x2, g1, win, vg, ws, bst, lbraw, og, wout, g2, wr, br):
    n_steps = N_TOK // TS
    const2 = lambda i: (0, 0)
    tok = lambda i: (i, 0)
    return pl.pallas_call(
        _mixer_kernel,
        grid=(n_steps,),
        in_specs=[
            pl.BlockSpec(memory_space=pl.ANY),
            pl.BlockSpec((TS, D_MODEL), lambda i: (jnp.minimum(i + 1, n_steps - 1), 0)),
            pl.BlockSpec((1, D_MODEL), const2),
            pl.BlockSpec(memory_space=pl.ANY),
            pl.BlockSpec((1, D_A), const2),
            pl.BlockSpec((HEADS, GMLP_BLOCK, GMLP_BLOCK), lambda i: (0, 0, 0)),
            pl.BlockSpec((TS, HEADS), const2),
            pl.BlockSpec((2, D_B), const2),
            pl.BlockSpec((1, D_B), const2),
            pl.BlockSpec(memory_space=pl.ANY),
            pl.BlockSpec((1, D_MODEL), const2),
            pl.BlockSpec((D_MODEL, LANES), const2),
            pl.BlockSpec((ROUTER_ROWS, 1), const2),
        ],
        out_specs=[
            pl.BlockSpec((TS, D_MODEL), tok),
            pl.BlockSpec((TS * PACK_ROWS, LANES), tok),
            pl.BlockSpec((META_ROWS, N_TOK), const2),
            pl.BlockSpec((META_ROWS, N_TOK), const2),
            pl.BlockSpec((ROUTER_ROWS, LANES), const2),
        ],
        out_shape=[
            jax.ShapeDtypeStruct((N_TOK, D_MODEL), F32),
            jax.ShapeDtypeStruct((N_TOK * PACK_ROWS, LANES), jnp.uint32),
            jax.ShapeDtypeStruct((META_ROWS, N_TOK), jnp.int32),
            jax.ShapeDtypeStruct((META_ROWS, N_TOK), F32),
            jax.ShapeDtypeStruct((ROUTER_ROWS, LANES), jnp.int32),
        ],
        scratch_shapes=[
            pltpu.VMEM((HEADS, HEAD_DIM, HEAD_DIM), F32),
            pltpu.VMEM((ROUTER_ROWS, LANES), F32),
            pltpu.VMEM((HEADS, TS, TS), BF16),
            pltpu.VMEM((TS, D_MODEL), BF16),
            pltpu.VMEM((2, TS, D_IN), F32),
            pltpu.VMEM((2, ROUTER_ROWS, TS), F32),
            pltpu.VMEM((2, TS, D_MODEL), F32),
            pltpu.SemaphoreType.DMA,
            pltpu.VMEM((D_MODEL, D_IN), BF16),
            pltpu.VMEM((D_MODEL, D_MODEL), BF16),
            pltpu.VMEM((2, D_MODEL, D_A), F32),
            pltpu.SemaphoreType.DMA((2,)),
        ],
        compiler_params=pltpu.CompilerParams(
            dimension_semantics=("arbitrary",),
            vmem_limit_bytes=MIXER_VMEM_LIMIT,
        ),
        name="mixer",
    )(x2, x2, g1, win, vg, ws, bst, lbraw, og, wout, g2, wr, br)


FILL_UNROLL = 8

def _invert(dest_ref, gap_lo_ref, gap_hi_ref, inv_ref):
    for e in range(N_EXPERTS):
        first = gap_lo_ref[e] // FILL_UNROLL

        def fill(g, carry):
            for u in range(FILL_UNROLL):
                inv_ref[g * FILL_UNROLL + u] = -1
            return carry

        lax.fori_loop(first, gap_hi_ref[e] // FILL_UNROLL, fill, 0)

    def place(a, carry):
        inv_ref[dest_ref[a]] = a
        return carry

    lax.fori_loop(0, N_ASSIGN, place, 0, unroll=16)


def _expert_kernel(bexp_ref, nused_ref, dest_ref, gap_lo_ref, gap_hi_ref,
                   n2_hbm, wg_ref, wu_ref, wd_ref, y_hbm,
                   inv_ref, n2_vmem, *rest):
    i = pl.program_id(0)
    nused = nused_ref[0]
    last = nused - 1
    xbuf = rest[:N_SLOTS]
    obuf = rest[N_SLOTS:2 * N_SLOTS]
    wg_bf, wu_bf, wd_bf, nsem, ssem = rest[2 * N_SLOTS:]

    def gather(s, blk):
        for j in range(MOE_BLOCK):
            tok = inv_ref[blk * MOE_BLOCK + j] & (N_TOK - 1)
            r = pl.multiple_of(tok * PACK_ROWS, PACK_ROWS)
            xbuf[s][pl.ds(j * PACK_ROWS, PACK_ROWS), :] = n2_vmem[pl.ds(r, PACK_ROWS), :]

    def start_scatter(s, blk):
        for j in range(MOE_BLOCK):
            a = inv_ref[blk * MOE_BLOCK + j]
            row = jnp.where(a >= 0, a, N_ASSIGN + s * MOE_BLOCK + j)
            r = pl.multiple_of(row * PACK_ROWS, PACK_ROWS)
            pltpu.make_async_copy(obuf[s].at[pl.ds(j * PACK_ROWS, PACK_ROWS)],
                                  y_hbm.at[pl.ds(r, PACK_ROWS)], ssem.at[s]).start()

    def wait_scatter(s):
        pltpu.make_async_copy(obuf[s], y_hbm.at[pl.ds(0, BLOCK_PACK_ROWS)],
                              ssem.at[s]).wait()

    @pl.when(i == 0)
    def _prologue():
        for o in obuf:
            o[...] = jnp.zeros_like(o)
        for s in range(N_SLOTS - 1):
            pltpu.make_async_copy(
                obuf[s],
                y_hbm.at[pl.ds((N_ASSIGN + s * MOE_BLOCK) * PACK_ROWS, BLOCK_PACK_ROWS)],
                ssem.at[s]).start()
        resident = pltpu.make_async_copy(n2_hbm, n2_vmem, nsem)
        resident.start()
        _invert(dest_ref, gap_lo_ref, gap_hi_ref, inv_ref)
        resident.wait()
        gather(0, 0)

    prev = bexp_ref[jnp.maximum(i - 1, 0)]
    changed = (i == 0) | (bexp_ref[i] != prev)

    @pl.when(changed)
    def _cast_weights():
        wg_bf[...] = wg_ref[0].astype(BF16)
        wu_bf[...] = wu_ref[0].astype(BF16)
        wd_bf[...] = wd_ref[0].astype(BF16)

    def active(cur, nxt, prv):
        wait_scatter(cur)
        start_scatter(prv, jnp.where(i == 0, N_BLOCKS, i - 1))
        gather(nxt, jnp.minimum(i + 1, last))
        xb = _unpack_bf16_pairs(_load_packed_rows(xbuf[cur], MOE_BLOCK)).astype(BF16)
        hg = jnp.dot(xb, wg_bf[...], preferred_element_type=F32)
        hu = jnp.dot(xb, wu_bf[...], preferred_element_type=F32)
        hh = (hg * _sigmoid(hg) * hu).astype(BF16)
        y = jnp.dot(hh, wd_bf[...], preferred_element_type=F32)
        _store_packed_rows(obuf[cur], _pack_bf16_pairs(y), MOE_BLOCK)

    def epilogue(cur, nxt, prv):
        start_scatter(cur, i)
        for back in range(1, N_SLOTS):
            wait_scatter((cur - back) % N_SLOTS)
        wait_scatter(cur)

    for cur in range(N_SLOTS):
        slots = (cur, (cur + 1) % N_SLOTS, (cur - 1) % N_SLOTS)

        @pl.when((i < nused) & (i % N_SLOTS == cur))
        def _():
            active(*slots)

        @pl.when((i == last) & (i % N_SLOTS == cur))
        def _():
            epilogue(*slots)


_EXPERT_W_BYTES = 3 * D_MODEL * D_EXPERT * 4
EXPERT_VMEM_LIMIT = (
    N_TOK * PACK_COLS * 4
    + 2 * _EXPERT_W_BYTES
    + _EXPERT_W_BYTES // 2
    + 2 * N_SLOTS * MOE_BLOCK * PACK_COLS * 4
    + 4 * 1024 * 1024)


def _experts(bexp, nused, dest, gap_lo, gap_hi, n2, w_gate, w_up, w_down):
    def w_blk(i, bexp, nused, dest, gap_lo, gap_hi):
        return (bexp[i], 0, 0)

    any_spec = pl.BlockSpec(memory_space=pl.ANY)
    return pl.pallas_call(
        _expert_kernel,
        grid_spec=pltpu.PrefetchScalarGridSpec(
            num_scalar_prefetch=5,
            grid=(N_BLOCKS,),
            in_specs=[
                any_spec,
                pl.BlockSpec((1, D_MODEL, D_EXPERT), w_blk),
                pl.BlockSpec((1, D_MODEL, D_EXPERT), w_blk),
                pl.BlockSpec((1, D_EXPERT, D_MODEL), w_blk),
            ],
            out_specs=any_spec,
            scratch_shapes=[
                pltpu.SMEM((INV_LEN,), jnp.int32),
                pltpu.VMEM((N_TOK * PACK_ROWS, LANES), jnp.uint32),
            ] + [pltpu.VMEM((BLOCK_PACK_ROWS, LANES), jnp.uint32)] * (2 * N_SLOTS) + [
                pltpu.VMEM((D_MODEL, D_EXPERT), BF16),
                pltpu.VMEM((D_MODEL, D_EXPERT), BF16),
                pltpu.VMEM((D_EXPERT, D_MODEL), BF16),
                pltpu.SemaphoreType.DMA,
                pltpu.SemaphoreType.DMA((N_SLOTS,)),
            ],
        ),
        out_shape=jax.ShapeDtypeStruct(((N_ASSIGN + DUMMY_ROWS) * PACK_ROWS, LANES), jnp.uint32),
        compiler_params=pltpu.CompilerParams(
            dimension_semantics=("arbitrary",),
            vmem_limit_bytes=EXPERT_VMEM_LIMIT,
        ),
        name="experts",
    )(bexp, nused, dest, gap_lo, gap_hi, n2, w_gate, w_up, w_down)


def _combine_kernel(h1_ref, mf_ref, fg_ref, y0_ref, y1_ref, out_ref):
    rows = mf_ref[...]
    sel = (lax.broadcasted_iota(jnp.int32, (META_ROWS, LANES), 0)
           == lax.broadcasted_iota(jnp.int32, (META_ROWS, LANES), 1)).astype(BF16)
    w = sum(lax.dot_general(part, sel, TN_DIMS, preferred_element_type=F32)
            for part in _split_bf16(rows))
    y0 = _unpack_bf16_pairs(_load_packed_rows(y0_ref, TD))
    y1 = _unpack_bf16_pairs(_load_packed_rows(y1_ref, TD))
    h = h1_ref[...] + y0 * w[:, 0:1] + y1 * w[:, 1:2]
    out_ref[...] = _rms(h, fg_ref[...])


COMBINE_VMEM_LIMIT = 2 * (2 * (2 * TD * D_MODEL * 4) + 2 * (2 * TD * PACK_COLS * 4))


def _combine(h1, mf, fg, y):
    n_steps = N_TOK // TD
    tok = lambda i: (i, 0)
    return pl.pallas_call(
        _combine_kernel,
        grid=(n_steps,),
        in_specs=[
            pl.BlockSpec((TD, D_MODEL), tok),
            pl.BlockSpec((META_ROWS, TD), lambda i: (0, i)),
            pl.BlockSpec((1, D_MODEL), lambda i: (0, 0)),
            pl.BlockSpec((TD * PACK_ROWS, LANES), tok),
            pl.BlockSpec((TD * PACK_ROWS, LANES), lambda i: (i + n_steps, 0)),
        ],
        out_specs=pl.BlockSpec((TD, D_MODEL), tok),
        out_shape=jax.ShapeDtypeStruct((N_TOK, D_MODEL), F32),
        compiler_params=pltpu.CompilerParams(
            dimension_semantics=("arbitrary",),
            vmem_limit_bytes=COMBINE_VMEM_LIMIT,
        ),
        name="combine",
    )(h1, mf, fg, y, y)


def kernel(x, norm1_gain, w_in, gmlp_v_gain, gmlp_w_s, gmlp_b_s, hgrn_lower_bounds,
           hgrn_out_gain, w_out, norm2_gain, w_group_router, b_group_router,
           w_expert_router, b_expert_router, w_gate, w_up, w_down, final_gain):
    l = 0
    x2 = x.reshape(N_TOK, D_MODEL)
    bst = jnp.tile(jnp.transpose(gmlp_b_s[l]), (TS // GMLP_BLOCK, 1))
    w_router = jnp.concatenate([w_group_router[l], w_expert_router[l]], axis=1)
    w_router = jnp.pad(w_router, ((0, 0), (0, LANES - w_router.shape[1]))).astype(BF16)
    b_router = jnp.concatenate([b_group_router[l], b_expert_router[l]])
    b_router = jnp.pad(b_router, (0, ROUTER_ROWS - b_router.shape[0])).reshape(ROUTER_ROWS, 1)

    h1, n2, meta_i, meta_f, counts = _mixer(
        x2, norm1_gain[l].reshape(1, D_MODEL), w_in[l],
        gmlp_v_gain[l].reshape(1, D_A), gmlp_w_s[l], bst,
        hgrn_lower_bounds, hgrn_out_gain[l].reshape(1, D_B), w_out[l],
        norm2_gain[l].reshape(1, D_MODEL), w_router, b_router)

    cnt = counts[EXPERT_ROW0:EXPERT_ROW0 + N_EXPERTS, 0]
    padded = ((cnt + MOE_BLOCK - 1) // MOE_BLOCK) * MOE_BLOCK
    pend = jnp.cumsum(padded)
    pstart = pend - padded
    eid = meta_i[0:TOP_K]
    rank = meta_i[TOP_K:2 * TOP_K]
    base = jnp.sum(jnp.where(eid[:, :, None] == jnp.arange(N_EXPERTS)[None, None, :],
                             pstart[None, None, :], 0), axis=-1)
    dest = (base + rank).astype(jnp.int32).reshape(N_ASSIGN)
    gap_lo = (pstart + cnt).astype(jnp.int32)
    gap_hi = jnp.concatenate([pstart[1:], jnp.full((1,), INV_LEN)]).astype(jnp.int32)
    blk_start = jnp.arange(N_BLOCKS, dtype=jnp.int32) * MOE_BLOCK
    bexp = jnp.clip(jnp.sum(blk_start[:, None] >= pend[None, :], axis=1), 0,
                    N_EXPERTS - 1).astype(jnp.int32)
    nused = (pend[-1:] // MOE_BLOCK).astype(jnp.int32)

    y = _experts(bexp, nused, dest, gap_lo, gap_hi, n2, w_gate[l], w_up[l], w_down[l])
    out = _combine(h1, meta_f, final_gain.reshape(1, D_MODEL), y)
    return out.reshape(BATCH, SEQ, D_MODEL)
```
